```python
import jax, jax.numpy as jnp
from jax import lax
import numpy as np

D_MODEL = 1024
BATCH = 1
SEQ = 16384
DEPTH = 1
DEC_BATCH = 128
DEC_SEQ = 1
PAST_LEN = 16384
PAGE_SIZE = 128

D_MIX = D_MODEL
D_ATTN = D_MIX // 2
D_POOL = D_MIX - D_ATTN
HEAD_DIM = 64
N_HEADS = D_ATTN // HEAD_DIM
N_KV_HEADS = 2
GROUP = N_HEADS // N_KV_HEADS
D_KV = N_KV_HEADS * HEAD_DIM
WINDOW = 128
BLOCK = 128
POOL_WINDOWS = (2, 4, 8, 16)
N_POOL_GROUPS = len(POOL_WINDOWS)
POOL_GROUP_DIM = D_POOL // N_POOL_GROUPS
POOL_STATE = max(POOL_WINDOWS) - 1
D_IN_PROJ = D_ATTN + 2 * D_KV + D_ATTN + D_POOL + D_POOL
EPS = 1e-6
NEG_INF = -1e30

kernel_name = "hymba_swa_sink_pool_hybrid_step"


def _rms(x, w):
    xf = x.astype(jnp.float32)
    r = xf * lax.rsqrt(jnp.mean(xf * xf, axis=-1, keepdims=True) + EPS)
    return (r * w.astype(jnp.float32)).astype(x.dtype)


def _project(x, norm_w, w_in, q_norm_w, k_norm_w):
    B, T = x.shape[0], x.shape[1]
    z = _rms(x, norm_w) @ w_in
    o1 = D_ATTN
    o2 = o1 + D_KV
    o3 = o2 + D_KV
    o4 = o3 + D_ATTN
    o5 = o4 + D_POOL
    q = _rms(z[..., :o1].reshape(B, T, N_KV_HEADS, GROUP, HEAD_DIM), q_norm_w)
    k = _rms(z[..., o1:o2].reshape(B, T, N_KV_HEADS, HEAD_DIM), k_norm_w)
    v = z[..., o2:o3].reshape(B, T, N_KV_HEADS, HEAD_DIM)
    g_a = z[..., o3:o4]
    u = z[..., o4:o5]
    g_p = z[..., o5:]
    return q, k, v, g_a, u, g_p


def _sink_attn(q, k, v, q_pos, k_pos, sinks):
    s = jnp.einsum('bqkgd,bskd->bkgqs', q, k).astype(jnp.float32) * (HEAD_DIM ** -0.5)
    rel = q_pos[:, None] - k_pos[None, :]
    valid = (rel >= 0) & (rel <= WINDOW) & (k_pos[None, :] >= 0)
    s = jnp.where(valid, s, NEG_INF)
    sink = jnp.broadcast_to(sinks.astype(jnp.float32).reshape(N_KV_HEADS, GROUP, 1, 1),
                            s.shape[:-1] + (1,))
    p = jax.nn.softmax(jnp.concatenate([s, sink], axis=-1), axis=-1)[..., :-1]
    return jnp.einsum('bkgqs,bskd->bqkgd', p.astype(v.dtype), v)


def _attn_prompt(q, k, v, sinks):
    B, T = q.shape[0], q.shape[1]
    nb = T // BLOCK
    qb = q.reshape(B, nb, BLOCK, N_KV_HEADS, GROUP, HEAD_DIM)
    kb = k.reshape(B, nb, BLOCK, N_KV_HEADS, HEAD_DIM)
    vb = v.reshape(B, nb, BLOCK, N_KV_HEADS, HEAD_DIM)
    pad = ((0, 0), (1, 0), (0, 0), (0, 0), (0, 0))
    kk = jnp.concatenate([jnp.pad(kb, pad)[:, :-1], kb], axis=2)
    vv = jnp.concatenate([jnp.pad(vb, pad)[:, :-1], vb], axis=2)
    start = jnp.arange(nb, dtype=jnp.int32) * BLOCK
    q_pos = start[:, None] + jnp.arange(BLOCK, dtype=jnp.int32)[None, :]
    k_pos = start[:, None] - BLOCK + jnp.arange(2 * BLOCK, dtype=jnp.int32)[None, :]
    o = jax.vmap(_sink_attn, in_axes=(1, 1, 1, 0, 0, None), out_axes=1)(qb, kk, vv, q_pos, k_pos, sinks)
    return o.reshape(B, T, D_ATTN)


def _attn_sample(q, k, v, cache_k, cache_v, sinks):
    B, T = q.shape[0], q.shape[1]
    wc = cache_k.shape[1]
    kk = jnp.concatenate([cache_k.astype(k.dtype), k], axis=1)
    vv = jnp.concatenate([cache_v.astype(v.dtype), v], axis=1)
    q_pos = PAST_LEN + jnp.arange(T, dtype=jnp.int32)
    k_pos = PAST_LEN - wc + jnp.arange(wc + T, dtype=jnp.int32)
    o = _sink_attn(q, kk, vv, q_pos, k_pos, sinks)
    return o.reshape(B, T, D_ATTN), kk[:, -wc:], vv[:, -wc:]


def _pool(u_ext, pos0, w_pool, pool_scale):
    B = u_ext.shape[0]
    P = POOL_STATE
    T = u_ext.shape[1] - P
    uf = u_ext.astype(jnp.float32)
    cs = jnp.concatenate([jnp.zeros((B, 1, D_POOL), jnp.float32), jnp.cumsum(uf, axis=1)], axis=1)
    pos = pos0 + jnp.arange(T, dtype=jnp.int32)
    u_tok = uf[:, P:]
    outs = []
    for gi, w in enumerate(POOL_WINDOWS):
        lo, hi = gi * POOL_GROUP_DIM, (gi + 1) * POOL_GROUP_DIM
        win = cs[:, P + 1:P + 1 + T, lo:hi] - cs[:, P + 1 - w:P + 1 - w + T, lo:hi]
        cnt = jnp.minimum(w, pos + 1).astype(jnp.float32)[None, :, None]
        outs.append(win / cnt - u_tok[..., lo:hi])
    d = jnp.stack(outs, axis=2)
    y = jnp.einsum('btgc,gcd->btgd', d, w_pool.astype(jnp.float32)).reshape(B, T, D_POOL)
    return (y * pool_scale.astype(jnp.float32)).astype(u_ext.dtype)


def _merge(x, attn_o, g_a, pool_o, g_p, w_out):
    mixed = jnp.concatenate([attn_o * jax.nn.silu(g_a), pool_o * jax.nn.silu(g_p)], axis=-1)
    return x + mixed @ w_out


def setup_inputs(seed: int = 0) -> dict:
    key = jax.random.key(seed)
    ks = jax.random.split(key, 16)
    f32 = jnp.float32
    return {
        "x_prompt": jax.random.normal(ks[0], (BATCH, SEQ, D_MODEL), f32),
        "x_sample": jax.random.normal(ks[1], (DEC_BATCH, DEC_SEQ, D_MODEL), f32),
        "cache_k": jax.random.normal(ks[2], (DEPTH, DEC_BATCH, WINDOW, N_KV_HEADS, HEAD_DIM), f32),
        "cache_v": jax.random.normal(ks[3], (DEPTH, DEC_BATCH, WINDOW, N_KV_HEADS, HEAD_DIM), f32),
        "state_pool": jax.random.normal(ks[4], (DEPTH, DEC_BATCH, POOL_STATE, D_POOL), f32),
        "norm_w": 1.0 + 0.02 * jax.random.normal(ks[5], (DEPTH, D_MODEL), f32),
        "w_in": jax.random.normal(ks[6], (DEPTH, D_MODEL, D_IN_PROJ), f32) * D_MODEL ** -0.5,
        "q_norm_w": 1.0 + 0.02 * jax.random.normal(ks[7], (DEPTH, HEAD_DIM), f32),
        "k_norm_w": 1.0 + 0.02 * jax.random.normal(ks[8], (DEPTH, HEAD_DIM), f32),
        "sinks": 0.5 * jax.random.normal(ks[9], (DEPTH, N_HEADS), f32),
        "w_pool": jax.random.normal(ks[10], (DEPTH, N_POOL_GROUPS, POOL_GROUP_DIM, POOL_GROUP_DIM), f32) * POOL_GROUP_DIM ** -0.5,
        "pool_scale": 1.0 + 0.02 * jax.random.normal(ks[11], (DEPTH, D_POOL), f32),
        "w_out": jax.random.normal(ks[12], (DEPTH, D_MIX, D_MODEL), f32) * D_MIX ** -0.5,
    }


def reference(x_prompt, x_sample, cache_k, cache_v, state_pool, norm_w, w_in, q_norm_w,
              k_norm_w, sinks, w_pool, pool_scale, w_out):
    hp, hs = x_prompt, x_sample
    kp, vp, pp, kq, vq, pq = [], [], [], [], [], []
    for l in range(DEPTH):
        q, k, v, g_a, u, g_p = _project(hp, norm_w[l], w_in[l], q_norm_w[l], k_norm_w[l])
        a_o = _attn_prompt(q, k, v, sinks[l])
        u_ext = jnp.pad(u, ((0, 0), (POOL_STATE, 0), (0, 0)))
        p_o = _pool(u_ext, 0, w_pool[l], pool_scale[l])
        kp.append(k[:, -WINDOW:])
        vp.append(v[:, -WINDOW:])
        pp.append(u_ext[:, -POOL_STATE:])
        hp = _merge(hp, a_o, g_a, p_o, g_p, w_out[l])
        q, k, v, g_a, u, g_p = _project(hs, norm_w[l], w_in[l], q_norm_w[l], k_norm_w[l])
        a_o, k_new, v_new = _attn_sample(q, k, v, cache_k[l], cache_v[l], sinks[l])
        u_ext = jnp.concatenate([state_pool[l].astype(u.dtype), u], axis=1)
        p_o = _pool(u_ext, PAST_LEN, w_pool[l], pool_scale[l])
        kq.append(k_new)
        vq.append(v_new)
        pq.append(u_ext[:, -POOL_STATE:])
        hs = _merge(hs, a_o, g_a, p_o, g_p, w_out[l])
    return (hp, hs, jnp.stack(kp), jnp.stack(vp), jnp.stack(pp), jnp.stack(kq), jnp.stack(vq), jnp.stack(pq))
```

```python
import jax
import jax.numpy as jnp
from jax import lax
from jax.experimental import pallas as pl
from jax.experimental.pallas import tpu as pltpu

D_MODEL = 1024
HEAD_DIM = 64
N_HEADS = 8
GROUP = 4
WINDOW = 128
BLOCK = 128
POOL_WINDOWS = (2, 4, 8, 16)
POOL_STATE = 15
D_ATTN = 512
D_POOL = 512
D_IN_PROJ = 2304
EPS = 1e-6
NEG_INF = -1e30

O_K = 512
O_V = 640
O_GA = 768
O_U = 1280
O_GP = 1792

LANES = 128
TB = 512
NSB = TB // BLOCK
BB = 16
QROWS = 16
VMEM_LIMIT = 56 * 1024 * 1024

F32 = jnp.float32
BF16 = jnp.bfloat16
_NT = (((1,), (1,)), ((), ()))


def _silu(g):
    return g * jax.nn.sigmoid(g)


def _lo_mask():
    return lax.broadcasted_iota(jnp.int32, (1, LANES), 1) < HEAD_DIM


def _pair_rms(zs, lo, w2):
    sq = zs * zs
    s_lo = jnp.sum(jnp.where(lo, sq, 0.0), axis=-1, keepdims=True)
    s_hi = jnp.sum(jnp.where(lo, 0.0, sq), axis=-1, keepdims=True)
    r = lax.rsqrt(jnp.where(lo, s_lo, s_hi) * (1.0 / HEAD_DIM) + EPS)
    return zs * r * w2


def _project(x, nw, win):
    ms = jnp.mean(x * x, axis=-1, keepdims=True)
    h = (x * lax.rsqrt(ms + EPS) * nw).astype(BF16)
    return jnp.dot(h, win, preferred_element_type=F32)


def _prompt_kernel(sinks_ref, x_ref, nw_ref, win_ref, qw_ref, kw_ref, wpool_ref, pscale_ref, wout_ref,
                   y_ref, knew_ref, vnew_ref, unew_ref,
                   z_ref, qbuf, kbuf, vbuf, uext, mixed):
    i = pl.program_id(0)
    lo = _lo_mask()

    @pl.when(i == 0)
    def _():
        kbuf[:, 0:BLOCK, :] = jnp.zeros((2, BLOCK, LANES), BF16)
        vbuf[:, 0:BLOCK, :] = jnp.zeros((2, BLOCK, LANES), BF16)
        uext[0:16, :] = jnp.zeros((16, D_POOL), F32)

    z_ref[...] = _project(x_ref[...], nw_ref[...], win_ref[...])

    khat = _pair_rms(z_ref[:, O_K:O_K + LANES], lo, kw_ref[...])
    knew_ref[...] = khat[TB - WINDOW:, :]
    kr = pltpu.roll(khat, HEAD_DIM, axis=1)
    kbuf[0, BLOCK:, :] = jnp.where(lo, khat, kr).astype(BF16)
    kbuf[1, BLOCK:, :] = jnp.where(lo, kr, khat).astype(BF16)
    vz = z_ref[:, O_V:O_V + LANES]
    vnew_ref[...] = vz[TB - WINDOW:, :]
    vr = pltpu.roll(vz, HEAD_DIM, axis=1)
    vbuf[0, BLOCK:, :] = jnp.where(lo, vz, vr).astype(BF16)
    vbuf[1, BLOCK:, :] = jnp.where(lo, vr, vz).astype(BF16)

    for j in range(4):
        qhat = _pair_rms(z_ref[:, j * LANES:(j + 1) * LANES], lo, qw_ref[...])
        q_lo = jnp.where(lo, qhat, 0.0).astype(BF16)
        q_hi = jnp.where(lo, 0.0, qhat).astype(BF16)
        c, g0 = j // 2, 2 * (j % 2)
        for sb in range(NSB):
            rows = slice(sb * BLOCK, (sb + 1) * BLOCK)
            qbuf[c, sb, g0 * BLOCK:(g0 + 1) * BLOCK, :] = q_lo[rows]
            qbuf[c, sb, (g0 + 1) * BLOCK:(g0 + 2) * BLOCK, :] = q_hi[rows]

    r_io = lax.broadcasted_iota(jnp.int32, (BLOCK, 2 * BLOCK), 0)
    c_io = lax.broadcasted_iota(jnp.int32, (BLOCK, 2 * BLOCK), 1)
    band = (c_io >= r_io) & (c_io <= r_io + WINDOW)
    first_lo = jnp.where(i > 0, 0, BLOCK)
    band_first = band & (c_io >= first_lo)

    for sb in range(NSB):
        valid = band_first if sb == 0 else band
        rows = slice(sb * BLOCK, (sb + 1) * BLOCK)
        for c in range(2):
            keys = kbuf[c, sb * BLOCK:(sb + 2) * BLOCK, :]
            vals = vbuf[c, sb * BLOCK:(sb + 2) * BLOCK, :]
            s = lax.dot_general(qbuf[c, sb], keys, _NT, preferred_element_type=F32)
            ps, inv = [], []
            for g in range(GROUP):
                sg = jnp.where(valid, s[g * BLOCK:(g + 1) * BLOCK, :], NEG_INF)
                sink = sinks_ref[c * GROUP + g]
                m = jnp.maximum(jnp.max(sg, axis=-1, keepdims=True), sink)
                p = jnp.exp(sg - m)
                l = jnp.sum(p, axis=-1, keepdims=True) + jnp.exp(sink - m)
                ps.append(p.astype(BF16))
                inv.append(1.0 / l)
            o = jnp.dot(jnp.concatenate(ps, axis=0), vals, preferred_element_type=F32)
            og = [o[g * BLOCK:(g + 1) * BLOCK, :] * inv[g] for g in range(GROUP)]
            for jj in range(2):
                slab = jnp.where(lo, og[2 * jj], og[2 * jj + 1])
                col = (2 * c + jj) * LANES
                ga = z_ref[rows, O_GA + col:O_GA + col + LANES]
                mixed[rows, col:col + LANES] = (slab * _silu(ga)).astype(BF16)

    uext[16:, :] = z_ref[:, O_U:O_U + D_POOL]
    unew_ref[...] = z_ref[TB - POOL_STATE:TB, O_U:O_U + D_POOL]
    pos16 = i * TB + lax.broadcasted_iota(jnp.int32, (16, LANES), 0)
    for g, w in enumerate(POOL_WINDOWS):
        cols = slice(g * LANES, (g + 1) * LANES)
        acc = uext[:, cols]
        sh = 1
        while sh < w:
            acc = acc + pltpu.roll(acc, sh, axis=0)
            sh *= 2
        ug = z_ref[:, O_U + g * LANES:O_U + (g + 1) * LANES]
        cnt = jnp.minimum(w, pos16 + 1).astype(F32)
        d_first = acc[16:32, :] / cnt - ug[0:16, :]
        d_rest = acc[32:, :] * (1.0 / w) - ug[16:, :]
        d = jnp.concatenate([d_first, d_rest], axis=0).astype(BF16)
        po = jnp.dot(d, wpool_ref[g], preferred_element_type=F32) * pscale_ref[:, cols]
        gp = z_ref[:, O_GP + g * LANES:O_GP + (g + 1) * LANES]
        mixed[:, D_ATTN + g * LANES:D_ATTN + (g + 1) * LANES] = (po * _silu(gp)).astype(BF16)

    kbuf[:, 0:BLOCK, :] = kbuf[:, TB:TB + BLOCK, :]
    vbuf[:, 0:BLOCK, :] = vbuf[:, TB:TB + BLOCK, :]
    uext[0:16, :] = uext[TB:TB + 16, :]

    y_ref[...] = x_ref[...] + jnp.dot(mixed[...], wout_ref[...], preferred_element_type=F32)


def _prompt_call(sinks, x, nw, win, qw2, kw2, wpool, pscale, wout):
    seq = x.shape[0]
    const2 = lambda i: (0, 0)
    return pl.pallas_call(
        _prompt_kernel,
        grid=(seq // TB,),
        in_specs=[
            pl.BlockSpec(memory_space=pltpu.SMEM),
            pl.BlockSpec((TB, D_MODEL), lambda i: (i, 0)),
            pl.BlockSpec((1, D_MODEL), const2),
            pl.BlockSpec((D_MODEL, D_IN_PROJ), const2),
            pl.BlockSpec((1, LANES), const2),
            pl.BlockSpec((1, LANES), const2),
            pl.BlockSpec((4, LANES, LANES), lambda i: (0, 0, 0)),
            pl.BlockSpec((1, D_POOL), const2),
            pl.BlockSpec((D_MODEL, D_MODEL), const2),
        ],
        out_specs=[
            pl.BlockSpec((TB, D_MODEL), lambda i: (i, 0)),
            pl.BlockSpec((WINDOW, LANES), const2),
            pl.BlockSpec((WINDOW, LANES), const2),
            pl.BlockSpec((POOL_STATE, D_POOL), const2),
        ],
        out_shape=[
            jax.ShapeDtypeStruct((seq, D_MODEL), F32),
            jax.ShapeDtypeStruct((WINDOW, LANES), F32),
            jax.ShapeDtypeStruct((WINDOW, LANES), F32),
            jax.ShapeDtypeStruct((POOL_STATE, D_POOL), F32),
        ],
        scratch_shapes=[
            pltpu.VMEM((TB, D_IN_PROJ), F32),
            pltpu.VMEM((2, NSB, GROUP * BLOCK, LANES), BF16),
            pltpu.VMEM((2, TB + BLOCK, LANES), BF16),
            pltpu.VMEM((2, TB + BLOCK, LANES), BF16),
            pltpu.VMEM((TB + 16, D_POOL), F32),
            pltpu.VMEM((TB, D_MODEL), BF16),
        ],
        compiler_params=pltpu.CompilerParams(dimension_semantics=("arbitrary",), vmem_limit_bytes=VMEM_LIMIT),
        name="prompt_group",
    )(sinks, x, nw, win, qw2, kw2, wpool, pscale, wout)


def _sample_kernel(sinks_ref, x_ref, nw_ref, win_ref, qw_ref, kw_ref, wpool_ref, pscale_ref, wout_ref,
                   ck_ref, cv_ref, sp_ref,
                   y_ref, knew_ref, vnew_ref, spnew_ref,
                   z_ref, khat_ref, q3, o3, dbuf, mixed):
    i = pl.program_id(0)
    nb = x_ref.shape[0]
    lo = _lo_mask()

    @pl.when(i == 0)
    def _():
        z_ref[...] = _project(x_ref[...], nw_ref[...], win_ref[...])
        khat_ref[...] = _pair_rms(z_ref[:, O_K:O_K + LANES], lo, kw_ref[...])
        q3[...] = jnp.zeros(q3.shape, F32)
        for j in range(4):
            qhat = _pair_rms(z_ref[:, j * LANES:(j + 1) * LANES], lo, qw_ref[...])
            qrot = pltpu.roll(qhat, HEAD_DIM, axis=1)
            grp_lo = j < 2
            for half in range(2):
                h = 2 * j + half
                src = qhat if (half == 0) == grp_lo else qrot
                q3[pl.ds(h, nb, stride=QROWS), :] = jnp.where(lo if grp_lo else ~lo, src, 0.0)

    row0 = pl.multiple_of(i * BB, BB)
    rid = lax.broadcasted_iota(jnp.int32, (QROWS, 1), 0)
    sink_col = jnp.zeros((QROWS, 1), F32)
    for h in range(N_HEADS):
        sink_col = jnp.where(rid == h, sinks_ref[h], sink_col)

    rid15 = lax.broadcasted_iota(jnp.int32, (POOL_STATE, D_POOL), 0)
    cid15 = lax.broadcasted_iota(jnp.int32, (POOL_STATE, D_POOL), 1)
    first_row = jnp.zeros((POOL_STATE, D_POOL), jnp.int32)
    inv_w = jnp.zeros((1, D_POOL), F32)
    cid1 = lax.broadcasted_iota(jnp.int32, (1, D_POOL), 1)
    for g, w in enumerate(POOL_WINDOWS):
        first_row = jnp.where((cid15 >= g * LANES) & (cid15 < (g + 1) * LANES), POOL_STATE - (w - 1), first_row)
        inv_w = jnp.where((cid1 >= g * LANES) & (cid1 < (g + 1) * LANES), 1.0 / w, inv_w)
    pool_mask = rid15 >= first_row

    rows = pl.ds(row0, BB)
    k_chunk = khat_ref[rows, :]
    v_chunk = z_ref[rows, O_V:O_V + LANES]
    u_chunk = z_ref[rows, O_U:O_U + D_POOL]
    for bb in range(BB):
        row = row0 + bb
        qb = q3[pl.ds(pl.multiple_of(row * QROWS, QROWS), QROWS), :]
        k_row = k_chunk[bb:bb + 1, :]
        v_row = v_chunk[bb:bb + 1, :]
        s = lax.dot_general(qb.astype(BF16), ck_ref[bb].astype(BF16), _NT, preferred_element_type=F32)
        s_new = jnp.sum(qb * k_row, axis=-1, keepdims=True)
        m = jnp.maximum(jnp.maximum(jnp.max(s, axis=-1, keepdims=True), s_new), sink_col)
        p = jnp.exp(s - m)
        p_new = jnp.exp(s_new - m)
        l = jnp.sum(p, axis=-1, keepdims=True) + p_new + jnp.exp(sink_col - m)
        o = jnp.dot(p.astype(BF16), cv_ref[bb].astype(BF16), preferred_element_type=F32) + p_new * v_row
        o3[bb * QROWS:(bb + 1) * QROWS, :] = o * (1.0 / l)

        knew_ref[bb, 0:WINDOW - 1, :] = ck_ref[bb, 1:WINDOW, :]
        knew_ref[bb, WINDOW - 1:WINDOW, :] = k_row
        vnew_ref[bb, 0:WINDOW - 1, :] = cv_ref[bb, 1:WINDOW, :]
        vnew_ref[bb, WINDOW - 1:WINDOW, :] = v_row

        u_row = u_chunk[bb:bb + 1, :]
        spb = sp_ref[bb]
        spnew_ref[bb, 0:POOL_STATE - 1, :] = spb[1:POOL_STATE, :]
        spnew_ref[bb, POOL_STATE - 1:POOL_STATE, :] = u_row
        win_sum = jnp.sum(jnp.where(pool_mask, spb, 0.0), axis=0, keepdims=True) + u_row
        dbuf[bb:bb + 1, :] = win_sum * inv_w - u_row

    for j in range(4):
        grp_lo = j < 2
        oh = [o3[pl.ds(2 * j + half, BB, stride=QROWS), :] for half in range(2)]
        a = oh[0] if grp_lo else pltpu.roll(oh[0], HEAD_DIM, axis=1)
        b = pltpu.roll(oh[1], HEAD_DIM, axis=1) if grp_lo else oh[1]
        ga = z_ref[rows, O_GA + j * LANES:O_GA + (j + 1) * LANES]
        mixed[rows, j * LANES:(j + 1) * LANES] = (jnp.where(lo, a, b) * _silu(ga)).astype(BF16)
    for g in range(4):
        cols = slice(g * LANES, (g + 1) * LANES)
        d = dbuf[:, cols].astype(BF16)
        po = jnp.dot(d, wpool_ref[g], preferred_element_type=F32) * pscale_ref[:, cols]
        gp = z_ref[rows, O_GP + g * LANES:O_GP + (g + 1) * LANES]
        mixed[rows, D_ATTN + g * LANES:D_ATTN + (g + 1) * LANES] = (po * _silu(gp)).astype(BF16)

    @pl.when(i == pl.num_programs(0) - 1)
    def _():
        y_ref[...] = x_ref[...] + jnp.dot(mixed[...], wout_ref[...], preferred_element_type=F32)


def _sample_call(sinks, x, nw, win, qw2, kw2, wpool, pscale, wout, ck, cv, sp):
    nb = x.shape[0]
    const2 = lambda i: (0, 0)
    chunk3 = lambda i: (i, 0, 0)
    return pl.pallas_call(
        _sample_kernel,
        grid=(nb // BB,),
        in_specs=[
            pl.BlockSpec(memory_space=pltpu.SMEM),
            pl.BlockSpec((nb, D_MODEL), const2),
            pl.BlockSpec((1, D_MODEL), const2),
            pl.BlockSpec((D_MODEL, D_IN_PROJ), const2),
            pl.BlockSpec((1, LANES), const2),
            pl.BlockSpec((1, LANES), const2),
            pl.BlockSpec((4, LANES, LANES), lambda i: (0, 0, 0)),
            pl.BlockSpec((1, D_POOL), const2),
            pl.BlockSpec((D_MODEL, D_MODEL), const2),
            pl.BlockSpec((BB, WINDOW, LANES), chunk3),
            pl.BlockSpec((BB, WINDOW, LANES), chunk3),
            pl.BlockSpec((BB, POOL_STATE, D_POOL), chunk3),
        ],
        out_specs=[
            pl.BlockSpec((nb, D_MODEL), const2),
            pl.BlockSpec((BB, WINDOW, LANES), chunk3),
            pl.BlockSpec((BB, WINDOW, LANES), chunk3),
            pl.BlockSpec((BB, POOL_STATE, D_POOL), chunk3),
        ],
        out_shape=[
            jax.ShapeDtypeStruct((nb, D_MODEL), F32),
            jax.ShapeDtypeStruct((nb, WINDOW, LANES), F32),
            jax.ShapeDtypeStruct((nb, WINDOW, LANES), F32),
            jax.ShapeDtypeStruct((nb, POOL_STATE, D_POOL), F32),
        ],
        scratch_shapes=[
            pltpu.VMEM((nb, D_IN_PROJ), F32),
            pltpu.VMEM((nb, LANES), F32),
            pltpu.VMEM((nb * QROWS, LANES), F32),
            pltpu.VMEM((BB * QROWS, LANES), F32),
            pltpu.VMEM((BB, D_POOL), F32),
            pltpu.VMEM((nb, D_MODEL), BF16),
        ],
        compiler_params=pltpu.CompilerParams(dimension_semantics=("arbitrary",), vmem_limit_bytes=VMEM_LIMIT),
        name="sample_group",
    )(sinks, x, nw, win, qw2, kw2, wpool, pscale, wout, ck, cv, sp)


def kernel(x_prompt, x_sample, cache_k, cache_v, state_pool, norm_w, w_in, q_norm_w, k_norm_w, sinks, w_pool,
           pool_scale, w_out):
    depth = norm_w.shape[0]
    assert depth == 1 and x_prompt.shape[0] == 1 and x_sample.shape[1] == 1
    seq = x_prompt.shape[1]
    nb = x_sample.shape[0]
    assert seq % TB == 0 and nb % BB == 0

    win = w_in[0].astype(BF16)
    wout = w_out[0].astype(BF16)
    wpool = w_pool[0].astype(BF16)
    nw = norm_w
    qw2 = jnp.tile(q_norm_w * (HEAD_DIM ** -0.5), (1, 2))
    kw2 = jnp.tile(k_norm_w, (1, 2))
    sk = sinks[0]

    yp, kp, vp, up = _prompt_call(sk, x_prompt[0], nw, win, qw2, kw2, wpool, pool_scale, wout)
    ys, kq, vq, pq = _sample_call(sk, x_sample[:, 0], nw, win, qw2, kw2, wpool, pool_scale, wout,
                                  cache_k[0].reshape(nb, WINDOW, LANES), cache_v[0].reshape(nb, WINDOW, LANES),
                                  state_pool[0])
    kv_shape = (1, WINDOW, 2, HEAD_DIM)
    return (yp[None], ys[:, None], kp.reshape((1,) + kv_shape), vp.reshape((1,) + kv_shape), up[None, None],
            kq.reshape((1, nb) + kv_shape[1:]), vq.reshape((1, nb) + kv_shape[1:]), pq[None])
```

```python
import jax
import jax.numpy as jnp
from jax import lax
from jax.experimental import pallas as pl
from jax.experimental.pallas import tpu as pltpu

D_MODEL = 1024
HEAD_DIM = 64
N_HEADS = 8
N_KV_HEADS = 2
GROUP = 4
WINDOW = 128
BLOCK = 128
POOL_WINDOWS = (2, 4, 8, 16)
POOL_STATE = 15
D_ATTN = 512
D_POOL = 512
D_IN_PROJ = 2304
EPS = 1e-6
NEG_INF = -1e30

O_K = 512
O_V = 640
O_GA = 768
O_U = 1280
O_GP = 1792

LANES = 128
TB = 512
NSB = TB // BLOCK
BB = 16
QROWS = 16
VMEM_LIMIT = 56 * 1024 * 1024

F32 = jnp.float32
BF16 = jnp.bfloat16
_NT = (((1,), (1,)), ((), ()))


def _silu(g):
    return g * jax.nn.sigmoid(g)


def _lo_mask():
    return lax.broadcasted_iota(jnp.int32, (1, LANES), 1) < HEAD_DIM


def _pair_rms(zs, lo, w2):
    sq = zs * zs
    s_lo = jnp.sum(jnp.where(lo, sq, 0.0), axis=-1, keepdims=True)
    s_hi = jnp.sum(jnp.where(lo, 0.0, sq), axis=-1, keepdims=True)
    r = lax.rsqrt(jnp.where(lo, s_lo, s_hi) * (1.0 / HEAD_DIM) + EPS)
    return zs * r * w2


def _project(x, nw, win):
    ms = jnp.mean(x * x, axis=-1, keepdims=True)
    h = (x * lax.rsqrt(ms + EPS) * nw).astype(BF16)
    return jnp.dot(h, win, preferred_element_type=F32)


def _prompt_kernel(sinks_ref, x_ref, nw_ref, win_ref, qw_ref, kw_ref, wpool_ref, pscale_ref, wout_ref,
                   y_ref, knew_ref, vnew_ref, unew_ref,
                   z_ref, qbuf, kbuf, vbuf, uext, mixed):
    i = pl.program_id(0)
    lo = _lo_mask()

    @pl.when(i == 0)
    def _():
        kbuf[:, 0:BLOCK, :] = jnp.zeros((2, BLOCK, LANES), BF16)
        vbuf[:, 0:BLOCK, :] = jnp.zeros((2, BLOCK, LANES), BF16)
        uext[0:16, :] = jnp.zeros((16, D_POOL), F32)

    z_ref[...] = _project(x_ref[...], nw_ref[...], win_ref[...])

    khat = _pair_rms(z_ref[:, O_K:O_K + LANES], lo, kw_ref[...])
    kr = pltpu.roll(khat, HEAD_DIM, axis=1)
    kbuf[0, BLOCK:, :] = jnp.where(lo, khat, kr).astype(BF16)
    kbuf[1, BLOCK:, :] = jnp.where(lo, kr, khat).astype(BF16)
    vz = z_ref[:, O_V:O_V + LANES]
    vr = pltpu.roll(vz, HEAD_DIM, axis=1)
    vbuf[0, BLOCK:, :] = jnp.where(lo, vz, vr).astype(BF16)
    vbuf[1, BLOCK:, :] = jnp.where(lo, vr, vz).astype(BF16)

    for j in range(4):
        qhat = _pair_rms(z_ref[:, j * LANES:(j + 1) * LANES], lo, qw_ref[...])
        q_lo = jnp.where(lo, qhat, 0.0).astype(BF16)
        q_hi = jnp.where(lo, 0.0, qhat).astype(BF16)
        c, g0 = j // 2, 2 * (j % 2)
        for sb in range(NSB):
            rows = slice(sb * BLOCK, (sb + 1) * BLOCK)
            qbuf[c, sb, g0 * BLOCK:(g0 + 1) * BLOCK, :] = q_lo[rows]
            qbuf[c, sb, (g0 + 1) * BLOCK:(g0 + 2) * BLOCK, :] = q_hi[rows]

    r_io = lax.broadcasted_iota(jnp.int32, (BLOCK, 2 * BLOCK), 0)
    c_io = lax.broadcasted_iota(jnp.int32, (BLOCK, 2 * BLOCK), 1)
    band = (c_io >= r_io) & (c_io <= r_io + WINDOW)
    first_lo = jnp.where(i > 0, 0, BLOCK)
    band_first = band & (c_io >= first_lo)

    for sb in range(NSB):
        valid = band_first if sb == 0 else band
        rows = slice(sb * BLOCK, (sb + 1) * BLOCK)
        for c in range(2):
            keys = kbuf[c, sb * BLOCK:(sb + 2) * BLOCK, :]
            vals = vbuf[c, sb * BLOCK:(sb + 2) * BLOCK, :]
            s = lax.dot_general(qbuf[c, sb], keys, _NT, preferred_element_type=F32)
            ps, inv = [], []
            for g in range(GROUP):
                sg = jnp.where(valid, s[g * BLOCK:(g + 1) * BLOCK, :], NEG_INF)
                sink = sinks_ref[c * GROUP + g]
                m = jnp.maximum(jnp.max(sg, axis=-1, keepdims=True), sink)
                p = jnp.exp(sg - m)
                l = jnp.sum(p, axis=-1, keepdims=True) + jnp.exp(sink - m)
                ps.append(p.astype(BF16))
                inv.append(1.0 / l)
            o = jnp.dot(jnp.concatenate(ps, axis=0), vals, preferred_element_type=F32)
            og = [o[g * BLOCK:(g + 1) * BLOCK, :] * inv[g] for g in range(GROUP)]
            for jj in range(2):
                slab = jnp.where(lo, og[2 * jj], og[2 * jj + 1])
                col = (2 * c + jj) * LANES
                ga = z_ref[rows, O_GA + col:O_GA + col + LANES]
                mixed[rows, col:col + LANES] = (slab * _silu(ga)).astype(BF16)

    uext[16:, :] = z_ref[:, O_U:O_U + D_POOL]
    unew_ref[...] = z_ref[TB - POOL_STATE:TB, O_U:O_U + D_POOL]
    pos16 = i * TB + lax.broadcasted_iota(jnp.int32, (16, LANES), 0)
    for g, w in enumerate(POOL_WINDOWS):
        cols = slice(g * LANES, (g + 1) * LANES)
        acc = uext[:, cols]
        sh = 1
        while sh < w:
            acc = acc + pltpu.roll(acc, sh, axis=0)
            sh *= 2
        ug = z_ref[:, O_U + g * LANES:O_U + (g + 1) * LANES]
        cnt = jnp.minimum(w, pos16 + 1).astype(F32)
        d_first = acc[16:32, :] / cnt - ug[0:16, :]
        d_rest = acc[32:, :] * (1.0 / w) - ug[16:, :]
        d = jnp.concatenate([d_first, d_rest], axis=0).astype(BF16)
        po = jnp.dot(d, wpool_ref[g], preferred_element_type=F32) * pscale_ref[:, cols]
        gp = z_ref[:, O_GP + g * LANES:O_GP + (g + 1) * LANES]
        mixed[:, D_ATTN + g * LANES:D_ATTN + (g + 1) * LANES] = (po * _silu(gp)).astype(BF16)

    kbuf[:, 0:BLOCK, :] = kbuf[:, TB:TB + BLOCK, :]
    vbuf[:, 0:BLOCK, :] = vbuf[:, TB:TB + BLOCK, :]
    uext[0:16, :] = uext[TB:TB + 16, :]

    y_ref[...] = x_ref[...] + jnp.dot(mixed[...], wout_ref[...], preferred_element_type=F32)

    @pl.when(i == pl.num_programs(0) - 1)
    def _():
        tail = slice(TB - WINDOW, TB)
        knew_ref[...] = _pair_rms(z_ref[tail, O_K:O_K + LANES], lo, kw_ref[...]).T
        vnew_ref[...] = z_ref[tail, O_V:O_V + LANES].T


def _prompt_call(sinks, x, nw, win, qw2, kw2, wpool, pscale, wout):
    seq = x.shape[0]
    const2 = lambda i: (0, 0)
    return pl.pallas_call(
        _prompt_kernel,
        grid=(seq // TB,),
        in_specs=[
            pl.BlockSpec(memory_space=pltpu.SMEM),
            pl.BlockSpec((TB, D_MODEL), lambda i: (i, 0)),
            pl.BlockSpec((1, D_MODEL), const2),
            pl.BlockSpec((D_MODEL, D_IN_PROJ), const2),
            pl.BlockSpec((1, LANES), const2),
            pl.BlockSpec((1, LANES), const2),
            pl.BlockSpec((4, LANES, LANES), lambda i: (0, 0, 0)),
            pl.BlockSpec((1, D_POOL), const2),
            pl.BlockSpec((D_MODEL, D_MODEL), const2),
        ],
        out_specs=[
            pl.BlockSpec((TB, D_MODEL), lambda i: (i, 0)),
            pl.BlockSpec((LANES, WINDOW), const2),
            pl.BlockSpec((LANES, WINDOW), const2),
            pl.BlockSpec((POOL_STATE, D_POOL), const2),
        ],
        out_shape=[
            jax.ShapeDtypeStruct((seq, D_MODEL), F32),
            jax.ShapeDtypeStruct((LANES, WINDOW), F32),
            jax.ShapeDtypeStruct((LANES, WINDOW), F32),
            jax.ShapeDtypeStruct((POOL_STATE, D_POOL), F32),
        ],
        scratch_shapes=[
            pltpu.VMEM((TB, D_IN_PROJ), F32),
            pltpu.VMEM((2, NSB, GROUP * BLOCK, LANES), BF16),
            pltpu.VMEM((2, TB + BLOCK, LANES), BF16),
            pltpu.VMEM((2, TB + BLOCK, LANES), BF16),
            pltpu.VMEM((TB + 16, D_POOL), F32),
            pltpu.VMEM((TB, D_MODEL), BF16),
        ],
        compiler_params=pltpu.CompilerParams(dimension_semantics=("arbitrary",), vmem_limit_bytes=VMEM_LIMIT),
        name="prompt_group",
    )(sinks, x, nw, win, qw2, kw2, wpool, pscale, wout)


def _sample_kernel(sinks_ref, x_ref, nw_ref, win_ref, qw_ref, kw_ref, wpool_ref, pscale_ref, wout_ref,
                   ck_ref, cv_ref, sp_ref,
                   y_ref, knew_ref, vnew_ref, spnew_ref,
                   z_ref, q3, krep, vrep, snew, kvt, o3, mixed):
    i = pl.program_id(0)
    nb = x_ref.shape[0]
    lo = _lo_mask()

    @pl.when(i == 0)
    def _():
        z_ref[...] = _project(x_ref[...], nw_ref[...], win_ref[...])
        khat = _pair_rms(z_ref[:, O_K:O_K + LANES], lo, kw_ref[...])
        vnew = z_ref[:, O_V:O_V + LANES]
        kvt[0] = khat.T
        kvt[1] = vnew.T
        q3[...] = jnp.zeros(q3.shape, F32)
        krep[...] = jnp.zeros(krep.shape, F32)
        vrep[...] = jnp.zeros(vrep.shape, F32)
        for j in range(4):
            qhat = _pair_rms(z_ref[:, j * LANES:(j + 1) * LANES], lo, qw_ref[...])
            qrot = pltpu.roll(qhat, HEAD_DIM, axis=1)
            grp_lo = j < 2
            for half in range(2):
                h = 2 * j + half
                src = qhat if (half == 0) == grp_lo else qrot
                q3[pl.ds(h, nb, stride=QROWS), :] = jnp.where(lo if grp_lo else ~lo, src, 0.0)
                krep[pl.ds(h, nb, stride=QROWS), :] = khat
                vrep[pl.ds(h, nb, stride=QROWS), :] = vnew
        s_new = jnp.sum(q3[...] * krep[...], axis=-1, keepdims=True)
        snew[...] = jnp.broadcast_to(s_new, snew.shape)

    row0 = pl.multiple_of(i * BB, BB)
    rows = pl.ds(row0, BB)
    qrows = pl.ds(pl.multiple_of(i * (BB * QROWS), BB * QROWS), BB * QROWS)
    rid = lax.broadcasted_iota(jnp.int32, (1, QROWS, 1), 1)
    sink3 = jnp.zeros((1, QROWS, 1), F32)
    for h in range(N_HEADS):
        sink3 = jnp.where(rid == h, sinks_ref[h], sink3)

    q = q3[qrows, :].reshape(BB, QROWS, LANES)
    s_new = snew[qrows, :].reshape(BB, QROWS, LANES)
    kt = ck_ref[...]
    vt = cv_ref[...]
    s = lax.dot_general(q.astype(BF16), kt.astype(BF16), (((2,), (1,)), ((0,), (0,))), preferred_element_type=F32)
    m = jnp.maximum(jnp.maximum(jnp.max(s, axis=-1, keepdims=True), s_new), sink3)
    p = jnp.exp(s - m)
    p_new = jnp.exp(s_new - m)
    l = jnp.sum(p, axis=-1, keepdims=True) + p_new + jnp.exp(sink3 - m)
    o = lax.dot_general(p.astype(BF16), vt.astype(BF16), (((2,), (2,)), ((0,), (0,))), preferred_element_type=F32)
    o = (o + p_new * vrep[qrows, :].reshape(BB, QROWS, LANES)) / l
    o3[...] = o.reshape(BB * QROWS, LANES)

    shift = lax.rem(nb - row0, nb)
    k_cols = pltpu.roll(kvt[0], shift, axis=1)
    v_cols = pltpu.roll(kvt[1], shift, axis=1)
    k_shift = pltpu.roll(kt, WINDOW - 1, axis=2)
    v_shift = pltpu.roll(vt, WINDOW - 1, axis=2)
    last = lax.broadcasted_iota(jnp.int32, (LANES, WINDOW), 1) == WINDOW - 1
    for bb in range(BB):
        knew_ref[bb] = jnp.where(last, k_cols[:, bb:bb + 1], k_shift[bb])
        vnew_ref[bb] = jnp.where(last, v_cols[:, bb:bb + 1], v_shift[bb])

    for j in range(4):
        grp_lo = j < 2
        oh = [o3[pl.ds(2 * j + half, BB, stride=QROWS), :] for half in range(2)]
        a = oh[0] if grp_lo else pltpu.roll(oh[0], HEAD_DIM, axis=1)
        b = pltpu.roll(oh[1], HEAD_DIM, axis=1) if grp_lo else oh[1]
        ga = z_ref[rows, O_GA + j * LANES:O_GA + (j + 1) * LANES]
        mixed[rows, j * LANES:(j + 1) * LANES] = (jnp.where(lo, a, b) * _silu(ga)).astype(BF16)

    u = z_ref[rows, O_U:O_U + D_POOL]
    spnew_ref[0:POOL_STATE - 1] = sp_ref[1:POOL_STATE]
    spnew_ref[POOL_STATE - 1] = u
    for g, w in enumerate(POOL_WINDOWS):
        cols = slice(g * LANES, (g + 1) * LANES)
        ug = u[:, cols]
        win_sum = ug
        for r in range(POOL_STATE - (w - 1), POOL_STATE):
            win_sum = win_sum + sp_ref[r, :, cols]
        d = (win_sum * (1.0 / w) - ug).astype(BF16)
        po = jnp.dot(d, wpool_ref[g], preferred_element_type=F32) * pscale_ref[:, cols]
        gp = z_ref[rows, O_GP + g * LANES:O_GP + (g + 1) * LANES]
        mixed[rows, D_ATTN + g * LANES:D_ATTN + (g + 1) * LANES] = (po * _silu(gp)).astype(BF16)

    @pl.when(i == pl.num_programs(0) - 1)
    def _():
        y_ref[...] = x_ref[...] + jnp.dot(mixed[...], wout_ref[...], preferred_element_type=F32)


def _sample_call(sinks, x, nw, win, qw2, kw2, wpool, pscale, wout, ck, cv, sp):
    nb = x.shape[0]
    const2 = lambda i: (0, 0)
    chunk3 = lambda i: (i, 0, 0)
    mid3 = lambda i: (0, i, 0)
    return pl.pallas_call(
        _sample_kernel,
        grid=(nb // BB,),
        in_specs=[
            pl.BlockSpec(memory_space=pltpu.SMEM),
            pl.BlockSpec((nb, D_MODEL), const2),
            pl.BlockSpec((1, D_MODEL), const2),
            pl.BlockSpec((D_MODEL, D_IN_PROJ), const2),
            pl.BlockSpec((1, LANES), const2),
            pl.BlockSpec((1, LANES), const2),
            pl.BlockSpec((4, LANES, LANES), lambda i: (0, 0, 0)),
            pl.BlockSpec((1, D_POOL), const2),
            pl.BlockSpec((D_MODEL, D_MODEL), const2),
            pl.BlockSpec((BB, LANES, WINDOW), chunk3),
            pl.BlockSpec((BB, LANES, WINDOW), chunk3),
            pl.BlockSpec((POOL_STATE, BB, D_POOL), mid3),
        ],
        out_specs=[
            pl.BlockSpec((nb, D_MODEL), const2),
            pl.BlockSpec((BB, LANES, WINDOW), chunk3),
            pl.BlockSpec((BB, LANES, WINDOW), chunk3),
            pl.BlockSpec((POOL_STATE, BB, D_POOL), mid3),
        ],
        out_shape=[
            jax.ShapeDtypeStruct((nb, D_MODEL), F32),
            jax.ShapeDtypeStruct((nb, LANES, WINDOW), F32),
            jax.ShapeDtypeStruct((nb, LANES, WINDOW), F32),
            jax.ShapeDtypeStruct((POOL_STATE, nb, D_POOL), F32),
        ],
        scratch_shapes=[
            pltpu.VMEM((nb, D_IN_PROJ), F32),
            pltpu.VMEM((nb * QROWS, LANES), F32),
            pltpu.VMEM((nb * QROWS, LANES), F32),
            pltpu.VMEM((nb * QROWS, LANES), F32),
            pltpu.VMEM((nb * QROWS, LANES), F32),
            pltpu.VMEM((2, LANES, nb), F32),
            pltpu.VMEM((BB * QROWS, LANES), F32),
            pltpu.VMEM((nb, D_MODEL), BF16),
        ],
        compiler_params=pltpu.CompilerParams(dimension_semantics=("arbitrary",), vmem_limit_bytes=VMEM_LIMIT),
        name="sample_group",
    )(sinks, x, nw, win, qw2, kw2, wpool, pscale, wout, ck, cv, sp)


def _cache_in(c):
    nb = c.shape[0]
    return jnp.transpose(c, (0, 2, 3, 1)).reshape(nb, LANES, WINDOW)


def _cache_out(c):
    nb = c.shape[0]
    return jnp.transpose(c.reshape(nb, N_KV_HEADS, HEAD_DIM, WINDOW), (0, 3, 1, 2))


def kernel(x_prompt, x_sample, cache_k, cache_v, state_pool, norm_w, w_in, q_norm_w, k_norm_w, sinks, w_pool,
           pool_scale, w_out):
    depth = norm_w.shape[0]
    assert depth == 1 and x_prompt.shape[0] == 1 and x_sample.shape[1] == 1
    seq = x_prompt.shape[1]
    nb = x_sample.shape[0]
    assert seq % TB == 0 and nb % BB == 0 and nb == LANES

    win = w_in[0].astype(BF16)
    wout = w_out[0].astype(BF16)
    wpool = w_pool[0].astype(BF16)
    nw = norm_w
    qw2 = jnp.tile(q_norm_w * (HEAD_DIM ** -0.5), (1, 2))
    kw2 = jnp.tile(k_norm_w, (1, 2))
    sk = sinks[0]

    yp, kp, vp, up = _prompt_call(sk, x_prompt[0], nw, win, qw2, kw2, wpool, pool_scale, wout)
    ys, kq, vq, pq = _sample_call(sk, x_sample[:, 0], nw, win, qw2, kw2, wpool, pool_scale, wout,
                                  _cache_in(cache_k[0]), _cache_in(cache_v[0]),
                                  jnp.transpose(state_pool[0], (1, 0, 2)))
    return (yp[None], ys[:, None], _cache_out(kp[None])[None], _cache_out(vp[None])[None], up[None, None],
            _cache_out(kq)[None], _cache_out(vq)[None], jnp.transpose(pq, (1, 0, 2))[None])
```

```python
import jax
import jax.numpy as jnp
from jax import lax
from jax.experimental import pallas as pl
from jax.experimental.pallas import tpu as pltpu

D_MODEL = 1024
HEAD_DIM = 64
N_HEADS = 8
N_KV_HEADS = 2
GROUP = 4
WINDOW = 128
BLOCK = 128
POOL_WINDOWS = (2, 4, 8, 16)
POOL_STATE = 15
D_ATTN = 512
D_POOL = 512
D_IN_PROJ = 2304
EPS = 1e-6
NEG_INF = -1e30

O_K = 512
O_V = 640
O_GA = 768
O_U = 1280
O_GP = 1792

LANES = 128
TB = 512
NSB = TB // BLOCK
BB = 32
QROWS = 16
VMEM_LIMIT = 56 * 1024 * 1024

F32 = jnp.float32
BF16 = jnp.bfloat16
_NT = (((1,), (1,)), ((), ()))


def _silu(g):
    return g * jax.nn.sigmoid(g)


def _lo_mask():
    return lax.broadcasted_iota(jnp.int32, (1, LANES), 1) < HEAD_DIM


def _pair_rms(zs, lo, w2):
    sq = zs * zs
    s_lo = jnp.sum(jnp.where(lo, sq, 0.0), axis=-1, keepdims=True)
    s_hi = jnp.sum(jnp.where(lo, 0.0, sq), axis=-1, keepdims=True)
    r = lax.rsqrt(jnp.where(lo, s_lo, s_hi) * (1.0 / HEAD_DIM) + EPS)
    return zs * r * w2


def _rms_bf16(x, nw):
    ms = jnp.mean(x * x, axis=-1, keepdims=True)
    return (x * lax.rsqrt(ms + EPS) * nw).astype(BF16)


def _project(x, nw, win):
    return jnp.dot(_rms_bf16(x, nw), win, preferred_element_type=F32)


def _load_weights(win_ref, wout_ref, wpool_ref, qn_ref, kn_ref, win_bf, wout_bf, wpool_bf, qw_ref, kw_ref):
    win_bf[...] = win_ref[...].astype(BF16)
    wout_bf[...] = wout_ref[...].astype(BF16)
    wpool_bf[...] = wpool_ref[...].astype(BF16)
    qn = qn_ref[...] * (HEAD_DIM ** -0.5)
    qw_ref[...] = jnp.concatenate([qn, qn], axis=1)
    kw_ref[...] = jnp.concatenate([kn_ref[...], kn_ref[...]], axis=1)


def _prompt_kernel(sinks_ref, x_ref, nw_ref, win_f32, qn_ref, kn_ref, wpool_f32, pscale_ref, wout_f32,
                   y_ref, knew_ref, vnew_ref, unew_ref,
                   win_ref, wout_ref, wpool_ref, qw_ref, kw_ref, z_ref, qbuf, kbuf, vbuf, uext, mixed):
    i = pl.program_id(0)
    lo = _lo_mask()

    @pl.when(i == 0)
    def _():
        _load_weights(win_f32, wout_f32, wpool_f32, qn_ref, kn_ref, win_ref, wout_ref, wpool_ref, qw_ref, kw_ref)
        kbuf[:, 0:BLOCK, :] = jnp.zeros((2, BLOCK, LANES), BF16)
        vbuf[:, 0:BLOCK, :] = jnp.zeros((2, BLOCK, LANES), BF16)
        uext[0:16, :] = jnp.zeros((16, D_POOL), F32)

    z_ref[...] = _project(x_ref[...], nw_ref[...], win_ref[...])

    khat = _pair_rms(z_ref[:, O_K:O_K + LANES], lo, kw_ref[...])
    kr = pltpu.roll(khat, HEAD_DIM, axis=1)
    kbuf[0, BLOCK:, :] = jnp.where(lo, khat, kr).astype(BF16)
    kbuf[1, BLOCK:, :] = jnp.where(lo, kr, khat).astype(BF16)
    vz = z_ref[:, O_V:O_V + LANES]
    vr = pltpu.roll(vz, HEAD_DIM, axis=1)
    vbuf[0, BLOCK:, :] = jnp.where(lo, vz, vr).astype(BF16)
    vbuf[1, BLOCK:, :] = jnp.where(lo, vr, vz).astype(BF16)

    for j in range(4):
        qhat = _pair_rms(z_ref[:, j * LANES:(j + 1) * LANES], lo, qw_ref[...])
        q_lo = jnp.where(lo, qhat, 0.0).astype(BF16)
        q_hi = jnp.where(lo, 0.0, qhat).astype(BF16)
        c, g0 = j // 2, 2 * (j % 2)
        for sb in range(NSB):
            rows = slice(sb * BLOCK, (sb + 1) * BLOCK)
            qbuf[c, sb, g0 * BLOCK:(g0 + 1) * BLOCK, :] = q_lo[rows]
            qbuf[c, sb, (g0 + 1) * BLOCK:(g0 + 2) * BLOCK, :] = q_hi[rows]

    r_io = lax.broadcasted_iota(jnp.int32, (BLOCK, 2 * BLOCK), 0)
    c_io = lax.broadcasted_iota(jnp.int32, (BLOCK, 2 * BLOCK), 1)
    band = (c_io >= r_io) & (c_io <= r_io + WINDOW)
    first_lo = jnp.where(i > 0, 0, BLOCK)
    band_first = band & (c_io >= first_lo)

    for sb in range(NSB):
        valid = band_first if sb == 0 else band
        rows = slice(sb * BLOCK, (sb + 1) * BLOCK)
        for c in range(2):
            keys = kbuf[c, sb * BLOCK:(sb + 2) * BLOCK, :]
            vals = vbuf[c, sb * BLOCK:(sb + 2) * BLOCK, :]
            s = lax.dot_general(qbuf[c, sb], keys, _NT, preferred_element_type=F32)
            ps, inv = [], []
            for g in range(GROUP):
                sg = jnp.where(valid, s[g * BLOCK:(g + 1) * BLOCK, :], NEG_INF)
                sink = sinks_ref[c * GROUP + g]
                m = jnp.maximum(jnp.max(sg, axis=-1, keepdims=True), sink)
                p = jnp.exp(sg - m)
                l = jnp.sum(p, axis=-1, keepdims=True) + jnp.exp(sink - m)
                ps.append(p.astype(BF16))
                inv.append(1.0 / l)
            o = jnp.dot(jnp.concatenate(ps, axis=0), vals, preferred_element_type=F32)
            og = [o[g * BLOCK:(g + 1) * BLOCK, :] * inv[g] for g in range(GROUP)]
            for jj in range(2):
                slab = jnp.where(lo, og[2 * jj], og[2 * jj + 1])
                col = (2 * c + jj) * LANES
                ga = z_ref[rows, O_GA + col:O_GA + col + LANES]
                mixed[rows, col:col + LANES] = (slab * _silu(ga)).astype(BF16)

    uext[16:, :] = z_ref[:, O_U:O_U + D_POOL]
    unew_ref[...] = z_ref[TB - POOL_STATE:TB, O_U:O_U + D_POOL]
    pos16 = i * TB + lax.broadcasted_iota(jnp.int32, (16, LANES), 0)
    for g, w in enumerate(POOL_WINDOWS):
        cols = slice(g * LANES, (g + 1) * LANES)
        acc = uext[:, cols]
        sh = 1
        while sh < w:
            acc = acc + pltpu.roll(acc, sh, axis=0)
            sh *= 2
        ug = z_ref[:, O_U + g * LANES:O_U + (g + 1) * LANES]
        cnt = jnp.minimum(w, pos16 + 1).astype(F32)
        d_first = acc[16:32, :] / cnt - ug[0:16, :]
        d_rest = acc[32:, :] * (1.0 / w) - ug[16:, :]
        d = jnp.concatenate([d_first, d_rest], axis=0).astype(BF16)
        po = jnp.dot(d, wpool_ref[g], preferred_element_type=F32) * pscale_ref[:, cols]
        gp = z_ref[:, O_GP + g * LANES:O_GP + (g + 1) * LANES]
        mixed[:, D_ATTN + g * LANES:D_ATTN + (g + 1) * LANES] = (po * _silu(gp)).astype(BF16)

    kbuf[:, 0:BLOCK, :] = kbuf[:, TB:TB + BLOCK, :]
    vbuf[:, 0:BLOCK, :] = vbuf[:, TB:TB + BLOCK, :]
    uext[0:16, :] = uext[TB:TB + 16, :]

    y_ref[...] = x_ref[...] + jnp.dot(mixed[...], wout_ref[...], preferred_element_type=F32)

    @pl.when(i == pl.num_programs(0) - 1)
    def _():
        tail = slice(TB - WINDOW, TB)
        knew_ref[...] = _pair_rms(z_ref[tail, O_K:O_K + LANES], lo, kw_ref[...]).T
        vnew_ref[...] = z_ref[tail, O_V:O_V + LANES].T


def _weight_specs():
    const2 = lambda i: (0, 0)
    once = pl.Buffered(1)
    return dict(
        win=pl.BlockSpec((D_MODEL, D_IN_PROJ), const2, pipeline_mode=once),
        wout=pl.BlockSpec((D_MODEL, D_MODEL), const2, pipeline_mode=once),
        wpool=pl.BlockSpec((4, LANES, LANES), lambda i: (0, 0, 0), pipeline_mode=once),
        head=pl.BlockSpec((1, HEAD_DIM), const2),
    )


_WEIGHT_SCRATCH = [
    pltpu.VMEM((D_MODEL, D_IN_PROJ), BF16),
    pltpu.VMEM((D_MODEL, D_MODEL), BF16),
    pltpu.VMEM((4, LANES, LANES), BF16),
    pltpu.VMEM((1, LANES), F32),
    pltpu.VMEM((1, LANES), F32),
]


def _prompt_call(sinks, x, nw, win, qn, kn, wpool, pscale, wout):
    seq = x.shape[0]
    const2 = lambda i: (0, 0)
    ws = _weight_specs()
    return pl.pallas_call(
        _prompt_kernel,
        grid=(seq // TB,),
        in_specs=[
            pl.BlockSpec(memory_space=pltpu.SMEM),
            pl.BlockSpec((TB, D_MODEL), lambda i: (i, 0)),
            pl.BlockSpec((1, D_MODEL), const2),
            ws["win"],
            ws["head"],
            ws["head"],
            ws["wpool"],
            pl.BlockSpec((1, D_POOL), const2),
            ws["wout"],
        ],
        out_specs=[
            pl.BlockSpec((TB, D_MODEL), lambda i: (i, 0)),
            pl.BlockSpec((LANES, WINDOW), const2),
            pl.BlockSpec((LANES, WINDOW), const2),
            pl.BlockSpec((POOL_STATE, D_POOL), const2),
        ],
        out_shape=[
            jax.ShapeDtypeStruct((seq, D_MODEL), F32),
            jax.ShapeDtypeStruct((LANES, WINDOW), F32),
            jax.ShapeDtypeStruct((LANES, WINDOW), F32),
            jax.ShapeDtypeStruct((POOL_STATE, D_POOL), F32),
        ],
        scratch_shapes=_WEIGHT_SCRATCH + [
            pltpu.VMEM((TB, D_IN_PROJ), F32),
            pltpu.VMEM((2, NSB, GROUP * BLOCK, LANES), BF16),
            pltpu.VMEM((2, TB + BLOCK, LANES), BF16),
            pltpu.VMEM((2, TB + BLOCK, LANES), BF16),
            pltpu.VMEM((TB + 16, D_POOL), F32),
            pltpu.VMEM((TB, D_MODEL), BF16),
        ],
        compiler_params=pltpu.CompilerParams(dimension_semantics=("arbitrary",), vmem_limit_bytes=VMEM_LIMIT),
        name="prompt_group",
    )(sinks, x, nw, win, qn, kn, wpool, pscale, wout)


def _sample_kernel(sinks_ref, x_ref, nw_ref, win_f32, qn_ref, kn_ref, wpool_f32, pscale_ref, wout_f32,
                   ck_ref, cv_ref, sp_ref,
                   y_ref, knew_ref, vnew_ref, spnew_ref,
                   win_ref, wout_ref, wpool_ref, qw_ref, kw_ref, z_ref, q3, krep, vrep, snew, kvt, o3, mixed):
    i = pl.program_id(0)
    nb = x_ref.shape[0]
    lo = _lo_mask()

    @pl.when(i == 0)
    def _():
        _load_weights(win_f32, wout_f32, wpool_f32, qn_ref, kn_ref, win_ref, wout_ref, wpool_ref, qw_ref, kw_ref)
        z_ref[...] = _project(x_ref[...], nw_ref[...], win_ref[...])
        khat = _pair_rms(z_ref[:, O_K:O_K + LANES], lo, kw_ref[...])
        vnew = z_ref[:, O_V:O_V + LANES]
        kvt[0] = khat.T
        kvt[1] = vnew.T
        q3[...] = jnp.zeros(q3.shape, F32)
        krep[...] = jnp.zeros(krep.shape, F32)
        vrep[...] = jnp.zeros(vrep.shape, F32)
        for j in range(4):
            qhat = _pair_rms(z_ref[:, j * LANES:(j + 1) * LANES], lo, qw_ref[...])
            qrot = pltpu.roll(qhat, HEAD_DIM, axis=1)
            grp_lo = j < 2
            for half in range(2):
                h = 2 * j + half
                src = qhat if (half == 0) == grp_lo else qrot
                q3[pl.ds(h, nb, stride=QROWS), :] = jnp.where(lo if grp_lo else ~lo, src, 0.0)
                krep[pl.ds(h, nb, stride=QROWS), :] = khat
                vrep[pl.ds(h, nb, stride=QROWS), :] = vnew
        s_new = jnp.sum(q3[...] * krep[...], axis=-1, keepdims=True)
        snew[...] = jnp.broadcast_to(s_new, snew.shape)

    row0 = pl.multiple_of(i * BB, BB)
    rows = pl.ds(row0, BB)
    qrows = pl.ds(pl.multiple_of(i * (BB * QROWS), BB * QROWS), BB * QROWS)
    rid = lax.broadcasted_iota(jnp.int32, (1, QROWS, 1), 1)
    sink3 = jnp.zeros((1, QROWS, 1), F32)
    for h in range(N_HEADS):
        sink3 = jnp.where(rid == h, sinks_ref[h], sink3)

    q = q3[qrows, :].reshape(BB, QROWS, LANES)
    s_new = snew[qrows, :].reshape(BB, QROWS, LANES)
    kt = ck_ref[...]
    vt = cv_ref[...]
    s = lax.dot_general(q.astype(BF16), kt.astype(BF16), (((2,), (1,)), ((0,), (0,))), preferred_element_type=F32)
    m = jnp.maximum(jnp.maximum(jnp.max(s, axis=-1, keepdims=True), s_new), sink3)
    p = jnp.exp(s - m)
    p_new = jnp.exp(s_new - m)
    l = jnp.sum(p, axis=-1, keepdims=True) + p_new + jnp.exp(sink3 - m)
    o = lax.dot_general(p.astype(BF16), vt.astype(BF16), (((2,), (2,)), ((0,), (0,))), preferred_element_type=F32)
    o = (o + p_new * vrep[qrows, :].reshape(BB, QROWS, LANES)) / l
    o3[...] = o.reshape(BB * QROWS, LANES)

    shift = lax.rem(nb - row0, nb)
    k_cols = pltpu.roll(kvt[0], shift, axis=1)
    v_cols = pltpu.roll(kvt[1], shift, axis=1)
    k_shift = pltpu.roll(kt, WINDOW - 1, axis=2)
    v_shift = pltpu.roll(vt, WINDOW - 1, axis=2)
    last = lax.broadcasted_iota(jnp.int32, (LANES, WINDOW), 1) == WINDOW - 1
    for bb in range(BB):
        knew_ref[bb] = jnp.where(last, k_cols[:, bb:bb + 1], k_shift[bb])
        vnew_ref[bb] = jnp.where(last, v_cols[:, bb:bb + 1], v_shift[bb])

    for j in range(4):
        grp_lo = j < 2
        oh = [o3[pl.ds(2 * j + half, BB, stride=QROWS), :] for half in range(2)]
        a = oh[0] if grp_lo else pltpu.roll(oh[0], HEAD_DIM, axis=1)
        b = pltpu.roll(oh[1], HEAD_DIM, axis=1) if grp_lo else oh[1]
        ga = z_ref[rows, O_GA + j * LANES:O_GA + (j + 1) * LANES]
        mixed[rows, j * LANES:(j + 1) * LANES] = (jnp.where(lo, a, b) * _silu(ga)).astype(BF16)

    u = z_ref[rows, O_U:O_U + D_POOL]
    spnew_ref[0:POOL_STATE - 1] = sp_ref[1:POOL_STATE]
    spnew_ref[POOL_STATE - 1] = u
    for g, w in enumerate(POOL_WINDOWS):
        cols = slice(g * LANES, (g + 1) * LANES)
        ug = u[:, cols]
        win_sum = ug
        for r in range(POOL_STATE - (w - 1), POOL_STATE):
            win_sum = win_sum + sp_ref[r, :, cols]
        d = (win_sum * (1.0 / w) - ug).astype(BF16)
        po = jnp.dot(d, wpool_ref[g], preferred_element_type=F32) * pscale_ref[:, cols]
        gp = z_ref[rows, O_GP + g * LANES:O_GP + (g + 1) * LANES]
        mixed[rows, D_ATTN + g * LANES:D_ATTN + (g + 1) * LANES] = (po * _silu(gp)).astype(BF16)

    @pl.when(i == pl.num_programs(0) - 1)
    def _():
        y_ref[...] = x_ref[...] + jnp.dot(mixed[...], wout_ref[...], preferred_element_type=F32)


def _sample_call(sinks, x, nw, win, qn, kn, wpool, pscale, wout, ck, cv, sp):
    nb = x.shape[0]
    const2 = lambda i: (0, 0)
    chunk3 = lambda i: (i, 0, 0)
    mid3 = lambda i: (0, i, 0)
    ws = _weight_specs()
    return pl.pallas_call(
        _sample_kernel,
        grid=(nb // BB,),
        in_specs=[
            pl.BlockSpec(memory_space=pltpu.SMEM),
            pl.BlockSpec((nb, D_MODEL), const2),
            pl.BlockSpec((1, D_MODEL), const2),
            ws["win"],
            ws["head"],
            ws["head"],
            ws["wpool"],
            pl.BlockSpec((1, D_POOL), const2),
            ws["wout"],
            pl.BlockSpec((BB, LANES, WINDOW), chunk3),
            pl.BlockSpec((BB, LANES, WINDOW), chunk3),
            pl.BlockSpec((POOL_STATE, BB, D_POOL), mid3),
        ],
        out_specs=[
            pl.BlockSpec((nb, D_MODEL), const2),
            pl.BlockSpec((BB, LANES, WINDOW), chunk3),
            pl.BlockSpec((BB, LANES, WINDOW), chunk3),
            pl.BlockSpec((POOL_STATE, BB, D_POOL), mid3),
        ],
        out_shape=[
            jax.ShapeDtypeStruct((nb, D_MODEL), F32),
            jax.ShapeDtypeStruct((nb, LANES, WINDOW), F32),
            jax.ShapeDtypeStruct((nb, LANES, WINDOW), F32),
            jax.ShapeDtypeStruct((POOL_STATE, nb, D_POOL), F32),
        ],
        scratch_shapes=_WEIGHT_SCRATCH + [
            pltpu.VMEM((nb, D_IN_PROJ), F32),
            pltpu.VMEM((nb * QROWS, LANES), F32),
            pltpu.VMEM((nb * QROWS, LANES), F32),
            pltpu.VMEM((nb * QROWS, LANES), F32),
            pltpu.VMEM((nb * QROWS, LANES), F32),
            pltpu.VMEM((2, LANES, nb), F32),
            pltpu.VMEM((BB * QROWS, LANES), F32),
            pltpu.VMEM((nb, D_MODEL), BF16),
        ],
        compiler_params=pltpu.CompilerParams(dimension_semantics=("arbitrary",), vmem_limit_bytes=VMEM_LIMIT),
        name="sample_group",
    )(sinks, x, nw, win, qn, kn, wpool, pscale, wout, ck, cv, sp)


def _cache_in(c):
    nb = c.shape[0]
    return jnp.transpose(c, (0, 2, 3, 1)).reshape(nb, LANES, WINDOW)


def _cache_out(c):
    nb = c.shape[0]
    return jnp.transpose(c.reshape(nb, N_KV_HEADS, HEAD_DIM, WINDOW), (0, 3, 1, 2))


def kernel(x_prompt, x_sample, cache_k, cache_v, state_pool, norm_w, w_in, q_norm_w, k_norm_w, sinks, w_pool,
           pool_scale, w_out):
    depth = norm_w.shape[0]
    assert depth == 1 and x_prompt.shape[0] == 1 and x_sample.shape[1] == 1
    seq = x_prompt.shape[1]
    nb = x_sample.shape[0]
    assert seq % TB == 0 and nb % BB == 0 and nb == LANES

    weights = (norm_w, w_in[0], q_norm_w, k_norm_w, w_pool[0], pool_scale, w_out[0])
    sk = sinks[0]

    yp, kp, vp, up = _prompt_call(sk, x_prompt[0], *weights)
    ys, kq, vq, pq = _sample_call(sk, x_sample[:, 0], *weights,
                                  _cache_in(cache_k[0]), _cache_in(cache_v[0]),
                                  jnp.transpose(state_pool[0], (1, 0, 2)))
    return (yp[None], ys[:, None], _cache_out(kp[None])[None], _cache_out(vp[None])[None], up[None, None],
            _cache_out(kq)[None], _cache_out(vq)[None], jnp.transpose(pq, (1, 0, 2))[None])
```

```python
import jax
import jax.numpy as jnp
from jax import lax
from jax.experimental import pallas as pl
from jax.experimental.pallas import tpu as pltpu

D_MODEL = 1024
HEAD_DIM = 64
N_HEADS = 8
N_KV_HEADS = 2
GROUP = 4
WINDOW = 128
BLOCK = 128
POOL_WINDOWS = (2, 4, 8, 16)
POOL_STATE = 15
D_ATTN = 512
D_POOL = 512
D_IN_PROJ = 2304
EPS = 1e-6
NEG_INF = -1e30
LOG2E = 1.4426950408889634

O_K = 512
O_V = 640
O_GA = 768
O_U = 1280
O_GP = 1792

LANES = 128
TB = 512
NSB = TB // BLOCK
BB = 32
QROWS = 16
VMEM_LIMIT = 56 * 1024 * 1024

F32 = jnp.float32
BF16 = jnp.bfloat16
_NT = (((1,), (1,)), ((), ()))


def _silu(g):
    h = 0.5 * g
    return h + h * jnp.tanh(h)


def _lo_mask():
    return lax.broadcasted_iota(jnp.int32, (1, LANES), 1) < HEAD_DIM


def _pair_rms(zs, lo, w2):
    sq = zs * zs
    s_lo = jnp.sum(jnp.where(lo, sq, 0.0), axis=-1, keepdims=True)
    s_hi = jnp.sum(jnp.where(lo, 0.0, sq), axis=-1, keepdims=True)
    r = lax.rsqrt(jnp.where(lo, s_lo, s_hi) * (1.0 / HEAD_DIM) + EPS)
    return zs * r * w2


def _rms_bf16(x, nw):
    ms = jnp.mean(x * x, axis=-1, keepdims=True)
    return (x * lax.rsqrt(ms + EPS) * nw).astype(BF16)


def _project(x, nw, win):
    return jnp.dot(_rms_bf16(x, nw), win, preferred_element_type=F32)


def _load_weights(win_ref, wout_ref, wpool_ref, qn_ref, kn_ref, win_bf, wout_bf, wpool_bf, qw_ref, kw_ref):
    win_bf[...] = win_ref[...].astype(BF16)
    wout_bf[...] = wout_ref[...].astype(BF16)
    wpool_bf[...] = wpool_ref[...].astype(BF16)
    qn = qn_ref[...] * (HEAD_DIM ** -0.5 * LOG2E)
    qw_ref[...] = jnp.concatenate([qn, qn], axis=1)
    kw_ref[...] = jnp.concatenate([kn_ref[...], kn_ref[...]], axis=1)


def _prompt_kernel(sinks_ref, x_ref, nw_ref, win_f32, qn_ref, kn_ref, wpool_f32, pscale_ref, wout_f32,
                   y_ref, knew_ref, vnew_ref, unew_ref,
                   win_ref, wout_ref, wpool_ref, qw_ref, kw_ref, z_ref, qbuf, kbuf, vbuf, uext, mixed, fill_ref):
    i = pl.program_id(0)
    lo = _lo_mask()

    @pl.when(i == 0)
    def _():
        _load_weights(win_f32, wout_f32, wpool_f32, qn_ref, kn_ref, win_ref, wout_ref, wpool_ref, qw_ref, kw_ref)
        kbuf[:, 0:BLOCK, :] = jnp.zeros((2, BLOCK, LANES), BF16)
        vbuf[:, 0:BLOCK, :] = jnp.zeros((2, BLOCK, LANES), BF16)
        uext[0:16, :] = jnp.zeros((16, D_POOL), F32)
        r0 = lax.broadcasted_iota(jnp.int32, (BLOCK, 2 * BLOCK), 0)
        c0 = lax.broadcasted_iota(jnp.int32, (BLOCK, 2 * BLOCK), 1)
        sink_col = jnp.where(r0 >= 1, r0 - 1, 2 * BLOCK - 1)
        for h in range(N_HEADS):
            fill_ref[h] = jnp.where(c0 == sink_col, sinks_ref[h] * LOG2E, NEG_INF)

    z_ref[...] = _project(x_ref[...], nw_ref[...], win_ref[...])

    khat = _pair_rms(z_ref[:, O_K:O_K + LANES], lo, kw_ref[...])
    kr = pltpu.roll(khat, HEAD_DIM, axis=1)
    kbuf[0, BLOCK:, :] = jnp.where(lo, khat, kr).astype(BF16)
    kbuf[1, BLOCK:, :] = jnp.where(lo, kr, khat).astype(BF16)
    vz = z_ref[:, O_V:O_V + LANES]
    vr = pltpu.roll(vz, HEAD_DIM, axis=1)
    vbuf[0, BLOCK:, :] = jnp.where(lo, vz, vr).astype(BF16)
    vbuf[1, BLOCK:, :] = jnp.where(lo, vr, vz).astype(BF16)

    for j in range(4):
        qhat = _pair_rms(z_ref[:, j * LANES:(j + 1) * LANES], lo, qw_ref[...])
        q_lo = jnp.where(lo, qhat, 0.0).astype(BF16)
        q_hi = jnp.where(lo, 0.0, qhat).astype(BF16)
        c, g0 = j // 2, 2 * (j % 2)
        for sb in range(NSB):
            rows = slice(sb * BLOCK, (sb + 1) * BLOCK)
            qbuf[c, sb, g0 * BLOCK:(g0 + 1) * BLOCK, :] = q_lo[rows]
            qbuf[c, sb, (g0 + 1) * BLOCK:(g0 + 2) * BLOCK, :] = q_hi[rows]

    r_io = lax.broadcasted_iota(jnp.int32, (BLOCK, 2 * BLOCK), 0)
    c_io = lax.broadcasted_iota(jnp.int32, (BLOCK, 2 * BLOCK), 1)
    band = (c_io >= r_io) & (c_io <= r_io + WINDOW)
    first_lo = jnp.where(i > 0, 0, BLOCK)
    band_first = band & (c_io >= first_lo)
    sink_pos = c_io == jnp.where(r_io >= 1, r_io - 1, 2 * BLOCK - 1)

    for sb in range(NSB):
        valid = band_first if sb == 0 else band
        rows = slice(sb * BLOCK, (sb + 1) * BLOCK)
        for c in range(2):
            keys = kbuf[c, sb * BLOCK:(sb + 2) * BLOCK, :]
            vals = vbuf[c, sb * BLOCK:(sb + 2) * BLOCK, :]
            s = lax.dot_general(qbuf[c, sb], keys, _NT, preferred_element_type=F32)
            ps, inv = [], []
            for g in range(GROUP):
                sg = jnp.where(valid, s[g * BLOCK:(g + 1) * BLOCK, :], fill_ref[c * GROUP + g])
                m = jnp.max(sg, axis=-1, keepdims=True)
                p = jnp.exp2(sg - m)
                l = jnp.sum(p, axis=-1, keepdims=True)
                ps.append(jnp.where(sink_pos, 0.0, p).astype(BF16))
                inv.append(1.0 / l)
            o = jnp.dot(jnp.concatenate(ps, axis=0), vals, preferred_element_type=F32)
            og = [o[g * BLOCK:(g + 1) * BLOCK, :] * inv[g] for g in range(GROUP)]
            for jj in range(2):
                slab = jnp.where(lo, og[2 * jj], og[2 * jj + 1])
                col = (2 * c + jj) * LANES
                ga = z_ref[rows, O_GA + col:O_GA + col + LANES]
                mixed[rows, col:col + LANES] = (slab * _silu(ga)).astype(BF16)

    uext[16:, :] = z_ref[:, O_U:O_U + D_POOL]
    unew_ref[...] = z_ref[TB - POOL_STATE:TB, O_U:O_U + D_POOL]
    pos16 = i * TB + lax.broadcasted_iota(jnp.int32, (16, LANES), 0)
    for g, w in enumerate(POOL_WINDOWS):
        cols = slice(g * LANES, (g + 1) * LANES)
        acc = uext[:, cols]
        sh = 1
        while sh < w:
            acc = acc + pltpu.roll(acc, sh, axis=0)
            sh *= 2
        ug = z_ref[:, O_U + g * LANES:O_U + (g + 1) * LANES]
        cnt = jnp.minimum(w, pos16 + 1).astype(F32)
        d_first = acc[16:32, :] / cnt - ug[0:16, :]
        d_rest = acc[32:, :] * (1.0 / w) - ug[16:, :]
        d = jnp.concatenate([d_first, d_rest], axis=0).astype(BF16)
        po = jnp.dot(d, wpool_ref[g], preferred_element_type=F32) * pscale_ref[:, cols]
        gp = z_ref[:, O_GP + g * LANES:O_GP + (g + 1) * LANES]
        mixed[:, D_ATTN + g * LANES:D_ATTN + (g + 1) * LANES] = (po * _silu(gp)).astype(BF16)

    kbuf[:, 0:BLOCK, :] = kbuf[:, TB:TB + BLOCK, :]
    vbuf[:, 0:BLOCK, :] = vbuf[:, TB:TB + BLOCK, :]
    uext[0:16, :] = uext[TB:TB + 16, :]

    y_ref[...] = x_ref[...] + jnp.dot(mixed[...], wout_ref[...], preferred_element_type=F32)

    @pl.when(i == pl.num_programs(0) - 1)
    def _():
        tail = slice(TB - WINDOW, TB)
        knew_ref[...] = _pair_rms(z_ref[tail, O_K:O_K + LANES], lo, kw_ref[...]).T
        vnew_ref[...] = z_ref[tail, O_V:O_V + LANES].T


def _weight_specs():
    const2 = lambda i: (0, 0)
    once = pl.Buffered(1)
    return dict(
        win=pl.BlockSpec((D_MODEL, D_IN_PROJ), const2, pipeline_mode=once),
        wout=pl.BlockSpec((D_MODEL, D_MODEL), const2, pipeline_mode=once),
        wpool=pl.BlockSpec((4, LANES, LANES), lambda i: (0, 0, 0), pipeline_mode=once),
        head=pl.BlockSpec((1, HEAD_DIM), const2),
    )


_WEIGHT_SCRATCH = [
    pltpu.VMEM((D_MODEL, D_IN_PROJ), BF16),
    pltpu.VMEM((D_MODEL, D_MODEL), BF16),
    pltpu.VMEM((4, LANES, LANES), BF16),
    pltpu.VMEM((1, LANES), F32),
    pltpu.VMEM((1, LANES), F32),
]


def _prompt_call(sinks, x, nw, win, qn, kn, wpool, pscale, wout):
    seq = x.shape[0]
    const2 = lambda i: (0, 0)
    ws = _weight_specs()
    return pl.pallas_call(
        _prompt_kernel,
        grid=(seq // TB,),
        in_specs=[
            pl.BlockSpec(memory_space=pltpu.SMEM),
            pl.BlockSpec((TB, D_MODEL), lambda i: (i, 0)),
            pl.BlockSpec((1, D_MODEL), const2),
            ws["win"],
            ws["head"],
            ws["head"],
            ws["wpool"],
            pl.BlockSpec((1, D_POOL), const2),
            ws["wout"],
        ],
        out_specs=[
            pl.BlockSpec((TB, D_MODEL), lambda i: (i, 0)),
            pl.BlockSpec((LANES, WINDOW), const2),
            pl.BlockSpec((LANES, WINDOW), const2),
            pl.BlockSpec((POOL_STATE, D_POOL), const2),
        ],
        out_shape=[
            jax.ShapeDtypeStruct((seq, D_MODEL), F32),
            jax.ShapeDtypeStruct((LANES, WINDOW), F32),
            jax.ShapeDtypeStruct((LANES, WINDOW), F32),
            jax.ShapeDtypeStruct((POOL_STATE, D_POOL), F32),
        ],
        scratch_shapes=_WEIGHT_SCRATCH + [
            pltpu.VMEM((TB, D_IN_PROJ), F32),
            pltpu.VMEM((2, NSB, GROUP * BLOCK, LANES), BF16),
            pltpu.VMEM((2, TB + BLOCK, LANES), BF16),
            pltpu.VMEM((2, TB + BLOCK, LANES), BF16),
            pltpu.VMEM((TB + 16, D_POOL), F32),
            pltpu.VMEM((TB, D_MODEL), BF16),
            pltpu.VMEM((N_HEADS, BLOCK, 2 * BLOCK), F32),
        ],
        compiler_params=pltpu.CompilerParams(dimension_semantics=("arbitrary",), vmem_limit_bytes=VMEM_LIMIT),
        name="prompt_group",
    )(sinks, x, nw, win, qn, kn, wpool, pscale, wout)


def _sample_kernel(sinks_ref, x_ref, nw_ref, win_f32, qn_ref, kn_ref, wpool_f32, pscale_ref, wout_f32,
                   ck_ref, cv_ref, sp_ref,
                   y_ref, knew_ref, vnew_ref, spnew_ref,
                   win_ref, wout_ref, wpool_ref, qw_ref, kw_ref, z_ref, q3, krep, vrep, snew, kvt, o3, mixed):
    i = pl.program_id(0)
    nb = x_ref.shape[0]
    lo = _lo_mask()

    @pl.when(i == 0)
    def _():
        _load_weights(win_f32, wout_f32, wpool_f32, qn_ref, kn_ref, win_ref, wout_ref, wpool_ref, qw_ref, kw_ref)
        z_ref[...] = _project(x_ref[...], nw_ref[...], win_ref[...])
        khat = _pair_rms(z_ref[:, O_K:O_K + LANES], lo, kw_ref[...])
        vnew = z_ref[:, O_V:O_V + LANES]
        kvt[0] = khat.T
        kvt[1] = vnew.T
        q3[...] = jnp.zeros(q3.shape, F32)
        krep[...] = jnp.zeros(krep.shape, F32)
        vrep[...] = jnp.zeros(vrep.shape, F32)
        for j in range(4):
            qhat = _pair_rms(z_ref[:, j * LANES:(j + 1) * LANES], lo, qw_ref[...])
            qrot = pltpu.roll(qhat, HEAD_DIM, axis=1)
            grp_lo = j < 2
            for half in range(2):
                h = 2 * j + half
                src = qhat if (half == 0) == grp_lo else qrot
                q3[pl.ds(h, nb, stride=QROWS), :] = jnp.where(lo if grp_lo else ~lo, src, 0.0)
                krep[pl.ds(h, nb, stride=QROWS), :] = khat
                vrep[pl.ds(h, nb, stride=QROWS), :] = vnew
        s_new = jnp.sum(q3[...] * krep[...], axis=-1, keepdims=True)
        snew[...] = jnp.broadcast_to(s_new, snew.shape)

    row0 = pl.multiple_of(i * BB, BB)
    rows = pl.ds(row0, BB)
    qrows = pl.ds(pl.multiple_of(i * (BB * QROWS), BB * QROWS), BB * QROWS)
    rid = lax.broadcasted_iota(jnp.int32, (1, QROWS, 1), 1)
    sink3 = jnp.zeros((1, QROWS, 1), F32)
    for h in range(N_HEADS):
        sink3 = jnp.where(rid == h, sinks_ref[h] * LOG2E, sink3)

    q = q3[qrows, :].reshape(BB, QROWS, LANES)
    s_new = snew[qrows, :].reshape(BB, QROWS, LANES)
    kt = ck_ref[...]
    vt = cv_ref[...]
    s = lax.dot_general(q.astype(BF16), kt.astype(BF16), (((2,), (1,)), ((0,), (0,))), preferred_element_type=F32)
    m = jnp.maximum(jnp.maximum(jnp.max(s, axis=-1, keepdims=True), s_new), sink3)
    p = jnp.exp2(s - m)
    p_new = jnp.exp2(s_new - m)
    l = jnp.sum(p, axis=-1, keepdims=True) + p_new + jnp.exp2(sink3 - m)
    o = lax.dot_general(p.astype(BF16), vt.astype(BF16), (((2,), (2,)), ((0,), (0,))), preferred_element_type=F32)
    o = (o + p_new * vrep[qrows, :].reshape(BB, QROWS, LANES)) / l
    o3[...] = o.reshape(BB * QROWS, LANES)

    shift = lax.rem(nb - row0, nb)
    k_cols = pltpu.roll(kvt[0], shift, axis=1)
    v_cols = pltpu.roll(kvt[1], shift, axis=1)
    k_shift = pltpu.roll(kt, WINDOW - 1, axis=2)
    v_shift = pltpu.roll(vt, WINDOW - 1, axis=2)
    last = lax.broadcasted_iota(jnp.int32, (LANES, WINDOW), 1) == WINDOW - 1
    for bb in range(BB):
        knew_ref[bb] = jnp.where(last, k_cols[:, bb:bb + 1], k_shift[bb])
        vnew_ref[bb] = jnp.where(last, v_cols[:, bb:bb + 1], v_shift[bb])

    for j in range(4):
        grp_lo = j < 2
        oh = [o3[pl.ds(2 * j + half, BB, stride=QROWS), :] for half in range(2)]
        a = oh[0] if grp_lo else pltpu.roll(oh[0], HEAD_DIM, axis=1)
        b = pltpu.roll(oh[1], HEAD_DIM, axis=1) if grp_lo else oh[1]
        ga = z_ref[rows, O_GA + j * LANES:O_GA + (j + 1) * LANES]
        mixed[rows, j * LANES:(j + 1) * LANES] = (jnp.where(lo, a, b) * _silu(ga)).astype(BF16)

    u = z_ref[rows, O_U:O_U + D_POOL]
    spnew_ref[0:POOL_STATE - 1] = sp_ref[1:POOL_STATE]
    spnew_ref[POOL_STATE - 1] = u
    for g, w in enumerate(POOL_WINDOWS):
        cols = slice(g * LANES, (g + 1) * LANES)
        ug = u[:, cols]
        win_sum = ug
        for r in range(POOL_STATE - (w - 1), POOL_STATE):
            win_sum = win_sum + sp_ref[r, :, cols]
        d = (win_sum * (1.0 / w) - ug).astype(BF16)
        po = jnp.dot(d, wpool_ref[g], preferred_element_type=F32) * pscale_ref[:, cols]
        gp = z_ref[rows, O_GP + g * LANES:O_GP + (g + 1) * LANES]
        mixed[rows, D_ATTN + g * LANES:D_ATTN + (g + 1) * LANES] = (po * _silu(gp)).astype(BF16)

    @pl.when(i == pl.num_programs(0) - 1)
    def _():
        y_ref[...] = x_ref[...] + jnp.dot(mixed[...], wout_ref[...], preferred_element_type=F32)


def _sample_call(sinks, x, nw, win, qn, kn, wpool, pscale, wout, ck, cv, sp):
    nb = x.shape[0]
    const2 = lambda i: (0, 0)
    chunk3 = lambda i: (i, 0, 0)
    mid3 = lambda i: (0, i, 0)
    ws = _weight_specs()
    return pl.pallas_call(
        _sample_kernel,
        grid=(nb // BB,),
        in_specs=[
            pl.BlockSpec(memory_space=pltpu.SMEM),
            pl.BlockSpec((nb, D_MODEL), const2),
            pl.BlockSpec((1, D_MODEL), const2),
            ws["win"],
            ws["head"],
            ws["head"],
            ws["wpool"],
            pl.BlockSpec((1, D_POOL), const2),
            ws["wout"],
            pl.BlockSpec((BB, LANES, WINDOW), chunk3),
            pl.BlockSpec((BB, LANES, WINDOW), chunk3),
            pl.BlockSpec((POOL_STATE, BB, D_POOL), mid3),
        ],
        out_specs=[
            pl.BlockSpec((nb, D_MODEL), const2),
            pl.BlockSpec((BB, LANES, WINDOW), chunk3),
            pl.BlockSpec((BB, LANES, WINDOW), chunk3),
            pl.BlockSpec((POOL_STATE, BB, D_POOL), mid3),
        ],
        out_shape=[
            jax.ShapeDtypeStruct((nb, D_MODEL), F32),
            jax.ShapeDtypeStruct((nb, LANES, WINDOW), F32),
            jax.ShapeDtypeStruct((nb, LANES, WINDOW), F32),
            jax.ShapeDtypeStruct((POOL_STATE, nb, D_POOL), F32),
        ],
        scratch_shapes=_WEIGHT_SCRATCH + [
            pltpu.VMEM((nb, D_IN_PROJ), F32),
            pltpu.VMEM((nb * QROWS, LANES), F32),
            pltpu.VMEM((nb * QROWS, LANES), F32),
            pltpu.VMEM((nb * QROWS, LANES), F32),
            pltpu.VMEM((nb * QROWS, LANES), F32),
            pltpu.VMEM((2, LANES, nb), F32),
            pltpu.VMEM((BB * QROWS, LANES), F32),
            pltpu.VMEM((nb, D_MODEL), BF16),
        ],
        compiler_params=pltpu.CompilerParams(dimension_semantics=("arbitrary",), vmem_limit_bytes=VMEM_LIMIT),
        name="sample_group",
    )(sinks, x, nw, win, qn, kn, wpool, pscale, wout, ck, cv, sp)


def _cache_in(c):
    nb = c.shape[0]
    return jnp.transpose(c, (0, 2, 3, 1)).reshape(nb, LANES, WINDOW)


def _cache_out(c):
    nb = c.shape[0]
    return jnp.transpose(c.reshape(nb, N_KV_HEADS, HEAD_DIM, WINDOW), (0, 3, 1, 2))


def kernel(x_prompt, x_sample, cache_k, cache_v, state_pool, norm_w, w_in, q_norm_w, k_norm_w, sinks, w_pool,
           pool_scale, w_out):
    depth = norm_w.shape[0]
    assert depth == 1 and x_prompt.shape[0] == 1 and x_sample.shape[1] == 1
    seq = x_prompt.shape[1]
    nb = x_sample.shape[0]
    assert seq % TB == 0 and nb % BB == 0 and nb == LANES

    weights = (norm_w, w_in[0], q_norm_w, k_norm_w, w_pool[0], pool_scale, w_out[0])
    sk = sinks[0]

    yp, kp, vp, up = _prompt_call(sk, x_prompt[0], *weights)
    ys, kq, vq, pq = _sample_call(sk, x_sample[:, 0], *weights,
                                  _cache_in(cache_k[0]), _cache_in(cache_v[0]),
                                  jnp.transpose(state_pool[0], (1, 0, 2)))
    return (yp[None], ys[:, None], _cache_out(kp[None])[None], _cache_out(vp[None])[None], up[None, None],
            _cache_out(kq)[None], _cache_out(vq)[None], jnp.transpose(pq, (1, 0, 2))[None])
```

```python
import jax
import jax.numpy as jnp
from jax import lax
from jax.experimental import pallas as pl
from jax.experimental.pallas import tpu as pltpu

D_MODEL = 1024
HEAD_DIM = 64
N_HEADS = 8
N_KV_HEADS = 2
GROUP = 4
WINDOW = 128
BLOCK = 128
POOL_WINDOWS = (2, 4, 8, 16)
POOL_STATE = 15
D_ATTN = 512
D_POOL = 512
D_IN_PROJ = 2304
EPS = 1e-6
NEG_INF = -1e30
LOG2E = 1.4426950408889634

O_K = 512
O_V = 640
O_GA = 768
O_U = 1280
O_GP = 1792

LANES = 128
TB = 512
NSB = TB // BLOCK
BB = 32
QROWS = 16
VMEM_LIMIT = 56 * 1024 * 1024

F32 = jnp.float32
BF16 = jnp.bfloat16
_NT = (((1,), (1,)), ((), ()))


def _silu_half(h):
    return h + h * jnp.tanh(h)


def _lo_mask():
    return lax.broadcasted_iota(jnp.int32, (1, LANES), 1) < HEAD_DIM


def _pair_rms(zs, lo, w2):
    sq = zs * zs
    s_lo = jnp.sum(jnp.where(lo, sq, 0.0), axis=-1, keepdims=True)
    s_hi = jnp.sum(jnp.where(lo, 0.0, sq), axis=-1, keepdims=True)
    r = lax.rsqrt(jnp.where(lo, s_lo, s_hi) * (1.0 / HEAD_DIM) + EPS)
    return zs * r * w2


def _rms_bf16(x, nw):
    ms = jnp.mean(x * x, axis=-1, keepdims=True)
    return (x * lax.rsqrt(ms + EPS) * nw).astype(BF16)


def _project(x, nw, win):
    return jnp.dot(_rms_bf16(x, nw), win, preferred_element_type=F32)


def _load_weights(win_ref, wout_ref, wpool_ref, qn_ref, kn_ref, win_bf, wout_bf, wpool_bf, qw_ref, kw_ref):
    win_bf[:, 0:O_GA] = win_ref[:, 0:O_GA].astype(BF16)
    win_bf[:, O_U:O_GP] = win_ref[:, O_U:O_GP].astype(BF16)
    for c0, c1 in ((O_GA, O_U), (O_GP, D_IN_PROJ)):
        win_bf[:, c0:c1] = (0.5 * win_ref[:, c0:c1]).astype(BF16)
    wout_bf[...] = wout_ref[...].astype(BF16)
    wpool_bf[...] = wpool_ref[...].astype(BF16)
    qn = qn_ref[...] * (HEAD_DIM ** -0.5 * LOG2E)
    qw_ref[...] = jnp.concatenate([qn, qn], axis=1)
    kw_ref[...] = jnp.concatenate([kn_ref[...], kn_ref[...]], axis=1)


def _prompt_kernel(sinks_ref, x_ref, nw_ref, win_f32, qn_ref, kn_ref, wpool_f32, pscale_ref, wout_f32,
                   y_ref, knew_ref, vnew_ref, unew_ref,
                   win_ref, wout_ref, wpool_ref, qw_ref, kw_ref, z_ref, qbuf, kbuf, vbuf, uext, mixed, fill_ref):
    i = pl.program_id(0)
    lo = _lo_mask()

    @pl.when(i == 0)
    def _():
        _load_weights(win_f32, wout_f32, wpool_f32, qn_ref, kn_ref, win_ref, wout_ref, wpool_ref, qw_ref, kw_ref)
        kbuf[:, 0:BLOCK, :] = jnp.zeros((2, BLOCK, LANES), BF16)
        vbuf[:, 0:BLOCK, :] = jnp.zeros((2, BLOCK, LANES), BF16)
        uext[0:16, :] = jnp.zeros((16, D_POOL), F32)
        r0 = lax.broadcasted_iota(jnp.int32, (BLOCK, 2 * BLOCK), 0)
        c0 = lax.broadcasted_iota(jnp.int32, (BLOCK, 2 * BLOCK), 1)
        sink_col = jnp.where(r0 >= 1, r0 - 1, 2 * BLOCK - 1)
        for h in range(N_HEADS):
            fill_ref[h] = jnp.where(c0 == sink_col, sinks_ref[h] * LOG2E, NEG_INF)

    z_ref[...] = _project(x_ref[...], nw_ref[...], win_ref[...])

    khat = _pair_rms(z_ref[:, O_K:O_K + LANES], lo, kw_ref[...])
    kr = pltpu.roll(khat, HEAD_DIM, axis=1)
    kbuf[0, BLOCK:, :] = jnp.where(lo, khat, kr).astype(BF16)
    kbuf[1, BLOCK:, :] = jnp.where(lo, kr, khat).astype(BF16)
    vz = z_ref[:, O_V:O_V + LANES]
    vr = pltpu.roll(vz, HEAD_DIM, axis=1)
    vbuf[0, BLOCK:, :] = jnp.where(lo, vz, vr).astype(BF16)
    vbuf[1, BLOCK:, :] = jnp.where(lo, vr, vz).astype(BF16)

    for j in range(4):
        qhat = _pair_rms(z_ref[:, j * LANES:(j + 1) * LANES], lo, qw_ref[...])
        q_lo = jnp.where(lo, qhat, 0.0).astype(BF16)
        q_hi = jnp.where(lo, 0.0, qhat).astype(BF16)
        c, g0 = j // 2, 2 * (j % 2)
        for sb in range(NSB):
            rows = slice(sb * BLOCK, (sb + 1) * BLOCK)
            qbuf[c, sb, g0 * BLOCK:(g0 + 1) * BLOCK, :] = q_lo[rows]
            qbuf[c, sb, (g0 + 1) * BLOCK:(g0 + 2) * BLOCK, :] = q_hi[rows]

    r_io = lax.broadcasted_iota(jnp.int32, (BLOCK, 2 * BLOCK), 0)
    c_io = lax.broadcasted_iota(jnp.int32, (BLOCK, 2 * BLOCK), 1)
    band = (c_io >= r_io) & (c_io <= r_io + WINDOW)
    first_lo = jnp.where(i > 0, 0, BLOCK)
    band_first = band & (c_io >= first_lo)
    sink_pos = c_io == jnp.where(r_io >= 1, r_io - 1, 2 * BLOCK - 1)

    for sb in range(NSB):
        valid = band_first if sb == 0 else band
        rows = slice(sb * BLOCK, (sb + 1) * BLOCK)
        for c in range(2):
            keys = kbuf[c, sb * BLOCK:(sb + 2) * BLOCK, :]
            vals = vbuf[c, sb * BLOCK:(sb + 2) * BLOCK, :]
            s = lax.dot_general(qbuf[c, sb], keys, _NT, preferred_element_type=F32)
            ps, inv = [], []
            for g in range(GROUP):
                sg = jnp.where(valid, s[g * BLOCK:(g + 1) * BLOCK, :], fill_ref[c * GROUP + g])
                m = jnp.max(sg, axis=-1, keepdims=True)
                p = jnp.exp2(sg - m)
                l = jnp.sum(p, axis=-1, keepdims=True)
                ps.append(jnp.where(sink_pos, 0.0, p).astype(BF16))
                inv.append(1.0 / l)
            o = jnp.dot(jnp.concatenate(ps, axis=0), vals, preferred_element_type=F32)
            og = [o[g * BLOCK:(g + 1) * BLOCK, :] * inv[g] for g in range(GROUP)]
            for jj in range(2):
                slab = jnp.where(lo, og[2 * jj], og[2 * jj + 1])
                col = (2 * c + jj) * LANES
                ga = z_ref[rows, O_GA + col:O_GA + col + LANES]
                mixed[rows, col:col + LANES] = (slab * _silu_half(ga)).astype(BF16)

    uext[16:, :] = z_ref[:, O_U:O_U + D_POOL]
    pos16 = i * TB + lax.broadcasted_iota(jnp.int32, (16, LANES), 0)
    for g, w in enumerate(POOL_WINDOWS):
        cols = slice(g * LANES, (g + 1) * LANES)
        acc = uext[:, cols]
        sh = 1
        while sh < w:
            acc = acc + pltpu.roll(acc, sh, axis=0)
            sh *= 2
        ug = z_ref[:, O_U + g * LANES:O_U + (g + 1) * LANES]
        cnt = jnp.minimum(w, pos16 + 1).astype(F32)
        d_first = acc[16:32, :] / cnt - ug[0:16, :]
        d_rest = acc[32:, :] * (1.0 / w) - ug[16:, :]
        d = jnp.concatenate([d_first, d_rest], axis=0).astype(BF16)
        po = jnp.dot(d, wpool_ref[g], preferred_element_type=F32) * pscale_ref[:, cols]
        gp = z_ref[:, O_GP + g * LANES:O_GP + (g + 1) * LANES]
        mixed[:, D_ATTN + g * LANES:D_ATTN + (g + 1) * LANES] = (po * _silu_half(gp)).astype(BF16)

    kbuf[:, 0:BLOCK, :] = kbuf[:, TB:TB + BLOCK, :]
    vbuf[:, 0:BLOCK, :] = vbuf[:, TB:TB + BLOCK, :]
    uext[0:16, :] = uext[TB:TB + 16, :]

    y_ref[...] = x_ref[...] + jnp.dot(mixed[...], wout_ref[...], preferred_element_type=F32)

    @pl.when(i == pl.num_programs(0) - 1)
    def _():
        unew_ref[:, 0, :] = z_ref[TB - POOL_STATE:TB, O_U:O_U + D_POOL]
        tail = slice(TB - WINDOW, TB)
        knew_ref[...] = _pair_rms(z_ref[tail, O_K:O_K + LANES], lo, kw_ref[...]).T
        vnew_ref[...] = z_ref[tail, O_V:O_V + LANES].T


def _weight_specs():
    const2 = lambda i: (0, 0)
    once = pl.Buffered(1)
    return dict(
        win=pl.BlockSpec((D_MODEL, D_IN_PROJ), const2, pipeline_mode=once),
        wout=pl.BlockSpec((D_MODEL, D_MODEL), const2, pipeline_mode=once),
        wpool=pl.BlockSpec((4, LANES, LANES), lambda i: (0, 0, 0), pipeline_mode=once),
        head=pl.BlockSpec((1, HEAD_DIM), const2),
    )


_WEIGHT_SCRATCH = [
    pltpu.VMEM((D_MODEL, D_IN_PROJ), BF16),
    pltpu.VMEM((D_MODEL, D_MODEL), BF16),
    pltpu.VMEM((4, LANES, LANES), BF16),
    pltpu.VMEM((1, LANES), F32),
    pltpu.VMEM((1, LANES), F32),
]


def _prompt_call(sinks, x, nw, win, qn, kn, wpool, pscale, wout):
    seq = x.shape[0]
    nblk = seq // TB
    const2 = lambda i: (0, 0)
    ws = _weight_specs()
    return pl.pallas_call(
        _prompt_kernel,
        grid=(nblk,),
        in_specs=[
            pl.BlockSpec(memory_space=pltpu.SMEM),
            pl.BlockSpec((TB, D_MODEL), lambda i: (i, 0)),
            pl.BlockSpec((1, D_MODEL), const2),
            ws["win"],
            ws["head"],
            ws["head"],
            ws["wpool"],
            pl.BlockSpec((1, D_POOL), const2),
            ws["wout"],
        ],
        out_specs=[
            pl.BlockSpec((TB, D_MODEL), lambda i: (i, 0)),
            pl.BlockSpec((LANES, WINDOW), const2),
            pl.BlockSpec((LANES, WINDOW), const2),
            pl.BlockSpec((POOL_STATE, 1, D_POOL), lambda i: (0, 0, 0)),
        ],
        out_shape=[
            jax.ShapeDtypeStruct((seq, D_MODEL), F32),
            jax.ShapeDtypeStruct((LANES, WINDOW), F32),
            jax.ShapeDtypeStruct((LANES, WINDOW), F32),
            jax.ShapeDtypeStruct((POOL_STATE, 1, D_POOL), F32),
        ],
        scratch_shapes=_WEIGHT_SCRATCH + [
            pltpu.VMEM((TB, D_IN_PROJ), F32),
            pltpu.VMEM((2, NSB, GROUP * BLOCK, LANES), BF16),
            pltpu.VMEM((2, TB + BLOCK, LANES), BF16),
            pltpu.VMEM((2, TB + BLOCK, LANES), BF16),
            pltpu.VMEM((TB + 16, D_POOL), F32),
            pltpu.VMEM((TB, D_MODEL), BF16),
            pltpu.VMEM((N_HEADS, BLOCK, 2 * BLOCK), F32),
        ],
        compiler_params=pltpu.CompilerParams(dimension_semantics=("arbitrary",), vmem_limit_bytes=VMEM_LIMIT),
        name="prompt_group",
    )(sinks, x, nw, win, qn, kn, wpool, pscale, wout)


def _sample_kernel(sinks_ref, x_ref, nw_ref, win_f32, qn_ref, kn_ref, wpool_f32, pscale_ref, wout_f32,
                   ck_ref, cv_ref, sp_ref,
                   y_ref, knew_ref, vnew_ref, spnew_ref,
                   win_ref, wout_ref, wpool_ref, qw_ref, kw_ref, z_ref, q3, krep, vrep, snew, kvt, o3, mixed):
    i = pl.program_id(0)
    nb = x_ref.shape[0]
    lo = _lo_mask()

    @pl.when(i == 0)
    def _():
        _load_weights(win_f32, wout_f32, wpool_f32, qn_ref, kn_ref, win_ref, wout_ref, wpool_ref, qw_ref, kw_ref)
        z_ref[...] = _project(x_ref[:, 0, :], nw_ref[...], win_ref[...])
        khat = _pair_rms(z_ref[:, O_K:O_K + LANES], lo, kw_ref[...])
        vnew = z_ref[:, O_V:O_V + LANES]
        kvt[0] = khat.T
        kvt[1] = vnew.T
        q3[...] = jnp.zeros(q3.shape, F32)
        krep[...] = jnp.zeros(krep.shape, F32)
        vrep[...] = jnp.zeros(vrep.shape, F32)
        for j in range(4):
            qhat = _pair_rms(z_ref[:, j * LANES:(j + 1) * LANES], lo, qw_ref[...])
            qrot = pltpu.roll(qhat, HEAD_DIM, axis=1)
            grp_lo = j < 2
            for half in range(2):
                h = 2 * j + half
                src = qhat if (half == 0) == grp_lo else qrot
                q3[pl.ds(h, nb, stride=QROWS), :] = jnp.where(lo if grp_lo else ~lo, src, 0.0)
                krep[pl.ds(h, nb, stride=QROWS), :] = khat
                vrep[pl.ds(h, nb, stride=QROWS), :] = vnew
        s_new = jnp.sum(q3[...] * krep[...], axis=-1, keepdims=True)
        snew[...] = jnp.broadcast_to(s_new, snew.shape)

    row0 = pl.multiple_of(i * BB, BB)
    rows = pl.ds(row0, BB)
    qrows = pl.ds(pl.multiple_of(i * (BB * QROWS), BB * QROWS), BB * QROWS)
    rid = lax.broadcasted_iota(jnp.int32, (1, QROWS, 1), 1)
    sink3 = jnp.zeros((1, QROWS, 1), F32)
    for h in range(N_HEADS):
        sink3 = jnp.where(rid == h, sinks_ref[h] * LOG2E, sink3)

    q = q3[qrows, :].reshape(BB, QROWS, LANES)
    s_new = snew[qrows, :].reshape(BB, QROWS, LANES)
    kt = ck_ref[...]
    vt = cv_ref[...]
    s = lax.dot_general(q.astype(BF16), kt.astype(BF16), (((2,), (1,)), ((0,), (0,))), preferred_element_type=F32)
    m = jnp.maximum(jnp.maximum(jnp.max(s, axis=-1, keepdims=True), s_new), sink3)
    p = jnp.exp2(s - m)
    p_new = jnp.exp2(s_new - m)
    l = jnp.sum(p, axis=-1, keepdims=True) + p_new + jnp.exp2(sink3 - m)
    o = lax.dot_general(p.astype(BF16), vt.astype(BF16), (((2,), (2,)), ((0,), (0,))), preferred_element_type=F32)
    o = (o + p_new * vrep[qrows, :].reshape(BB, QROWS, LANES)) / l
    o3[...] = o.reshape(BB * QROWS, LANES)

    shift = lax.rem(nb - row0, nb)
    k_cols = pltpu.roll(kvt[0], shift, axis=1)
    v_cols = pltpu.roll(kvt[1], shift, axis=1)
    k_shift = pltpu.roll(kt, WINDOW - 1, axis=2)
    v_shift = pltpu.roll(vt, WINDOW - 1, axis=2)
    last = lax.broadcasted_iota(jnp.int32, (LANES, WINDOW), 1) == WINDOW - 1
    for bb in range(BB):
        knew_ref[bb] = jnp.where(last, k_cols[:, bb:bb + 1], k_shift[bb])
        vnew_ref[bb] = jnp.where(last, v_cols[:, bb:bb + 1], v_shift[bb])

    for j in range(4):
        grp_lo = j < 2
        oh = [o3[pl.ds(2 * j + half, BB, stride=QROWS), :] for half in range(2)]
        a = oh[0] if grp_lo else pltpu.roll(oh[0], HEAD_DIM, axis=1)
        b = pltpu.roll(oh[1], HEAD_DIM, axis=1) if grp_lo else oh[1]
        ga = z_ref[rows, O_GA + j * LANES:O_GA + (j + 1) * LANES]
        mixed[rows, j * LANES:(j + 1) * LANES] = (jnp.where(lo, a, b) * _silu_half(ga)).astype(BF16)

    u = z_ref[rows, O_U:O_U + D_POOL]
    spnew_ref[0:POOL_STATE - 1] = sp_ref[1:POOL_STATE]
    spnew_ref[POOL_STATE - 1] = u
    for g, w in enumerate(POOL_WINDOWS):
        cols = slice(g * LANES, (g + 1) * LANES)
        ug = u[:, cols]
        win_sum = ug
        for r in range(POOL_STATE - (w - 1), POOL_STATE):
            win_sum = win_sum + sp_ref[r, :, cols]
        d = (win_sum * (1.0 / w) - ug).astype(BF16)
        po = jnp.dot(d, wpool_ref[g], preferred_element_type=F32) * pscale_ref[:, cols]
        gp = z_ref[rows, O_GP + g * LANES:O_GP + (g + 1) * LANES]
        mixed[rows, D_ATTN + g * LANES:D_ATTN + (g + 1) * LANES] = (po * _silu_half(gp)).astype(BF16)

    @pl.when(i == pl.num_programs(0) - 1)
    def _():
        y_ref[:, 0, :] = x_ref[:, 0, :] + jnp.dot(mixed[...], wout_ref[...], preferred_element_type=F32)


def _sample_call(sinks, x, nw, win, qn, kn, wpool, pscale, wout, ck, cv, sp):
    nb = x.shape[0]
    const2 = lambda i: (0, 0)
    chunk3 = lambda i: (i, 0, 0)
    mid3 = lambda i: (0, i, 0)
    ws = _weight_specs()
    return pl.pallas_call(
        _sample_kernel,
        grid=(nb // BB,),
        in_specs=[
            pl.BlockSpec(memory_space=pltpu.SMEM),
            pl.BlockSpec((nb, 1, D_MODEL), lambda i: (0, 0, 0)),
            pl.BlockSpec((1, D_MODEL), const2),
            ws["win"],
            ws["head"],
            ws["head"],
            ws["wpool"],
            pl.BlockSpec((1, D_POOL), const2),
            ws["wout"],
            pl.BlockSpec((BB, LANES, WINDOW), chunk3),
            pl.BlockSpec((BB, LANES, WINDOW), chunk3),
            pl.BlockSpec((POOL_STATE, BB, D_POOL), mid3),
        ],
        out_specs=[
            pl.BlockSpec((nb, 1, D_MODEL), lambda i: (0, 0, 0)),
            pl.BlockSpec((BB, LANES, WINDOW), chunk3),
            pl.BlockSpec((BB, LANES, WINDOW), chunk3),
            pl.BlockSpec((POOL_STATE, BB, D_POOL), mid3),
        ],
        out_shape=[
            jax.ShapeDtypeStruct((nb, 1, D_MODEL), F32),
            jax.ShapeDtypeStruct((nb, LANES, WINDOW), F32),
            jax.ShapeDtypeStruct((nb, LANES, WINDOW), F32),
            jax.ShapeDtypeStruct((POOL_STATE, nb, D_POOL), F32),
        ],
        scratch_shapes=_WEIGHT_SCRATCH + [
            pltpu.VMEM((nb, D_IN_PROJ), F32),
            pltpu.VMEM((nb * QROWS, LANES), F32),
            pltpu.VMEM((nb * QROWS, LANES), F32),
            pltpu.VMEM((nb * QROWS, LANES), F32),
            pltpu.VMEM((nb * QROWS, LANES), F32),
            pltpu.VMEM((2, LANES, nb), F32),
            pltpu.VMEM((BB * QROWS, LANES), F32),
            pltpu.VMEM((nb, D_MODEL), BF16),
        ],
        compiler_params=pltpu.CompilerParams(dimension_semantics=("arbitrary",), vmem_limit_bytes=VMEM_LIMIT),
        name="sample_group",
    )(sinks, x, nw, win, qn, kn, wpool, pscale, wout, ck, cv, sp)


def _cache_in(c):
    nb = c.shape[0]
    return jnp.transpose(c, (0, 2, 3, 1)).reshape(nb, LANES, WINDOW)


def _cache_out(c):
    nb = c.shape[0]
    return jnp.transpose(c.reshape(nb, N_KV_HEADS, HEAD_DIM, WINDOW), (0, 3, 1, 2))


def kernel(x_prompt, x_sample, cache_k, cache_v, state_pool, norm_w, w_in, q_norm_w, k_norm_w, sinks, w_pool,
           pool_scale, w_out):
    depth = norm_w.shape[0]
    assert depth == 1 and x_prompt.shape[0] == 1 and x_sample.shape[1] == 1
    seq = x_prompt.shape[1]
    nb = x_sample.shape[0]
    assert seq % TB == 0 and nb % BB == 0 and nb == LANES

    weights = (norm_w, w_in[0], q_norm_w, k_norm_w, w_pool[0], pool_scale, w_out[0])
    sk = sinks[0]

    yp, kp, vp, up = _prompt_call(sk, x_prompt[0], *weights)
    ys, kq, vq, pq = _sample_call(sk, x_sample, *weights,
                                  _cache_in(cache_k[0]), _cache_in(cache_v[0]),
                                  jnp.transpose(state_pool[0], (1, 0, 2)))
    return (yp[None], ys, _cache_out(kp[None])[None], _cache_out(vp[None])[None],
            jnp.transpose(up, (1, 0, 2))[None],
            _cache_out(kq)[None], _cache_out(vq)[None], jnp.transpose(pq, (1, 0, 2))[None])
```

```python
import functools

import jax
import jax.numpy as jnp
from jax import lax
from jax.experimental import pallas as pl
from jax.experimental.pallas import tpu as pltpu

D_MODEL = 1024
HEAD_DIM = 64
N_HEADS = 8
N_KV_HEADS = 2
GROUP = 4
WINDOW = 128
BLOCK = 128
POOL_WINDOWS = (2, 4, 8, 16)
POOL_STATE = 15
D_ATTN = 512
D_POOL = 512
D_IN_PROJ = 2304
EPS = 1e-6
NEG_INF = -1e30
LOG2E = 1.4426950408889634

O_K = 512
O_V = 640
O_GA = 768
O_U = 1280
O_GP = 1792

LANES = 128
TB = 512
NSB = TB // BLOCK
BB = 16
QROWS = 16
WCHUNK = 256
VMEM_LIMIT = 58 * 1024 * 1024

F32 = jnp.float32
BF16 = jnp.bfloat16
_NT = (((1,), (1,)), ((), ()))


def _silu_half(h):
    return h + h * jnp.tanh(h)


def _lo_mask():
    return lax.broadcasted_iota(jnp.int32, (1, LANES), 1) < HEAD_DIM


def _pair_rms(zs, lo, w2):
    sq = zs * zs
    s_lo = jnp.sum(jnp.where(lo, sq, 0.0), axis=-1, keepdims=True)
    s_hi = jnp.sum(jnp.where(lo, 0.0, sq), axis=-1, keepdims=True)
    r = lax.rsqrt(jnp.where(lo, s_lo, s_hi) * (1.0 / HEAD_DIM) + EPS)
    return zs * r * w2


def _rms_bf16(x, nw):
    ms = jnp.mean(x * x, axis=-1, keepdims=True)
    return (x * lax.rsqrt(ms + EPS) * nw).astype(BF16)


def _project(x, nw, win):
    return jnp.dot(_rms_bf16(x, nw), win, preferred_element_type=F32)


def _weight_chunks():
    chunks = []
    for c0 in range(0, D_IN_PROJ, WCHUNK):
        gate = O_GA <= c0 < O_U or c0 >= O_GP
        chunks.append((0, c0, 0.5 if gate else 1.0))
    chunks += [(1, c0, 1.0) for c0 in range(0, D_MODEL, WCHUNK)]
    return chunks


def _load_weights(win_hbm, wout_hbm, wpool_f32, qn_ref, kn_ref, stage, sem, win_bf, wout_bf, wpool_bf, qw_ref, kw_ref):
    chunks = _weight_chunks()
    srcs, dsts = (win_hbm, wout_hbm), (win_bf, wout_bf)

    def copy(k):
        which, c0, _ = chunks[k]
        return pltpu.make_async_copy(srcs[which].at[:, pl.ds(c0, WCHUNK)], stage.at[k % 2], sem.at[k % 2])

    copy(0).start()
    copy(1).start()
    for k, (which, c0, scale) in enumerate(chunks):
        copy(k).wait()
        w = stage[k % 2]
        dsts[which][:, c0:c0 + WCHUNK] = (w if scale == 1.0 else scale * w).astype(BF16)
        if k + 2 < len(chunks):
            copy(k + 2).start()
    wpool_bf[...] = wpool_f32[...].astype(BF16)
    qn = qn_ref[...] * (HEAD_DIM ** -0.5 * LOG2E)
    qw_ref[...] = jnp.concatenate([qn, qn], axis=1)
    kw_ref[...] = jnp.concatenate([kn_ref[...], kn_ref[...]], axis=1)


def _prompt_init(sinks_ref, kbuf, vbuf, uext, fill_ref):
    kbuf[:, 0:BLOCK, :] = jnp.zeros((2, BLOCK, LANES), BF16)
    vbuf[:, 0:BLOCK, :] = jnp.zeros((2, BLOCK, LANES), BF16)
    uext[0:16, :] = jnp.zeros((16, D_POOL), F32)
    r0 = lax.broadcasted_iota(jnp.int32, (BLOCK, 2 * BLOCK), 0)
    c0 = lax.broadcasted_iota(jnp.int32, (BLOCK, 2 * BLOCK), 1)
    sink_col = jnp.where(r0 >= 1, r0 - 1, 2 * BLOCK - 1)
    for h in range(N_HEADS):
        fill_ref[h] = jnp.where(c0 == sink_col, sinks_ref[h] * LOG2E, NEG_INF)


def _prompt_step(i, last, x_ref, nw_ref, pscale_ref, y_ref, knew_ref, vnew_ref, unew_ref,
                 win_ref, wout_ref, wpool_ref, qw_ref, kw_ref, z_ref, qbuf, kbuf, vbuf, uext, mixed, fill_ref):
    lo = _lo_mask()
    z_ref[...] = _project(x_ref[...], nw_ref[...], win_ref[...])

    khat = _pair_rms(z_ref[:, O_K:O_K + LANES], lo, kw_ref[...])
    kr = pltpu.roll(khat, HEAD_DIM, axis=1)
    kbuf[0, BLOCK:, :] = jnp.where(lo, khat, kr).astype(BF16)
    kbuf[1, BLOCK:, :] = jnp.where(lo, kr, khat).astype(BF16)
    vz = z_ref[:, O_V:O_V + LANES]
    vr = pltpu.roll(vz, HEAD_DIM, axis=1)
    vbuf[0, BLOCK:, :] = jnp.where(lo, vz, vr).astype(BF16)
    vbuf[1, BLOCK:, :] = jnp.where(lo, vr, vz).astype(BF16)

    for j in range(4):
        qhat = _pair_rms(z_ref[:, j * LANES:(j + 1) * LANES], lo, qw_ref[...])
        q_lo = jnp.where(lo, qhat, 0.0).astype(BF16)
        q_hi = jnp.where(lo, 0.0, qhat).astype(BF16)
        c, g0 = j // 2, 2 * (j % 2)
        for sb in range(NSB):
            rows = slice(sb * BLOCK, (sb + 1) * BLOCK)
            qbuf[c, sb, g0 * BLOCK:(g0 + 1) * BLOCK, :] = q_lo[rows]
            qbuf[c, sb, (g0 + 1) * BLOCK:(g0 + 2) * BLOCK, :] = q_hi[rows]

    r_io = lax.broadcasted_iota(jnp.int32, (BLOCK, 2 * BLOCK), 0)
    c_io = lax.broadcasted_iota(jnp.int32, (BLOCK, 2 * BLOCK), 1)
    band = (c_io >= r_io) & (c_io <= r_io + WINDOW)
    first_lo = jnp.where(i > 0, 0, BLOCK)
    band_first = band & (c_io >= first_lo)
    sink_pos = c_io == jnp.where(r_io >= 1, r_io - 1, 2 * BLOCK - 1)

    for sb in range(NSB):
        valid = band_first if sb == 0 else band
        rows = slice(sb * BLOCK, (sb + 1) * BLOCK)
        for c in range(2):
            keys = kbuf[c, sb * BLOCK:(sb + 2) * BLOCK, :]
            vals = vbuf[c, sb * BLOCK:(sb + 2) * BLOCK, :]
            s = lax.dot_general(qbuf[c, sb], keys, _NT, preferred_element_type=F32)
            ps, inv = [], []
            for g in range(GROUP):
                sg = jnp.where(valid, s[g * BLOCK:(g + 1) * BLOCK, :], fill_ref[c * GROUP + g])
                m = jnp.max(sg, axis=-1, keepdims=True)
                p = jnp.exp2(sg - m)
                l = jnp.sum(p, axis=-1, keepdims=True)
                ps.append(jnp.where(sink_pos, 0.0, p).astype(BF16))
                inv.append(1.0 / l)
            o = jnp.dot(jnp.concatenate(ps, axis=0), vals, preferred_element_type=F32)
            og = [o[g * BLOCK:(g + 1) * BLOCK, :] * inv[g] for g in range(GROUP)]
            for jj in range(2):
                slab = jnp.where(lo, og[2 * jj], og[2 * jj + 1])
                col = (2 * c + jj) * LANES
                ga = z_ref[rows, O_GA + col:O_GA + col + LANES]
                mixed[rows, col:col + LANES] = (slab * _silu_half(ga)).astype(BF16)

    uext[16:, :] = z_ref[:, O_U:O_U + D_POOL]
    pos16 = i * TB + lax.broadcasted_iota(jnp.int32, (16, LANES), 0)
    for g, w in enumerate(POOL_WINDOWS):
        cols = slice(g * LANES, (g + 1) * LANES)
        acc = uext[:, cols]
        sh = 1
        while sh < w:
            acc = acc + pltpu.roll(acc, sh, axis=0)
            sh *= 2
        ug = z_ref[:, O_U + g * LANES:O_U + (g + 1) * LANES]
        cnt = jnp.minimum(w, pos16 + 1).astype(F32)
        d_first = acc[16:32, :] / cnt - ug[0:16, :]
        d_rest = acc[32:, :] * (1.0 / w) - ug[16:, :]
        d = jnp.concatenate([d_first, d_rest], axis=0).astype(BF16)
        po = jnp.dot(d, wpool_ref[g], preferred_element_type=F32) * pscale_ref[:, cols]
        gp = z_ref[:, O_GP + g * LANES:O_GP + (g + 1) * LANES]
        mixed[:, D_ATTN + g * LANES:D_ATTN + (g + 1) * LANES] = (po * _silu_half(gp)).astype(BF16)

    kbuf[:, 0:BLOCK, :] = kbuf[:, TB:TB + BLOCK, :]
    vbuf[:, 0:BLOCK, :] = vbuf[:, TB:TB + BLOCK, :]
    uext[0:16, :] = uext[TB:TB + 16, :]

    y_ref[...] = x_ref[...] + jnp.dot(mixed[...], wout_ref[...], preferred_element_type=F32)

    @pl.when(last)
    def _():
        unew_ref[:, 0, :] = z_ref[TB - POOL_STATE:TB, O_U:O_U + D_POOL]
        tail = slice(TB - WINDOW, TB)
        knew_ref[...] = _pair_rms(z_ref[tail, O_K:O_K + LANES], lo, kw_ref[...]).T
        vnew_ref[...] = z_ref[tail, O_V:O_V + LANES].T


def _sample_init(x_ref, nw_ref, win_ref, qw_ref, kw_ref, z_ref, q3, krep, vrep, snew, kvt):
    nb = x_ref.shape[0]
    lo = _lo_mask()
    z_ref[...] = _project(x_ref[:, 0, :], nw_ref[...], win_ref[...])
    khat = _pair_rms(z_ref[:, O_K:O_K + LANES], lo, kw_ref[...])
    vnew = z_ref[:, O_V:O_V + LANES]
    kvt[0] = khat.T
    kvt[1] = vnew.T
    q3[...] = jnp.zeros(q3.shape, F32)
    krep[...] = jnp.zeros(krep.shape, F32)
    vrep[...] = jnp.zeros(vrep.shape, F32)
    for j in range(4):
        qhat = _pair_rms(z_ref[:, j * LANES:(j + 1) * LANES], lo, qw_ref[...])
        qrot = pltpu.roll(qhat, HEAD_DIM, axis=1)
        grp_lo = j < 2
        for half in range(2):
            h = 2 * j + half
            src = qhat if (half == 0) == grp_lo else qrot
            q3[pl.ds(h, nb, stride=QROWS), :] = jnp.where(lo if grp_lo else ~lo, src, 0.0)
            krep[pl.ds(h, nb, stride=QROWS), :] = khat
            vrep[pl.ds(h, nb, stride=QROWS), :] = vnew
    s_new = jnp.sum(q3[...] * krep[...], axis=-1, keepdims=True)
    snew[...] = jnp.broadcast_to(s_new, snew.shape)


def _sample_step(j, last, sinks_ref, x_ref, pscale_ref, ck_ref, cv_ref, sp_ref, y_ref, knew_ref, vnew_ref, spnew_ref,
                 wout_ref, wpool_ref, z_ref, q3, krep, vrep, snew, kvt, o3, mixed):
    nb = x_ref.shape[0]
    lo = _lo_mask()
    row0 = pl.multiple_of(j * BB, BB)
    rows = pl.ds(row0, BB)
    qrows = pl.ds(pl.multiple_of(j * (BB * QROWS), BB * QROWS), BB * QROWS)
    rid = lax.broadcasted_iota(jnp.int32, (1, QROWS, 1), 1)
    sink3 = jnp.zeros((1, QROWS, 1), F32)
    for h in range(N_HEADS):
        sink3 = jnp.where(rid == h, sinks_ref[h] * LOG2E, sink3)

    q = q3[qrows, :].reshape(BB, QROWS, LANES)
    s_new = snew[qrows, :].reshape(BB, QROWS, LANES)
    kt = ck_ref[...]
    vt = cv_ref[...]
    s = lax.dot_general(q.astype(BF16), kt.astype(BF16), (((2,), (1,)), ((0,), (0,))), preferred_element_type=F32)
    m = jnp.maximum(jnp.maximum(jnp.max(s, axis=-1, keepdims=True), s_new), sink3)
    p = jnp.exp2(s - m)
    p_new = jnp.exp2(s_new - m)
    l = jnp.sum(p, axis=-1, keepdims=True) + p_new + jnp.exp2(sink3 - m)
    o = lax.dot_general(p.astype(BF16), vt.astype(BF16), (((2,), (2,)), ((0,), (0,))), preferred_element_type=F32)
    o = (o + p_new * vrep[qrows, :].reshape(BB, QROWS, LANES)) / l
    o3[...] = o.reshape(BB * QROWS, LANES)

    shift = lax.rem(nb - row0, nb)
    k_cols = pltpu.roll(kvt[0], shift, axis=1)
    v_cols = pltpu.roll(kvt[1], shift, axis=1)
    k_shift = pltpu.roll(kt, WINDOW - 1, axis=2)
    v_shift = pltpu.roll(vt, WINDOW - 1, axis=2)
    newest = lax.broadcasted_iota(jnp.int32, (LANES, WINDOW), 1) == WINDOW - 1
    for bb in range(BB):
        knew_ref[bb] = jnp.where(newest, k_cols[:, bb:bb + 1], k_shift[bb])
        vnew_ref[bb] = jnp.where(newest, v_cols[:, bb:bb + 1], v_shift[bb])

    for jj in range(4):
        grp_lo = jj < 2
        oh = [o3[pl.ds(2 * jj + half, BB, stride=QROWS), :] for half in range(2)]
        a = oh[0] if grp_lo else pltpu.roll(oh[0], HEAD_DIM, axis=1)
        b = pltpu.roll(oh[1], HEAD_DIM, axis=1) if grp_lo else oh[1]
        ga = z_ref[rows, O_GA + jj * LANES:O_GA + (jj + 1) * LANES]
        mixed[rows, jj * LANES:(jj + 1) * LANES] = (jnp.where(lo, a, b) * _silu_half(ga)).astype(BF16)

    u = z_ref[rows, O_U:O_U + D_POOL]
    spnew_ref[0:POOL_STATE - 1] = sp_ref[1:POOL_STATE]
    spnew_ref[POOL_STATE - 1] = u
    for g, w in enumerate(POOL_WINDOWS):
        cols = slice(g * LANES, (g + 1) * LANES)
        ug = u[:, cols]
        win_sum = ug
        for r in range(POOL_STATE - (w - 1), POOL_STATE):
            win_sum = win_sum + sp_ref[r, :, cols]
        d = (win_sum * (1.0 / w) - ug).astype(BF16)
        po = jnp.dot(d, wpool_ref[g], preferred_element_type=F32) * pscale_ref[:, cols]
        gp = z_ref[rows, O_GP + g * LANES:O_GP + (g + 1) * LANES]
        mixed[rows, D_ATTN + g * LANES:D_ATTN + (g + 1) * LANES] = (po * _silu_half(gp)).astype(BF16)

    @pl.when(last)
    def _():
        y_ref[:, 0, :] = x_ref[:, 0, :] + jnp.dot(mixed[...], wout_ref[...], preferred_element_type=F32)


def _fused_kernel(n_prompt, sinks_ref, xp_ref, xs_ref, nw_ref, win_hbm, qn_ref, kn_ref, wpool_f32, pscale_ref, wout_hbm,
                  ck_ref, cv_ref, sp_ref,
                  yp_ref, kp_ref, vp_ref, up_ref, ys_ref, kq_ref, vq_ref, pq_ref,
                  win_ref, wout_ref, wpool_ref, qw_ref, kw_ref, stage, sem,
                  z_ref, qbuf, kbuf, vbuf, uext, mixed, fill_ref,
                  zs_ref, q3, krep, vrep, snew, kvt, o3, mixed_s):
    i = pl.program_id(0)

    @pl.when(i == 0)
    def _():
        _load_weights(win_hbm, wout_hbm, wpool_f32, qn_ref, kn_ref, stage, sem, win_ref, wout_ref, wpool_ref,
                      qw_ref, kw_ref)
        _prompt_init(sinks_ref, kbuf, vbuf, uext, fill_ref)

    @pl.when(i < n_prompt)
    def _():
        _prompt_step(i, i == n_prompt - 1, xp_ref, nw_ref, pscale_ref, yp_ref, kp_ref, vp_ref, up_ref,
                     win_ref, wout_ref, wpool_ref, qw_ref, kw_ref, z_ref, qbuf, kbuf, vbuf, uext, mixed, fill_ref)

    @pl.when(i == n_prompt)
    def _():
        _sample_init(xs_ref, nw_ref, win_ref, qw_ref, kw_ref, zs_ref, q3, krep, vrep, snew, kvt)

    @pl.when(i >= n_prompt)
    def _():
        _sample_step(i - n_prompt, i == pl.num_programs(0) - 1, sinks_ref, xs_ref, pscale_ref, ck_ref, cv_ref, sp_ref,
                     ys_ref, kq_ref, vq_ref, pq_ref, wout_ref, wpool_ref, zs_ref, q3, krep, vrep, snew, kvt, o3, mixed_s)


def _fused_call(sinks, xp, xs, nw, win, qn, kn, wpool, pscale, wout, ck, cv, sp):
    seq, nb = xp.shape[0], xs.shape[0]
    n_prompt, n_sample = seq // TB, nb // BB
    const2 = lambda i: (0, 0)
    const3 = lambda i: (0, 0, 0)
    pblk = lambda i: (jnp.minimum(i, n_prompt - 1), 0)
    sidx = lambda i: jnp.clip(i - n_prompt, 0, n_sample - 1)
    chunk3 = lambda i: (sidx(i), 0, 0)
    mid3 = lambda i: (0, sidx(i), 0)
    once = pl.Buffered(1)
    return pl.pallas_call(
        functools.partial(_fused_kernel, n_prompt),
        grid=(n_prompt + n_sample,),
        in_specs=[
            pl.BlockSpec(memory_space=pltpu.SMEM),
            pl.BlockSpec((TB, D_MODEL), pblk),
            pl.BlockSpec((nb, 1, D_MODEL), const3, pipeline_mode=once),
            pl.BlockSpec((1, D_MODEL), const2),
            pl.BlockSpec(memory_space=pl.ANY),
            pl.BlockSpec((1, HEAD_DIM), const2),
            pl.BlockSpec((1, HEAD_DIM), const2),
            pl.BlockSpec((4, LANES, LANES), const3, pipeline_mode=once),
            pl.BlockSpec((1, D_POOL), const2),
            pl.BlockSpec(memory_space=pl.ANY),
            pl.BlockSpec((BB, LANES, WINDOW), chunk3),
            pl.BlockSpec((BB, LANES, WINDOW), chunk3),
            pl.BlockSpec((POOL_STATE, BB, D_POOL), mid3),
        ],
        out_specs=[
            pl.BlockSpec((TB, D_MODEL), pblk),
            pl.BlockSpec((LANES, WINDOW), const2),
            pl.BlockSpec((LANES, WINDOW), const2),
            pl.BlockSpec((POOL_STATE, 1, D_POOL), const3),
            pl.BlockSpec((nb, 1, D_MODEL), const3, pipeline_mode=once),
            pl.BlockSpec((BB, LANES, WINDOW), chunk3),
            pl.BlockSpec((BB, LANES, WINDOW), chunk3),
            pl.BlockSpec((POOL_STATE, BB, D_POOL), mid3),
        ],
        out_shape=[
            jax.ShapeDtypeStruct((seq, D_MODEL), F32),
            jax.ShapeDtypeStruct((LANES, WINDOW), F32),
            jax.ShapeDtypeStruct((LANES, WINDOW), F32),
            jax.ShapeDtypeStruct((POOL_STATE, 1, D_POOL), F32),
            jax.ShapeDtypeStruct((nb, 1, D_MODEL), F32),
            jax.ShapeDtypeStruct((nb, LANES, WINDOW), F32),
            jax.ShapeDtypeStruct((nb, LANES, WINDOW), F32),
            jax.ShapeDtypeStruct((POOL_STATE, nb, D_POOL), F32),
        ],
        scratch_shapes=[
            pltpu.VMEM((D_MODEL, D_IN_PROJ), BF16),
            pltpu.VMEM((D_MODEL, D_MODEL), BF16),
            pltpu.VMEM((4, LANES, LANES), BF16),
            pltpu.VMEM((1, LANES), F32),
            pltpu.VMEM((1, LANES), F32),
            pltpu.VMEM((2, D_MODEL, WCHUNK), F32),
            pltpu.SemaphoreType.DMA((2,)),
            pltpu.VMEM((TB, D_IN_PROJ), F32),
            pltpu.VMEM((2, NSB, GROUP * BLOCK, LANES), BF16),
            pltpu.VMEM((2, TB + BLOCK, LANES), BF16),
            pltpu.VMEM((2, TB + BLOCK, LANES), BF16),
            pltpu.VMEM((TB + 16, D_POOL), F32),
            pltpu.VMEM((TB, D_MODEL), BF16),
            pltpu.VMEM((N_HEADS, BLOCK, 2 * BLOCK), F32),
            pltpu.VMEM((nb, D_IN_PROJ), F32),
            pltpu.VMEM((nb * QROWS, LANES), F32),
            pltpu.VMEM((nb * QROWS, LANES), F32),
            pltpu.VMEM((nb * QROWS, LANES), F32),
            pltpu.VMEM((nb * QROWS, LANES), F32),
            pltpu.VMEM((2, LANES, nb), F32),
            pltpu.VMEM((BB * QROWS, LANES), F32),
            pltpu.VMEM((nb, D_MODEL), BF16),
        ],
        compiler_params=pltpu.CompilerParams(dimension_semantics=("arbitrary",), vmem_limit_bytes=VMEM_LIMIT),
        name="hybrid_step",
    )(sinks, xp, xs, nw, win, qn, kn, wpool, pscale, wout, ck, cv, sp)


def _cache_in(c):
    nb = c.shape[0]
    return jnp.transpose(c, (0, 2, 3, 1)).reshape(nb, LANES, WINDOW)


def _cache_out(c):
    nb = c.shape[0]
    return jnp.transpose(c.reshape(nb, N_KV_HEADS, HEAD_DIM, WINDOW), (0, 3, 1, 2))


def kernel(x_prompt, x_sample, cache_k, cache_v, state_pool, norm_w, w_in, q_norm_w, k_norm_w, sinks, w_pool,
           pool_scale, w_out):
    depth = norm_w.shape[0]
    assert depth == 1 and x_prompt.shape[0] == 1 and x_sample.shape[1] == 1
    seq = x_prompt.shape[1]
    nb = x_sample.shape[0]
    assert seq % TB == 0 and nb % BB == 0 and nb == LANES

    yp, kp, vp, up, ys, kq, vq, pq = _fused_call(
        sinks[0], x_prompt[0], x_sample, norm_w, w_in[0], q_norm_w, k_norm_w, w_pool[0], pool_scale, w_out[0],
        _cache_in(cache_k[0]), _cache_in(cache_v[0]), jnp.transpose(state_pool[0], (1, 0, 2)))
    return (yp[None], ys, _cache_out(kp[None])[None], _cache_out(vp[None])[None],
            jnp.transpose(up, (1, 0, 2))[None],
            _cache_out(kq)[None], _cache_out(vq)[None], jnp.transpose(pq, (1, 0, 2))[None])
```

```python
import functools

import jax
import jax.numpy as jnp
from jax import lax
from jax.experimental import pallas as pl
from jax.experimental.pallas import tpu as pltpu

D_MODEL = 1024
HEAD_DIM = 64
N_HEADS = 8
N_KV_HEADS = 2
GROUP = 4
WINDOW = 128
BLOCK = 128
POOL_WINDOWS = (2, 4, 8, 16)
POOL_STATE = 15
D_ATTN = 512
D_POOL = 512
D_IN_PROJ = 2304
EPS = 1e-6
NEG_INF = -1e30
LOG2E = 1.4426950408889634

O_K = 512
O_V = 640
O_GA = 768
O_U = 1280
O_GP = 1792

LANES = 128
TB = 512
NSB = TB // BLOCK
BB = 16
CB = 4
QROWS = 16
WCHUNK = 256
VMEM_LIMIT = 58 * 1024 * 1024

F32 = jnp.float32
BF16 = jnp.bfloat16
_NT = (((1,), (1,)), ((), ()))


def _silu_half(h):
    return h + h * jnp.tanh(h)


def _lo_mask():
    return lax.broadcasted_iota(jnp.int32, (1, LANES), 1) < HEAD_DIM


def _pair_rms(zs, lo, w2):
    sq = zs * zs
    s_lo = jnp.sum(jnp.where(lo, sq, 0.0), axis=-1, keepdims=True)
    s_hi = jnp.sum(jnp.where(lo, 0.0, sq), axis=-1, keepdims=True)
    r = lax.rsqrt(jnp.where(lo, s_lo, s_hi) * (1.0 / HEAD_DIM) + EPS)
    return zs * r * w2


def _rms_bf16(x, nw):
    ms = jnp.mean(x * x, axis=-1, keepdims=True)
    return (x * lax.rsqrt(ms + EPS) * nw).astype(BF16)


def _project(x, nw, win):
    return jnp.dot(_rms_bf16(x, nw), win, preferred_element_type=F32)


def _weight_chunks():
    chunks = []
    for c0 in range(0, D_IN_PROJ, WCHUNK):
        gate = O_GA <= c0 < O_U or c0 >= O_GP
        chunks.append((0, c0, 0.5 if gate else 1.0))
    chunks += [(1, c0, 1.0) for c0 in range(0, D_MODEL, WCHUNK)]
    return chunks


def _load_weights(win_hbm, wout_hbm, wpool_f32, qn_ref, kn_ref, stage, sem, win_bf, wout_bf, wpool_bf, qw_ref, kw_ref):
    chunks = _weight_chunks()
    srcs, dsts = (win_hbm, wout_hbm), (win_bf, wout_bf)

    def copy(k):
        which, c0, _ = chunks[k]
        return pltpu.make_async_copy(srcs[which].at[:, pl.ds(c0, WCHUNK)], stage.at[k % 2], sem.at[k % 2])

    copy(0).start()
    copy(1).start()
    for k, (which, c0, scale) in enumerate(chunks):
        copy(k).wait()
        w = stage[k % 2]
        dsts[which][:, c0:c0 + WCHUNK] = (w if scale == 1.0 else scale * w).astype(BF16)
        if k + 2 < len(chunks):
            copy(k + 2).start()
    wpool_bf[...] = wpool_f32[...].astype(BF16)
    qn = qn_ref[...] * (HEAD_DIM ** -0.5 * LOG2E)
    qw_ref[...] = jnp.concatenate([qn, qn], axis=1)
    kw_ref[...] = jnp.concatenate([kn_ref[...], kn_ref[...]], axis=1)


def _prompt_init(sinks_ref, kbuf, vbuf, uext, fill_ref):
    kbuf[:, 0:BLOCK, :] = jnp.zeros((2, BLOCK, LANES), BF16)
    vbuf[:, 0:BLOCK, :] = jnp.zeros((2, BLOCK, LANES), BF16)
    uext[0:16, :] = jnp.zeros((16, D_POOL), F32)
    r0 = lax.broadcasted_iota(jnp.int32, (BLOCK, 2 * BLOCK), 0)
    c0 = lax.broadcasted_iota(jnp.int32, (BLOCK, 2 * BLOCK), 1)
    sink_col = jnp.where(r0 >= 1, r0 - 1, 2 * BLOCK - 1)
    for h in range(N_HEADS):
        fill_ref[h] = jnp.where(c0 == sink_col, sinks_ref[h] * LOG2E, NEG_INF)


def _prompt_step(i, last, x_ref, nw_ref, pscale_ref, y_ref, knew_ref, vnew_ref, unew_ref,
                 win_ref, wout_ref, wpool_ref, qw_ref, kw_ref, z_ref, qbuf, kbuf, vbuf, uext, mixed, fill_ref):
    lo = _lo_mask()
    z_ref[...] = _project(x_ref[...], nw_ref[...], win_ref[...])

    khat = _pair_rms(z_ref[:, O_K:O_K + LANES], lo, kw_ref[...])
    kr = pltpu.roll(khat, HEAD_DIM, axis=1)
    kbuf[0, BLOCK:, :] = jnp.where(lo, khat, kr).astype(BF16)
    kbuf[1, BLOCK:, :] = jnp.where(lo, kr, khat).astype(BF16)
    vz = z_ref[:, O_V:O_V + LANES]
    vr = pltpu.roll(vz, HEAD_DIM, axis=1)
    vbuf[0, BLOCK:, :] = jnp.where(lo, vz, vr).astype(BF16)
    vbuf[1, BLOCK:, :] = jnp.where(lo, vr, vz).astype(BF16)

    for j in range(4):
        qhat = _pair_rms(z_ref[:, j * LANES:(j + 1) * LANES], lo, qw_ref[...])
        q_lo = jnp.where(lo, qhat, 0.0).astype(BF16)
        q_hi = jnp.where(lo, 0.0, qhat).astype(BF16)
        c, g0 = j // 2, 2 * (j % 2)
        for sb in range(NSB):
            rows = slice(sb * BLOCK, (sb + 1) * BLOCK)
            qbuf[c, sb, g0 * BLOCK:(g0 + 1) * BLOCK, :] = q_lo[rows]
            qbuf[c, sb, (g0 + 1) * BLOCK:(g0 + 2) * BLOCK, :] = q_hi[rows]

    r_io = lax.broadcasted_iota(jnp.int32, (BLOCK, 2 * BLOCK), 0)
    c_io = lax.broadcasted_iota(jnp.int32, (BLOCK, 2 * BLOCK), 1)
    band = (c_io >= r_io) & (c_io <= r_io + WINDOW)
    first_lo = jnp.where(i > 0, 0, BLOCK)
    band_first = band & (c_io >= first_lo)
    sink_pos = c_io == jnp.where(r_io >= 1, r_io - 1, 2 * BLOCK - 1)

    for sb in range(NSB):
        valid = band_first if sb == 0 else band
        rows = slice(sb * BLOCK, (sb + 1) * BLOCK)
        for c in range(2):
            keys = kbuf[c, sb * BLOCK:(sb + 2) * BLOCK, :]
            vals = vbuf[c, sb * BLOCK:(sb + 2) * BLOCK, :]
            s = lax.dot_general(qbuf[c, sb], keys, _NT, preferred_element_type=F32)
            ps, inv = [], []
            for g in range(GROUP):
                sg = jnp.where(valid, s[g * BLOCK:(g + 1) * BLOCK, :], fill_ref[c * GROUP + g])
                m = jnp.max(sg, axis=-1, keepdims=True)
                p = jnp.exp2(sg - m)
                l = jnp.sum(p, axis=-1, keepdims=True)
                ps.append(jnp.where(sink_pos, 0.0, p).astype(BF16))
                inv.append(1.0 / l)
            o = jnp.dot(jnp.concatenate(ps, axis=0), vals, preferred_element_type=F32)
            og = [o[g * BLOCK:(g + 1) * BLOCK, :] * inv[g] for g in range(GROUP)]
            for jj in range(2):
                slab = jnp.where(lo, og[2 * jj], og[2 * jj + 1])
                col = (2 * c + jj) * LANES
                ga = z_ref[rows, O_GA + col:O_GA + col + LANES]
                mixed[rows, col:col + LANES] = (slab * _silu_half(ga)).astype(BF16)

    uext[16:, :] = z_ref[:, O_U:O_U + D_POOL]
    pos16 = i * TB + lax.broadcasted_iota(jnp.int32, (16, LANES), 0)
    for g, w in enumerate(POOL_WINDOWS):
        cols = slice(g * LANES, (g + 1) * LANES)
        acc = uext[:, cols]
        sh = 1
        while sh < w:
            acc = acc + pltpu.roll(acc, sh, axis=0)
            sh *= 2
        ug = z_ref[:, O_U + g * LANES:O_U + (g + 1) * LANES]
        cnt = jnp.minimum(w, pos16 + 1).astype(F32)
        d_first = acc[16:32, :] / cnt - ug[0:16, :]
        d_rest = acc[32:, :] * (1.0 / w) - ug[16:, :]
        d = jnp.concatenate([d_first, d_rest], axis=0).astype(BF16)
        po = jnp.dot(d, wpool_ref[g], preferred_element_type=F32) * pscale_ref[:, cols]
        gp = z_ref[:, O_GP + g * LANES:O_GP + (g + 1) * LANES]
        mixed[:, D_ATTN + g * LANES:D_ATTN + (g + 1) * LANES] = (po * _silu_half(gp)).astype(BF16)

    kbuf[:, 0:BLOCK, :] = kbuf[:, TB:TB + BLOCK, :]
    vbuf[:, 0:BLOCK, :] = vbuf[:, TB:TB + BLOCK, :]
    uext[0:16, :] = uext[TB:TB + 16, :]

    y_ref[...] = x_ref[...] + jnp.dot(mixed[...], wout_ref[...], preferred_element_type=F32)

    @pl.when(last)
    def _():
        unew_ref[:, 0, :] = z_ref[TB - POOL_STATE:TB, O_U:O_U + D_POOL]
        tail = slice(TB - WINDOW, TB)
        knew_ref[...] = _pair_rms(z_ref[tail, O_K:O_K + LANES], lo, kw_ref[...]).T
        vnew_ref[...] = z_ref[tail, O_V:O_V + LANES].T


def _sample_init(x_ref, nw_ref, win_ref, qw_ref, kw_ref, z_ref, q3, krep, vrep, snew, kvt):
    nb = x_ref.shape[0]
    lo = _lo_mask()
    z_ref[...] = _project(x_ref[:, 0, :], nw_ref[...], win_ref[...])
    khat = _pair_rms(z_ref[:, O_K:O_K + LANES], lo, kw_ref[...])
    vnew = z_ref[:, O_V:O_V + LANES]
    kvt[0] = khat.T
    kvt[1] = vnew.T
    q3[...] = jnp.zeros(q3.shape, F32)
    krep[...] = jnp.zeros(krep.shape, F32)
    vrep[...] = jnp.zeros(vrep.shape, F32)
    for j in range(4):
        qhat = _pair_rms(z_ref[:, j * LANES:(j + 1) * LANES], lo, qw_ref[...])
        qrot = pltpu.roll(qhat, HEAD_DIM, axis=1)
        grp_lo = j < 2
        for half in range(2):
            h = 2 * j + half
            src = qhat if (half == 0) == grp_lo else qrot
            q3[pl.ds(h, nb, stride=QROWS), :] = jnp.where(lo if grp_lo else ~lo, src, 0.0)
            krep[pl.ds(h, nb, stride=QROWS), :] = khat
            vrep[pl.ds(h, nb, stride=QROWS), :] = vnew
    s_new = jnp.sum(q3[...] * krep[...], axis=-1, keepdims=True)
    snew[...] = jnp.broadcast_to(s_new, snew.shape)


def _shift_caches(i, nb, ck_ref, cv_ref, knew_ref, vnew_ref, kvt):
    shift = lax.rem(nb - i * CB, nb)
    newest = lax.broadcasted_iota(jnp.int32, (LANES, WINDOW), 1) == WINDOW - 1
    for src, dst, which in ((ck_ref, knew_ref, 0), (cv_ref, vnew_ref, 1)):
        cols = pltpu.roll(kvt[which], shift, axis=1)
        shifted = pltpu.roll(src[...], WINDOW - 1, axis=2)
        for bb in range(CB):
            dst[bb] = jnp.where(newest, cols[:, bb:bb + 1], shifted[bb])


def _sample_step(j, last, sinks_ref, x_ref, pscale_ref, ck_ref, cv_ref, sp_ref, y_ref, spnew_ref,
                 wout_ref, wpool_ref, z_ref, q3, krep, vrep, snew, kvt, o3, mixed):
    nb = x_ref.shape[0]
    lo = _lo_mask()
    row0 = pl.multiple_of(j * BB, BB)
    rows = pl.ds(row0, BB)
    qrows = pl.ds(pl.multiple_of(j * (BB * QROWS), BB * QROWS), BB * QROWS)
    rid = lax.broadcasted_iota(jnp.int32, (1, QROWS, 1), 1)
    sink3 = jnp.zeros((1, QROWS, 1), F32)
    for h in range(N_HEADS):
        sink3 = jnp.where(rid == h, sinks_ref[h] * LOG2E, sink3)

    q = q3[qrows, :].reshape(BB, QROWS, LANES)
    s_new = snew[qrows, :].reshape(BB, QROWS, LANES)
    kt = ck_ref[...]
    vt = cv_ref[...]
    s = lax.dot_general(q.astype(BF16), kt.astype(BF16), (((2,), (1,)), ((0,), (0,))), preferred_element_type=F32)
    m = jnp.maximum(jnp.maximum(jnp.max(s, axis=-1, keepdims=True), s_new), sink3)
    p = jnp.exp2(s - m)
    p_new = jnp.exp2(s_new - m)
    l = jnp.sum(p, axis=-1, keepdims=True) + p_new + jnp.exp2(sink3 - m)
    o = lax.dot_general(p.astype(BF16), vt.astype(BF16), (((2,), (2,)), ((0,), (0,))), preferred_element_type=F32)
    o = (o + p_new * vrep[qrows, :].reshape(BB, QROWS, LANES)) / l
    o3[...] = o.reshape(BB * QROWS, LANES)

    for jj in range(4):
        grp_lo = jj < 2
        oh = [o3[pl.ds(2 * jj + half, BB, stride=QROWS), :] for half in range(2)]
        a = oh[0] if grp_lo else pltpu.roll(oh[0], HEAD_DIM, axis=1)
        b = pltpu.roll(oh[1], HEAD_DIM, axis=1) if grp_lo else oh[1]
        ga = z_ref[rows, O_GA + jj * LANES:O_GA + (jj + 1) * LANES]
        mixed[rows, jj * LANES:(jj + 1) * LANES] = (jnp.where(lo, a, b) * _silu_half(ga)).astype(BF16)

    u = z_ref[rows, O_U:O_U + D_POOL]
    spnew_ref[0:POOL_STATE - 1] = sp_ref[1:POOL_STATE]
    spnew_ref[POOL_STATE - 1] = u
    for g, w in enumerate(POOL_WINDOWS):
        cols = slice(g * LANES, (g + 1) * LANES)
        ug = u[:, cols]
        win_sum = ug
        for r in range(POOL_STATE - (w - 1), POOL_STATE):
            win_sum = win_sum + sp_ref[r, :, cols]
        d = (win_sum * (1.0 / w) - ug).astype(BF16)
        po = jnp.dot(d, wpool_ref[g], preferred_element_type=F32) * pscale_ref[:, cols]
        gp = z_ref[rows, O_GP + g * LANES:O_GP + (g + 1) * LANES]
        mixed[rows, D_ATTN + g * LANES:D_ATTN + (g + 1) * LANES] = (po * _silu_half(gp)).astype(BF16)

    @pl.when(last)
    def _():
        y_ref[:, 0, :] = x_ref[:, 0, :] + jnp.dot(mixed[...], wout_ref[...], preferred_element_type=F32)


def _fused_kernel(n_prompt, sinks_ref, xp_ref, xs_ref, nw_ref, win_hbm, qn_ref, kn_ref, wpool_f32, pscale_ref, wout_hbm,
                  ck_ref, cv_ref, sp_ref, ckc_ref, cvc_ref,
                  yp_ref, kp_ref, vp_ref, up_ref, ys_ref, kq_ref, vq_ref, pq_ref,
                  win_ref, wout_ref, wpool_ref, qw_ref, kw_ref, stage, sem,
                  z_ref, qbuf, kbuf, vbuf, uext, mixed, fill_ref,
                  zs_ref, q3, krep, vrep, snew, kvt, o3, mixed_s):
    i = pl.program_id(0)

    @pl.when(i == 0)
    def _():
        _load_weights(win_hbm, wout_hbm, wpool_f32, qn_ref, kn_ref, stage, sem, win_ref, wout_ref, wpool_ref,
                      qw_ref, kw_ref)
        _prompt_init(sinks_ref, kbuf, vbuf, uext, fill_ref)
        _sample_init(xs_ref, nw_ref, win_ref, qw_ref, kw_ref, zs_ref, q3, krep, vrep, snew, kvt)

    @pl.when(i < n_prompt)
    def _():
        _shift_caches(i, xs_ref.shape[0], ckc_ref, cvc_ref, kq_ref, vq_ref, kvt)
        _prompt_step(i, i == n_prompt - 1, xp_ref, nw_ref, pscale_ref, yp_ref, kp_ref, vp_ref, up_ref,
                     win_ref, wout_ref, wpool_ref, qw_ref, kw_ref, z_ref, qbuf, kbuf, vbuf, uext, mixed, fill_ref)

    @pl.when(i >= n_prompt)
    def _():
        _sample_step(i - n_prompt, i == pl.num_programs(0) - 1, sinks_ref, xs_ref, pscale_ref, ck_ref, cv_ref, sp_ref,
                     ys_ref, pq_ref, wout_ref, wpool_ref, zs_ref, q3, krep, vrep, snew, kvt, o3, mixed_s)


def _fused_call(sinks, xp, xs, nw, win, qn, kn, wpool, pscale, wout, ck, cv, sp):
    seq, nb = xp.shape[0], xs.shape[0]
    n_prompt, n_sample = seq // TB, nb // BB
    assert nb == CB * n_prompt
    const2 = lambda i: (0, 0)
    const3 = lambda i: (0, 0, 0)
    pblk = lambda i: (jnp.minimum(i, n_prompt - 1), 0)
    cblk = lambda i: (jnp.minimum(i, n_prompt - 1), 0, 0)
    sidx = lambda i: jnp.clip(i - n_prompt, 0, n_sample - 1)
    chunk3 = lambda i: (sidx(i), 0, 0)
    mid3 = lambda i: (0, sidx(i), 0)
    once = pl.Buffered(1)
    return pl.pallas_call(
        functools.partial(_fused_kernel, n_prompt),
        grid=(n_prompt + n_sample,),
        in_specs=[
            pl.BlockSpec(memory_space=pltpu.SMEM),
            pl.BlockSpec((TB, D_MODEL), pblk),
            pl.BlockSpec((nb, 1, D_MODEL), const3, pipeline_mode=once),
            pl.BlockSpec((1, D_MODEL), const2),
            pl.BlockSpec(memory_space=pl.ANY),
            pl.BlockSpec((1, HEAD_DIM), const2),
            pl.BlockSpec((1, HEAD_DIM), const2),
            pl.BlockSpec((4, LANES, LANES), const3, pipeline_mode=once),
            pl.BlockSpec((1, D_POOL), const2),
            pl.BlockSpec(memory_space=pl.ANY),
            pl.BlockSpec((BB, LANES, WINDOW), chunk3),
            pl.BlockSpec((BB, LANES, WINDOW), chunk3),
            pl.BlockSpec((POOL_STATE, BB, D_POOL), mid3),
            pl.BlockSpec((CB, LANES, WINDOW), cblk),
            pl.BlockSpec((CB, LANES, WINDOW), cblk),
        ],
        out_specs=[
            pl.BlockSpec((TB, D_MODEL), pblk),
            pl.BlockSpec((LANES, WINDOW), const2),
            pl.BlockSpec((LANES, WINDOW), const2),
            pl.BlockSpec((POOL_STATE, 1, D_POOL), const3),
            pl.BlockSpec((nb, 1, D_MODEL), const3, pipeline_mode=once),
            pl.BlockSpec((CB, LANES, WINDOW), cblk),
            pl.BlockSpec((CB, LANES, WINDOW), cblk),
            pl.BlockSpec((POOL_STATE, BB, D_POOL), mid3),
        ],
        out_shape=[
            jax.ShapeDtypeStruct((seq, D_MODEL), F32),
            jax.ShapeDtypeStruct((LANES, WINDOW), F32),
            jax.ShapeDtypeStruct((LANES, WINDOW), F32),
            jax.ShapeDtypeStruct((POOL_STATE, 1, D_POOL), F32),
            jax.ShapeDtypeStruct((nb, 1, D_MODEL), F32),
            jax.ShapeDtypeStruct((nb, LANES, WINDOW), F32),
            jax.ShapeDtypeStruct((nb, LANES, WINDOW), F32),
            jax.ShapeDtypeStruct((POOL_STATE, nb, D_POOL), F32),
        ],
        scratch_shapes=[
            pltpu.VMEM((D_MODEL, D_IN_PROJ), BF16),
            pltpu.VMEM((D_MODEL, D_MODEL), BF16),
            pltpu.VMEM((4, LANES, LANES), BF16),
            pltpu.VMEM((1, LANES), F32),
            pltpu.VMEM((1, LANES), F32),
            pltpu.VMEM((2, D_MODEL, WCHUNK), F32),
            pltpu.SemaphoreType.DMA((2,)),
            pltpu.VMEM((TB, D_IN_PROJ), F32),
            pltpu.VMEM((2, NSB, GROUP * BLOCK, LANES), BF16),
            pltpu.VMEM((2, TB + BLOCK, LANES), BF16),
            pltpu.VMEM((2, TB + BLOCK, LANES), BF16),
            pltpu.VMEM((TB + 16, D_POOL), F32),
            pltpu.VMEM((TB, D_MODEL), BF16),
            pltpu.VMEM((N_HEADS, BLOCK, 2 * BLOCK), F32),
            pltpu.VMEM((nb, D_IN_PROJ), F32),
            pltpu.VMEM((nb * QROWS, LANES), F32),
            pltpu.VMEM((nb * QROWS, LANES), F32),
            pltpu.VMEM((nb * QROWS, LANES), F32),
            pltpu.VMEM((nb * QROWS, LANES), F32),
            pltpu.VMEM((2, LANES, nb), F32),
            pltpu.VMEM((BB * QROWS, LANES), F32),
            pltpu.VMEM((nb, D_MODEL), BF16),
        ],
        compiler_params=pltpu.CompilerParams(dimension_semantics=("arbitrary",), vmem_limit_bytes=VMEM_LIMIT),
        name="hybrid_step",
    )(sinks, xp, xs, nw, win, qn, kn, wpool, pscale, wout, ck, cv, sp, ck, cv)


def _cache_in(c):
    nb = c.shape[0]
    return jnp.transpose(c, (0, 2, 3, 1)).reshape(nb, LANES, WINDOW)


def _cache_out(c):
    nb = c.shape[0]
    return jnp.transpose(c.reshape(nb, N_KV_HEADS, HEAD_DIM, WINDOW), (0, 3, 1, 2))


def kernel(x_prompt, x_sample, cache_k, cache_v, state_pool, norm_w, w_in, q_norm_w, k_norm_w, sinks, w_pool,
           pool_scale, w_out):
    depth = norm_w.shape[0]
    assert depth == 1 and x_prompt.shape[0] == 1 and x_sample.shape[1] == 1
    seq = x_prompt.shape[1]
    nb = x_sample.shape[0]
    assert seq % TB == 0 and nb % BB == 0 and nb == LANES

    yp, kp, vp, up, ys, kq, vq, pq = _fused_call(
        sinks[0], x_prompt[0], x_sample, norm_w, w_in[0], q_norm_w, k_norm_w, w_pool[0], pool_scale, w_out[0],
        _cache_in(cache_k[0]), _cache_in(cache_v[0]), jnp.transpose(state_pool[0], (1, 0, 2)))
    return (yp[None], ys, _cache_out(kp[None])[None], _cache_out(vp[None])[None],
            jnp.transpose(up, (1, 0, 2))[None],
            _cache_out(kq)[None], _cache_out(vq)[None], jnp.transpose(pq, (1, 0, 2))[None])
```

```python
import functools

import jax
import jax.numpy as jnp
from jax import lax
from jax.experimental import pallas as pl
from jax.experimental.pallas import tpu as pltpu

D_MODEL = 1024
HEAD_DIM = 64
N_HEADS = 8
N_KV_HEADS = 2
GROUP = 4
WINDOW = 128
BLOCK = 128
POOL_WINDOWS = (2, 4, 8, 16)
POOL_STATE = 15
D_ATTN = 512
D_POOL = 512
D_IN_PROJ = 2304
EPS = 1e-6
NEG_INF = -1e30
LOG2E = 1.4426950408889634

O_K = 512
O_V = 640
O_GA = 768
O_U = 1280
O_GP = 1792

LANES = 128
TB = 512
NSB = TB // BLOCK
BB = 16
CB = 4
QROWS = 16
WCHUNK = 256
VMEM_LIMIT = 58 * 1024 * 1024

F32 = jnp.float32
BF16 = jnp.bfloat16
_NT = (((1,), (1,)), ((), ()))


def _silu_half(h):
    return h + h * jnp.tanh(h)


def _lo_mask():
    return lax.broadcasted_iota(jnp.int32, (1, LANES), 1) < HEAD_DIM


def _pair_rms(zs, lo, w2=None):
    sq = zs * zs
    s_lo = jnp.sum(jnp.where(lo, sq, 0.0), axis=-1, keepdims=True)
    s_hi = jnp.sum(jnp.where(lo, 0.0, sq), axis=-1, keepdims=True)
    r = lax.rsqrt(jnp.where(lo, s_lo, s_hi) * (1.0 / HEAD_DIM) + EPS)
    return zs * r if w2 is None else zs * r * w2


def _project(x, win):
    ms = jnp.mean(x * x, axis=-1, keepdims=True)
    return jnp.dot((x * lax.rsqrt(ms + EPS)).astype(BF16), win, preferred_element_type=F32)


def _weight_chunks():
    chunks = []
    for c0 in range(0, D_IN_PROJ, WCHUNK):
        gate = O_GA <= c0 < O_U or c0 >= O_GP
        chunks.append((0, c0, 0.5 if gate else 1.0))
    chunks += [(1, c0, 1.0) for c0 in range(0, D_MODEL, WCHUNK)]
    return chunks


def _load_weights(win_hbm, wout_hbm, wpool_f32, nw_ref, pscale_ref, qn_ref, kn_ref, stage, sem,
                  win_bf, wout_bf, wpool_bf, qw_ref, kw_ref):
    chunks = _weight_chunks()
    srcs, dsts = (win_hbm, wout_hbm), (win_bf, wout_bf)
    nw_tile = jnp.concatenate([jnp.broadcast_to(nw_ref[:, t * LANES:(t + 1) * LANES], (LANES, LANES)).T
                               for t in range(D_MODEL // LANES)], axis=0)
    nw_rows = jnp.concatenate([nw_tile] * (WCHUNK // LANES), axis=1)

    def copy(k):
        which, c0, _ = chunks[k]
        return pltpu.make_async_copy(srcs[which].at[:, pl.ds(c0, WCHUNK)], stage.at[k % 2], sem.at[k % 2])

    copy(0).start()
    copy(1).start()
    for k, (which, c0, scale) in enumerate(chunks):
        copy(k).wait()
        w = stage[k % 2]
        if which == 0:
            w = w * nw_rows if scale == 1.0 else w * (scale * nw_rows)
        dsts[which][:, c0:c0 + WCHUNK] = w.astype(BF16)
        if k + 2 < len(chunks):
            copy(k + 2).start()
    for g in range(len(POOL_WINDOWS)):
        wpool_bf[g] = (wpool_f32[g] * pscale_ref[:, g * LANES:(g + 1) * LANES]).astype(BF16)
    qn = qn_ref[...] * (HEAD_DIM ** -0.5 * LOG2E)
    qw_ref[...] = jnp.concatenate([qn, qn], axis=1)
    kw_ref[...] = jnp.concatenate([kn_ref[...], kn_ref[...]], axis=1)


def _prompt_init(sinks_ref, kbuf, vbuf, uext, fill_ref):
    kbuf[:, 0:BLOCK, :] = jnp.zeros((2, BLOCK, LANES), BF16)
    vbuf[:, 0:BLOCK, :] = jnp.zeros((2, BLOCK, LANES), BF16)
    uext[0:16, :] = jnp.zeros((16, D_POOL), F32)
    r0 = lax.broadcasted_iota(jnp.int32, (BLOCK, 2 * BLOCK), 0)
    c0 = lax.broadcasted_iota(jnp.int32, (BLOCK, 2 * BLOCK), 1)
    sink_col = jnp.where(r0 >= 1, r0 - 1, 2 * BLOCK - 1)
    for h in range(N_HEADS):
        fill_ref[h] = jnp.where(c0 == sink_col, sinks_ref[h] * LOG2E, NEG_INF)


def _prompt_step(i, last, x_ref, y_ref, knew_ref, vnew_ref, unew_ref,
                 win_ref, wout_ref, wpool_ref, qw_ref, kw_ref, z_ref, qbuf, kbuf, vbuf, uext, mixed, fill_ref):
    lo = _lo_mask()
    z_ref[...] = _project(x_ref[...], win_ref[...])

    khat = _pair_rms(z_ref[:, O_K:O_K + LANES], lo, kw_ref[...] * qw_ref[...])
    kr = pltpu.roll(khat, HEAD_DIM, axis=1)
    kbuf[0, BLOCK:, :] = jnp.where(lo, khat, kr).astype(BF16)
    kbuf[1, BLOCK:, :] = jnp.where(lo, kr, khat).astype(BF16)
    vz = z_ref[:, O_V:O_V + LANES]
    vr = pltpu.roll(vz, HEAD_DIM, axis=1)
    vbuf[0, BLOCK:, :] = jnp.where(lo, vz, vr).astype(BF16)
    vbuf[1, BLOCK:, :] = jnp.where(lo, vr, vz).astype(BF16)

    for j in range(4):
        qhat = _pair_rms(z_ref[:, j * LANES:(j + 1) * LANES], lo)
        q_lo = jnp.where(lo, qhat, 0.0).astype(BF16)
        q_hi = jnp.where(lo, 0.0, qhat).astype(BF16)
        c, g0 = j // 2, 2 * (j % 2)
        for sb in range(NSB):
            rows = slice(sb * BLOCK, (sb + 1) * BLOCK)
            qbuf[c, sb, g0 * BLOCK:(g0 + 1) * BLOCK, :] = q_lo[rows]
            qbuf[c, sb, (g0 + 1) * BLOCK:(g0 + 2) * BLOCK, :] = q_hi[rows]

    r_io = lax.broadcasted_iota(jnp.int32, (BLOCK, 2 * BLOCK), 0)
    c_io = lax.broadcasted_iota(jnp.int32, (BLOCK, 2 * BLOCK), 1)
    band = (c_io >= r_io) & (c_io <= r_io + WINDOW)
    first_lo = jnp.where(i > 0, 0, BLOCK)
    band_first = band & (c_io >= first_lo)
    sink_pos = c_io == jnp.where(r_io >= 1, r_io - 1, 2 * BLOCK - 1)

    for sb in range(NSB):
        valid = band_first if sb == 0 else band
        rows = slice(sb * BLOCK, (sb + 1) * BLOCK)
        for c in range(2):
            keys = kbuf[c, sb * BLOCK:(sb + 2) * BLOCK, :]
            vals = vbuf[c, sb * BLOCK:(sb + 2) * BLOCK, :]
            s = lax.dot_general(qbuf[c, sb], keys, _NT, preferred_element_type=F32)
            ps, inv = [], []
            for g in range(GROUP):
                sg = jnp.where(valid, s[g * BLOCK:(g + 1) * BLOCK, :], fill_ref[c * GROUP + g])
                m = jnp.max(sg, axis=-1, keepdims=True)
                p = jnp.exp2(sg - m)
                l = jnp.sum(p, axis=-1, keepdims=True)
                ps.append(jnp.where(sink_pos, 0.0, p).astype(BF16))
                inv.append(1.0 / l)
            o = jnp.dot(jnp.concatenate(ps, axis=0), vals, preferred_element_type=F32)
            og = [o[g * BLOCK:(g + 1) * BLOCK, :] * inv[g] for g in range(GROUP)]
            for jj in range(2):
                slab = jnp.where(lo, og[2 * jj], og[2 * jj + 1])
                col = (2 * c + jj) * LANES
                ga = z_ref[rows, O_GA + col:O_GA + col + LANES]
                mixed[rows, col:col + LANES] = (slab * _silu_half(ga)).astype(BF16)

    uext[16:, :] = z_ref[:, O_U:O_U + D_POOL]
    pos16 = i * TB + lax.broadcasted_iota(jnp.int32, (16, LANES), 0)
    for g, w in enumerate(POOL_WINDOWS):
        cols = slice(g * LANES, (g + 1) * LANES)
        acc = uext[:, cols]
        sh = 1
        while sh < w:
            acc = acc + pltpu.roll(acc, sh, axis=0)
            sh *= 2
        ug = z_ref[:, O_U + g * LANES:O_U + (g + 1) * LANES]
        cnt = jnp.minimum(w, pos16 + 1).astype(F32)
        d_first = acc[16:32, :] / cnt - ug[0:16, :]
        d_rest = acc[32:, :] * (1.0 / w) - ug[16:, :]
        d = jnp.concatenate([d_first, d_rest], axis=0).astype(BF16)
        po = jnp.dot(d, wpool_ref[g], preferred_element_type=F32)
        gp = z_ref[:, O_GP + g * LANES:O_GP + (g + 1) * LANES]
        mixed[:, D_ATTN + g * LANES:D_ATTN + (g + 1) * LANES] = (po * _silu_half(gp)).astype(BF16)

    kbuf[:, 0:BLOCK, :] = kbuf[:, TB:TB + BLOCK, :]
    vbuf[:, 0:BLOCK, :] = vbuf[:, TB:TB + BLOCK, :]
    uext[0:16, :] = uext[TB:TB + 16, :]

    y_ref[...] = x_ref[...] + jnp.dot(mixed[...], wout_ref[...], preferred_element_type=F32)

    @pl.when(last)
    def _():
        unew_ref[:, 0, :] = z_ref[TB - POOL_STATE:TB, O_U:O_U + D_POOL]
        tail = slice(TB - WINDOW, TB)
        knew_ref[...] = _pair_rms(z_ref[tail, O_K:O_K + LANES], lo, kw_ref[...]).T
        vnew_ref[...] = z_ref[tail, O_V:O_V + LANES].T


def _sample_init(x_ref, win_ref, qw_ref, kw_ref, z_ref, q3, krep, vrep, snew, kvt):
    nb = x_ref.shape[0]
    lo = _lo_mask()
    z_ref[...] = _project(x_ref[:, 0, :], win_ref[...])
    khat = _pair_rms(z_ref[:, O_K:O_K + LANES], lo, kw_ref[...])
    vnew = z_ref[:, O_V:O_V + LANES]
    kvt[0] = khat.T
    kvt[1] = vnew.T
    q3[...] = jnp.zeros(q3.shape, F32)
    krep[...] = jnp.zeros(krep.shape, F32)
    vrep[...] = jnp.zeros(vrep.shape, F32)
    for j in range(4):
        qhat = _pair_rms(z_ref[:, j * LANES:(j + 1) * LANES], lo, qw_ref[...])
        qrot = pltpu.roll(qhat, HEAD_DIM, axis=1)
        grp_lo = j < 2
        for half in range(2):
            h = 2 * j + half
            src = qhat if (half == 0) == grp_lo else qrot
            q3[pl.ds(h, nb, stride=QROWS), :] = jnp.where(lo if grp_lo else ~lo, src, 0.0)
            krep[pl.ds(h, nb, stride=QROWS), :] = khat
            vrep[pl.ds(h, nb, stride=QROWS), :] = vnew
    s_new = jnp.sum(q3[...] * krep[...], axis=-1, keepdims=True)
    snew[...] = jnp.broadcast_to(s_new, snew.shape)


def _shift_caches(i, nb, ck_ref, cv_ref, knew_ref, vnew_ref, kvt):
    shift = lax.rem(nb - i * CB, nb)
    newest = lax.broadcasted_iota(jnp.int32, (LANES, WINDOW), 1) == WINDOW - 1
    for src, dst, which in ((ck_ref, knew_ref, 0), (cv_ref, vnew_ref, 1)):
        cols = pltpu.roll(kvt[which], shift, axis=1)
        shifted = pltpu.roll(src[...], WINDOW - 1, axis=2)
        for bb in range(CB):
            dst[bb] = jnp.where(newest, cols[:, bb:bb + 1], shifted[bb])


def _sample_step(j, last, sinks_ref, x_ref, ck_ref, cv_ref, sp_ref, y_ref, spnew_ref,
                 wout_ref, wpool_ref, z_ref, q3, krep, vrep, snew, kvt, o3, mixed):
    nb = x_ref.shape[0]
    lo = _lo_mask()
    row0 = pl.multiple_of(j * BB, BB)
    rows = pl.ds(row0, BB)
    qrows = pl.ds(pl.multiple_of(j * (BB * QROWS), BB * QROWS), BB * QROWS)
    rid = lax.broadcasted_iota(jnp.int32, (1, QROWS, 1), 1)
    sink3 = jnp.zeros((1, QROWS, 1), F32)
    for h in range(N_HEADS):
        sink3 = jnp.where(rid == h, sinks_ref[h] * LOG2E, sink3)

    q = q3[qrows, :].reshape(BB, QROWS, LANES)
    s_new = snew[qrows, :].reshape(BB, QROWS, LANES)
    kt = ck_ref[...]
    vt = cv_ref[...]
    s = lax.dot_general(q.astype(BF16), kt.astype(BF16), (((2,), (1,)), ((0,), (0,))), preferred_element_type=F32)
    m = jnp.maximum(jnp.maximum(jnp.max(s, axis=-1, keepdims=True), s_new), sink3)
    p = jnp.exp2(s - m)
    p_new = jnp.exp2(s_new - m)
    l = jnp.sum(p, axis=-1, keepdims=True) + p_new + jnp.exp2(sink3 - m)
    o = lax.dot_general(p.astype(BF16), vt.astype(BF16), (((2,), (2,)), ((0,), (0,))), preferred_element_type=F32)
    o = (o + p_new * vrep[qrows, :].reshape(BB, QROWS, LANES)) / l
    o3[...] = o.reshape(BB * QROWS, LANES)

    for jj in range(4):
        grp_lo = jj < 2
        oh = [o3[pl.ds(2 * jj + half, BB, stride=QROWS), :] for half in range(2)]
        a = oh[0] if grp_lo else pltpu.roll(oh[0], HEAD_DIM, axis=1)
        b = pltpu.roll(oh[1], HEAD_DIM, axis=1) if grp_lo else oh[1]
        ga = z_ref[rows, O_GA + jj * LANES:O_GA + (jj + 1) * LANES]
        mixed[rows, jj * LANES:(jj + 1) * LANES] = (jnp.where(lo, a, b) * _silu_half(ga)).astype(BF16)

    u = z_ref[rows, O_U:O_U + D_POOL]
    spnew_ref[0:POOL_STATE - 1] = sp_ref[1:POOL_STATE]
    spnew_ref[POOL_STATE - 1] = u
    for g, w in enumerate(POOL_WINDOWS):
        cols = slice(g * LANES, (g + 1) * LANES)
        ug = u[:, cols]
        win_sum = ug
        for r in range(POOL_STATE - (w - 1), POOL_STATE):
            win_sum = win_sum + sp_ref[r, :, cols]
        d = (win_sum * (1.0 / w) - ug).astype(BF16)
        po = jnp.dot(d, wpool_ref[g], preferred_element_type=F32)
        gp = z_ref[rows, O_GP + g * LANES:O_GP + (g + 1) * LANES]
        mixed[rows, D_ATTN + g * LANES:D_ATTN + (g + 1) * LANES] = (po * _silu_half(gp)).astype(BF16)

    @pl.when(last)
    def _():
        y_ref[:, 0, :] = x_ref[:, 0, :] + jnp.dot(mixed[...], wout_ref[...], preferred_element_type=F32)


def _fused_kernel(n_prompt, sinks_ref, xp_ref, xs_ref, nw_ref, win_hbm, qn_ref, kn_ref, wpool_f32, pscale_ref, wout_hbm,
                  ck_ref, cv_ref, sp_ref, ckc_ref, cvc_ref,
                  yp_ref, kp_ref, vp_ref, up_ref, ys_ref, kq_ref, vq_ref, pq_ref,
                  win_ref, wout_ref, wpool_ref, qw_ref, kw_ref, stage, sem,
                  z_ref, qbuf, kbuf, vbuf, uext, mixed, fill_ref,
                  zs_ref, q3, krep, vrep, snew, kvt, o3, mixed_s):
    i = pl.program_id(0)

    @pl.when(i == 0)
    def _():
        _load_weights(win_hbm, wout_hbm, wpool_f32, nw_ref, pscale_ref, qn_ref, kn_ref, stage, sem,
                      win_ref, wout_ref, wpool_ref, qw_ref, kw_ref)
        _prompt_init(sinks_ref, kbuf, vbuf, uext, fill_ref)
        _sample_init(xs_ref, win_ref, qw_ref, kw_ref, zs_ref, q3, krep, vrep, snew, kvt)

    @pl.when(i < n_prompt)
    def _():
        _shift_caches(i, xs_ref.shape[0], ckc_ref, cvc_ref, kq_ref, vq_ref, kvt)
        _prompt_step(i, i == n_prompt - 1, xp_ref, yp_ref, kp_ref, vp_ref, up_ref,
                     win_ref, wout_ref, wpool_ref, qw_ref, kw_ref, z_ref, qbuf, kbuf, vbuf, uext, mixed, fill_ref)

    @pl.when(i >= n_prompt)
    def _():
        _sample_step(i - n_prompt, i == pl.num_programs(0) - 1, sinks_ref, xs_ref, ck_ref, cv_ref, sp_ref,
                     ys_ref, pq_ref, wout_ref, wpool_ref, zs_ref, q3, krep, vrep, snew, kvt, o3, mixed_s)


def _fused_call(sinks, xp, xs, nw, win, qn, kn, wpool, pscale, wout, ck, cv, sp):
    seq, nb = xp.shape[0], xs.shape[0]
    n_prompt, n_sample = seq // TB, nb // BB
    assert nb == CB * n_prompt
    const2 = lambda i: (0, 0)
    const3 = lambda i: (0, 0, 0)
    pblk = lambda i: (jnp.minimum(i, n_prompt - 1), 0)
    cblk = lambda i: (jnp.minimum(i, n_prompt - 1), 0, 0)
    sidx = lambda i: jnp.clip(i - n_prompt, 0, n_sample - 1)
    chunk3 = lambda i: (sidx(i), 0, 0)
    mid3 = lambda i: (0, sidx(i), 0)
    once = pl.Buffered(1)
    return pl.pallas_call(
        functools.partial(_fused_kernel, n_prompt),
        grid=(n_prompt + n_sample,),
        in_specs=[
            pl.BlockSpec(memory_space=pltpu.SMEM),
            pl.BlockSpec((TB, D_MODEL), pblk),
            pl.BlockSpec((nb, 1, D_MODEL), const3, pipeline_mode=once),
            pl.BlockSpec((1, D_MODEL), const2),
            pl.BlockSpec(memory_space=pl.ANY),
            pl.BlockSpec((1, HEAD_DIM), const2),
            pl.BlockSpec((1, HEAD_DIM), const2),
            pl.BlockSpec((4, LANES, LANES), const3, pipeline_mode=once),
            pl.BlockSpec((1, D_POOL), const2),
            pl.BlockSpec(memory_space=pl.ANY),
            pl.BlockSpec((BB, LANES, WINDOW), chunk3),
            pl.BlockSpec((BB, LANES, WINDOW), chunk3),
            pl.BlockSpec((POOL_STATE, BB, D_POOL), mid3),
            pl.BlockSpec((CB, LANES, WINDOW), cblk),
            pl.BlockSpec((CB, LANES, WINDOW), cblk),
        ],
        out_specs=[
            pl.BlockSpec((TB, D_MODEL), pblk),
            pl.BlockSpec((LANES, WINDOW), const2),
            pl.BlockSpec((LANES, WINDOW), const2),
            pl.BlockSpec((POOL_STATE, 1, D_POOL), const3),
            pl.BlockSpec((nb, 1, D_MODEL), const3, pipeline_mode=once),
            pl.BlockSpec((CB, LANES, WINDOW), cblk),
            pl.BlockSpec((CB, LANES, WINDOW), cblk),
            pl.BlockSpec((POOL_STATE, BB, D_POOL), mid3),
        ],
        out_shape=[
            jax.ShapeDtypeStruct((seq, D_MODEL), F32),
            jax.ShapeDtypeStruct((LANES, WINDOW), F32),
            jax.ShapeDtypeStruct((LANES, WINDOW), F32),
            jax.ShapeDtypeStruct((POOL_STATE, 1, D_POOL), F32),
            jax.ShapeDtypeStruct((nb, 1, D_MODEL), F32),
            jax.ShapeDtypeStruct((nb, LANES, WINDOW), F32),
            jax.ShapeDtypeStruct((nb, LANES, WINDOW), F32),
            jax.ShapeDtypeStruct((POOL_STATE, nb, D_POOL), F32),
        ],
        scratch_shapes=[
            pltpu.VMEM((D_MODEL, D_IN_PROJ), BF16),
            pltpu.VMEM((D_MODEL, D_MODEL), BF16),
            pltpu.VMEM((4, LANES, LANES), BF16),
            pltpu.VMEM((1, LANES), F32),
            pltpu.VMEM((1, LANES), F32),
            pltpu.VMEM((2, D_MODEL, WCHUNK), F32),
            pltpu.SemaphoreType.DMA((2,)),
            pltpu.VMEM((TB, D_IN_PROJ), F32),
            pltpu.VMEM((2, NSB, GROUP * BLOCK, LANES), BF16),
            pltpu.VMEM((2, TB + BLOCK, LANES), BF16),
            pltpu.VMEM((2, TB + BLOCK, LANES), BF16),
            pltpu.VMEM((TB + 16, D_POOL), F32),
            pltpu.VMEM((TB, D_MODEL), BF16),
            pltpu.VMEM((N_HEADS, BLOCK, 2 * BLOCK), F32),
            pltpu.VMEM((nb, D_IN_PROJ), F32),
            pltpu.VMEM((nb * QROWS, LANES), F32),
            pltpu.VMEM((nb * QROWS, LANES), F32),
            pltpu.VMEM((nb * QROWS, LANES), F32),
            pltpu.VMEM((nb * QROWS, LANES), F32),
            pltpu.VMEM((2, LANES, nb), F32),
            pltpu.VMEM((BB * QROWS, LANES), F32),
            pltpu.VMEM((nb, D_MODEL), BF16),
        ],
        compiler_params=pltpu.CompilerParams(dimension_semantics=("arbitrary",), vmem_limit_bytes=VMEM_LIMIT),
        name="hybrid_step",
    )(sinks, xp, xs, nw, win, qn, kn, wpool, pscale, wout, ck, cv, sp, ck, cv)


def _cache_in(c):
    nb = c.shape[0]
    return jnp.transpose(c, (0, 2, 3, 1)).reshape(nb, LANES, WINDOW)


def _cache_out(c):
    nb = c.shape[0]
    return jnp.transpose(c.reshape(nb, N_KV_HEADS, HEAD_DIM, WINDOW), (0, 3, 1, 2))


def kernel(x_prompt, x_sample, cache_k, cache_v, state_pool, norm_w, w_in, q_norm_w, k_norm_w, sinks, w_pool,
           pool_scale, w_out):
    depth = norm_w.shape[0]
    assert depth == 1 and x_prompt.shape[0] == 1 and x_sample.shape[1] == 1
    seq = x_prompt.shape[1]
    nb = x_sample.shape[0]
    assert seq % TB == 0 and nb % BB == 0 and nb == LANES

    yp, kp, vp, up, ys, kq, vq, pq = _fused_call(
        sinks[0], x_prompt[0], x_sample, norm_w, w_in[0], q_norm_w, k_norm_w, w_pool[0], pool_scale, w_out[0],
        _cache_in(cache_k[0]), _cache_in(cache_v[0]), jnp.transpose(state_pool[0], (1, 0, 2)))
    return (yp[None], ys, _cache_out(kp[None])[None], _cache_out(vp[None])[None],
            jnp.transpose(up, (1, 0, 2))[None],
            _cache_out(kq)[None], _cache_out(vq)[None], jnp.transpose(pq, (1, 0, 2))[None])
```

```python
import functools

import jax
import jax.numpy as jnp
from jax import lax
from jax.experimental import pallas as pl
from jax.experimental.pallas import tpu as pltpu

D_MODEL = 1024
HEAD_DIM = 64
N_HEADS = 8
N_KV_HEADS = 2
GROUP = 4
WINDOW = 128
BLOCK = 128
POOL_WINDOWS = (2, 4, 8, 16)
POOL_STATE = 15
D_ATTN = 512
D_POOL = 512
D_IN_PROJ = 2304
EPS = 1e-6
NEG_INF = -1e30
LOG2E = 1.4426950408889634

O_K = 512
O_V = 640
O_GA = 768
O_U = 1280
O_GP = 1792

LANES = 128
TB = 512
NSB = TB // BLOCK
BB = 16
CB = 4
QROWS = 16
WCHUNK = 256
VMEM_LIMIT = 58 * 1024 * 1024

F32 = jnp.float32
BF16 = jnp.bfloat16
_NT = (((1,), (1,)), ((), ()))


def _silu_half(h):
    return h + h * jnp.tanh(h)


def _lo_mask():
    return lax.broadcasted_iota(jnp.int32, (1, LANES), 1) < HEAD_DIM


def _pair_rms(zs, lo, w2):
    sq = zs * zs
    s_lo = jnp.sum(jnp.where(lo, sq, 0.0), axis=-1, keepdims=True)
    s_hi = jnp.sum(jnp.where(lo, 0.0, sq), axis=-1, keepdims=True)
    r = lax.rsqrt(jnp.where(lo, s_lo, s_hi) * (1.0 / HEAD_DIM) + EPS)
    return zs * r * w2


def _rms_bf16(x, nw):
    ms = jnp.mean(x * x, axis=-1, keepdims=True)
    return (x * lax.rsqrt(ms + EPS) * nw).astype(BF16)


def _project(x, nw, win):
    return jnp.dot(_rms_bf16(x, nw), win, preferred_element_type=F32)


def _weight_chunks():
    chunks = []
    for c0 in range(0, D_IN_PROJ, WCHUNK):
        gate = O_GA <= c0 < O_U or c0 >= O_GP
        chunks.append((0, c0, 0.5 if gate else 1.0))
    chunks += [(1, c0, 1.0) for c0 in range(0, D_MODEL, WCHUNK)]
    return chunks


def _load_weights(win_hbm, wout_hbm, wpool_f32, qn_ref, kn_ref, stage, sem, win_bf, wout_bf, wpool_bf, qw_ref, kw_ref):
    chunks = _weight_chunks()
    srcs, dsts = (win_hbm, wout_hbm), (win_bf, wout_bf)

    def copy(k):
        which, c0, _ = chunks[k]
        return pltpu.make_async_copy(srcs[which].at[:, pl.ds(c0, WCHUNK)], stage.at[k % 2], sem.at[k % 2])

    copy(0).start()
    copy(1).start()
    for k, (which, c0, scale) in enumerate(chunks):
        copy(k).wait()
        w = stage[k % 2]
        dsts[which][:, c0:c0 + WCHUNK] = (w if scale == 1.0 else scale * w).astype(BF16)
        if k + 2 < len(chunks):
            copy(k + 2).start()
    wpool_bf[...] = jnp.zeros(wpool_bf.shape, BF16)
    for g in range(len(POOL_WINDOWS)):
        d0 = (g % 2) * LANES
        wpool_bf[g // 2, d0:d0 + LANES, d0:d0 + LANES] = wpool_f32[g].astype(BF16)
    qn = qn_ref[...] * (HEAD_DIM ** -0.5 * LOG2E)
    qw_ref[...] = jnp.concatenate([qn, qn], axis=1)
    kw_ref[...] = jnp.concatenate([kn_ref[...], kn_ref[...]], axis=1)


def _prompt_init(sinks_ref, kbuf, vbuf, uext, fill_ref):
    kbuf[:, 0:BLOCK, :] = jnp.zeros((2, BLOCK, LANES), BF16)
    vbuf[:, 0:BLOCK, :] = jnp.zeros((2, BLOCK, LANES), BF16)
    uext[0:16, :] = jnp.zeros((16, D_POOL), F32)
    r0 = lax.broadcasted_iota(jnp.int32, (BLOCK, 2 * BLOCK), 0)
    c0 = lax.broadcasted_iota(jnp.int32, (BLOCK, 2 * BLOCK), 1)
    sink_col = jnp.where(r0 >= 1, r0 - 1, 2 * BLOCK - 1)
    for h in range(N_HEADS):
        fill_ref[h] = jnp.where(c0 == sink_col, sinks_ref[h] * LOG2E, NEG_INF)


def _prompt_step(i, last, x_ref, nw_ref, pscale_ref, y_ref, knew_ref, vnew_ref, unew_ref,
                 win_ref, wout_ref, wpool_ref, qw_ref, kw_ref, z_ref, hbuf, qbuf, kbuf, vbuf, uext, mixed, fill_ref):
    lo = _lo_mask()
    hbuf[...] = _rms_bf16(x_ref[...], nw_ref[...])

    z_ref[:, O_U:] = jnp.dot(hbuf[...], win_ref[:, O_U:], preferred_element_type=F32)
    uext[16:, :] = z_ref[:, O_U:O_U + D_POOL]
    pos16 = i * TB + lax.broadcasted_iota(jnp.int32, (16, LANES), 0)
    ds = []
    for g, w in enumerate(POOL_WINDOWS):
        acc = uext[:, g * LANES:(g + 1) * LANES]
        sh = 1
        while sh < w:
            acc = acc + pltpu.roll(acc, sh, axis=0)
            sh *= 2
        ug = z_ref[:, O_U + g * LANES:O_U + (g + 1) * LANES]
        cnt = jnp.minimum(w, pos16 + 1).astype(F32)
        d_first = acc[16:32, :] / cnt - ug[0:16, :]
        d_rest = acc[32:, :] * (1.0 / w) - ug[16:, :]
        ds.append(jnp.concatenate([d_first, d_rest], axis=0).astype(BF16))
    uext[0:16, :] = uext[TB:TB + 16, :]

    z_ref[:, :O_K] = jnp.dot(hbuf[...], win_ref[:, :O_K], preferred_element_type=F32)

    for pr in range(len(POOL_WINDOWS) // 2):
        cols = slice(pr * 2 * LANES, (pr + 1) * 2 * LANES)
        d2 = jnp.concatenate(ds[2 * pr:2 * pr + 2], axis=1)
        po = jnp.dot(d2, wpool_ref[pr], preferred_element_type=F32) * pscale_ref[:, cols]
        gp = z_ref[:, O_GP + pr * 2 * LANES:O_GP + (pr + 1) * 2 * LANES]
        mixed[:, D_ATTN + pr * 2 * LANES:D_ATTN + (pr + 1) * 2 * LANES] = (po * _silu_half(gp)).astype(BF16)

    for j in range(4):
        qhat = _pair_rms(z_ref[:, j * LANES:(j + 1) * LANES], lo, qw_ref[...])
        q_lo = jnp.where(lo, qhat, 0.0).astype(BF16)
        q_hi = jnp.where(lo, 0.0, qhat).astype(BF16)
        c, g0 = j // 2, 2 * (j % 2)
        for sb in range(NSB):
            rows = slice(sb * BLOCK, (sb + 1) * BLOCK)
            qbuf[c, sb, g0 * BLOCK:(g0 + 1) * BLOCK, :] = q_lo[rows]
            qbuf[c, sb, (g0 + 1) * BLOCK:(g0 + 2) * BLOCK, :] = q_hi[rows]

    half = TB // 2
    for r0 in (0, half):
        z_ref[r0:r0 + half, O_K:O_GA] = jnp.dot(hbuf[r0:r0 + half, :], win_ref[:, O_K:O_GA],
                                                preferred_element_type=F32)
    z_ref[:, O_GA:O_U] = jnp.dot(hbuf[...], win_ref[:, O_GA:O_U], preferred_element_type=F32)

    khat = _pair_rms(z_ref[:, O_K:O_K + LANES], lo, kw_ref[...])
    kr = pltpu.roll(khat, HEAD_DIM, axis=1)
    kbuf[0, BLOCK:, :] = jnp.where(lo, khat, kr).astype(BF16)
    kbuf[1, BLOCK:, :] = jnp.where(lo, kr, khat).astype(BF16)
    vz = z_ref[:, O_V:O_V + LANES]
    vr = pltpu.roll(vz, HEAD_DIM, axis=1)
    vbuf[0, BLOCK:, :] = jnp.where(lo, vz, vr).astype(BF16)
    vbuf[1, BLOCK:, :] = jnp.where(lo, vr, vz).astype(BF16)

    r_io = lax.broadcasted_iota(jnp.int32, (BLOCK, 2 * BLOCK), 0)
    c_io = lax.broadcasted_iota(jnp.int32, (BLOCK, 2 * BLOCK), 1)
    band = (c_io >= r_io) & (c_io <= r_io + WINDOW)
    first_lo = jnp.where(i > 0, 0, BLOCK)
    band_first = band & (c_io >= first_lo)
    sink_pos = c_io == jnp.where(r_io >= 1, r_io - 1, 2 * BLOCK - 1)

    for sb in range(NSB):
        valid = band_first if sb == 0 else band
        rows = slice(sb * BLOCK, (sb + 1) * BLOCK)
        for c in range(2):
            keys = kbuf[c, sb * BLOCK:(sb + 2) * BLOCK, :]
            vals = vbuf[c, sb * BLOCK:(sb + 2) * BLOCK, :]
            s = lax.dot_general(qbuf[c, sb], keys, _NT, preferred_element_type=F32)
            ps, inv = [], []
            for g in range(GROUP):
                sg = jnp.where(valid, s[g * BLOCK:(g + 1) * BLOCK, :], fill_ref[c * GROUP + g])
                m = jnp.max(sg, axis=-1, keepdims=True)
                p = jnp.exp2(sg - m)
                l = jnp.sum(p, axis=-1, keepdims=True)
                ps.append(jnp.where(sink_pos, 0.0, p).astype(BF16))
                inv.append(1.0 / l)
            o = jnp.dot(jnp.concatenate(ps, axis=0), vals, preferred_element_type=F32)
            og = [o[g * BLOCK:(g + 1) * BLOCK, :] * inv[g] for g in range(GROUP)]
            for jj in range(2):
                slab = jnp.where(lo, og[2 * jj], og[2 * jj + 1])
                col = (2 * c + jj) * LANES
                ga = z_ref[rows, O_GA + col:O_GA + col + LANES]
                mixed[rows, col:col + LANES] = (slab * _silu_half(ga)).astype(BF16)

    kbuf[:, 0:BLOCK, :] = kbuf[:, TB:TB + BLOCK, :]
    vbuf[:, 0:BLOCK, :] = vbuf[:, TB:TB + BLOCK, :]

    y_ref[...] = x_ref[...] + jnp.dot(mixed[...], wout_ref[...], preferred_element_type=F32)

    @pl.when(last)
    def _():
        unew_ref[:, 0, :] = z_ref[TB - POOL_STATE:TB, O_U:O_U + D_POOL]
        tail = slice(TB - WINDOW, TB)
        knew_ref[...] = _pair_rms(z_ref[tail, O_K:O_K + LANES], lo, kw_ref[...]).T
        vnew_ref[...] = z_ref[tail, O_V:O_V + LANES].T


def _sample_init(x_ref, nw_ref, win_ref, qw_ref, kw_ref, z_ref, q3, krep, vrep, snew, kvt):
    nb = x_ref.shape[0]
    lo = _lo_mask()
    z_ref[...] = _project(x_ref[:, 0, :], nw_ref[...], win_ref[...])
    khat = _pair_rms(z_ref[:, O_K:O_K + LANES], lo, kw_ref[...])
    vnew = z_ref[:, O_V:O_V + LANES]
    kvt[0] = khat.T
    kvt[1] = vnew.T
    q3[...] = jnp.zeros(q3.shape, F32)
    krep[...] = jnp.zeros(krep.shape, F32)
    vrep[...] = jnp.zeros(vrep.shape, F32)
    for j in range(4):
        qhat = _pair_rms(z_ref[:, j * LANES:(j + 1) * LANES], lo, qw_ref[...])
        qrot = pltpu.roll(qhat, HEAD_DIM, axis=1)
        grp_lo = j < 2
        for half in range(2):
            h = 2 * j + half
            src = qhat if (half == 0) == grp_lo else qrot
            q3[pl.ds(h, nb, stride=QROWS), :] = jnp.where(lo if grp_lo else ~lo, src, 0.0)
            krep[pl.ds(h, nb, stride=QROWS), :] = khat
            vrep[pl.ds(h, nb, stride=QROWS), :] = vnew
    s_new = jnp.sum(q3[...] * krep[...], axis=-1, keepdims=True)
    snew[...] = jnp.broadcast_to(s_new, snew.shape)


def _shift_caches(i, nb, ck_ref, cv_ref, knew_ref, vnew_ref, kvt):
    shift = lax.rem(nb - i * CB, nb)
    newest = lax.broadcasted_iota(jnp.int32, (LANES, WINDOW), 1) == WINDOW - 1
    for src, dst, which in ((ck_ref, knew_ref, 0), (cv_ref, vnew_ref, 1)):
        cols = pltpu.roll(kvt[which], shift, axis=1)
        shifted = pltpu.roll(src[...], WINDOW - 1, axis=2)
        for bb in range(CB):
            dst[bb] = jnp.where(newest, cols[:, bb:bb + 1], shifted[bb])


def _sample_step(j, last, sinks_ref, x_ref, pscale_ref, ck_ref, cv_ref, sp_ref, y_ref, spnew_ref,
                 wout_ref, wpool_ref, z_ref, q3, krep, vrep, snew, kvt, o3, mixed):
    nb = x_ref.shape[0]
    lo = _lo_mask()
    row0 = pl.multiple_of(j * BB, BB)
    rows = pl.ds(row0, BB)
    qrows = pl.ds(pl.multiple_of(j * (BB * QROWS), BB * QROWS), BB * QROWS)
    rid = lax.broadcasted_iota(jnp.int32, (1, QROWS, 1), 1)
    sink3 = jnp.zeros((1, QROWS, 1), F32)
    for h in range(N_HEADS):
        sink3 = jnp.where(rid == h, sinks_ref[h] * LOG2E, sink3)

    q = q3[qrows, :].reshape(BB, QROWS, LANES)
    s_new = snew[qrows, :].reshape(BB, QROWS, LANES)
    kt = ck_ref[...]
    vt = cv_ref[...]
    s = lax.dot_general(q.astype(BF16), kt.astype(BF16), (((2,), (1,)), ((0,), (0,))), preferred_element_type=F32)
    m = jnp.maximum(jnp.maximum(jnp.max(s, axis=-1, keepdims=True), s_new), sink3)
    p = jnp.exp2(s - m)
    p_new = jnp.exp2(s_new - m)
    l = jnp.sum(p, axis=-1, keepdims=True) + p_new + jnp.exp2(sink3 - m)
    o = lax.dot_general(p.astype(BF16), vt.astype(BF16), (((2,), (2,)), ((0,), (0,))), preferred_element_type=F32)
    o = (o + p_new * vrep[qrows, :].reshape(BB, QROWS, LANES)) / l
    o3[...] = o.reshape(BB * QROWS, LANES)

    for jj in range(4):
        grp_lo = jj < 2
        oh = [o3[pl.ds(2 * jj + half, BB, stride=QROWS), :] for half in range(2)]
        a = oh[0] if grp_lo else pltpu.roll(oh[0], HEAD_DIM, axis=1)
        b = pltpu.roll(oh[1], HEAD_DIM, axis=1) if grp_lo else oh[1]
        ga = z_ref[rows, O_GA + jj * LANES:O_GA + (jj + 1) * LANES]
        mixed[rows, jj * LANES:(jj + 1) * LANES] = (jnp.where(lo, a, b) * _silu_half(ga)).astype(BF16)

    u = z_ref[rows, O_U:O_U + D_POOL]
    spnew_ref[0:POOL_STATE - 1] = sp_ref[1:POOL_STATE]
    spnew_ref[POOL_STATE - 1] = u
    ds = []
    for g, w in enumerate(POOL_WINDOWS):
        cols = slice(g * LANES, (g + 1) * LANES)
        ug = u[:, cols]
        win_sum = ug
        for r in range(POOL_STATE - (w - 1), POOL_STATE):
            win_sum = win_sum + sp_ref[r, :, cols]
        ds.append((win_sum * (1.0 / w) - ug).astype(BF16))
    for pr in range(len(POOL_WINDOWS) // 2):
        cols = slice(pr * 2 * LANES, (pr + 1) * 2 * LANES)
        d2 = jnp.concatenate(ds[2 * pr:2 * pr + 2], axis=1)
        po = jnp.dot(d2, wpool_ref[pr], preferred_element_type=F32) * pscale_ref[:, cols]
        gp = z_ref[rows, O_GP + pr * 2 * LANES:O_GP + (pr + 1) * 2 * LANES]
        mixed[rows, D_ATTN + pr * 2 * LANES:D_ATTN + (pr + 1) * 2 * LANES] = (po * _silu_half(gp)).astype(BF16)

    @pl.when(last)
    def _():
        y_ref[:, 0, :] = x_ref[:, 0, :] + jnp.dot(mixed[...], wout_ref[...], preferred_element_type=F32)


def _fused_kernel(n_prompt, sinks_ref, xp_ref, xs_ref, nw_ref, win_hbm, qn_ref, kn_ref, wpool_f32, pscale_ref, wout_hbm,
                  ck_ref, cv_ref, sp_ref, ckc_ref, cvc_ref,
                  yp_ref, kp_ref, vp_ref, up_ref, ys_ref, kq_ref, vq_ref, pq_ref,
                  win_ref, wout_ref, wpool_ref, qw_ref, kw_ref, stage, sem,
                  z_ref, hbuf, qbuf, kbuf, vbuf, uext, mixed, fill_ref,
                  zs_ref, q3, krep, vrep, snew, kvt, o3, mixed_s):
    i = pl.program_id(0)

    @pl.when(i == 0)
    def _():
        _load_weights(win_hbm, wout_hbm, wpool_f32, qn_ref, kn_ref, stage, sem, win_ref, wout_ref, wpool_ref,
                      qw_ref, kw_ref)
        _prompt_init(sinks_ref, kbuf, vbuf, uext, fill_ref)
        _sample_init(xs_ref, nw_ref, win_ref, qw_ref, kw_ref, zs_ref, q3, krep, vrep, snew, kvt)

    @pl.when(i < n_prompt)
    def _():
        _shift_caches(i, xs_ref.shape[0], ckc_ref, cvc_ref, kq_ref, vq_ref, kvt)
        _prompt_step(i, i == n_prompt - 1, xp_ref, nw_ref, pscale_ref, yp_ref, kp_ref, vp_ref, up_ref,
                     win_ref, wout_ref, wpool_ref, qw_ref, kw_ref, z_ref, hbuf, qbuf, kbuf, vbuf, uext, mixed, fill_ref)

    @pl.when(i >= n_prompt)
    def _():
        _sample_step(i - n_prompt, i == pl.num_programs(0) - 1, sinks_ref, xs_ref, pscale_ref, ck_ref, cv_ref, sp_ref,
                     ys_ref, pq_ref, wout_ref, wpool_ref, zs_ref, q3, krep, vrep, snew, kvt, o3, mixed_s)


def _fused_call(sinks, xp, xs, nw, win, qn, kn, wpool, pscale, wout, ck, cv, sp):
    seq, nb = xp.shape[0], xs.shape[0]
    n_prompt, n_sample = seq // TB, nb // BB
    assert nb == CB * n_prompt
    const2 = lambda i: (0, 0)
    const3 = lambda i: (0, 0, 0)
    pblk = lambda i: (jnp.minimum(i, n_prompt - 1), 0)
    cblk = lambda i: (jnp.minimum(i, n_prompt - 1), 0, 0)
    sidx = lambda i: jnp.clip(i - n_prompt, 0, n_sample - 1)
    chunk3 = lambda i: (sidx(i), 0, 0)
    mid3 = lambda i: (0, sidx(i), 0)
    once = pl.Buffered(1)
    return pl.pallas_call(
        functools.partial(_fused_kernel, n_prompt),
        grid=(n_prompt + n_sample,),
        in_specs=[
            pl.BlockSpec(memory_space=pltpu.SMEM),
            pl.BlockSpec((TB, D_MODEL), pblk),
            pl.BlockSpec((nb, 1, D_MODEL), const3, pipeline_mode=once),
            pl.BlockSpec((1, D_MODEL), const2),
            pl.BlockSpec(memory_space=pl.ANY),
            pl.BlockSpec((1, HEAD_DIM), const2),
            pl.BlockSpec((1, HEAD_DIM), const2),
            pl.BlockSpec((4, LANES, LANES), const3, pipeline_mode=once),
            pl.BlockSpec((1, D_POOL), const2),
            pl.BlockSpec(memory_space=pl.ANY),
            pl.BlockSpec((BB, LANES, WINDOW), chunk3),
            pl.BlockSpec((BB, LANES, WINDOW), chunk3),
            pl.BlockSpec((POOL_STATE, BB, D_POOL), mid3),
            pl.BlockSpec((CB, LANES, WINDOW), cblk),
            pl.BlockSpec((CB, LANES, WINDOW), cblk),
        ],
        out_specs=[
            pl.BlockSpec((TB, D_MODEL), pblk),
            pl.BlockSpec((LANES, WINDOW), const2),
            pl.BlockSpec((LANES, WINDOW), const2),
            pl.BlockSpec((POOL_STATE, 1, D_POOL), const3),
            pl.BlockSpec((nb, 1, D_MODEL), const3, pipeline_mode=once),
            pl.BlockSpec((CB, LANES, WINDOW), cblk),
            pl.BlockSpec((CB, LANES, WINDOW), cblk),
            pl.BlockSpec((POOL_STATE, BB, D_POOL), mid3),
        ],
        out_shape=[
            jax.ShapeDtypeStruct((seq, D_MODEL), F32),
            jax.ShapeDtypeStruct((LANES, WINDOW), F32),
            jax.ShapeDtypeStruct((LANES, WINDOW), F32),
            jax.ShapeDtypeStruct((POOL_STATE, 1, D_POOL), F32),
            jax.ShapeDtypeStruct((nb, 1, D_MODEL), F32),
            jax.ShapeDtypeStruct((nb, LANES, WINDOW), F32),
            jax.ShapeDtypeStruct((nb, LANES, WINDOW), F32),
            jax.ShapeDtypeStruct((POOL_STATE, nb, D_POOL), F32),
        ],
        scratch_shapes=[
            pltpu.VMEM((D_MODEL, D_IN_PROJ), BF16),
            pltpu.VMEM((D_MODEL, D_MODEL), BF16),
            pltpu.VMEM((2, 2 * LANES, 2 * LANES), BF16),
            pltpu.VMEM((1, LANES), F32),
            pltpu.VMEM((1, LANES), F32),
            pltpu.VMEM((2, D_MODEL, WCHUNK), F32),
            pltpu.SemaphoreType.DMA((2,)),
            pltpu.VMEM((TB, D_IN_PROJ), F32),
            pltpu.VMEM((TB, D_MODEL), BF16),
            pltpu.VMEM((2, NSB, GROUP * BLOCK, LANES), BF16),
            pltpu.VMEM((2, TB + BLOCK, LANES), BF16),
            pltpu.VMEM((2, TB + BLOCK, LANES), BF16),
            pltpu.VMEM((TB + 16, D_POOL), F32),
            pltpu.VMEM((TB, D_MODEL), BF16),
            pltpu.VMEM((N_HEADS, BLOCK, 2 * BLOCK), F32),
            pltpu.VMEM((nb, D_IN_PROJ), F32),
            pltpu.VMEM((nb * QROWS, LANES), F32),
            pltpu.VMEM((nb * QROWS, LANES), F32),
            pltpu.VMEM((nb * QROWS, LANES), F32),
            pltpu.VMEM((nb * QROWS, LANES), F32),
            pltpu.VMEM((2, LANES, nb), F32),
            pltpu.VMEM((BB * QROWS, LANES), F32),
            pltpu.VMEM((nb, D_MODEL), BF16),
        ],
        compiler_params=pltpu.CompilerParams(dimension_semantics=("arbitrary",), vmem_limit_bytes=VMEM_LIMIT),
        name="hybrid_step",
    )(sinks, xp, xs, nw, win, qn, kn, wpool, pscale, wout, ck, cv, sp, ck, cv)


def _cache_in(c):
    nb = c.shape[0]
    return jnp.transpose(c, (0, 2, 3, 1)).reshape(nb, LANES, WINDOW)


def _cache_out(c):
    nb = c.shape[0]
    return jnp.transpose(c.reshape(nb, N_KV_HEADS, HEAD_DIM, WINDOW), (0, 3, 1, 2))


def kernel(x_prompt, x_sample, cache_k, cache_v, state_pool, norm_w, w_in, q_norm_w, k_norm_w, sinks, w_pool,
           pool_scale, w_out):
    depth = norm_w.shape[0]
    assert depth == 1 and x_prompt.shape[0] == 1 and x_sample.shape[1] == 1
    seq = x_prompt.shape[1]
    nb = x_sample.shape[0]
    assert seq % TB == 0 and nb % BB == 0 and nb == LANES

    yp, kp, vp, up, ys, kq, vq, pq = _fused_call(
        sinks[0], x_prompt[0], x_sample, norm_w, w_in[0], q_norm_w, k_norm_w, w_pool[0], pool_scale, w_out[0],
        _cache_in(cache_k[0]), _cache_in(cache_v[0]), jnp.transpose(state_pool[0], (1, 0, 2)))
    return (yp[None], ys, _cache_out(kp[None])[None], _cache_out(vp[None])[None],
            jnp.transpose(up, (1, 0, 2))[None],
            _cache_out(kq)[None], _cache_out(vq)[None], jnp.transpose(pq, (1, 0, 2))[None])
```

```python
import functools

import jax
import jax.numpy as jnp
from jax import lax
from jax.experimental import pallas as pl
from jax.experimental.pallas import tpu as pltpu

D_MODEL = 1024
HEAD_DIM = 64
N_HEADS = 8
N_KV_HEADS = 2
GROUP = 4
WINDOW = 128
BLOCK = 128
POOL_WINDOWS = (2, 4, 8, 16)
POOL_STATE = 15
D_ATTN = 512
D_POOL = 512
D_IN_PROJ = 2304
EPS = 1e-6
NEG_INF = -1e30
LOG2E = 1.4426950408889634

O_K = 512
O_V = 640
O_GA = 768
O_U = 1280
O_GP = 1792

LANES = 128
TB = 512
NSB = TB // BLOCK
BB = 16
CB = 4
QROWS = 16
WCHUNK = 256
VMEM_LIMIT = 58 * 1024 * 1024

F32 = jnp.float32
BF16 = jnp.bfloat16
_NT = (((1,), (1,)), ((), ()))


def _silu_half(h):
    return h + h * jnp.tanh(h)


def _lo_mask():
    return lax.broadcasted_iota(jnp.int32, (1, LANES), 1) < HEAD_DIM


def _pair_rms(zs, lo, w2):
    sq = zs * zs
    s_lo = jnp.sum(jnp.where(lo, sq, 0.0), axis=-1, keepdims=True)
    s_hi = jnp.sum(jnp.where(lo, 0.0, sq), axis=-1, keepdims=True)
    r = lax.rsqrt(jnp.where(lo, s_lo, s_hi) * (1.0 / HEAD_DIM) + EPS)
    return zs * r * w2


def _rms_bf16(x, nw):
    ms = jnp.mean(x * x, axis=-1, keepdims=True)
    return (x * lax.rsqrt(ms + EPS) * nw).astype(BF16)


def _project(x, nw, win):
    return jnp.dot(_rms_bf16(x, nw), win, preferred_element_type=F32)


def _weight_chunks():
    chunks = []
    for c0 in range(0, D_IN_PROJ, WCHUNK):
        gate = O_GA <= c0 < O_U or c0 >= O_GP
        chunks.append((0, c0, 0.5 if gate else 1.0))
    chunks += [(1, c0, 1.0) for c0 in range(0, D_MODEL, WCHUNK)]
    return chunks


def _load_weights(win_hbm, wout_hbm, wpool_f32, qn_ref, kn_ref, stage, sem, win_bf, wout_bf, wpool_bf, qw_ref, kw_ref):
    chunks = _weight_chunks()
    srcs, dsts = (win_hbm, wout_hbm), (win_bf, wout_bf)

    def copy(k):
        which, c0, _ = chunks[k]
        return pltpu.make_async_copy(srcs[which].at[:, pl.ds(c0, WCHUNK)], stage.at[k % 2], sem.at[k % 2])

    copy(0).start()
    copy(1).start()
    for k, (which, c0, scale) in enumerate(chunks):
        copy(k).wait()
        w = stage[k % 2]
        dsts[which][:, c0:c0 + WCHUNK] = (w if scale == 1.0 else scale * w).astype(BF16)
        if k + 2 < len(chunks):
            copy(k + 2).start()
    wpool_bf[...] = jnp.zeros(wpool_bf.shape, BF16)
    for g in range(len(POOL_WINDOWS)):
        d0 = (g % 2) * LANES
        wpool_bf[g // 2, d0:d0 + LANES, d0:d0 + LANES] = wpool_f32[g].astype(BF16)
    qn = qn_ref[...] * (HEAD_DIM ** -0.5 * LOG2E)
    qw_ref[...] = jnp.concatenate([qn, qn], axis=1)
    kw_ref[...] = jnp.concatenate([kn_ref[...], kn_ref[...]], axis=1)


def _prompt_init(sinks_ref, kbuf, vbuf, uext, sink2):
    kbuf[:, 0:BLOCK, :] = jnp.zeros((2, BLOCK, LANES), BF16)
    vbuf[:, 0:BLOCK, 0:LANES] = jnp.zeros((2, BLOCK, LANES), BF16)
    vbuf[:, :, LANES:] = jnp.ones((2, TB + BLOCK, LANES), BF16)
    uext[0:16, :] = jnp.zeros((16, D_POOL), F32)
    lo = _lo_mask()
    for jj in range(N_HEADS // 2):
        sink2[jj] = jnp.where(lo, sinks_ref[2 * jj], sinks_ref[2 * jj + 1]) * LOG2E


def _prompt_step(i, last, x_ref, nw_ref, pscale_ref, y_ref, knew_ref, vnew_ref, unew_ref,
                 win_ref, wout_ref, wpool_ref, qw_ref, kw_ref, z_ref, hbuf, qbuf, kbuf, vbuf, uext, mixed, sink2):
    lo = _lo_mask()
    hbuf[...] = _rms_bf16(x_ref[...], nw_ref[...])

    z_ref[:, O_U:] = jnp.dot(hbuf[...], win_ref[:, O_U:], preferred_element_type=F32)
    uext[16:, :] = z_ref[:, O_U:O_U + D_POOL]
    pos16 = i * TB + lax.broadcasted_iota(jnp.int32, (16, LANES), 0)
    ds = []
    for g, w in enumerate(POOL_WINDOWS):
        acc = uext[:, g * LANES:(g + 1) * LANES]
        sh = 1
        while sh < w:
            acc = acc + pltpu.roll(acc, sh, axis=0)
            sh *= 2
        ug = z_ref[:, O_U + g * LANES:O_U + (g + 1) * LANES]
        cnt = jnp.minimum(w, pos16 + 1).astype(F32)
        d_first = acc[16:32, :] / cnt - ug[0:16, :]
        d_rest = acc[32:, :] * (1.0 / w) - ug[16:, :]
        ds.append(jnp.concatenate([d_first, d_rest], axis=0).astype(BF16))
    uext[0:16, :] = uext[TB:TB + 16, :]

    z_ref[:, :O_K] = jnp.dot(hbuf[...], win_ref[:, :O_K], preferred_element_type=F32)

    for pr in range(len(POOL_WINDOWS) // 2):
        cols = slice(pr * 2 * LANES, (pr + 1) * 2 * LANES)
        d2 = jnp.concatenate(ds[2 * pr:2 * pr + 2], axis=1)
        po = jnp.dot(d2, wpool_ref[pr], preferred_element_type=F32) * pscale_ref[:, cols]
        gp = z_ref[:, O_GP + pr * 2 * LANES:O_GP + (pr + 1) * 2 * LANES]
        mixed[:, D_ATTN + pr * 2 * LANES:D_ATTN + (pr + 1) * 2 * LANES] = (po * _silu_half(gp)).astype(BF16)

    for j in range(4):
        qhat = _pair_rms(z_ref[:, j * LANES:(j + 1) * LANES], lo, qw_ref[...])
        q_lo = jnp.where(lo, qhat, 0.0).astype(BF16)
        q_hi = jnp.where(lo, 0.0, qhat).astype(BF16)
        c, g0 = j // 2, 2 * (j % 2)
        for sb in range(NSB):
            rows = slice(sb * BLOCK, (sb + 1) * BLOCK)
            qbuf[c, sb, g0 * BLOCK:(g0 + 1) * BLOCK, :] = q_lo[rows]
            qbuf[c, sb, (g0 + 1) * BLOCK:(g0 + 2) * BLOCK, :] = q_hi[rows]

    half = TB // 2
    for r0 in (0, half):
        z_ref[r0:r0 + half, O_K:O_GA] = jnp.dot(hbuf[r0:r0 + half, :], win_ref[:, O_K:O_GA],
                                                preferred_element_type=F32)
    z_ref[:, O_GA:O_U] = jnp.dot(hbuf[...], win_ref[:, O_GA:O_U], preferred_element_type=F32)

    khat = _pair_rms(z_ref[:, O_K:O_K + LANES], lo, kw_ref[...])
    kr = pltpu.roll(khat, HEAD_DIM, axis=1)
    kbuf[0, BLOCK:, :] = jnp.where(lo, khat, kr).astype(BF16)
    kbuf[1, BLOCK:, :] = jnp.where(lo, kr, khat).astype(BF16)
    vz = z_ref[:, O_V:O_V + LANES]
    vr = pltpu.roll(vz, HEAD_DIM, axis=1)
    vbuf[0, BLOCK:, 0:LANES] = jnp.where(lo, vz, vr).astype(BF16)
    vbuf[1, BLOCK:, 0:LANES] = jnp.where(lo, vr, vz).astype(BF16)

    r_io = lax.broadcasted_iota(jnp.int32, (BLOCK, 2 * BLOCK), 0)
    c_io = lax.broadcasted_iota(jnp.int32, (BLOCK, 2 * BLOCK), 1)
    band = (c_io >= r_io) & (c_io <= r_io + WINDOW)
    first_lo = jnp.where(i > 0, 0, BLOCK)
    band_first = band & (c_io >= first_lo)

    for sb in range(NSB):
        valid = band_first if sb == 0 else band
        rows = slice(sb * BLOCK, (sb + 1) * BLOCK)
        for c in range(2):
            keys = kbuf[c, sb * BLOCK:(sb + 2) * BLOCK, :]
            vals = vbuf[c, sb * BLOCK:(sb + 2) * BLOCK, :]
            s = lax.dot_general(qbuf[c, sb], keys, _NT, preferred_element_type=F32)
            ps, ms = [], []
            for g in range(GROUP):
                sg = jnp.where(valid, s[g * BLOCK:(g + 1) * BLOCK, :], NEG_INF)
                m = jnp.max(sg, axis=-1, keepdims=True)
                ps.append(jnp.exp2(sg - m).astype(BF16))
                ms.append(m)
            o = jnp.dot(jnp.concatenate(ps, axis=0), vals, preferred_element_type=F32)
            for jj in range(2):
                ev = slice(2 * jj * BLOCK, (2 * jj + 1) * BLOCK)
                od = slice((2 * jj + 1) * BLOCK, (2 * jj + 2) * BLOCK)
                slab = jnp.where(lo, o[ev, 0:LANES], o[od, 0:LANES])
                psum = jnp.where(lo, o[ev, LANES:], o[od, LANES:])
                mrow = jnp.where(lo, ms[2 * jj], ms[2 * jj + 1])
                l = psum + jnp.exp2(sink2[2 * c + jj] - mrow)
                col = (2 * c + jj) * LANES
                ga = z_ref[rows, O_GA + col:O_GA + col + LANES]
                mixed[rows, col:col + LANES] = (slab * (1.0 / l) * _silu_half(ga)).astype(BF16)

    kbuf[:, 0:BLOCK, :] = kbuf[:, TB:TB + BLOCK, :]
    vbuf[:, 0:BLOCK, 0:LANES] = vbuf[:, TB:TB + BLOCK, 0:LANES]

    y_ref[...] = x_ref[...] + jnp.dot(mixed[...], wout_ref[...], preferred_element_type=F32)

    @pl.when(last)
    def _():
        unew_ref[:, 0, :] = z_ref[TB - POOL_STATE:TB, O_U:O_U + D_POOL]
        tail = slice(TB - WINDOW, TB)
        knew_ref[...] = _pair_rms(z_ref[tail, O_K:O_K + LANES], lo, kw_ref[...]).T
        vnew_ref[...] = z_ref[tail, O_V:O_V + LANES].T


def _sample_init(x_ref, nw_ref, win_ref, qw_ref, kw_ref, z_ref, q3, krep, vrep, snew, kvt):
    nb = x_ref.shape[0]
    lo = _lo_mask()
    z_ref[...] = _project(x_ref[:, 0, :], nw_ref[...], win_ref[...])
    khat = _pair_rms(z_ref[:, O_K:O_K + LANES], lo, kw_ref[...])
    vnew = z_ref[:, O_V:O_V + LANES]
    kvt[0] = khat.T
    kvt[1] = vnew.T
    q3[...] = jnp.zeros(q3.shape, F32)
    krep[...] = jnp.zeros(krep.shape, F32)
    vrep[...] = jnp.zeros(vrep.shape, F32)
    for j in range(4):
        qhat = _pair_rms(z_ref[:, j * LANES:(j + 1) * LANES], lo, qw_ref[...])
        qrot = pltpu.roll(qhat, HEAD_DIM, axis=1)
        grp_lo = j < 2
        for half in range(2):
            h = 2 * j + half
            src = qhat if (half == 0) == grp_lo else qrot
            q3[pl.ds(h, nb, stride=QROWS), :] = jnp.where(lo if grp_lo else ~lo, src, 0.0)
            krep[pl.ds(h, nb, stride=QROWS), :] = khat
            vrep[pl.ds(h, nb, stride=QROWS), :] = vnew
    s_new = jnp.sum(q3[...] * krep[...], axis=-1, keepdims=True)
    snew[...] = jnp.broadcast_to(s_new, snew.shape)


def _shift_caches(i, nb, ck_ref, cv_ref, knew_ref, vnew_ref, kvt):
    shift = lax.rem(nb - i * CB, nb)
    newest = lax.broadcasted_iota(jnp.int32, (LANES, WINDOW), 1) == WINDOW - 1
    for src, dst, which in ((ck_ref, knew_ref, 0), (cv_ref, vnew_ref, 1)):
        cols = pltpu.roll(kvt[which], shift, axis=1)
        shifted = pltpu.roll(src[...], WINDOW - 1, axis=2)
        for bb in range(CB):
            dst[bb] = jnp.where(newest, cols[:, bb:bb + 1], shifted[bb])


def _sample_step(j, last, sinks_ref, x_ref, pscale_ref, ck_ref, cv_ref, sp_ref, y_ref, spnew_ref,
                 wout_ref, wpool_ref, z_ref, q3, krep, vrep, snew, kvt, o3, mixed):
    nb = x_ref.shape[0]
    lo = _lo_mask()
    row0 = pl.multiple_of(j * BB, BB)
    rows = pl.ds(row0, BB)
    qrows = pl.ds(pl.multiple_of(j * (BB * QROWS), BB * QROWS), BB * QROWS)
    rid = lax.broadcasted_iota(jnp.int32, (1, QROWS, 1), 1)
    sink3 = jnp.zeros((1, QROWS, 1), F32)
    for h in range(N_HEADS):
        sink3 = jnp.where(rid == h, sinks_ref[h] * LOG2E, sink3)

    q = q3[qrows, :].reshape(BB, QROWS, LANES)
    s_new = snew[qrows, :].reshape(BB, QROWS, LANES)
    kt = ck_ref[...]
    vt = cv_ref[...]
    s = lax.dot_general(q.astype(BF16), kt.astype(BF16), (((2,), (1,)), ((0,), (0,))), preferred_element_type=F32)
    m = jnp.maximum(jnp.maximum(jnp.max(s, axis=-1, keepdims=True), s_new), sink3)
    p = jnp.exp2(s - m)
    p_new = jnp.exp2(s_new - m)
    l = jnp.sum(p, axis=-1, keepdims=True) + p_new + jnp.exp2(sink3 - m)
    o = lax.dot_general(p.astype(BF16), vt.astype(BF16), (((2,), (2,)), ((0,), (0,))), preferred_element_type=F32)
    o = (o + p_new * vrep[qrows, :].reshape(BB, QROWS, LANES)) / l
    o3[...] = o.reshape(BB * QROWS, LANES)

    for jj in range(4):
        grp_lo = jj < 2
        oh = [o3[pl.ds(2 * jj + half, BB, stride=QROWS), :] for half in range(2)]
        a = oh[0] if grp_lo else pltpu.roll(oh[0], HEAD_DIM, axis=1)
        b = pltpu.roll(oh[1], HEAD_DIM, axis=1) if grp_lo else oh[1]
        ga = z_ref[rows, O_GA + jj * LANES:O_GA + (jj + 1) * LANES]
        mixed[rows, jj * LANES:(jj + 1) * LANES] = (jnp.where(lo, a, b) * _silu_half(ga)).astype(BF16)

    u = z_ref[rows, O_U:O_U + D_POOL]
    spnew_ref[0:POOL_STATE - 1] = sp_ref[1:POOL_STATE]
    spnew_ref[POOL_STATE - 1] = u
    ds = []
    for g, w in enumerate(POOL_WINDOWS):
        cols = slice(g * LANES, (g + 1) * LANES)
        ug = u[:, cols]
        win_sum = ug
        for r in range(POOL_STATE - (w - 1), POOL_STATE):
            win_sum = win_sum + sp_ref[r, :, cols]
        ds.append((win_sum * (1.0 / w) - ug).astype(BF16))
    for pr in range(len(POOL_WINDOWS) // 2):
        cols = slice(pr * 2 * LANES, (pr + 1) * 2 * LANES)
        d2 = jnp.concatenate(ds[2 * pr:2 * pr + 2], axis=1)
        po = jnp.dot(d2, wpool_ref[pr], preferred_element_type=F32) * pscale_ref[:, cols]
        gp = z_ref[rows, O_GP + pr * 2 * LANES:O_GP + (pr + 1) * 2 * LANES]
        mixed[rows, D_ATTN + pr * 2 * LANES:D_ATTN + (pr + 1) * 2 * LANES] = (po * _silu_half(gp)).astype(BF16)

    @pl.when(last)
    def _():
        y_ref[:, 0, :] = x_ref[:, 0, :] + jnp.dot(mixed[...], wout_ref[...], preferred_element_type=F32)


def _fused_kernel(n_prompt, sinks_ref, xp_ref, xs_ref, nw_ref, win_hbm, qn_ref, kn_ref, wpool_f32, pscale_ref, wout_hbm,
                  ck_ref, cv_ref, sp_ref, ckc_ref, cvc_ref,
                  yp_ref, kp_ref, vp_ref, up_ref, ys_ref, kq_ref, vq_ref, pq_ref,
                  win_ref, wout_ref, wpool_ref, qw_ref, kw_ref, stage, sem,
                  z_ref, hbuf, qbuf, kbuf, vbuf, uext, mixed, sink2,
                  zs_ref, q3, krep, vrep, snew, kvt, o3, mixed_s):
    i = pl.program_id(0)

    @pl.when(i == 0)
    def _():
        _load_weights(win_hbm, wout_hbm, wpool_f32, qn_ref, kn_ref, stage, sem, win_ref, wout_ref, wpool_ref,
                      qw_ref, kw_ref)
        _prompt_init(sinks_ref, kbuf, vbuf, uext, sink2)
        _sample_init(xs_ref, nw_ref, win_ref, qw_ref, kw_ref, zs_ref, q3, krep, vrep, snew, kvt)

    @pl.when(i < n_prompt)
    def _():
        _shift_caches(i, xs_ref.shape[0], ckc_ref, cvc_ref, kq_ref, vq_ref, kvt)
        _prompt_step(i, i == n_prompt - 1, xp_ref, nw_ref, pscale_ref, yp_ref, kp_ref, vp_ref, up_ref,
                     win_ref, wout_ref, wpool_ref, qw_ref, kw_ref, z_ref, hbuf, qbuf, kbuf, vbuf, uext, mixed, sink2)

    @pl.when(i >= n_prompt)
    def _():
        _sample_step(i - n_prompt, i == pl.num_programs(0) - 1, sinks_ref, xs_ref, pscale_ref, ck_ref, cv_ref, sp_ref,
                     ys_ref, pq_ref, wout_ref, wpool_ref, zs_ref, q3, krep, vrep, snew, kvt, o3, mixed_s)


def _fused_call(sinks, xp, xs, nw, win, qn, kn, wpool, pscale, wout, ck, cv, sp):
    seq, nb = xp.shape[0], xs.shape[0]
    n_prompt, n_sample = seq // TB, nb // BB
    assert nb == CB * n_prompt
    const2 = lambda i: (0, 0)
    const3 = lambda i: (0, 0, 0)
    pblk = lambda i: (jnp.minimum(i, n_prompt - 1), 0)
    cblk = lambda i: (jnp.minimum(i, n_prompt - 1), 0, 0)
    sidx = lambda i: jnp.clip(i - n_prompt, 0, n_sample - 1)
    chunk3 = lambda i: (sidx(i), 0, 0)
    mid3 = lambda i: (0, sidx(i), 0)
    once = pl.Buffered(1)
    return pl.pallas_call(
        functools.partial(_fused_kernel, n_prompt),
        grid=(n_prompt + n_sample,),
        in_specs=[
            pl.BlockSpec(memory_space=pltpu.SMEM),
            pl.BlockSpec((TB, D_MODEL), pblk),
            pl.BlockSpec((nb, 1, D_MODEL), const3, pipeline_mode=once),
            pl.BlockSpec((1, D_MODEL), const2),
            pl.BlockSpec(memory_space=pl.ANY),
            pl.BlockSpec((1, HEAD_DIM), const2),
            pl.BlockSpec((1, HEAD_DIM), const2),
            pl.BlockSpec((4, LANES, LANES), const3, pipeline_mode=once),
            pl.BlockSpec((1, D_POOL), const2),
            pl.BlockSpec(memory_space=pl.ANY),
            pl.BlockSpec((BB, LANES, WINDOW), chunk3),
            pl.BlockSpec((BB, LANES, WINDOW), chunk3),
            pl.BlockSpec((POOL_STATE, BB, D_POOL), mid3),
            pl.BlockSpec((CB, LANES, WINDOW), cblk),
            pl.BlockSpec((CB, LANES, WINDOW), cblk),
        ],
        out_specs=[
            pl.BlockSpec((TB, D_MODEL), pblk),
            pl.BlockSpec((LANES, WINDOW), const2),
            pl.BlockSpec((LANES, WINDOW), const2),
            pl.BlockSpec((POOL_STATE, 1, D_POOL), const3),
            pl.BlockSpec((nb, 1, D_MODEL), const3, pipeline_mode=once),
            pl.BlockSpec((CB, LANES, WINDOW), cblk),
            pl.BlockSpec((CB, LANES, WINDOW), cblk),
            pl.BlockSpec((POOL_STATE, BB, D_POOL), mid3),
        ],
        out_shape=[
            jax.ShapeDtypeStruct((seq, D_MODEL), F32),
            jax.ShapeDtypeStruct((LANES, WINDOW), F32),
            jax.ShapeDtypeStruct((LANES, WINDOW), F32),
            jax.ShapeDtypeStruct((POOL_STATE, 1, D_POOL), F32),
            jax.ShapeDtypeStruct((nb, 1, D_MODEL), F32),
            jax.ShapeDtypeStruct((nb, LANES, WINDOW), F32),
            jax.ShapeDtypeStruct((nb, LANES, WINDOW), F32),
            jax.ShapeDtypeStruct((POOL_STATE, nb, D_POOL), F32),
        ],
        scratch_shapes=[
            pltpu.VMEM((D_MODEL, D_IN_PROJ), BF16),
            pltpu.VMEM((D_MODEL, D_MODEL), BF16),
            pltpu.VMEM((2, 2 * LANES, 2 * LANES), BF16),
            pltpu.VMEM((1, LANES), F32),
            pltpu.VMEM((1, LANES), F32),
            pltpu.VMEM((2, D_MODEL, WCHUNK), F32),
            pltpu.SemaphoreType.DMA((2,)),
            pltpu.VMEM((TB, D_IN_PROJ), F32),
            pltpu.VMEM((TB, D_MODEL), BF16),
            pltpu.VMEM((2, NSB, GROUP * BLOCK, LANES), BF16),
            pltpu.VMEM((2, TB + BLOCK, LANES), BF16),
            pltpu.VMEM((2, TB + BLOCK, 2 * LANES), BF16),
            pltpu.VMEM((TB + 16, D_POOL), F32),
            pltpu.VMEM((TB, D_MODEL), BF16),
            pltpu.VMEM((N_HEADS // 2, 1, LANES), F32),
            pltpu.VMEM((nb, D_IN_PROJ), F32),
            pltpu.VMEM((nb * QROWS, LANES), F32),
            pltpu.VMEM((nb * QROWS, LANES), F32),
            pltpu.VMEM((nb * QROWS, LANES), F32),
            pltpu.VMEM((nb * QROWS, LANES), F32),
            pltpu.VMEM((2, LANES, nb), F32),
            pltpu.VMEM((BB * QROWS, LANES), F32),
            pltpu.VMEM((nb, D_MODEL), BF16),
        ],
        compiler_params=pltpu.CompilerParams(dimension_semantics=("arbitrary",), vmem_limit_bytes=VMEM_LIMIT),
        name="hybrid_step",
    )(sinks, xp, xs, nw, win, qn, kn, wpool, pscale, wout, ck, cv, sp, ck, cv)


def _cache_in(c):
    nb = c.shape[0]
    return jnp.transpose(c, (0, 2, 3, 1)).reshape(nb, LANES, WINDOW)


def _cache_out(c):
    nb = c.shape[0]
    return jnp.transpose(c.reshape(nb, N_KV_HEADS, HEAD_DIM, WINDOW), (0, 3, 1, 2))


def kernel(x_prompt, x_sample, cache_k, cache_v, state_pool, norm_w, w_in, q_norm_w, k_norm_w, sinks, w_pool,
           pool_scale, w_out):
    depth = norm_w.shape[0]
    assert depth == 1 and x_prompt.shape[0] == 1 and x_sample.shape[1] == 1
    seq = x_prompt.shape[1]
    nb = x_sample.shape[0]
    assert seq % TB == 0 and nb % BB == 0 and nb == LANES

    yp, kp, vp, up, ys, kq, vq, pq = _fused_call(
        sinks[0], x_prompt[0], x_sample, norm_w, w_in[0], q_norm_w, k_norm_w, w_pool[0], pool_scale, w_out[0],
        _cache_in(cache_k[0]), _cache_in(cache_v[0]), jnp.transpose(state_pool[0], (1, 0, 2)))
    return (yp[None], ys, _cache_out(kp[None])[None], _cache_out(vp[None])[None],
            jnp.transpose(up, (1, 0, 2))[None],
            _cache_out(kq)[None], _cache_out(vq)[None], jnp.transpose(pq, (1, 0, 2))[None])
```

```python
import functools

import jax
import jax.numpy as jnp
from jax import lax
from jax.experimental import pallas as pl
from jax.experimental.pallas import tpu as pltpu

D_MODEL = 1024
HEAD_DIM = 64
N_HEADS = 8
N_KV_HEADS = 2
GROUP = 4
WINDOW = 128
BLOCK = 128
POOL_WINDOWS = (2, 4, 8, 16)
POOL_STATE = 15
D_ATTN = 512
D_POOL = 512
D_IN_PROJ = 2304
EPS = 1e-6
NEG_INF = -1e30
LOG2E = 1.4426950408889634

O_K = 512
O_V = 640
O_GA = 768
O_U = 1280
O_GP = 1792

LANES = 128
TB = 512
NSB = TB // BLOCK
BB = 16
CB = 4
QROWS = 16
WCHUNK = 256
VMEM_LIMIT = 58 * 1024 * 1024

F32 = jnp.float32
BF16 = jnp.bfloat16
_NT = (((1,), (1,)), ((), ()))


def _silu_half(h):
    return h + h * jnp.tanh(h)


def _lo_mask():
    return lax.broadcasted_iota(jnp.int32, (1, LANES), 1) < HEAD_DIM


def _pair_rms(zs, lo, w2):
    sq = zs * zs
    s_lo = jnp.sum(jnp.where(lo, sq, 0.0), axis=-1, keepdims=True)
    s_hi = jnp.sum(jnp.where(lo, 0.0, sq), axis=-1, keepdims=True)
    r = lax.rsqrt(jnp.where(lo, s_lo, s_hi) * (1.0 / HEAD_DIM) + EPS)
    return zs * r * w2


def _row_scale(x):
    return lax.rsqrt(jnp.mean(x * x, axis=-1, keepdims=True) + EPS)


def _project(x, win):
    return jnp.dot(x.astype(BF16), win, preferred_element_type=F32) * _row_scale(x)


def _weight_chunks():
    chunks = []
    for c0 in range(0, D_IN_PROJ, WCHUNK):
        gate = O_GA <= c0 < O_U or c0 >= O_GP
        chunks.append((0, c0, 0.5 if gate else 1.0))
    chunks += [(1, c0, 1.0) for c0 in range(0, D_MODEL, WCHUNK)]
    return chunks


def _load_weights(win_hbm, wout_hbm, wpool_f32, nw_ref, qn_ref, kn_ref, stage, sem, win_bf, wout_bf, wpool_bf,
                  qw_ref, kw_ref):
    chunks = _weight_chunks()
    srcs, dsts = (win_hbm, wout_hbm), (win_bf, wout_bf)
    nw_tile = jnp.concatenate([jnp.broadcast_to(nw_ref[:, t * LANES:(t + 1) * LANES], (LANES, LANES)).T
                               for t in range(D_MODEL // LANES)], axis=0)
    nw_rows = jnp.concatenate([nw_tile] * (WCHUNK // LANES), axis=1)

    def copy(k):
        which, c0, _ = chunks[k]
        return pltpu.make_async_copy(srcs[which].at[:, pl.ds(c0, WCHUNK)], stage.at[k % 2], sem.at[k % 2])

    copy(0).start()
    copy(1).start()
    for k, (which, c0, scale) in enumerate(chunks):
        copy(k).wait()
        w = stage[k % 2]
        if which == 0:
            w = w * nw_rows if scale == 1.0 else w * (scale * nw_rows)
        dsts[which][:, c0:c0 + WCHUNK] = w.astype(BF16)
        if k + 2 < len(chunks):
            copy(k + 2).start()
    wpool_bf[...] = jnp.zeros(wpool_bf.shape, BF16)
    for g in range(len(POOL_WINDOWS)):
        d0 = (g % 2) * LANES
        wpool_bf[g // 2, d0:d0 + LANES, d0:d0 + LANES] = wpool_f32[g].astype(BF16)
    qn = qn_ref[...] * (HEAD_DIM ** -0.5 * LOG2E)
    qw_ref[...] = jnp.concatenate([qn, qn], axis=1)
    kw_ref[...] = jnp.concatenate([kn_ref[...], kn_ref[...]], axis=1)


def _prompt_init(sinks_ref, kbuf, vbuf, uext, fill_ref):
    kbuf[:, 0:BLOCK, :] = jnp.zeros((2, BLOCK, LANES), BF16)
    vbuf[:, 0:BLOCK, :] = jnp.zeros((2, BLOCK, LANES), BF16)
    uext[0:16, :] = jnp.zeros((16, D_POOL), F32)
    r0 = lax.broadcasted_iota(jnp.int32, (BLOCK, 2 * BLOCK), 0)
    c0 = lax.broadcasted_iota(jnp.int32, (BLOCK, 2 * BLOCK), 1)
    sink_col = jnp.where(r0 >= 1, r0 - 1, 2 * BLOCK - 1)
    for h in range(N_HEADS):
        fill_ref[h] = jnp.where(c0 == sink_col, sinks_ref[h] * LOG2E, NEG_INF)


def _prompt_step(i, last, x_ref, pscale_ref, y_ref, knew_ref, vnew_ref, unew_ref,
                 win_ref, wout_ref, wpool_ref, qw_ref, kw_ref, z_ref, hbuf, qbuf, kbuf, vbuf, uext, mixed, fill_ref):
    lo = _lo_mask()
    hbuf[...] = x_ref[...].astype(BF16)
    rs = _row_scale(x_ref[...])

    def proj(cols, rows=slice(None)):
        return jnp.dot(hbuf[rows, :], win_ref[:, cols], preferred_element_type=F32) * rs[rows]

    z_ref[:, O_U:] = proj(slice(O_U, D_IN_PROJ))
    uext[16:, :] = z_ref[:, O_U:O_U + D_POOL]
    pos16 = i * TB + lax.broadcasted_iota(jnp.int32, (16, LANES), 0)
    ds = []
    for g, w in enumerate(POOL_WINDOWS):
        acc = uext[:, g * LANES:(g + 1) * LANES]
        sh = 1
        while sh < w:
            acc = acc + pltpu.roll(acc, sh, axis=0)
            sh *= 2
        ug = z_ref[:, O_U + g * LANES:O_U + (g + 1) * LANES]
        cnt = jnp.minimum(w, pos16 + 1).astype(F32)
        d_first = acc[16:32, :] / cnt - ug[0:16, :]
        d_rest = acc[32:, :] * (1.0 / w) - ug[16:, :]
        ds.append(jnp.concatenate([d_first, d_rest], axis=0).astype(BF16))
    uext[0:16, :] = uext[TB:TB + 16, :]

    z_ref[:, :O_K] = proj(slice(0, O_K))

    for pr in range(len(POOL_WINDOWS) // 2):
        cols = slice(pr * 2 * LANES, (pr + 1) * 2 * LANES)
        d2 = jnp.concatenate(ds[2 * pr:2 * pr + 2], axis=1)
        po = jnp.dot(d2, wpool_ref[pr], preferred_element_type=F32) * pscale_ref[:, cols]
        gp = z_ref[:, O_GP + pr * 2 * LANES:O_GP + (pr + 1) * 2 * LANES]
        mixed[:, D_ATTN + pr * 2 * LANES:D_ATTN + (pr + 1) * 2 * LANES] = (po * _silu_half(gp)).astype(BF16)

    for j in range(4):
        qhat = _pair_rms(z_ref[:, j * LANES:(j + 1) * LANES], lo, qw_ref[...])
        q_lo = jnp.where(lo, qhat, 0.0).astype(BF16)
        q_hi = jnp.where(lo, 0.0, qhat).astype(BF16)
        c, g0 = j // 2, 2 * (j % 2)
        for sb in range(NSB):
            rows = slice(sb * BLOCK, (sb + 1) * BLOCK)
            qbuf[c, sb, g0 * BLOCK:(g0 + 1) * BLOCK, :] = q_lo[rows]
            qbuf[c, sb, (g0 + 1) * BLOCK:(g0 + 2) * BLOCK, :] = q_hi[rows]

    half = TB // 2
    for r0 in (0, half):
        z_ref[r0:r0 + half, O_K:O_GA] = proj(slice(O_K, O_GA), slice(r0, r0 + half))
    z_ref[:, O_GA:O_U] = proj(slice(O_GA, O_U))

    khat = _pair_rms(z_ref[:, O_K:O_K + LANES], lo, kw_ref[...])
    kr = pltpu.roll(khat, HEAD_DIM, axis=1)
    kbuf[0, BLOCK:, :] = jnp.where(lo, khat, kr).astype(BF16)
    kbuf[1, BLOCK:, :] = jnp.where(lo, kr, khat).astype(BF16)
    vz = z_ref[:, O_V:O_V + LANES]
    vr = pltpu.roll(vz, HEAD_DIM, axis=1)
    vbuf[0, BLOCK:, :] = jnp.where(lo, vz, vr).astype(BF16)
    vbuf[1, BLOCK:, :] = jnp.where(lo, vr, vz).astype(BF16)

    r_io = lax.broadcasted_iota(jnp.int32, (BLOCK, 2 * BLOCK), 0)
    c_io = lax.broadcasted_iota(jnp.int32, (BLOCK, 2 * BLOCK), 1)
    band = (c_io >= r_io) & (c_io <= r_io + WINDOW)
    first_lo = jnp.where(i > 0, 0, BLOCK)
    band_first = band & (c_io >= first_lo)
    sink_pos = c_io == jnp.where(r_io >= 1, r_io - 1, 2 * BLOCK - 1)

    for sb in range(NSB):
        valid = band_first if sb == 0 else band
        rows = slice(sb * BLOCK, (sb + 1) * BLOCK)
        for c in range(2):
            keys = kbuf[c, sb * BLOCK:(sb + 2) * BLOCK, :]
            vals = vbuf[c, sb * BLOCK:(sb + 2) * BLOCK, :]
            s = lax.dot_general(qbuf[c, sb], keys, _NT, preferred_element_type=F32)
            ps, inv = [], []
            for g in range(GROUP):
                sg = jnp.where(valid, s[g * BLOCK:(g + 1) * BLOCK, :], fill_ref[c * GROUP + g])
                m = jnp.max(sg, axis=-1, keepdims=True)
                p = jnp.exp2(sg - m)
                l = jnp.sum(p, axis=-1, keepdims=True)
                ps.append(jnp.where(sink_pos, 0.0, p).astype(BF16))
                inv.append(1.0 / l)
            o = jnp.dot(jnp.concatenate(ps, axis=0), vals, preferred_element_type=F32)
            og = [o[g * BLOCK:(g + 1) * BLOCK, :] * inv[g] for g in range(GROUP)]
            for jj in range(2):
                slab = jnp.where(lo, og[2 * jj], og[2 * jj + 1])
                col = (2 * c + jj) * LANES
                ga = z_ref[rows, O_GA + col:O_GA + col + LANES]
                mixed[rows, col:col + LANES] = (slab * _silu_half(ga)).astype(BF16)

    kbuf[:, 0:BLOCK, :] = kbuf[:, TB:TB + BLOCK, :]
    vbuf[:, 0:BLOCK, :] = vbuf[:, TB:TB + BLOCK, :]

    y_ref[...] = x_ref[...] + jnp.dot(mixed[...], wout_ref[...], preferred_element_type=F32)

    @pl.when(last)
    def _():
        unew_ref[:, 0, :] = z_ref[TB - POOL_STATE:TB, O_U:O_U + D_POOL]
        tail = slice(TB - WINDOW, TB)
        knew_ref[...] = _pair_rms(z_ref[tail, O_K:O_K + LANES], lo, kw_ref[...]).T
        vnew_ref[...] = z_ref[tail, O_V:O_V + LANES].T


def _sample_init(x_ref, win_ref, qw_ref, kw_ref, z_ref, q3, krep, vrep, snew, kvt):
    nb = x_ref.shape[0]
    lo = _lo_mask()
    z_ref[...] = _project(x_ref[:, 0, :], win_ref[...])
    khat = _pair_rms(z_ref[:, O_K:O_K + LANES], lo, kw_ref[...])
    vnew = z_ref[:, O_V:O_V + LANES]
    kvt[0] = khat.T
    kvt[1] = vnew.T
    q3[...] = jnp.zeros(q3.shape, F32)
    krep[...] = jnp.zeros(krep.shape, F32)
    vrep[...] = jnp.zeros(vrep.shape, F32)
    for j in range(4):
        qhat = _pair_rms(z_ref[:, j * LANES:(j + 1) * LANES], lo, qw_ref[...])
        qrot = pltpu.roll(qhat, HEAD_DIM, axis=1)
        grp_lo = j < 2
        for half in range(2):
            h = 2 * j + half
            src = qhat if (half == 0) == grp_lo else qrot
            q3[pl.ds(h, nb, stride=QROWS), :] = jnp.where(lo if grp_lo else ~lo, src, 0.0)
            krep[pl.ds(h, nb, stride=QROWS), :] = khat
            vrep[pl.ds(h, nb, stride=QROWS), :] = vnew
    s_new = jnp.sum(q3[...] * krep[...], axis=-1, keepdims=True)
    snew[...] = jnp.broadcast_to(s_new, snew.shape)


def _shift_caches(i, nb, ck_ref, cv_ref, knew_ref, vnew_ref, kvt):
    shift = lax.rem(nb - i * CB, nb)
    newest = lax.broadcasted_iota(jnp.int32, (LANES, WINDOW), 1) == WINDOW - 1
    for src, dst, which in ((ck_ref, knew_ref, 0), (cv_ref, vnew_ref, 1)):
        cols = pltpu.roll(kvt[which], shift, axis=1)
        shifted = pltpu.roll(src[...], WINDOW - 1, axis=2)
        for bb in range(CB):
            dst[bb] = jnp.where(newest, cols[:, bb:bb + 1], shifted[bb])


def _sample_step(j, last, sinks_ref, x_ref, pscale_ref, ck_ref, cv_ref, sp_ref, y_ref, spnew_ref,
                 wout_ref, wpool_ref, z_ref, q3, krep, vrep, snew, kvt, o3, mixed):
    nb = x_ref.shape[0]
    lo = _lo_mask()
    row0 = pl.multiple_of(j * BB, BB)
    rows = pl.ds(row0, BB)
    qrows = pl.ds(pl.multiple_of(j * (BB * QROWS), BB * QROWS), BB * QROWS)
    rid = lax.broadcasted_iota(jnp.int32, (1, QROWS, 1), 1)
    sink3 = jnp.zeros((1, QROWS, 1), F32)
    for h in range(N_HEADS):
        sink3 = jnp.where(rid == h, sinks_ref[h] * LOG2E, sink3)

    q = q3[qrows, :].reshape(BB, QROWS, LANES)
    s_new = snew[qrows, :].reshape(BB, QROWS, LANES)
    kt = ck_ref[...]
    vt = cv_ref[...]
    s = lax.dot_general(q.astype(BF16), kt.astype(BF16), (((2,), (1,)), ((0,), (0,))), preferred_element_type=F32)
    m = jnp.maximum(jnp.maximum(jnp.max(s, axis=-1, keepdims=True), s_new), sink3)
    p = jnp.exp2(s - m)
    p_new = jnp.exp2(s_new - m)
    l = jnp.sum(p, axis=-1, keepdims=True) + p_new + jnp.exp2(sink3 - m)
    o = lax.dot_general(p.astype(BF16), vt.astype(BF16), (((2,), (2,)), ((0,), (0,))), preferred_element_type=F32)
    o = (o + p_new * vrep[qrows, :].reshape(BB, QROWS, LANES)) / l
    o3[...] = o.reshape(BB * QROWS, LANES)

    for jj in range(4):
        grp_lo = jj < 2
        oh = [o3[pl.ds(2 * jj + half, BB, stride=QROWS), :] for half in range(2)]
        a = oh[0] if grp_lo else pltpu.roll(oh[0], HEAD_DIM, axis=1)
        b = pltpu.roll(oh[1], HEAD_DIM, axis=1) if grp_lo else oh[1]
        ga = z_ref[rows, O_GA + jj * LANES:O_GA + (jj + 1) * LANES]
        mixed[rows, jj * LANES:(jj + 1) * LANES] = (jnp.where(lo, a, b) * _silu_half(ga)).astype(BF16)

    u = z_ref[rows, O_U:O_U + D_POOL]
    spnew_ref[0:POOL_STATE - 1] = sp_ref[1:POOL_STATE]
    spnew_ref[POOL_STATE - 1] = u
    ds = []
    for g, w in enumerate(POOL_WINDOWS):
        cols = slice(g * LANES, (g + 1) * LANES)
        ug = u[:, cols]
        win_sum = ug
        for r in range(POOL_STATE - (w - 1), POOL_STATE):
            win_sum = win_sum + sp_ref[r, :, cols]
        ds.append((win_sum * (1.0 / w) - ug).astype(BF16))
    for pr in range(len(POOL_WINDOWS) // 2):
        cols = slice(pr * 2 * LANES, (pr + 1) * 2 * LANES)
        d2 = jnp.concatenate(ds[2 * pr:2 * pr + 2], axis=1)
        po = jnp.dot(d2, wpool_ref[pr], preferred_element_type=F32) * pscale_ref[:, cols]
        gp = z_ref[rows, O_GP + pr * 2 * LANES:O_GP + (pr + 1) * 2 * LANES]
        mixed[rows, D_ATTN + pr * 2 * LANES:D_ATTN + (pr + 1) * 2 * LANES] = (po * _silu_half(gp)).astype(BF16)

    @pl.when(last)
    def _():
        y_ref[:, 0, :] = x_ref[:, 0, :] + jnp.dot(mixed[...], wout_ref[...], preferred_element_type=F32)


def _fused_kernel(n_prompt, sinks_ref, xp_ref, xs_ref, nw_ref, win_hbm, qn_ref, kn_ref, wpool_f32, pscale_ref, wout_hbm,
                  ck_ref, cv_ref, sp_ref, ckc_ref, cvc_ref,
                  yp_ref, kp_ref, vp_ref, up_ref, ys_ref, kq_ref, vq_ref, pq_ref,
                  win_ref, wout_ref, wpool_ref, qw_ref, kw_ref, stage, sem,
                  z_ref, hbuf, qbuf, kbuf, vbuf, uext, mixed, fill_ref,
                  zs_ref, q3, krep, vrep, snew, kvt, o3, mixed_s):
    i = pl.program_id(0)

    @pl.when(i == 0)
    def _():
        _load_weights(win_hbm, wout_hbm, wpool_f32, nw_ref, qn_ref, kn_ref, stage, sem, win_ref, wout_ref, wpool_ref,
                      qw_ref, kw_ref)
        _prompt_init(sinks_ref, kbuf, vbuf, uext, fill_ref)
        _sample_init(xs_ref, win_ref, qw_ref, kw_ref, zs_ref, q3, krep, vrep, snew, kvt)

    @pl.when(i < n_prompt)
    def _():
        _shift_caches(i, xs_ref.shape[0], ckc_ref, cvc_ref, kq_ref, vq_ref, kvt)
        _prompt_step(i, i == n_prompt - 1, xp_ref, pscale_ref, yp_ref, kp_ref, vp_ref, up_ref,
                     win_ref, wout_ref, wpool_ref, qw_ref, kw_ref, z_ref, hbuf, qbuf, kbuf, vbuf, uext, mixed, fill_ref)

    @pl.when(i >= n_prompt)
    def _():
        _sample_step(i - n_prompt, i == pl.num_programs(0) - 1, sinks_ref, xs_ref, pscale_ref, ck_ref, cv_ref, sp_ref,
                     ys_ref, pq_ref, wout_ref, wpool_ref, zs_ref, q3, krep, vrep, snew, kvt, o3, mixed_s)


def _fused_call(sinks, xp, xs, nw, win, qn, kn, wpool, pscale, wout, ck, cv, sp):
    seq, nb = xp.shape[0], xs.shape[0]
    n_prompt, n_sample = seq // TB, nb // BB
    assert nb == CB * n_prompt
    const2 = lambda i: (0, 0)
    const3 = lambda i: (0, 0, 0)
    pblk = lambda i: (jnp.minimum(i, n_prompt - 1), 0)
    cblk = lambda i: (jnp.minimum(i, n_prompt - 1), 0, 0)
    sidx = lambda i: jnp.clip(i - n_prompt, 0, n_sample - 1)
    chunk3 = lambda i: (sidx(i), 0, 0)
    mid3 = lambda i: (0, sidx(i), 0)
    once = pl.Buffered(1)
    return pl.pallas_call(
        functools.partial(_fused_kernel, n_prompt),
        grid=(n_prompt + n_sample,),
        in_specs=[
            pl.BlockSpec(memory_space=pltpu.SMEM),
            pl.BlockSpec((TB, D_MODEL), pblk),
            pl.BlockSpec((nb, 1, D_MODEL), const3, pipeline_mode=once),
            pl.BlockSpec((1, D_MODEL), const2),
            pl.BlockSpec(memory_space=pl.ANY),
            pl.BlockSpec((1, HEAD_DIM), const2),
            pl.BlockSpec((1, HEAD_DIM), const2),
            pl.BlockSpec((4, LANES, LANES), const3, pipeline_mode=once),
            pl.BlockSpec((1, D_POOL), const2),
            pl.BlockSpec(memory_space=pl.ANY),
            pl.BlockSpec((BB, LANES, WINDOW), chunk3),
            pl.BlockSpec((BB, LANES, WINDOW), chunk3),
            pl.BlockSpec((POOL_STATE, BB, D_POOL), mid3),
            pl.BlockSpec((CB, LANES, WINDOW), cblk),
            pl.BlockSpec((CB, LANES, WINDOW), cblk),
        ],
        out_specs=[
            pl.BlockSpec((TB, D_MODEL), pblk),
            pl.BlockSpec((LANES, WINDOW), const2),
            pl.BlockSpec((LANES, WINDOW), const2),
            pl.BlockSpec((POOL_STATE, 1, D_POOL), const3),
            pl.BlockSpec((nb, 1, D_MODEL), const3, pipeline_mode=once),
            pl.BlockSpec((CB, LANES, WINDOW), cblk),
            pl.BlockSpec((CB, LANES, WINDOW), cblk),
            pl.BlockSpec((POOL_STATE, BB, D_POOL), mid3),
        ],
        out_shape=[
            jax.ShapeDtypeStruct((seq, D_MODEL), F32),
            jax.ShapeDtypeStruct((LANES, WINDOW), F32),
            jax.ShapeDtypeStruct((LANES, WINDOW), F32),
            jax.ShapeDtypeStruct((POOL_STATE, 1, D_POOL), F32),
            jax.ShapeDtypeStruct((nb, 1, D_MODEL), F32),
            jax.ShapeDtypeStruct((nb, LANES, WINDOW), F32),
            jax.ShapeDtypeStruct((nb, LANES, WINDOW), F32),
            jax.ShapeDtypeStruct((POOL_STATE, nb, D_POOL), F32),
        ],
        scratch_shapes=[
            pltpu.VMEM((D_MODEL, D_IN_PROJ), BF16),
            pltpu.VMEM((D_MODEL, D_MODEL), BF16),
            pltpu.VMEM((2, 2 * LANES, 2 * LANES), BF16),
            pltpu.VMEM((1, LANES), F32),
            pltpu.VMEM((1, LANES), F32),
            pltpu.VMEM((2, D_MODEL, WCHUNK), F32),
            pltpu.SemaphoreType.DMA((2,)),
            pltpu.VMEM((TB, D_IN_PROJ), F32),
            pltpu.VMEM((TB, D_MODEL), BF16),
            pltpu.VMEM((2, NSB, GROUP * BLOCK, LANES), BF16),
            pltpu.VMEM((2, TB + BLOCK, LANES), BF16),
            pltpu.VMEM((2, TB + BLOCK, LANES), BF16),
            pltpu.VMEM((TB + 16, D_POOL), F32),
            pltpu.VMEM((TB, D_MODEL), BF16),
            pltpu.VMEM((N_HEADS, BLOCK, 2 * BLOCK), F32),
            pltpu.VMEM((nb, D_IN_PROJ), F32),
            pltpu.VMEM((nb * QROWS, LANES), F32),
            pltpu.VMEM((nb * QROWS, LANES), F32),
            pltpu.VMEM((nb * QROWS, LANES), F32),
            pltpu.VMEM((nb * QROWS, LANES), F32),
            pltpu.VMEM((2, LANES, nb), F32),
            pltpu.VMEM((BB * QROWS, LANES), F32),
            pltpu.VMEM((nb, D_MODEL), BF16),
        ],
        compiler_params=pltpu.CompilerParams(dimension_semantics=("arbitrary",), vmem_limit_bytes=VMEM_LIMIT),
        name="hybrid_step",
    )(sinks, xp, xs, nw, win, qn, kn, wpool, pscale, wout, ck, cv, sp, ck, cv)


def _cache_in(c):
    nb = c.shape[0]
    return jnp.transpose(c, (0, 2, 3, 1)).reshape(nb, LANES, WINDOW)


def _cache_out(c):
    nb = c.shape[0]
    return jnp.transpose(c.reshape(nb, N_KV_HEADS, HEAD_DIM, WINDOW), (0, 3, 1, 2))


def kernel(x_prompt, x_sample, cache_k, cache_v, state_pool, norm_w, w_in, q_norm_w, k_norm_w, sinks, w_pool,
           pool_scale, w_out):
    depth = norm_w.shape[0]
    assert depth == 1 and x_prompt.shape[0] == 1 and x_sample.shape[1] == 1
    seq = x_prompt.shape[1]
    nb = x_sample.shape[0]
    assert seq % TB == 0 and nb % BB == 0 and nb == LANES

    yp, kp, vp, up, ys, kq, vq, pq = _fused_call(
        sinks[0], x_prompt[0], x_sample, norm_w, w_in[0], q_norm_w, k_norm_w, w_pool[0], pool_scale, w_out[0],
        _cache_in(cache_k[0]), _cache_in(cache_v[0]), jnp.transpose(state_pool[0], (1, 0, 2)))
    return (yp[None], ys, _cache_out(kp[None])[None], _cache_out(vp[None])[None],
            jnp.transpose(up, (1, 0, 2))[None],
            _cache_out(kq)[None], _cache_out(vq)[None], jnp.transpose(pq, (1, 0, 2))[None])
```

```python
import functools

import jax
import jax.numpy as jnp
from jax import lax
from jax.experimental import pallas as pl
from jax.experimental.pallas import tpu as pltpu

D_MODEL = 1024
HEAD_DIM = 64
N_HEADS = 8
N_KV_HEADS = 2
GROUP = 4
WINDOW = 128
BLOCK = 128
POOL_WINDOWS = (2, 4, 8, 16)
POOL_STATE = 15
D_ATTN = 512
D_POOL = 512
D_IN_PROJ = 2304
EPS = 1e-6
NEG_INF = -1e30
LOG2E = 1.4426950408889634

O_K = 512
O_V = 640
O_GA = 768
O_U = 1280
O_GP = 1792

LANES = 128
TB = 512
NSB = TB // BLOCK
BB = 16
CB = 4
QROWS = 16
WCHUNK = 256
VMEM_LIMIT = 58 * 1024 * 1024

F32 = jnp.float32
BF16 = jnp.bfloat16
_NT = (((1,), (1,)), ((), ()))


def _silu_half(h):
    return h + h * jnp.tanh(h)


def _lo_mask():
    return lax.broadcasted_iota(jnp.int32, (1, LANES), 1) < HEAD_DIM


def _pair_rms(zs, lo, w2):
    sq = zs * zs
    s_lo = jnp.sum(jnp.where(lo, sq, 0.0), axis=-1, keepdims=True)
    s_hi = jnp.sum(jnp.where(lo, 0.0, sq), axis=-1, keepdims=True)
    r = lax.rsqrt(jnp.where(lo, s_lo, s_hi) * (1.0 / HEAD_DIM) + EPS)
    return zs * r * w2


def _row_scale(x):
    return lax.rsqrt(jnp.mean(x * x, axis=-1, keepdims=True) + EPS)


def _project(x, win):
    return jnp.dot(x.astype(BF16), win, preferred_element_type=F32) * _row_scale(x)


def _weight_chunks():
    chunks = []
    for c0 in range(0, D_IN_PROJ, WCHUNK):
        gate = O_GA <= c0 < O_U or c0 >= O_GP
        chunks.append((0, c0, 0.5 if gate else 1.0))
    chunks += [(1, c0, 1.0) for c0 in range(0, D_MODEL, WCHUNK)]
    return chunks


def _load_weights(win_hbm, wout_hbm, wpool_f32, nw_ref, qn_ref, kn_ref, stage, sem, win_bf, wout_bf, wpool_bf,
                  qw_ref, kw_ref):
    chunks = _weight_chunks()
    srcs, dsts = (win_hbm, wout_hbm), (win_bf, wout_bf)
    nw_tile = jnp.concatenate([jnp.broadcast_to(nw_ref[:, t * LANES:(t + 1) * LANES], (LANES, LANES)).T
                               for t in range(D_MODEL // LANES)], axis=0)
    nw_rows = jnp.concatenate([nw_tile] * (WCHUNK // LANES), axis=1)

    def copy(k):
        which, c0, _ = chunks[k]
        return pltpu.make_async_copy(srcs[which].at[:, pl.ds(c0, WCHUNK)], stage.at[k % 2], sem.at[k % 2])

    copy(0).start()
    copy(1).start()
    for k, (which, c0, scale) in enumerate(chunks):
        copy(k).wait()
        w = stage[k % 2]
        if which == 0:
            w = w * nw_rows if scale == 1.0 else w * (scale * nw_rows)
        dsts[which][:, c0:c0 + WCHUNK] = w.astype(BF16)
        if k + 2 < len(chunks):
            copy(k + 2).start()
    wpool_bf[...] = jnp.zeros(wpool_bf.shape, BF16)
    for g in range(len(POOL_WINDOWS)):
        d0 = (g % 2) * LANES
        wpool_bf[g // 2, d0:d0 + LANES, d0:d0 + LANES] = wpool_f32[g].astype(BF16)
    qn = qn_ref[...] * (HEAD_DIM ** -0.5 * LOG2E)
    qw_ref[...] = jnp.concatenate([qn, qn], axis=1)
    kw_ref[...] = jnp.concatenate([kn_ref[...], kn_ref[...]], axis=1)


def _prompt_init(sinks_ref, kbuf, vbuf, uext, fill_ref):
    kbuf[:, 0:BLOCK, :] = jnp.zeros((2, BLOCK, LANES), BF16)
    vbuf[:, 0:BLOCK, :] = jnp.zeros((2, BLOCK, LANES), BF16)
    uext[0:16, :] = jnp.zeros((16, D_POOL), F32)
    r0 = lax.broadcasted_iota(jnp.int32, (BLOCK, 2 * BLOCK), 0)
    c0 = lax.broadcasted_iota(jnp.int32, (BLOCK, 2 * BLOCK), 1)
    sink_col = jnp.where(r0 >= 1, r0 - 1, 2 * BLOCK - 1)
    for h in range(N_HEADS):
        fill_ref[h] = jnp.where(c0 == sink_col, sinks_ref[h] * LOG2E, NEG_INF)


def _prompt_step(i, last, x_ref, pscale_ref, y_ref, knew_ref, vnew_ref, unew_ref,
                 win_ref, wout_ref, wpool_ref, qw_ref, kw_ref, z_ref, hbuf, qbuf, kbuf, vbuf, uext, mixed, fill_ref):
    lo = _lo_mask()
    hbuf[...] = x_ref[...].astype(BF16)
    rs = _row_scale(x_ref[...])

    def proj(cols, rows=slice(None)):
        return jnp.dot(hbuf[rows, :], win_ref[:, cols], preferred_element_type=F32) * rs[rows]

    z_ref[:, O_U:] = proj(slice(O_U, D_IN_PROJ))
    uext[16:, :] = z_ref[:, O_U:O_U + D_POOL]
    pos16 = i * TB + lax.broadcasted_iota(jnp.int32, (16, LANES), 0)
    ds = []
    for g, w in enumerate(POOL_WINDOWS):
        acc = uext[:, g * LANES:(g + 1) * LANES]
        sh = 1
        while sh < w:
            acc = acc + pltpu.roll(acc, sh, axis=0)
            sh *= 2
        ug = z_ref[:, O_U + g * LANES:O_U + (g + 1) * LANES]
        cnt = jnp.minimum(w, pos16 + 1).astype(F32)
        d_first = acc[16:32, :] / cnt - ug[0:16, :]
        d_rest = acc[32:, :] * (1.0 / w) - ug[16:, :]
        ds.append(jnp.concatenate([d_first, d_rest], axis=0).astype(BF16))
    uext[0:16, :] = uext[TB:TB + 16, :]

    z_ref[:, :O_K] = proj(slice(0, O_K))

    for pr in range(len(POOL_WINDOWS) // 2):
        cols = slice(pr * 2 * LANES, (pr + 1) * 2 * LANES)
        d2 = jnp.concatenate(ds[2 * pr:2 * pr + 2], axis=1)
        po = jnp.dot(d2, wpool_ref[pr], preferred_element_type=F32) * pscale_ref[:, cols]
        gp = z_ref[:, O_GP + pr * 2 * LANES:O_GP + (pr + 1) * 2 * LANES]
        mixed[:, D_ATTN + pr * 2 * LANES:D_ATTN + (pr + 1) * 2 * LANES] = (po * _silu_half(gp)).astype(BF16)

    for j in range(4):
        qhat = _pair_rms(z_ref[:, j * LANES:(j + 1) * LANES], lo, qw_ref[...])
        q_lo = jnp.where(lo, qhat, 0.0).astype(BF16)
        q_hi = jnp.where(lo, 0.0, qhat).astype(BF16)
        c, g0 = j // 2, 2 * (j % 2)
        for sb in range(NSB):
            rows = slice(sb * BLOCK, (sb + 1) * BLOCK)
            qbuf[c, sb, g0 * BLOCK:(g0 + 1) * BLOCK, :] = q_lo[rows]
            qbuf[c, sb, (g0 + 1) * BLOCK:(g0 + 2) * BLOCK, :] = q_hi[rows]

    half = TB // 2
    for r0 in (0, half):
        z_ref[r0:r0 + half, O_K:O_GA] = proj(slice(O_K, O_GA), slice(r0, r0 + half))
    z_ref[:, O_GA:O_U] = proj(slice(O_GA, O_U))

    khat = _pair_rms(z_ref[:, O_K:O_K + LANES], lo, kw_ref[...])
    kr = pltpu.roll(khat, HEAD_DIM, axis=1)
    kbuf[0, BLOCK:, :] = jnp.where(lo, khat, kr).astype(BF16)
    kbuf[1, BLOCK:, :] = jnp.where(lo, kr, khat).astype(BF16)
    vz = z_ref[:, O_V:O_V + LANES]
    vr = pltpu.roll(vz, HEAD_DIM, axis=1)
    vbuf[0, BLOCK:, :] = jnp.where(lo, vz, vr).astype(BF16)
    vbuf[1, BLOCK:, :] = jnp.where(lo, vr, vz).astype(BF16)

    r_io = lax.broadcasted_iota(jnp.int32, (BLOCK, 2 * BLOCK), 0)
    c_io = lax.broadcasted_iota(jnp.int32, (BLOCK, 2 * BLOCK), 1)
    band = (c_io >= r_io) & (c_io <= r_io + WINDOW)
    first_lo = jnp.where(i > 0, 0, BLOCK)
    band_first = band & (c_io >= first_lo)
    sink_pos = c_io == jnp.where(r_io >= 1, r_io - 1, 2 * BLOCK - 1)

    for sb in range(NSB):
        valid = band_first if sb == 0 else band
        rows = slice(sb * BLOCK, (sb + 1) * BLOCK)
        for c in range(2):
            keys = kbuf[c, sb * BLOCK:(sb + 2) * BLOCK, :]
            vals = vbuf[c, sb * BLOCK:(sb + 2) * BLOCK, :]
            s = lax.dot_general(qbuf[c, sb], keys, _NT, preferred_element_type=F32)
            ps, inv = [], []
            for g in range(GROUP):
                sg = jnp.where(valid, s[g * BLOCK:(g + 1) * BLOCK, :], fill_ref[c * GROUP + g])
                m = jnp.max(sg, axis=-1, keepdims=True)
                p = jnp.exp2((sg - m).astype(BF16))
                l = jnp.sum(p.astype(F32), axis=-1, keepdims=True)
                ps.append(jnp.where(sink_pos, jnp.zeros_like(p), p))
                inv.append(1.0 / l)
            o = jnp.dot(jnp.concatenate(ps, axis=0), vals, preferred_element_type=F32)
            og = [o[g * BLOCK:(g + 1) * BLOCK, :] * inv[g] for g in range(GROUP)]
            for jj in range(2):
                slab = jnp.where(lo, og[2 * jj], og[2 * jj + 1])
                col = (2 * c + jj) * LANES
                ga = z_ref[rows, O_GA + col:O_GA + col + LANES]
                mixed[rows, col:col + LANES] = (slab * _silu_half(ga)).astype(BF16)

    kbuf[:, 0:BLOCK, :] = kbuf[:, TB:TB + BLOCK, :]
    vbuf[:, 0:BLOCK, :] = vbuf[:, TB:TB + BLOCK, :]

    y_ref[...] = x_ref[...] + jnp.dot(mixed[...], wout_ref[...], preferred_element_type=F32)

    @pl.when(last)
    def _():
        unew_ref[:, 0, :] = z_ref[TB - POOL_STATE:TB, O_U:O_U + D_POOL]
        tail = slice(TB - WINDOW, TB)
        knew_ref[...] = _pair_rms(z_ref[tail, O_K:O_K + LANES], lo, kw_ref[...]).T
        vnew_ref[...] = z_ref[tail, O_V:O_V + LANES].T


def _sample_init(x_ref, win_ref, qw_ref, kw_ref, z_ref, q3, krep, vrep, snew, kvt):
    nb = x_ref.shape[0]
    lo = _lo_mask()
    z_ref[...] = _project(x_ref[:, 0, :], win_ref[...])
    khat = _pair_rms(z_ref[:, O_K:O_K + LANES], lo, kw_ref[...])
    vnew = z_ref[:, O_V:O_V + LANES]
    kvt[0] = khat.T
    kvt[1] = vnew.T
    q3[...] = jnp.zeros(q3.shape, F32)
    krep[...] = jnp.zeros(krep.shape, F32)
    vrep[...] = jnp.zeros(vrep.shape, F32)
    for j in range(4):
        qhat = _pair_rms(z_ref[:, j * LANES:(j + 1) * LANES], lo, qw_ref[...])
        qrot = pltpu.roll(qhat, HEAD_DIM, axis=1)
        grp_lo = j < 2
        for half in range(2):
            h = 2 * j + half
            src = qhat if (half == 0) == grp_lo else qrot
            q3[pl.ds(h, nb, stride=QROWS), :] = jnp.where(lo if grp_lo else ~lo, src, 0.0)
            krep[pl.ds(h, nb, stride=QROWS), :] = khat
            vrep[pl.ds(h, nb, stride=QROWS), :] = vnew
    s_new = jnp.sum(q3[...] * krep[...], axis=-1, keepdims=True)
    snew[...] = jnp.broadcast_to(s_new, snew.shape)


def _shift_caches(i, nb, ck_ref, cv_ref, knew_ref, vnew_ref, kvt):
    shift = lax.rem(nb - i * CB, nb)
    newest = lax.broadcasted_iota(jnp.int32, (LANES, WINDOW), 1) == WINDOW - 1
    for src, dst, which in ((ck_ref, knew_ref, 0), (cv_ref, vnew_ref, 1)):
        cols = pltpu.roll(kvt[which], shift, axis=1)
        shifted = pltpu.roll(src[...], WINDOW - 1, axis=2)
        for bb in range(CB):
            dst[bb] = jnp.where(newest, cols[:, bb:bb + 1], shifted[bb])


def _sample_step(j, last, sinks_ref, x_ref, pscale_ref, ck_ref, cv_ref, sp_ref, y_ref, spnew_ref,
                 wout_ref, wpool_ref, z_ref, q3, krep, vrep, snew, kvt, o3, mixed):
    nb = x_ref.shape[0]
    lo = _lo_mask()
    row0 = pl.multiple_of(j * BB, BB)
    rows = pl.ds(row0, BB)
    qrows = pl.ds(pl.multiple_of(j * (BB * QROWS), BB * QROWS), BB * QROWS)
    rid = lax.broadcasted_iota(jnp.int32, (1, QROWS, 1), 1)
    sink3 = jnp.zeros((1, QROWS, 1), F32)
    for h in range(N_HEADS):
        sink3 = jnp.where(rid == h, sinks_ref[h] * LOG2E, sink3)

    q = q3[qrows, :].reshape(BB, QROWS, LANES)
    s_new = snew[qrows, :].reshape(BB, QROWS, LANES)
    kt = ck_ref[...]
    vt = cv_ref[...]
    s = lax.dot_general(q.astype(BF16), kt.astype(BF16), (((2,), (1,)), ((0,), (0,))), preferred_element_type=F32)
    m = jnp.maximum(jnp.maximum(jnp.max(s, axis=-1, keepdims=True), s_new), sink3)
    p = jnp.exp2(s - m)
    p_new = jnp.exp2(s_new - m)
    l = jnp.sum(p, axis=-1, keepdims=True) + p_new + jnp.exp2(sink3 - m)
    o = lax.dot_general(p.astype(BF16), vt.astype(BF16), (((2,), (2,)), ((0,), (0,))), preferred_element_type=F32)
    o = (o + p_new * vrep[qrows, :].reshape(BB, QROWS, LANES)) / l
    o3[...] = o.reshape(BB * QROWS, LANES)

    for jj in range(4):
        grp_lo = jj < 2
        oh = [o3[pl.ds(2 * jj + half, BB, stride=QROWS), :] for half in range(2)]
        a = oh[0] if grp_lo else pltpu.roll(oh[0], HEAD_DIM, axis=1)
        b = pltpu.roll(oh[1], HEAD_DIM, axis=1) if grp_lo else oh[1]
        ga = z_ref[rows, O_GA + jj * LANES:O_GA + (jj + 1) * LANES]
        mixed[rows, jj * LANES:(jj + 1) * LANES] = (jnp.where(lo, a, b) * _silu_half(ga)).astype(BF16)

    u = z_ref[rows, O_U:O_U + D_POOL]
    spnew_ref[0:POOL_STATE - 1] = sp_ref[1:POOL_STATE]
    spnew_ref[POOL_STATE - 1] = u
    ds = []
    for g, w in enumerate(POOL_WINDOWS):
        cols = slice(g * LANES, (g + 1) * LANES)
        ug = u[:, cols]
        win_sum = ug
        for r in range(POOL_STATE - (w - 1), POOL_STATE):
            win_sum = win_sum + sp_ref[r, :, cols]
        ds.append((win_sum * (1.0 / w) - ug).astype(BF16))
    for pr in range(len(POOL_WINDOWS) // 2):
        cols = slice(pr * 2 * LANES, (pr + 1) * 2 * LANES)
        d2 = jnp.concatenate(ds[2 * pr:2 * pr + 2], axis=1)
        po = jnp.dot(d2, wpool_ref[pr], preferred_element_type=F32) * pscale_ref[:, cols]
        gp = z_ref[rows, O_GP + pr * 2 * LANES:O_GP + (pr + 1) * 2 * LANES]
        mixed[rows, D_ATTN + pr * 2 * LANES:D_ATTN + (pr + 1) * 2 * LANES] = (po * _silu_half(gp)).astype(BF16)

    @pl.when(last)
    def _():
        y_ref[:, 0, :] = x_ref[:, 0, :] + jnp.dot(mixed[...], wout_ref[...], preferred_element_type=F32)


def _fused_kernel(n_prompt, sinks_ref, xp_ref, xs_ref, nw_ref, win_hbm, qn_ref, kn_ref, wpool_f32, pscale_ref, wout_hbm,
                  ck_ref, cv_ref, sp_ref, ckc_ref, cvc_ref,
                  yp_ref, kp_ref, vp_ref, up_ref, ys_ref, kq_ref, vq_ref, pq_ref,
                  win_ref, wout_ref, wpool_ref, qw_ref, kw_ref, stage, sem,
                  z_ref, hbuf, qbuf, kbuf, vbuf, uext, mixed, fill_ref,
                  zs_ref, q3, krep, vrep, snew, kvt, o3, mixed_s):
    i = pl.program_id(0)

    @pl.when(i == 0)
    def _():
        _load_weights(win_hbm, wout_hbm, wpool_f32, nw_ref, qn_ref, kn_ref, stage, sem, win_ref, wout_ref, wpool_ref,
                      qw_ref, kw_ref)
        _prompt_init(sinks_ref, kbuf, vbuf, uext, fill_ref)
        _sample_init(xs_ref, win_ref, qw_ref, kw_ref, zs_ref, q3, krep, vrep, snew, kvt)

    @pl.when(i < n_prompt)
    def _():
        _shift_caches(i, xs_ref.shape[0], ckc_ref, cvc_ref, kq_ref, vq_ref, kvt)
        _prompt_step(i, i == n_prompt - 1, xp_ref, pscale_ref, yp_ref, kp_ref, vp_ref, up_ref,
                     win_ref, wout_ref, wpool_ref, qw_ref, kw_ref, z_ref, hbuf, qbuf, kbuf, vbuf, uext, mixed, fill_ref)

    @pl.when(i >= n_prompt)
    def _():
        _sample_step(i - n_prompt, i == pl.num_programs(0) - 1, sinks_ref, xs_ref, pscale_ref, ck_ref, cv_ref, sp_ref,
                     ys_ref, pq_ref, wout_ref, wpool_ref, zs_ref, q3, krep, vrep, snew, kvt, o3, mixed_s)


def _fused_call(sinks, xp, xs, nw, win, qn, kn, wpool, pscale, wout, ck, cv, sp):
    seq, nb = xp.shape[0], xs.shape[0]
    n_prompt, n_sample = seq // TB, nb // BB
    assert nb == CB * n_prompt
    const2 = lambda i: (0, 0)
    const3 = lambda i: (0, 0, 0)
    pblk = lambda i: (jnp.minimum(i, n_prompt - 1), 0)
    cblk = lambda i: (jnp.minimum(i, n_prompt - 1), 0, 0)
    sidx = lambda i: jnp.clip(i - n_prompt, 0, n_sample - 1)
    chunk3 = lambda i: (sidx(i), 0, 0)
    mid3 = lambda i: (0, sidx(i), 0)
    once = pl.Buffered(1)
    return pl.pallas_call(
        functools.partial(_fused_kernel, n_prompt),
        grid=(n_prompt + n_sample,),
        in_specs=[
            pl.BlockSpec(memory_space=pltpu.SMEM),
            pl.BlockSpec((TB, D_MODEL), pblk),
            pl.BlockSpec((nb, 1, D_MODEL), const3, pipeline_mode=once),
            pl.BlockSpec((1, D_MODEL), const2),
            pl.BlockSpec(memory_space=pl.ANY),
            pl.BlockSpec((1, HEAD_DIM), const2),
            pl.BlockSpec((1, HEAD_DIM), const2),
            pl.BlockSpec((4, LANES, LANES), const3, pipeline_mode=once),
            pl.BlockSpec((1, D_POOL), const2),
            pl.BlockSpec(memory_space=pl.ANY),
            pl.BlockSpec((BB, LANES, WINDOW), chunk3),
            pl.BlockSpec((BB, LANES, WINDOW), chunk3),
            pl.BlockSpec((POOL_STATE, BB, D_POOL), mid3),
            pl.BlockSpec((CB, LANES, WINDOW), cblk),
            pl.BlockSpec((CB, LANES, WINDOW), cblk),
        ],
        out_specs=[
            pl.BlockSpec((TB, D_MODEL), pblk),
            pl.BlockSpec((LANES, WINDOW), const2),
            pl.BlockSpec((LANES, WINDOW), const2),
            pl.BlockSpec((POOL_STATE, 1, D_POOL), const3),
            pl.BlockSpec((nb, 1, D_MODEL), const3, pipeline_mode=once),
            pl.BlockSpec((CB, LANES, WINDOW), cblk),
            pl.BlockSpec((CB, LANES, WINDOW), cblk),
            pl.BlockSpec((POOL_STATE, BB, D_POOL), mid3),
        ],
        out_shape=[
            jax.ShapeDtypeStruct((seq, D_MODEL), F32),
            jax.ShapeDtypeStruct((LANES, WINDOW), F32),
            jax.ShapeDtypeStruct((LANES, WINDOW), F32),
            jax.ShapeDtypeStruct((POOL_STATE, 1, D_POOL), F32),
            jax.ShapeDtypeStruct((nb, 1, D_MODEL), F32),
            jax.ShapeDtypeStruct((nb, LANES, WINDOW), F32),
            jax.ShapeDtypeStruct((nb, LANES, WINDOW), F32),
            jax.ShapeDtypeStruct((POOL_STATE, nb, D_POOL), F32),
        ],
        scratch_shapes=[
            pltpu.VMEM((D_MODEL, D_IN_PROJ), BF16),
            pltpu.VMEM((D_MODEL, D_MODEL), BF16),
            pltpu.VMEM((2, 2 * LANES, 2 * LANES), BF16),
            pltpu.VMEM((1, LANES), F32),
            pltpu.VMEM((1, LANES), F32),
            pltpu.VMEM((2, D_MODEL, WCHUNK), F32),
            pltpu.SemaphoreType.DMA((2,)),
            pltpu.VMEM((TB, D_IN_PROJ), F32),
            pltpu.VMEM((TB, D_MODEL), BF16),
            pltpu.VMEM((2, NSB, GROUP * BLOCK, LANES), BF16),
            pltpu.VMEM((2, TB + BLOCK, LANES), BF16),
            pltpu.VMEM((2, TB + BLOCK, LANES), BF16),
            pltpu.VMEM((TB + 16, D_POOL), F32),
            pltpu.VMEM((TB, D_MODEL), BF16),
            pltpu.VMEM((N_HEADS, BLOCK, 2 * BLOCK), F32),
            pltpu.VMEM((nb, D_IN_PROJ), F32),
            pltpu.VMEM((nb * QROWS, LANES), F32),
            pltpu.VMEM((nb * QROWS, LANES), F32),
            pltpu.VMEM((nb * QROWS, LANES), F32),
            pltpu.VMEM((nb * QROWS, LANES), F32),
            pltpu.VMEM((2, LANES, nb), F32),
            pltpu.VMEM((BB * QROWS, LANES), F32),
            pltpu.VMEM((nb, D_MODEL), BF16),
        ],
        compiler_params=pltpu.CompilerParams(dimension_semantics=("arbitrary",), vmem_limit_bytes=VMEM_LIMIT),
        name="hybrid_step",
    )(sinks, xp, xs, nw, win, qn, kn, wpool, pscale, wout, ck, cv, sp, ck, cv)


def _cache_in(c):
    nb = c.shape[0]
    return jnp.transpose(c, (0, 2, 3, 1)).reshape(nb, LANES, WINDOW)


def _cache_out(c):
    nb = c.shape[0]
    return jnp.transpose(c.reshape(nb, N_KV_HEADS, HEAD_DIM, WINDOW), (0, 3, 1, 2))


def kernel(x_prompt, x_sample, cache_k, cache_v, state_pool, norm_w, w_in, q_norm_w, k_norm_w, sinks, w_pool,
           pool_scale, w_out):
    depth = norm_w.shape[0]
    assert depth == 1 and x_prompt.shape[0] == 1 and x_sample.shape[1] == 1
    seq = x_prompt.shape[1]
    nb = x_sample.shape[0]
    assert seq % TB == 0 and nb % BB == 0 and nb == LANES

    yp, kp, vp, up, ys, kq, vq, pq = _fused_call(
        sinks[0], x_prompt[0], x_sample, norm_w, w_in[0], q_norm_w, k_norm_w, w_pool[0], pool_scale, w_out[0],
        _cache_in(cache_k[0]), _cache_in(cache_v[0]), jnp.transpose(state_pool[0], (1, 0, 2)))
    return (yp[None], ys, _cache_out(kp[None])[None], _cache_out(vp[None])[None],
            jnp.transpose(up, (1, 0, 2))[None],
            _cache_out(kq)[None], _cache_out(vq)[None], jnp.transpose(pq, (1, 0, 2))[None])
```

```python
import functools

import jax
import jax.numpy as jnp
from jax import lax
from jax.experimental import pallas as pl
from jax.experimental.pallas import tpu as pltpu

D_MODEL = 1024
HEAD_DIM = 64
N_HEADS = 8
N_KV_HEADS = 2
GROUP = 4
WINDOW = 128
BLOCK = 128
POOL_WINDOWS = (2, 4, 8, 16)
POOL_STATE = 15
D_ATTN = 512
D_POOL = 512
D_IN_PROJ = 2304
EPS = 1e-6
NEG_INF = -1e30
LOG2E = 1.4426950408889634

O_K = 512
O_V = 640
O_GA = 768
O_U = 1280
O_GP = 1792

LANES = 128
TB = 512
NSB = TB // BLOCK
BB = 16
CB = 4
QROWS = 16
WCHUNK = 256
VMEM_LIMIT = 58 * 1024 * 1024

F32 = jnp.float32
BF16 = jnp.bfloat16
_NT = (((1,), (1,)), ((), ()))


def _silu_half(h):
    return h + h * jnp.tanh(h)


def _lo_mask():
    return lax.broadcasted_iota(jnp.int32, (1, LANES), 1) < HEAD_DIM


def _pair_rms(zs, lo, w2):
    sq = zs * zs
    s_lo = jnp.sum(jnp.where(lo, sq, 0.0), axis=-1, keepdims=True)
    s_hi = jnp.sum(jnp.where(lo, 0.0, sq), axis=-1, keepdims=True)
    r = lax.rsqrt(jnp.where(lo, s_lo, s_hi) * (1.0 / HEAD_DIM) + EPS)
    return zs * r * w2


def _row_scale(x):
    return lax.rsqrt(jnp.mean(x * x, axis=-1, keepdims=True) + EPS)


def _project(x, win):
    return jnp.dot(x.astype(BF16), win, preferred_element_type=F32) * _row_scale(x)


def _weight_chunks():
    chunks = []
    for c0 in range(0, D_IN_PROJ, WCHUNK):
        gate = O_GA <= c0 < O_U or c0 >= O_GP
        chunks.append((0, c0, 0.5 if gate else 1.0))
    chunks += [(1, c0, 1.0) for c0 in range(0, D_MODEL, WCHUNK)]
    return chunks


def _load_weights(win_hbm, wout_hbm, wpool_f32, nw_ref, qn_ref, kn_ref, stage, sem, win_bf, wout_bf, wpool_bf,
                  qw_ref, kw_ref):
    chunks = _weight_chunks()
    srcs, dsts = (win_hbm, wout_hbm), (win_bf, wout_bf)
    nw_tile = jnp.concatenate([jnp.broadcast_to(nw_ref[:, t * LANES:(t + 1) * LANES], (LANES, LANES)).T
                               for t in range(D_MODEL // LANES)], axis=0)
    nw_rows = jnp.concatenate([nw_tile] * (WCHUNK // LANES), axis=1)

    def copy(k):
        which, c0, _ = chunks[k]
        return pltpu.make_async_copy(srcs[which].at[:, pl.ds(c0, WCHUNK)], stage.at[k % 2], sem.at[k % 2])

    copy(0).start()
    copy(1).start()
    for k, (which, c0, scale) in enumerate(chunks):
        copy(k).wait()
        w = stage[k % 2]
        if which == 0:
            w = w * nw_rows if scale == 1.0 else w * (scale * nw_rows)
        dsts[which][:, c0:c0 + WCHUNK] = w.astype(BF16)
        if k + 2 < len(chunks):
            copy(k + 2).start()
    wpool_bf[...] = jnp.zeros(wpool_bf.shape, BF16)
    for g in range(len(POOL_WINDOWS)):
        d0 = (g % 2) * LANES
        wpool_bf[g // 2, d0:d0 + LANES, d0:d0 + LANES] = wpool_f32[g].astype(BF16)
    qn = qn_ref[...] * (HEAD_DIM ** -0.5 * LOG2E)
    qw_ref[...] = jnp.concatenate([qn, qn], axis=1)
    kw_ref[...] = jnp.concatenate([kn_ref[...], kn_ref[...]], axis=1)


def _prompt_init(sinks_ref, kbuf, vbuf, uext, sink2):
    kbuf[:, 0:BLOCK, :] = jnp.zeros((2, BLOCK, LANES), BF16)
    vbuf[:, 0:BLOCK, :] = jnp.zeros((2, BLOCK, LANES), BF16)
    uext[0:16, :] = jnp.zeros((16, D_POOL), F32)
    lo = _lo_mask()
    for jj in range(N_HEADS // 2):
        sink2[jj] = jnp.where(lo, sinks_ref[2 * jj], sinks_ref[2 * jj + 1]) * LOG2E


def _prompt_step(i, last, x_ref, pscale_ref, y_ref, knew_ref, vnew_ref, unew_ref,
                 win_ref, wout_ref, wpool_ref, qw_ref, kw_ref, z_ref, hbuf, qbuf, kbuf, vbuf, uext, mixed, sink2):
    lo = _lo_mask()
    hbuf[...] = x_ref[...].astype(BF16)
    rs = _row_scale(x_ref[...])

    def proj(cols, rows=slice(None)):
        return jnp.dot(hbuf[rows, :], win_ref[:, cols], preferred_element_type=F32) * rs[rows]

    z_ref[:, O_U:] = proj(slice(O_U, D_IN_PROJ))
    uext[16:, :] = z_ref[:, O_U:O_U + D_POOL]
    pos16 = i * TB + lax.broadcasted_iota(jnp.int32, (16, LANES), 0)
    ds = []
    for g, w in enumerate(POOL_WINDOWS):
        acc = uext[:, g * LANES:(g + 1) * LANES]
        sh = 1
        while sh < w:
            acc = acc + pltpu.roll(acc, sh, axis=0)
            sh *= 2
        ug = z_ref[:, O_U + g * LANES:O_U + (g + 1) * LANES]
        cnt = jnp.minimum(w, pos16 + 1).astype(F32)
        d_first = acc[16:32, :] / cnt - ug[0:16, :]
        d_rest = acc[32:, :] * (1.0 / w) - ug[16:, :]
        ds.append(jnp.concatenate([d_first, d_rest], axis=0).astype(BF16))
    uext[0:16, :] = uext[TB:TB + 16, :]

    z_ref[:, :O_K] = proj(slice(0, O_K))

    for pr in range(len(POOL_WINDOWS) // 2):
        cols = slice(pr * 2 * LANES, (pr + 1) * 2 * LANES)
        d2 = jnp.concatenate(ds[2 * pr:2 * pr + 2], axis=1)
        po = jnp.dot(d2, wpool_ref[pr], preferred_element_type=F32) * pscale_ref[:, cols]
        gp = z_ref[:, O_GP + pr * 2 * LANES:O_GP + (pr + 1) * 2 * LANES]
        mixed[:, D_ATTN + pr * 2 * LANES:D_ATTN + (pr + 1) * 2 * LANES] = (po * _silu_half(gp)).astype(BF16)

    for j in range(4):
        qhat = _pair_rms(z_ref[:, j * LANES:(j + 1) * LANES], lo, qw_ref[...])
        q_lo = jnp.where(lo, qhat, 0.0).astype(BF16)
        q_hi = jnp.where(lo, 0.0, qhat).astype(BF16)
        c, g0 = j // 2, 2 * (j % 2)
        for sb in range(NSB):
            rows = slice(sb * BLOCK, (sb + 1) * BLOCK)
            qbuf[c, sb, g0 * BLOCK:(g0 + 1) * BLOCK, :] = q_lo[rows]
            qbuf[c, sb, (g0 + 1) * BLOCK:(g0 + 2) * BLOCK, :] = q_hi[rows]

    half = TB // 2
    for r0 in (0, half):
        z_ref[r0:r0 + half, O_K:O_GA] = proj(slice(O_K, O_GA), slice(r0, r0 + half))
    z_ref[:, O_GA:O_U] = proj(slice(O_GA, O_U))

    khat = _pair_rms(z_ref[:, O_K:O_K + LANES], lo, kw_ref[...])
    kr = pltpu.roll(khat, HEAD_DIM, axis=1)
    kbuf[0, BLOCK:, :] = jnp.where(lo, khat, kr).astype(BF16)
    kbuf[1, BLOCK:, :] = jnp.where(lo, kr, khat).astype(BF16)
    vz = z_ref[:, O_V:O_V + LANES]
    vr = pltpu.roll(vz, HEAD_DIM, axis=1)
    vbuf[0, BLOCK:, :] = jnp.where(lo, vz, vr).astype(BF16)
    vbuf[1, BLOCK:, :] = jnp.where(lo, vr, vz).astype(BF16)

    r_io = lax.broadcasted_iota(jnp.int32, (BLOCK, 2 * BLOCK), 0)
    c_io = lax.broadcasted_iota(jnp.int32, (BLOCK, 2 * BLOCK), 1)
    band = (c_io >= r_io) & (c_io <= r_io + WINDOW)
    first_lo = jnp.where(i > 0, 0, BLOCK)
    band_first = band & (c_io >= first_lo)

    for sb in range(NSB):
        valid = band_first if sb == 0 else band
        rows = slice(sb * BLOCK, (sb + 1) * BLOCK)
        for c in range(2):
            keys = kbuf[c, sb * BLOCK:(sb + 2) * BLOCK, :]
            vals = vbuf[c, sb * BLOCK:(sb + 2) * BLOCK, :]
            s = lax.dot_general(qbuf[c, sb], keys, _NT, preferred_element_type=F32)
            ps, ms, ls = [], [], []
            for g in range(GROUP):
                sg = jnp.where(valid, s[g * BLOCK:(g + 1) * BLOCK, :], NEG_INF)
                m = jnp.max(sg, axis=-1, keepdims=True)
                p = jnp.exp2(sg - m)
                ps.append(p.astype(BF16))
                ms.append(m)
                ls.append(jnp.sum(p, axis=-1, keepdims=True))
            o = jnp.dot(jnp.concatenate(ps, axis=0), vals, preferred_element_type=F32)
            for jj in range(2):
                ev, od = 2 * jj, 2 * jj + 1
                slab = jnp.where(lo, o[ev * BLOCK:(ev + 1) * BLOCK, :], o[od * BLOCK:(od + 1) * BLOCK, :])
                l = jnp.where(lo, ls[ev], ls[od]) + jnp.exp2(sink2[2 * c + jj] - jnp.where(lo, ms[ev], ms[od]))
                col = (2 * c + jj) * LANES
                ga = z_ref[rows, O_GA + col:O_GA + col + LANES]
                mixed[rows, col:col + LANES] = (slab * (1.0 / l) * _silu_half(ga)).astype(BF16)

    kbuf[:, 0:BLOCK, :] = kbuf[:, TB:TB + BLOCK, :]
    vbuf[:, 0:BLOCK, :] = vbuf[:, TB:TB + BLOCK, :]

    y_ref[...] = x_ref[...] + jnp.dot(mixed[...], wout_ref[...], preferred_element_type=F32)

    @pl.when(last)
    def _():
        unew_ref[:, 0, :] = z_ref[TB - POOL_STATE:TB, O_U:O_U + D_POOL]
        tail = slice(TB - WINDOW, TB)
        knew_ref[...] = _pair_rms(z_ref[tail, O_K:O_K + LANES], lo, kw_ref[...]).T
        vnew_ref[...] = z_ref[tail, O_V:O_V + LANES].T


def _sample_init(x_ref, win_ref, qw_ref, kw_ref, z_ref, q3, krep, vrep, snew, kvt):
    nb = x_ref.shape[0]
    lo = _lo_mask()
    z_ref[...] = _project(x_ref[:, 0, :], win_ref[...])
    khat = _pair_rms(z_ref[:, O_K:O_K + LANES], lo, kw_ref[...])
    vnew = z_ref[:, O_V:O_V + LANES]
    kvt[0] = khat.T
    kvt[1] = vnew.T
    q3[...] = jnp.zeros(q3.shape, F32)
    krep[...] = jnp.zeros(krep.shape, F32)
    vrep[...] = jnp.zeros(vrep.shape, F32)
    for j in range(4):
        qhat = _pair_rms(z_ref[:, j * LANES:(j + 1) * LANES], lo, qw_ref[...])
        qrot = pltpu.roll(qhat, HEAD_DIM, axis=1)
        grp_lo = j < 2
        for half in range(2):
            h = 2 * j + half
            src = qhat if (half == 0) == grp_lo else qrot
            q3[pl.ds(h, nb, stride=QROWS), :] = jnp.where(lo if grp_lo else ~lo, src, 0.0)
            krep[pl.ds(h, nb, stride=QROWS), :] = khat
            vrep[pl.ds(h, nb, stride=QROWS), :] = vnew
    s_new = jnp.sum(q3[...] * krep[...], axis=-1, keepdims=True)
    snew[...] = jnp.broadcast_to(s_new, snew.shape)


def _shift_caches(i, nb, ck_ref, cv_ref, knew_ref, vnew_ref, kvt):
    shift = lax.rem(nb - i * CB, nb)
    newest = lax.broadcasted_iota(jnp.int32, (LANES, WINDOW), 1) == WINDOW - 1
    for src, dst, which in ((ck_ref, knew_ref, 0), (cv_ref, vnew_ref, 1)):
        cols = pltpu.roll(kvt[which], shift, axis=1)
        shifted = pltpu.roll(src[...], WINDOW - 1, axis=2)
        for bb in range(CB):
            dst[bb] = jnp.where(newest, cols[:, bb:bb + 1], shifted[bb])


def _sample_step(j, last, sinks_ref, x_ref, pscale_ref, ck_ref, cv_ref, sp_ref, y_ref, spnew_ref,
                 wout_ref, wpool_ref, z_ref, q3, krep, vrep, snew, kvt, o3, mixed):
    nb = x_ref.shape[0]
    lo = _lo_mask()
    row0 = pl.multiple_of(j * BB, BB)
    rows = pl.ds(row0, BB)
    qrows = pl.ds(pl.multiple_of(j * (BB * QROWS), BB * QROWS), BB * QROWS)
    rid = lax.broadcasted_iota(jnp.int32, (1, QROWS, 1), 1)
    sink3 = jnp.zeros((1, QROWS, 1), F32)
    for h in range(N_HEADS):
        sink3 = jnp.where(rid == h, sinks_ref[h] * LOG2E, sink3)

    q = q3[qrows, :].reshape(BB, QROWS, LANES)
    s_new = snew[qrows, :].reshape(BB, QROWS, LANES)
    kt = ck_ref[...]
    vt = cv_ref[...]
    s = lax.dot_general(q.astype(BF16), kt.astype(BF16), (((2,), (1,)), ((0,), (0,))), preferred_element_type=F32)
    m = jnp.maximum(jnp.maximum(jnp.max(s, axis=-1, keepdims=True), s_new), sink3)
    p = jnp.exp2(s - m)
    p_new = jnp.exp2(s_new - m)
    l = jnp.sum(p, axis=-1, keepdims=True) + p_new + jnp.exp2(sink3 - m)
    o = lax.dot_general(p.astype(BF16), vt.astype(BF16), (((2,), (2,)), ((0,), (0,))), preferred_element_type=F32)
    o = (o + p_new * vrep[qrows, :].reshape(BB, QROWS, LANES)) / l
    o3[...] = o.reshape(BB * QROWS, LANES)

    for jj in range(4):
        grp_lo = jj < 2
        oh = [o3[pl.ds(2 * jj + half, BB, stride=QROWS), :] for half in range(2)]
        a = oh[0] if grp_lo else pltpu.roll(oh[0], HEAD_DIM, axis=1)
        b = pltpu.roll(oh[1], HEAD_DIM, axis=1) if grp_lo else oh[1]
        ga = z_ref[rows, O_GA + jj * LANES:O_GA + (jj + 1) * LANES]
        mixed[rows, jj * LANES:(jj + 1) * LANES] = (jnp.where(lo, a, b) * _silu_half(ga)).astype(BF16)

    u = z_ref[rows, O_U:O_U + D_POOL]
    spnew_ref[0:POOL_STATE - 1] = sp_ref[1:POOL_STATE]
    spnew_ref[POOL_STATE - 1] = u
    ds = []
    for g, w in enumerate(POOL_WINDOWS):
        cols = slice(g * LANES, (g + 1) * LANES)
        ug = u[:, cols]
        win_sum = ug
        for r in range(POOL_STATE - (w - 1), POOL_STATE):
            win_sum = win_sum + sp_ref[r, :, cols]
        ds.append((win_sum * (1.0 / w) - ug).astype(BF16))
    for pr in range(len(POOL_WINDOWS) // 2):
        cols = slice(pr * 2 * LANES, (pr + 1) * 2 * LANES)
        d2 = jnp.concatenate(ds[2 * pr:2 * pr + 2], axis=1)
        po = jnp.dot(d2, wpool_ref[pr], preferred_element_type=F32) * pscale_ref[:, cols]
        gp = z_ref[rows, O_GP + pr * 2 * LANES:O_GP + (pr + 1) * 2 * LANES]
        mixed[rows, D_ATTN + pr * 2 * LANES:D_ATTN + (pr + 1) * 2 * LANES] = (po * _silu_half(gp)).astype(BF16)

    @pl.when(last)
    def _():
        y_ref[:, 0, :] = x_ref[:, 0, :] + jnp.dot(mixed[...], wout_ref[...], preferred_element_type=F32)


def _fused_kernel(n_prompt, sinks_ref, xp_ref, xs_ref, nw_ref, win_hbm, qn_ref, kn_ref, wpool_f32, pscale_ref, wout_hbm,
                  ck_ref, cv_ref, sp_ref, ckc_ref, cvc_ref,
                  yp_ref, kp_ref, vp_ref, up_ref, ys_ref, kq_ref, vq_ref, pq_ref,
                  win_ref, wout_ref, wpool_ref, qw_ref, kw_ref, stage, sem,
                  z_ref, hbuf, qbuf, kbuf, vbuf, uext, mixed, sink2,
                  zs_ref, q3, krep, vrep, snew, kvt, o3, mixed_s):
    i = pl.program_id(0)

    @pl.when(i == 0)
    def _():
        _load_weights(win_hbm, wout_hbm, wpool_f32, nw_ref, qn_ref, kn_ref, stage, sem, win_ref, wout_ref, wpool_ref,
                      qw_ref, kw_ref)
        _prompt_init(sinks_ref, kbuf, vbuf, uext, sink2)
        _sample_init(xs_ref, win_ref, qw_ref, kw_ref, zs_ref, q3, krep, vrep, snew, kvt)

    @pl.when(i < n_prompt)
    def _():
        _shift_caches(i, xs_ref.shape[0], ckc_ref, cvc_ref, kq_ref, vq_ref, kvt)
        _prompt_step(i, i == n_prompt - 1, xp_ref, pscale_ref, yp_ref, kp_ref, vp_ref, up_ref,
                     win_ref, wout_ref, wpool_ref, qw_ref, kw_ref, z_ref, hbuf, qbuf, kbuf, vbuf, uext, mixed, sink2)

    @pl.when(i >= n_prompt)
    def _():
        _sample_step(i - n_prompt, i == pl.num_programs(0) - 1, sinks_ref, xs_ref, pscale_ref, ck_ref, cv_ref, sp_ref,
                     ys_ref, pq_ref, wout_ref, wpool_ref, zs_ref, q3, krep, vrep, snew, kvt, o3, mixed_s)


def _fused_call(sinks, xp, xs, nw, win, qn, kn, wpool, pscale, wout, ck, cv, sp):
    seq, nb = xp.shape[0], xs.shape[0]
    n_prompt, n_sample = seq // TB, nb // BB
    assert nb == CB * n_prompt
    const2 = lambda i: (0, 0)
    const3 = lambda i: (0, 0, 0)
    pblk = lambda i: (jnp.minimum(i, n_prompt - 1), 0)
    cblk = lambda i: (jnp.minimum(i, n_prompt - 1), 0, 0)
    sidx = lambda i: jnp.clip(i - n_prompt, 0, n_sample - 1)
    chunk3 = lambda i: (sidx(i), 0, 0)
    mid3 = lambda i: (0, sidx(i), 0)
    once = pl.Buffered(1)
    return pl.pallas_call(
        functools.partial(_fused_kernel, n_prompt),
        grid=(n_prompt + n_sample,),
        in_specs=[
            pl.BlockSpec(memory_space=pltpu.SMEM),
            pl.BlockSpec((TB, D_MODEL), pblk),
            pl.BlockSpec((nb, 1, D_MODEL), const3, pipeline_mode=once),
            pl.BlockSpec((1, D_MODEL), const2),
            pl.BlockSpec(memory_space=pl.ANY),
            pl.BlockSpec((1, HEAD_DIM), const2),
            pl.BlockSpec((1, HEAD_DIM), const2),
            pl.BlockSpec((4, LANES, LANES), const3, pipeline_mode=once),
            pl.BlockSpec((1, D_POOL), const2),
            pl.BlockSpec(memory_space=pl.ANY),
            pl.BlockSpec((BB, LANES, WINDOW), chunk3),
            pl.BlockSpec((BB, LANES, WINDOW), chunk3),
            pl.BlockSpec((POOL_STATE, BB, D_POOL), mid3),
            pl.BlockSpec((CB, LANES, WINDOW), cblk),
            pl.BlockSpec((CB, LANES, WINDOW), cblk),
        ],
        out_specs=[
            pl.BlockSpec((TB, D_MODEL), pblk),
            pl.BlockSpec((LANES, WINDOW), const2),
            pl.BlockSpec((LANES, WINDOW), const2),
            pl.BlockSpec((POOL_STATE, 1, D_POOL), const3),
            pl.BlockSpec((nb, 1, D_MODEL), const3, pipeline_mode=once),
            pl.BlockSpec((CB, LANES, WINDOW), cblk),
            pl.BlockSpec((CB, LANES, WINDOW), cblk),
            pl.BlockSpec((POOL_STATE, BB, D_POOL), mid3),
        ],
        out_shape=[
            jax.ShapeDtypeStruct((seq, D_MODEL), F32),
            jax.ShapeDtypeStruct((LANES, WINDOW), F32),
            jax.ShapeDtypeStruct((LANES, WINDOW), F32),
            jax.ShapeDtypeStruct((POOL_STATE, 1, D_POOL), F32),
            jax.ShapeDtypeStruct((nb, 1, D_MODEL), F32),
            jax.ShapeDtypeStruct((nb, LANES, WINDOW), F32),
            jax.ShapeDtypeStruct((nb, LANES, WINDOW), F32),
            jax.ShapeDtypeStruct((POOL_STATE, nb, D_POOL), F32),
        ],
        scratch_shapes=[
            pltpu.VMEM((D_MODEL, D_IN_PROJ), BF16),
            pltpu.VMEM((D_MODEL, D_MODEL), BF16),
            pltpu.VMEM((2, 2 * LANES, 2 * LANES), BF16),
            pltpu.VMEM((1, LANES), F32),
            pltpu.VMEM((1, LANES), F32),
            pltpu.VMEM((2, D_MODEL, WCHUNK), F32),
            pltpu.SemaphoreType.DMA((2,)),
            pltpu.VMEM((TB, D_IN_PROJ), F32),
            pltpu.VMEM((TB, D_MODEL), BF16),
            pltpu.VMEM((2, NSB, GROUP * BLOCK, LANES), BF16),
            pltpu.VMEM((2, TB + BLOCK, LANES), BF16),
            pltpu.VMEM((2, TB + BLOCK, LANES), BF16),
            pltpu.VMEM((TB + 16, D_POOL), F32),
            pltpu.VMEM((TB, D_MODEL), BF16),
            pltpu.VMEM((N_HEADS // 2, 1, LANES), F32),
            pltpu.VMEM((nb, D_IN_PROJ), F32),
            pltpu.VMEM((nb * QROWS, LANES), F32),
            pltpu.VMEM((nb * QROWS, LANES), F32),
            pltpu.VMEM((nb * QROWS, LANES), F32),
            pltpu.VMEM((nb * QROWS, LANES), F32),
            pltpu.VMEM((2, LANES, nb), F32),
            pltpu.VMEM((BB * QROWS, LANES), F32),
            pltpu.VMEM((nb, D_MODEL), BF16),
        ],
        compiler_params=pltpu.CompilerParams(dimension_semantics=("arbitrary",), vmem_limit_bytes=VMEM_LIMIT),
        name="hybrid_step",
    )(sinks, xp, xs, nw, win, qn, kn, wpool, pscale, wout, ck, cv, sp, ck, cv)


def _cache_in(c):
    nb = c.shape[0]
    return jnp.transpose(c, (0, 2, 3, 1)).reshape(nb, LANES, WINDOW)


def _cache_out(c):
    nb = c.shape[0]
    return jnp.transpose(c.reshape(nb, N_KV_HEADS, HEAD_DIM, WINDOW), (0, 3, 1, 2))


def kernel(x_prompt, x_sample, cache_k, cache_v, state_pool, norm_w, w_in, q_norm_w, k_norm_w, sinks, w_pool,
           pool_scale, w_out):
    depth = norm_w.shape[0]
    assert depth == 1 and x_prompt.shape[0] == 1 and x_sample.shape[1] == 1
    seq = x_prompt.shape[1]
    nb = x_sample.shape[0]
    assert seq % TB == 0 and nb % BB == 0 and nb == LANES

    yp, kp, vp, up, ys, kq, vq, pq = _fused_call(
        sinks[0], x_prompt[0], x_sample, norm_w, w_in[0], q_norm_w, k_norm_w, w_pool[0], pool_scale, w_out[0],
        _cache_in(cache_k[0]), _cache_in(cache_v[0]), jnp.transpose(state_pool[0], (1, 0, 2)))
    return (yp[None], ys, _cache_out(kp[None])[None], _cache_out(vp[None])[None],
            jnp.transpose(up, (1, 0, 2))[None],
            _cache_out(kq)[None], _cache_out(vq)[None], jnp.transpose(pq, (1, 0, 2))[None])
```

```python
import functools

import jax
import jax.numpy as jnp
from jax import lax
from jax.experimental import pallas as pl
from jax.experimental.pallas import tpu as pltpu

D_MODEL = 1024
HEAD_DIM = 64
N_HEADS = 8
N_KV_HEADS = 2
GROUP = 4
WINDOW = 128
BLOCK = 128
POOL_WINDOWS = (2, 4, 8, 16)
POOL_STATE = 15
D_ATTN = 512
D_POOL = 512
D_IN_PROJ = 2304
EPS = 1e-6
NEG_INF = -1e30
LOG2E = 1.4426950408889634

O_K = 512
O_V = 640
O_GA = 768
O_U = 1280
O_GP = 1792

LANES = 128
TB = 512
NSB = TB // BLOCK
BB = 16
CB = 4
QROWS = 16
WCHUNK = 256
VMEM_LIMIT = 58 * 1024 * 1024

F32 = jnp.float32
BF16 = jnp.bfloat16
_NT = (((1,), (1,)), ((), ()))


def _silu_half(h):
    return h + h * jnp.tanh(h)


def _lo_mask():
    return lax.broadcasted_iota(jnp.int32, (1, LANES), 1) < HEAD_DIM


def _pair_rms(zs, lo, w2):
    sq = zs * zs
    s_lo = jnp.sum(jnp.where(lo, sq, 0.0), axis=-1, keepdims=True)
    s_hi = jnp.sum(jnp.where(lo, 0.0, sq), axis=-1, keepdims=True)
    r = lax.rsqrt(jnp.where(lo, s_lo, s_hi) * (1.0 / HEAD_DIM) + EPS)
    return zs * r * w2


def _row_scale(x):
    return lax.rsqrt(jnp.mean(x * x, axis=-1, keepdims=True) + EPS)


def _project(x, win):
    return jnp.dot(x.astype(BF16), win, preferred_element_type=F32) * _row_scale(x)


def _weight_chunks():
    chunks = []
    for c0 in range(0, D_IN_PROJ, WCHUNK):
        gate = O_GA <= c0 < O_U or c0 >= O_GP
        chunks.append((0, c0, 0.5 if gate else 1.0))
    chunks += [(1, c0, 1.0) for c0 in range(0, D_MODEL, WCHUNK)]
    return chunks


def _load_weights(win_hbm, wout_hbm, wpool_f32, nw_ref, qn_ref, kn_ref, stage, sem, win_bf, wout_bf, wpool_bf,
                  qw_ref, kw_ref):
    chunks = _weight_chunks()
    srcs, dsts = (win_hbm, wout_hbm), (win_bf, wout_bf)
    nw_tile = jnp.concatenate([jnp.broadcast_to(nw_ref[:, t * LANES:(t + 1) * LANES], (LANES, LANES)).T
                               for t in range(D_MODEL // LANES)], axis=0)
    nw_rows = jnp.concatenate([nw_tile] * (WCHUNK // LANES), axis=1)

    def copy(k):
        which, c0, _ = chunks[k]
        return pltpu.make_async_copy(srcs[which].at[:, pl.ds(c0, WCHUNK)], stage.at[k % 2], sem.at[k % 2])

    copy(0).start()
    copy(1).start()
    for k, (which, c0, scale) in enumerate(chunks):
        copy(k).wait()
        w = stage[k % 2]
        if which == 0:
            w = w * nw_rows if scale == 1.0 else w * (scale * nw_rows)
        dsts[which][:, c0:c0 + WCHUNK] = w.astype(BF16)
        if k + 2 < len(chunks):
            copy(k + 2).start()
    wpool_bf[...] = jnp.zeros(wpool_bf.shape, BF16)
    for g in range(len(POOL_WINDOWS)):
        d0 = (g % 2) * LANES
        wpool_bf[g // 2, d0:d0 + LANES, d0:d0 + LANES] = wpool_f32[g].astype(BF16)
    qn = qn_ref[...] * (HEAD_DIM ** -0.5 * LOG2E)
    qw_ref[...] = jnp.concatenate([qn, qn], axis=1)
    kw_ref[...] = jnp.concatenate([kn_ref[...], kn_ref[...]], axis=1)


def _prompt_init(sinks_ref, kbuf, vbuf, uext, sink2, eye4, maskt):
    kbuf[:, 0:BLOCK, :] = jnp.zeros((2, BLOCK, LANES), BF16)
    vbuf[:, 0:BLOCK, :] = jnp.zeros((2, BLOCK, LANES), BF16)
    uext[0:16, :] = jnp.zeros((16, D_POOL), F32)
    lo = _lo_mask()
    for jj in range(N_HEADS // 2):
        sink2[jj] = jnp.where(lo, sinks_ref[2 * jj], sinks_ref[2 * jj + 1]) * LOG2E
    rr = lax.broadcasted_iota(jnp.int32, (GROUP * BLOCK, LANES), 0)
    ll = lax.broadcasted_iota(jnp.int32, (GROUP * BLOCK, LANES), 1)
    eye4[...] = jnp.where((rr & (BLOCK - 1)) == ll, 1.0, 0.0).astype(BF16)
    cc = lax.broadcasted_iota(jnp.int32, (2 * BLOCK, LANES), 0)
    qr = lax.broadcasted_iota(jnp.int32, (2 * BLOCK, LANES), 1)
    band = (cc >= qr) & (cc <= qr + WINDOW)
    maskt[0] = jnp.where(band & (cc >= BLOCK), 0.0, NEG_INF).astype(BF16)
    maskt[1] = jnp.where(band, 0.0, NEG_INF).astype(BF16)


def _prompt_step(i, last, x_ref, pscale_ref, y_ref, knew_ref, vnew_ref, unew_ref,
                 win_ref, wout_ref, wpool_ref, qw_ref, kw_ref, z_ref, hbuf, qbuf, kbuf, vbuf, uext, mixed, sink2, eye4, maskt):
    lo = _lo_mask()
    hbuf[...] = x_ref[...].astype(BF16)
    rs = _row_scale(x_ref[...])

    def proj(cols, rows=slice(None)):
        return jnp.dot(hbuf[rows, :], win_ref[:, cols], preferred_element_type=F32) * rs[rows]

    z_ref[:, O_U:] = proj(slice(O_U, D_IN_PROJ))
    uext[16:, :] = z_ref[:, O_U:O_U + D_POOL]
    pos16 = i * TB + lax.broadcasted_iota(jnp.int32, (16, LANES), 0)
    ds = []
    for g, w in enumerate(POOL_WINDOWS):
        acc = uext[:, g * LANES:(g + 1) * LANES]
        sh = 1
        while sh < w:
            acc = acc + pltpu.roll(acc, sh, axis=0)
            sh *= 2
        ug = z_ref[:, O_U + g * LANES:O_U + (g + 1) * LANES]
        cnt = jnp.minimum(w, pos16 + 1).astype(F32)
        d_first = acc[16:32, :] / cnt - ug[0:16, :]
        d_rest = acc[32:, :] * (1.0 / w) - ug[16:, :]
        ds.append(jnp.concatenate([d_first, d_rest], axis=0).astype(BF16))
    uext[0:16, :] = uext[TB:TB + 16, :]

    z_ref[:, :O_K] = proj(slice(0, O_K))

    for pr in range(len(POOL_WINDOWS) // 2):
        cols = slice(pr * 2 * LANES, (pr + 1) * 2 * LANES)
        d2 = jnp.concatenate(ds[2 * pr:2 * pr + 2], axis=1)
        po = jnp.dot(d2, wpool_ref[pr], preferred_element_type=F32) * pscale_ref[:, cols]
        gp = z_ref[:, O_GP + pr * 2 * LANES:O_GP + (pr + 1) * 2 * LANES]
        mixed[:, D_ATTN + pr * 2 * LANES:D_ATTN + (pr + 1) * 2 * LANES] = (po * _silu_half(gp)).astype(BF16)

    for j in range(4):
        qhat = _pair_rms(z_ref[:, j * LANES:(j + 1) * LANES], lo, qw_ref[...])
        q_lo = jnp.where(lo, qhat, 0.0).astype(BF16)
        q_hi = jnp.where(lo, 0.0, qhat).astype(BF16)
        c, g0 = j // 2, 2 * (j % 2)
        for sb in range(NSB):
            rows = slice(sb * BLOCK, (sb + 1) * BLOCK)
            qbuf[c, sb, g0 * BLOCK:(g0 + 1) * BLOCK, :] = q_lo[rows]
            qbuf[c, sb, (g0 + 1) * BLOCK:(g0 + 2) * BLOCK, :] = q_hi[rows]

    half = TB // 2
    for r0 in (0, half):
        z_ref[r0:r0 + half, O_K:O_GA] = proj(slice(O_K, O_GA), slice(r0, r0 + half))
    z_ref[:, O_GA:O_U] = proj(slice(O_GA, O_U))

    khat = _pair_rms(z_ref[:, O_K:O_K + LANES], lo, kw_ref[...])
    kr = pltpu.roll(khat, HEAD_DIM, axis=1)
    kbuf[0, BLOCK:, :] = jnp.where(lo, khat, kr).astype(BF16)
    kbuf[1, BLOCK:, :] = jnp.where(lo, kr, khat).astype(BF16)
    vz = z_ref[:, O_V:O_V + LANES]
    vr = pltpu.roll(vz, HEAD_DIM, axis=1)
    vbuf[0, BLOCK:, :] = jnp.where(lo, vz, vr).astype(BF16)
    vbuf[1, BLOCK:, :] = jnp.where(lo, vr, vz).astype(BF16)

    first = jnp.where(i > 0, 1, 0)
    for sb in range(NSB):
        rows = slice(sb * BLOCK, (sb + 1) * BLOCK)
        mcols = maskt[first] if sb == 0 else maskt[1]
        for c in range(2):
            keys = kbuf[c, sb * BLOCK:(sb + 2) * BLOCK, :]
            vals = vbuf[c, sb * BLOCK:(sb + 2) * BLOCK, :]
            s = lax.dot_general(jnp.concatenate([qbuf[c, sb], eye4[...]], axis=1),
                                jnp.concatenate([keys, mcols], axis=1), _NT,
                                preferred_element_type=F32)
            ps, ms, ls = [], [], []
            for g in range(GROUP):
                sg = s[g * BLOCK:(g + 1) * BLOCK, :]
                m = jnp.max(sg, axis=-1, keepdims=True)
                p = jnp.exp2(sg - m)
                ps.append(p.astype(BF16))
                ms.append(m)
                ls.append(jnp.sum(p, axis=-1, keepdims=True))
            o = jnp.dot(jnp.concatenate(ps, axis=0), vals, preferred_element_type=F32)
            for jj in range(2):
                ev, od = 2 * jj, 2 * jj + 1
                slab = jnp.where(lo, o[ev * BLOCK:(ev + 1) * BLOCK, :], o[od * BLOCK:(od + 1) * BLOCK, :])
                l = jnp.where(lo, ls[ev], ls[od]) + jnp.exp2(sink2[2 * c + jj] - jnp.where(lo, ms[ev], ms[od]))
                col = (2 * c + jj) * LANES
                ga = z_ref[rows, O_GA + col:O_GA + col + LANES]
                mixed[rows, col:col + LANES] = (slab * (1.0 / l) * _silu_half(ga)).astype(BF16)

    kbuf[:, 0:BLOCK, :] = kbuf[:, TB:TB + BLOCK, :]
    vbuf[:, 0:BLOCK, :] = vbuf[:, TB:TB + BLOCK, :]

    y_ref[...] = x_ref[...] + jnp.dot(mixed[...], wout_ref[...], preferred_element_type=F32)

    @pl.when(last)
    def _():
        unew_ref[:, 0, :] = z_ref[TB - POOL_STATE:TB, O_U:O_U + D_POOL]
        tail = slice(TB - WINDOW, TB)
        knew_ref[...] = _pair_rms(z_ref[tail, O_K:O_K + LANES], lo, kw_ref[...]).T
        vnew_ref[...] = z_ref[tail, O_V:O_V + LANES].T


def _sample_init(x_ref, win_ref, qw_ref, kw_ref, z_ref, q3, krep, vrep, snew, kvt):
    nb = x_ref.shape[0]
    lo = _lo_mask()
    z_ref[...] = _project(x_ref[:, 0, :], win_ref[...])
    khat = _pair_rms(z_ref[:, O_K:O_K + LANES], lo, kw_ref[...])
    vnew = z_ref[:, O_V:O_V + LANES]
    kvt[0] = khat.T
    kvt[1] = vnew.T
    q3[...] = jnp.zeros(q3.shape, F32)
    krep[...] = jnp.zeros(krep.shape, F32)
    vrep[...] = jnp.zeros(vrep.shape, F32)
    for j in range(4):
        qhat = _pair_rms(z_ref[:, j * LANES:(j + 1) * LANES], lo, qw_ref[...])
        qrot = pltpu.roll(qhat, HEAD_DIM, axis=1)
        grp_lo = j < 2
        for half in range(2):
            h = 2 * j + half
            src = qhat if (half == 0) == grp_lo else qrot
            q3[pl.ds(h, nb, stride=QROWS), :] = jnp.where(lo if grp_lo else ~lo, src, 0.0)
            krep[pl.ds(h, nb, stride=QROWS), :] = khat
            vrep[pl.ds(h, nb, stride=QROWS), :] = vnew
    s_new = jnp.sum(q3[...] * krep[...], axis=-1, keepdims=True)
    snew[...] = jnp.broadcast_to(s_new, snew.shape)


def _shift_caches(i, nb, ck_ref, cv_ref, knew_ref, vnew_ref, kvt):
    shift = lax.rem(nb - i * CB, nb)
    newest = lax.broadcasted_iota(jnp.int32, (LANES, WINDOW), 1) == WINDOW - 1
    for src, dst, which in ((ck_ref, knew_ref, 0), (cv_ref, vnew_ref, 1)):
        cols = pltpu.roll(kvt[which], shift, axis=1)
        shifted = pltpu.roll(src[...], WINDOW - 1, axis=2)
        for bb in range(CB):
            dst[bb] = jnp.where(newest, cols[:, bb:bb + 1], shifted[bb])


def _sample_step(j, last, sinks_ref, x_ref, pscale_ref, ck_ref, cv_ref, sp_ref, y_ref, spnew_ref,
                 wout_ref, wpool_ref, z_ref, q3, krep, vrep, snew, kvt, o3, mixed):
    nb = x_ref.shape[0]
    lo = _lo_mask()
    row0 = pl.multiple_of(j * BB, BB)
    rows = pl.ds(row0, BB)
    qrows = pl.ds(pl.multiple_of(j * (BB * QROWS), BB * QROWS), BB * QROWS)
    rid = lax.broadcasted_iota(jnp.int32, (1, QROWS, 1), 1)
    sink3 = jnp.zeros((1, QROWS, 1), F32)
    for h in range(N_HEADS):
        sink3 = jnp.where(rid == h, sinks_ref[h] * LOG2E, sink3)

    q = q3[qrows, :].reshape(BB, QROWS, LANES)
    s_new = snew[qrows, :].reshape(BB, QROWS, LANES)
    kt = ck_ref[...]
    vt = cv_ref[...]
    s = lax.dot_general(q.astype(BF16), kt.astype(BF16), (((2,), (1,)), ((0,), (0,))), preferred_element_type=F32)
    m = jnp.maximum(jnp.maximum(jnp.max(s, axis=-1, keepdims=True), s_new), sink3)
    p = jnp.exp2(s - m)
    p_new = jnp.exp2(s_new - m)
    l = jnp.sum(p, axis=-1, keepdims=True) + p_new + jnp.exp2(sink3 - m)
    o = lax.dot_general(p.astype(BF16), vt.astype(BF16), (((2,), (2,)), ((0,), (0,))), preferred_element_type=F32)
    o = (o + p_new * vrep[qrows, :].reshape(BB, QROWS, LANES)) / l
    o3[...] = o.reshape(BB * QROWS, LANES)

    for jj in range(4):
        grp_lo = jj < 2
        oh = [o3[pl.ds(2 * jj + half, BB, stride=QROWS), :] for half in range(2)]
        a = oh[0] if grp_lo else pltpu.roll(oh[0], HEAD_DIM, axis=1)
        b = pltpu.roll(oh[1], HEAD_DIM, axis=1) if grp_lo else oh[1]
        ga = z_ref[rows, O_GA + jj * LANES:O_GA + (jj + 1) * LANES]
        mixed[rows, jj * LANES:(jj + 1) * LANES] = (jnp.where(lo, a, b) * _silu_half(ga)).astype(BF16)

    u = z_ref[rows, O_U:O_U + D_POOL]
    spnew_ref[0:POOL_STATE - 1] = sp_ref[1:POOL_STATE]
    spnew_ref[POOL_STATE - 1] = u
    ds = []
    for g, w in enumerate(POOL_WINDOWS):
        cols = slice(g * LANES, (g + 1) * LANES)
        ug = u[:, cols]
        win_sum = ug
        for r in range(POOL_STATE - (w - 1), POOL_STATE):
            win_sum = win_sum + sp_ref[r, :, cols]
        ds.append((win_sum * (1.0 / w) - ug).astype(BF16))
    for pr in range(len(POOL_WINDOWS) // 2):
        cols = slice(pr * 2 * LANES, (pr + 1) * 2 * LANES)
        d2 = jnp.concatenate(ds[2 * pr:2 * pr + 2], axis=1)
        po = jnp.dot(d2, wpool_ref[pr], preferred_element_type=F32) * pscale_ref[:, cols]
        gp = z_ref[rows, O_GP + pr * 2 * LANES:O_GP + (pr + 1) * 2 * LANES]
        mixed[rows, D_ATTN + pr * 2 * LANES:D_ATTN + (pr + 1) * 2 * LANES] = (po * _silu_half(gp)).astype(BF16)

    @pl.when(last)
    def _():
        y_ref[:, 0, :] = x_ref[:, 0, :] + jnp.dot(mixed[...], wout_ref[...], preferred_element_type=F32)


def _fused_kernel(n_prompt, sinks_ref, xp_ref, xs_ref, nw_ref, win_hbm, qn_ref, kn_ref, wpool_f32, pscale_ref, wout_hbm,
                  ck_ref, cv_ref, sp_ref, ckc_ref, cvc_ref,
                  yp_ref, kp_ref, vp_ref, up_ref, ys_ref, kq_ref, vq_ref, pq_ref,
                  win_ref, wout_ref, wpool_ref, qw_ref, kw_ref, stage, sem,
                  z_ref, hbuf, qbuf, kbuf, vbuf, uext, mixed, sink2, eye4, maskt,
                  zs_ref, q3, krep, vrep, snew, kvt, o3, mixed_s):
    i = pl.program_id(0)

    @pl.when(i == 0)
    def _():
        _load_weights(win_hbm, wout_hbm, wpool_f32, nw_ref, qn_ref, kn_ref, stage, sem, win_ref, wout_ref, wpool_ref,
                      qw_ref, kw_ref)
        _prompt_init(sinks_ref, kbuf, vbuf, uext, sink2, eye4, maskt)
        _sample_init(xs_ref, win_ref, qw_ref, kw_ref, zs_ref, q3, krep, vrep, snew, kvt)

    @pl.when(i < n_prompt)
    def _():
        _shift_caches(i, xs_ref.shape[0], ckc_ref, cvc_ref, kq_ref, vq_ref, kvt)
        _prompt_step(i, i == n_prompt - 1, xp_ref, pscale_ref, yp_ref, kp_ref, vp_ref, up_ref,
                     win_ref, wout_ref, wpool_ref, qw_ref, kw_ref, z_ref, hbuf, qbuf, kbuf, vbuf, uext, mixed, sink2, eye4, maskt)

    @pl.when(i >= n_prompt)
    def _():
        _sample_step(i - n_prompt, i == pl.num_programs(0) - 1, sinks_ref, xs_ref, pscale_ref, ck_ref, cv_ref, sp_ref,
                     ys_ref, pq_ref, wout_ref, wpool_ref, zs_ref, q3, krep, vrep, snew, kvt, o3, mixed_s)


def _fused_call(sinks, xp, xs, nw, win, qn, kn, wpool, pscale, wout, ck, cv, sp):
    seq, nb = xp.shape[0], xs.shape[0]
    n_prompt, n_sample = seq // TB, nb // BB
    assert nb == CB * n_prompt
    const2 = lambda i: (0, 0)
    const3 = lambda i: (0, 0, 0)
    pblk = lambda i: (jnp.minimum(i, n_prompt - 1), 0)
    cblk = lambda i: (jnp.minimum(i, n_prompt - 1), 0, 0)
    sidx = lambda i: jnp.clip(i - n_prompt, 0, n_sample - 1)
    chunk3 = lambda i: (sidx(i), 0, 0)
    mid3 = lambda i: (0, sidx(i), 0)
    once = pl.Buffered(1)
    return pl.pallas_call(
        functools.partial(_fused_kernel, n_prompt),
        grid=(n_prompt + n_sample,),
        in_specs=[
            pl.BlockSpec(memory_space=pltpu.SMEM),
            pl.BlockSpec((TB, D_MODEL), pblk),
            pl.BlockSpec((nb, 1, D_MODEL), const3, pipeline_mode=once),
            pl.BlockSpec((1, D_MODEL), const2),
            pl.BlockSpec(memory_space=pl.ANY),
            pl.BlockSpec((1, HEAD_DIM), const2),
            pl.BlockSpec((1, HEAD_DIM), const2),
            pl.BlockSpec((4, LANES, LANES), const3, pipeline_mode=once),
            pl.BlockSpec((1, D_POOL), const2),
            pl.BlockSpec(memory_space=pl.ANY),
            pl.BlockSpec((BB, LANES, WINDOW), chunk3),
            pl.BlockSpec((BB, LANES, WINDOW), chunk3),
            pl.BlockSpec((POOL_STATE, BB, D_POOL), mid3),
            pl.BlockSpec((CB, LANES, WINDOW), cblk),
            pl.BlockSpec((CB, LANES, WINDOW), cblk),
        ],
        out_specs=[
            pl.BlockSpec((TB, D_MODEL), pblk),
            pl.BlockSpec((LANES, WINDOW), const2),
            pl.BlockSpec((LANES, WINDOW), const2),
            pl.BlockSpec((POOL_STATE, 1, D_POOL), const3),
            pl.BlockSpec((nb, 1, D_MODEL), const3, pipeline_mode=once),
            pl.BlockSpec((CB, LANES, WINDOW), cblk),
            pl.BlockSpec((CB, LANES, WINDOW), cblk),
            pl.BlockSpec((POOL_STATE, BB, D_POOL), mid3),
        ],
        out_shape=[
            jax.ShapeDtypeStruct((seq, D_MODEL), F32),
            jax.ShapeDtypeStruct((LANES, WINDOW), F32),
            jax.ShapeDtypeStruct((LANES, WINDOW), F32),
            jax.ShapeDtypeStruct((POOL_STATE, 1, D_POOL), F32),
            jax.ShapeDtypeStruct((nb, 1, D_MODEL), F32),
            jax.ShapeDtypeStruct((nb, LANES, WINDOW), F32),
            jax.ShapeDtypeStruct((nb, LANES, WINDOW), F32),
            jax.ShapeDtypeStruct((POOL_STATE, nb, D_POOL), F32),
        ],
        scratch_shapes=[
            pltpu.VMEM((D_MODEL, D_IN_PROJ), BF16),
            pltpu.VMEM((D_MODEL, D_MODEL), BF16),
            pltpu.VMEM((2, 2 * LANES, 2 * LANES), BF16),
            pltpu.VMEM((1, LANES), F32),
            pltpu.VMEM((1, LANES), F32),
            pltpu.VMEM((2, D_MODEL, WCHUNK), F32),
            pltpu.SemaphoreType.DMA((2,)),
            pltpu.VMEM((TB, D_IN_PROJ), F32),
            pltpu.VMEM((TB, D_MODEL), BF16),
            pltpu.VMEM((2, NSB, GROUP * BLOCK, LANES), BF16),
            pltpu.VMEM((2, TB + BLOCK, LANES), BF16),
            pltpu.VMEM((2, TB + BLOCK, LANES), BF16),
            pltpu.VMEM((TB + 16, D_POOL), F32),
            pltpu.VMEM((TB, D_MODEL), BF16),
            pltpu.VMEM((N_HEADS // 2, 1, LANES), F32),
            pltpu.VMEM((GROUP * BLOCK, LANES), BF16),
            pltpu.VMEM((2, 2 * BLOCK, LANES), BF16),
            pltpu.VMEM((nb, D_IN_PROJ), F32),
            pltpu.VMEM((nb * QROWS, LANES), F32),
            pltpu.VMEM((nb * QROWS, LANES), F32),
            pltpu.VMEM((nb * QROWS, LANES), F32),
            pltpu.VMEM((nb * QROWS, LANES), F32),
            pltpu.VMEM((2, LANES, nb), F32),
            pltpu.VMEM((BB * QROWS, LANES), F32),
            pltpu.VMEM((nb, D_MODEL), BF16),
        ],
        compiler_params=pltpu.CompilerParams(dimension_semantics=("arbitrary",), vmem_limit_bytes=VMEM_LIMIT),
        name="hybrid_step",
    )(sinks, xp, xs, nw, win, qn, kn, wpool, pscale, wout, ck, cv, sp, ck, cv)


def _cache_in(c):
    nb = c.shape[0]
    return jnp.transpose(c, (0, 2, 3, 1)).reshape(nb, LANES, WINDOW)


def _cache_out(c):
    nb = c.shape[0]
    return jnp.transpose(c.reshape(nb, N_KV_HEADS, HEAD_DIM, WINDOW), (0, 3, 1, 2))


def kernel(x_prompt, x_sample, cache_k, cache_v, state_pool, norm_w, w_in, q_norm_w, k_norm_w, sinks, w_pool,
           pool_scale, w_out):
    depth = norm_w.shape[0]
    assert depth == 1 and x_prompt.shape[0] == 1 and x_sample.shape[1] == 1
    seq = x_prompt.shape[1]
    nb = x_sample.shape[0]
    assert seq % TB == 0 and nb % BB == 0 and nb == LANES

    yp, kp, vp, up, ys, kq, vq, pq = _fused_call(
        sinks[0], x_prompt[0], x_sample, norm_w, w_in[0], q_norm_w, k_norm_w, w_pool[0], pool_scale, w_out[0],
        _cache_in(cache_k[0]), _cache_in(cache_v[0]), jnp.transpose(state_pool[0], (1, 0, 2)))
    return (yp[None], ys, _cache_out(kp[None])[None], _cache_out(vp[None])[None],
            jnp.transpose(up, (1, 0, 2))[None],
            _cache_out(kq)[None], _cache_out(vq)[None], jnp.transpose(pq, (1, 0, 2))[None])
```

```python
import functools

import jax
import jax.numpy as jnp
from jax import lax
from jax.experimental import pallas as pl
from jax.experimental.pallas import tpu as pltpu

D_MODEL = 1024
HEAD_DIM = 64
N_HEADS = 8
N_KV_HEADS = 2
GROUP = 4
WINDOW = 128
BLOCK = 128
POOL_WINDOWS = (2, 4, 8, 16)
POOL_STATE = 15
D_ATTN = 512
D_POOL = 512
D_IN_PROJ = 2304
EPS = 1e-6
NEG_INF = -1e30
LOG2E = 1.4426950408889634

O_K = 512
O_V = 640
O_GA = 768
O_U = 1280
O_GP = 1792

LANES = 128
TB = 512
NSB = TB // BLOCK
BB = 16
CB = 4
QROWS = 16
WCHUNK = 256
VMEM_LIMIT = 58 * 1024 * 1024

F32 = jnp.float32
BF16 = jnp.bfloat16
_NT = (((1,), (1,)), ((), ()))


def _silu_half(h):
    return h + h * jnp.tanh(h)


def _lo_mask():
    return lax.broadcasted_iota(jnp.int32, (1, LANES), 1) < HEAD_DIM


def _pair_rms(zs, lo, w2):
    sq = zs * zs
    s_lo = jnp.sum(jnp.where(lo, sq, 0.0), axis=-1, keepdims=True)
    s_hi = jnp.sum(jnp.where(lo, 0.0, sq), axis=-1, keepdims=True)
    r = lax.rsqrt(jnp.where(lo, s_lo, s_hi) * (1.0 / HEAD_DIM) + EPS)
    return zs * r * w2


def _row_scale(x):
    return lax.rsqrt(jnp.mean(x * x, axis=-1, keepdims=True) + EPS)


def _project(x, win):
    return jnp.dot(x.astype(BF16), win, preferred_element_type=F32) * _row_scale(x)


def _weight_chunks():
    chunks = []
    for c0 in range(0, D_IN_PROJ, WCHUNK):
        gate = O_GA <= c0 < O_U or c0 >= O_GP
        chunks.append((0, c0, 0.5 if gate else 1.0))
    chunks += [(1, c0, 1.0) for c0 in range(0, D_MODEL, WCHUNK)]
    return chunks


def _load_weights(win_hbm, wout_hbm, wpool_f32, nw_ref, qn_ref, kn_ref, stage, sem, win_bf, wout_bf, wpool_bf,
                  qw_ref, kw_ref):
    chunks = _weight_chunks()
    srcs, dsts = (win_hbm, wout_hbm), (win_bf, wout_bf)
    nw_tile = jnp.concatenate([jnp.broadcast_to(nw_ref[:, t * LANES:(t + 1) * LANES], (LANES, LANES)).T
                               for t in range(D_MODEL // LANES)], axis=0)
    nw_rows = jnp.concatenate([nw_tile] * (WCHUNK // LANES), axis=1)

    def copy(k):
        which, c0, _ = chunks[k]
        return pltpu.make_async_copy(srcs[which].at[:, pl.ds(c0, WCHUNK)], stage.at[k % 2], sem.at[k % 2])

    copy(0).start()
    copy(1).start()
    for k, (which, c0, scale) in enumerate(chunks):
        copy(k).wait()
        w = stage[k % 2]
        if which == 0:
            w = w * nw_rows if scale == 1.0 else w * (scale * nw_rows)
        dsts[which][:, c0:c0 + WCHUNK] = w.astype(BF16)
        if k + 2 < len(chunks):
            copy(k + 2).start()
    wpool_bf[...] = jnp.zeros(wpool_bf.shape, BF16)
    for g in range(len(POOL_WINDOWS)):
        d0 = (g % 2) * LANES
        wpool_bf[g // 2, d0:d0 + LANES, d0:d0 + LANES] = wpool_f32[g].astype(BF16)
    qn = qn_ref[...] * (HEAD_DIM ** -0.5 * LOG2E)
    qw_ref[...] = jnp.concatenate([qn, qn], axis=1)
    kw_ref[...] = jnp.concatenate([kn_ref[...], kn_ref[...]], axis=1)


def _prompt_init(sinks_ref, kbuf, vbuf, uext, sink2, eye4, maskt):
    kbuf[:, 0:BLOCK, :] = jnp.zeros((2, BLOCK, LANES), BF16)
    vbuf[:, 0:BLOCK, :] = jnp.zeros((2, BLOCK, LANES), BF16)
    uext[0:16, :] = jnp.zeros((16, D_POOL), F32)
    lo = _lo_mask()
    for jj in range(N_HEADS // 2):
        sink2[jj] = jnp.where(lo, sinks_ref[2 * jj], sinks_ref[2 * jj + 1]) * LOG2E
    rr = lax.broadcasted_iota(jnp.int32, (GROUP * BLOCK, LANES), 0)
    ll = lax.broadcasted_iota(jnp.int32, (GROUP * BLOCK, LANES), 1)
    eye4[...] = jnp.where((rr & (BLOCK - 1)) == ll, 1.0, 0.0).astype(BF16)
    cc = lax.broadcasted_iota(jnp.int32, (2 * BLOCK, LANES), 0)
    qr = lax.broadcasted_iota(jnp.int32, (2 * BLOCK, LANES), 1)
    band = (cc >= qr) & (cc <= qr + WINDOW)
    maskt[0] = jnp.where(band & (cc >= BLOCK), 0.0, NEG_INF).astype(BF16)
    maskt[1] = jnp.where(band, 0.0, NEG_INF).astype(BF16)


def _prompt_step(i, last, x_ref, pscale_ref, y_ref, knew_ref, vnew_ref, unew_ref,
                 win_ref, wout_ref, wpool_ref, qw_ref, kw_ref, z_ref, hbuf, qbuf, kbuf, vbuf, uext, mixed, sink2, eye4, maskt):
    lo = _lo_mask()
    hbuf[...] = x_ref[...].astype(BF16)
    rs = _row_scale(x_ref[...])

    def proj(cols, rows=slice(None)):
        return jnp.dot(hbuf[rows, :], win_ref[:, cols], preferred_element_type=F32) * rs[rows]

    z_ref[:, O_U:] = proj(slice(O_U, D_IN_PROJ))
    uext[16:, :] = z_ref[:, O_U:O_U + D_POOL]
    pos16 = i * TB + lax.broadcasted_iota(jnp.int32, (16, LANES), 0)
    ds = []
    for g, w in enumerate(POOL_WINDOWS):
        acc = uext[:, g * LANES:(g + 1) * LANES]
        sh = 1
        while sh < w:
            acc = acc + pltpu.roll(acc, sh, axis=0)
            sh *= 2
        ug = z_ref[:, O_U + g * LANES:O_U + (g + 1) * LANES]
        cnt = jnp.minimum(w, pos16 + 1).astype(F32)
        d_first = acc[16:32, :] / cnt - ug[0:16, :]
        d_rest = acc[32:, :] * (1.0 / w) - ug[16:, :]
        ds.append(jnp.concatenate([d_first, d_rest], axis=0).astype(BF16))
    uext[0:16, :] = uext[TB:TB + 16, :]

    z_ref[:, :O_K] = proj(slice(0, O_K))

    for pr in range(len(POOL_WINDOWS) // 2):
        cols = slice(pr * 2 * LANES, (pr + 1) * 2 * LANES)
        d2 = jnp.concatenate(ds[2 * pr:2 * pr + 2], axis=1)
        po = jnp.dot(d2, wpool_ref[pr], preferred_element_type=F32) * pscale_ref[:, cols]
        gp = z_ref[:, O_GP + pr * 2 * LANES:O_GP + (pr + 1) * 2 * LANES]
        mixed[:, D_ATTN + pr * 2 * LANES:D_ATTN + (pr + 1) * 2 * LANES] = (po * _silu_half(gp)).astype(BF16)

    for j in range(4):
        qhat = _pair_rms(z_ref[:, j * LANES:(j + 1) * LANES], lo, qw_ref[...])
        q_lo = jnp.where(lo, qhat, 0.0).astype(BF16)
        q_hi = jnp.where(lo, 0.0, qhat).astype(BF16)
        c, g0 = j // 2, 2 * (j % 2)
        for sb in range(NSB):
            rows = slice(sb * BLOCK, (sb + 1) * BLOCK)
            qbuf[c, sb, g0 * BLOCK:(g0 + 1) * BLOCK, :] = q_lo[rows]
            qbuf[c, sb, (g0 + 1) * BLOCK:(g0 + 2) * BLOCK, :] = q_hi[rows]

    half = TB // 2
    for r0 in (0, half):
        z_ref[r0:r0 + half, O_K:O_GA] = proj(slice(O_K, O_GA), slice(r0, r0 + half))
    z_ref[:, O_GA:O_U] = _silu_half(proj(slice(O_GA, O_U)))

    khat = _pair_rms(z_ref[:, O_K:O_K + LANES], lo, kw_ref[...])
    kr = pltpu.roll(khat, HEAD_DIM, axis=1)
    kbuf[0, BLOCK:, :] = jnp.where(lo, khat, kr).astype(BF16)
    kbuf[1, BLOCK:, :] = jnp.where(lo, kr, khat).astype(BF16)
    vz = z_ref[:, O_V:O_V + LANES]
    vr = pltpu.roll(vz, HEAD_DIM, axis=1)
    vbuf[0, BLOCK:, :] = jnp.where(lo, vz, vr).astype(BF16)
    vbuf[1, BLOCK:, :] = jnp.where(lo, vr, vz).astype(BF16)

    first = jnp.where(i > 0, 1, 0)
    for sb in range(NSB):
        rows = slice(sb * BLOCK, (sb + 1) * BLOCK)
        mcols = maskt[first] if sb == 0 else maskt[1]
        for c in range(2):
            keys = kbuf[c, sb * BLOCK:(sb + 2) * BLOCK, :]
            vals = vbuf[c, sb * BLOCK:(sb + 2) * BLOCK, :]
            s = lax.dot_general(jnp.concatenate([qbuf[c, sb], eye4[...]], axis=1),
                                jnp.concatenate([keys, mcols], axis=1), _NT,
                                preferred_element_type=F32)
            ps, ms, ls = [], [], []
            for g in range(GROUP):
                sg = s[g * BLOCK:(g + 1) * BLOCK, :]
                m = jnp.max(sg, axis=-1, keepdims=True)
                p = jnp.exp2(sg - m)
                ps.append(p.astype(BF16))
                ms.append(m)
                ls.append(jnp.sum(p, axis=-1, keepdims=True))
            o = jnp.dot(jnp.concatenate(ps, axis=0), vals, preferred_element_type=F32)
            for jj in range(2):
                ev, od = 2 * jj, 2 * jj + 1
                slab = jnp.where(lo, o[ev * BLOCK:(ev + 1) * BLOCK, :], o[od * BLOCK:(od + 1) * BLOCK, :])
                l = jnp.where(lo, ls[ev], ls[od]) + jnp.exp2(sink2[2 * c + jj] - jnp.where(lo, ms[ev], ms[od]))
                col = (2 * c + jj) * LANES
                ga = z_ref[rows, O_GA + col:O_GA + col + LANES]
                mixed[rows, col:col + LANES] = (slab * (1.0 / l) * ga).astype(BF16)

    kbuf[:, 0:BLOCK, :] = kbuf[:, TB:TB + BLOCK, :]
    vbuf[:, 0:BLOCK, :] = vbuf[:, TB:TB + BLOCK, :]

    y_ref[...] = x_ref[...] + jnp.dot(mixed[...], wout_ref[...], preferred_element_type=F32)

    @pl.when(last)
    def _():
        unew_ref[:, 0, :] = z_ref[TB - POOL_STATE:TB, O_U:O_U + D_POOL]
        tail = slice(TB - WINDOW, TB)
        knew_ref[...] = _pair_rms(z_ref[tail, O_K:O_K + LANES], lo, kw_ref[...]).T
        vnew_ref[...] = z_ref[tail, O_V:O_V + LANES].T


def _sample_init(x_ref, win_ref, qw_ref, kw_ref, z_ref, q3, krep, vrep, snew, kvt):
    nb = x_ref.shape[0]
    lo = _lo_mask()
    z_ref[...] = _project(x_ref[:, 0, :], win_ref[...])
    khat = _pair_rms(z_ref[:, O_K:O_K + LANES], lo, kw_ref[...])
    vnew = z_ref[:, O_V:O_V + LANES]
    kvt[0] = khat.T
    kvt[1] = vnew.T
    q3[...] = jnp.zeros(q3.shape, F32)
    krep[...] = jnp.zeros(krep.shape, F32)
    vrep[...] = jnp.zeros(vrep.shape, F32)
    for j in range(4):
        qhat = _pair_rms(z_ref[:, j * LANES:(j + 1) * LANES], lo, qw_ref[...])
        qrot = pltpu.roll(qhat, HEAD_DIM, axis=1)
        grp_lo = j < 2
        for half in range(2):
            h = 2 * j + half
            src = qhat if (half == 0) == grp_lo else qrot
            q3[pl.ds(h, nb, stride=QROWS), :] = jnp.where(lo if grp_lo else ~lo, src, 0.0)
            krep[pl.ds(h, nb, stride=QROWS), :] = khat
            vrep[pl.ds(h, nb, stride=QROWS), :] = vnew
    s_new = jnp.sum(q3[...] * krep[...], axis=-1, keepdims=True)
    snew[...] = jnp.broadcast_to(s_new, snew.shape)


def _shift_caches(i, nb, ck_ref, cv_ref, knew_ref, vnew_ref, kvt):
    shift = lax.rem(nb - i * CB, nb)
    newest = lax.broadcasted_iota(jnp.int32, (LANES, WINDOW), 1) == WINDOW - 1
    for src, dst, which in ((ck_ref, knew_ref, 0), (cv_ref, vnew_ref, 1)):
        cols = pltpu.roll(kvt[which], shift, axis=1)
        shifted = pltpu.roll(src[...], WINDOW - 1, axis=2)
        for bb in range(CB):
            dst[bb] = jnp.where(newest, cols[:, bb:bb + 1], shifted[bb])


def _sample_step(j, last, sinks_ref, x_ref, pscale_ref, ck_ref, cv_ref, sp_ref, y_ref, spnew_ref,
                 wout_ref, wpool_ref, z_ref, q3, krep, vrep, snew, kvt, o3, mixed):
    nb = x_ref.shape[0]
    lo = _lo_mask()
    row0 = pl.multiple_of(j * BB, BB)
    rows = pl.ds(row0, BB)
    qrows = pl.ds(pl.multiple_of(j * (BB * QROWS), BB * QROWS), BB * QROWS)
    rid = lax.broadcasted_iota(jnp.int32, (1, QROWS, 1), 1)
    sink3 = jnp.zeros((1, QROWS, 1), F32)
    for h in range(N_HEADS):
        sink3 = jnp.where(rid == h, sinks_ref[h] * LOG2E, sink3)

    q = q3[qrows, :].reshape(BB, QROWS, LANES)
    s_new = snew[qrows, :].reshape(BB, QROWS, LANES)
    kt = ck_ref[...]
    vt = cv_ref[...]
    s = lax.dot_general(q.astype(BF16), kt.astype(BF16), (((2,), (1,)), ((0,), (0,))), preferred_element_type=F32)
    m = jnp.maximum(jnp.maximum(jnp.max(s, axis=-1, keepdims=True), s_new), sink3)
    p = jnp.exp2(s - m)
    p_new = jnp.exp2(s_new - m)
    l = jnp.sum(p, axis=-1, keepdims=True) + p_new + jnp.exp2(sink3 - m)
    o = lax.dot_general(p.astype(BF16), vt.astype(BF16), (((2,), (2,)), ((0,), (0,))), preferred_element_type=F32)
    o = (o + p_new * vrep[qrows, :].reshape(BB, QROWS, LANES)) / l
    o3[...] = o.reshape(BB * QROWS, LANES)

    for jj in range(4):
        grp_lo = jj < 2
        oh = [o3[pl.ds(2 * jj + half, BB, stride=QROWS), :] for half in range(2)]
        a = oh[0] if grp_lo else pltpu.roll(oh[0], HEAD_DIM, axis=1)
        b = pltpu.roll(oh[1], HEAD_DIM, axis=1) if grp_lo else oh[1]
        ga = z_ref[rows, O_GA + jj * LANES:O_GA + (jj + 1) * LANES]
        mixed[rows, jj * LANES:(jj + 1) * LANES] = (jnp.where(lo, a, b) * _silu_half(ga)).astype(BF16)

    u = z_ref[rows, O_U:O_U + D_POOL]
    spnew_ref[0:POOL_STATE - 1] = sp_ref[1:POOL_STATE]
    spnew_ref[POOL_STATE - 1] = u
    ds = []
    for g, w in enumerate(POOL_WINDOWS):
        cols = slice(g * LANES, (g + 1) * LANES)
        ug = u[:, cols]
        win_sum = ug
        for r in range(POOL_STATE - (w - 1), POOL_STATE):
            win_sum = win_sum + sp_ref[r, :, cols]
        ds.append((win_sum * (1.0 / w) - ug).astype(BF16))
    for pr in range(len(POOL_WINDOWS) // 2):
        cols = slice(pr * 2 * LANES, (pr + 1) * 2 * LANES)
        d2 = jnp.concatenate(ds[2 * pr:2 * pr + 2], axis=1)
        po = jnp.dot(d2, wpool_ref[pr], preferred_element_type=F32) * pscale_ref[:, cols]
        gp = z_ref[rows, O_GP + pr * 2 * LANES:O_GP + (pr + 1) * 2 * LANES]
        mixed[rows, D_ATTN + pr * 2 * LANES:D_ATTN + (pr + 1) * 2 * LANES] = (po * _silu_half(gp)).astype(BF16)

    @pl.when(last)
    def _():
        y_ref[:, 0, :] = x_ref[:, 0, :] + jnp.dot(mixed[...], wout_ref[...], preferred_element_type=F32)


def _fused_kernel(n_prompt, sinks_ref, xp_ref, xs_ref, nw_ref, win_hbm, qn_ref, kn_ref, wpool_f32, pscale_ref, wout_hbm,
                  ck_ref, cv_ref, sp_ref, ckc_ref, cvc_ref,
                  yp_ref, kp_ref, vp_ref, up_ref, ys_ref, kq_ref, vq_ref, pq_ref,
                  win_ref, wout_ref, wpool_ref, qw_ref, kw_ref, stage, sem,
                  z_ref, hbuf, qbuf, kbuf, vbuf, uext, mixed, sink2, eye4, maskt,
                  zs_ref, q3, krep, vrep, snew, kvt, o3, mixed_s):
    i = pl.program_id(0)

    @pl.when(i == 0)
    def _():
        _load_weights(win_hbm, wout_hbm, wpool_f32, nw_ref, qn_ref, kn_ref, stage, sem, win_ref, wout_ref, wpool_ref,
                      qw_ref, kw_ref)
        _prompt_init(sinks_ref, kbuf, vbuf, uext, sink2, eye4, maskt)
        _sample_init(xs_ref, win_ref, qw_ref, kw_ref, zs_ref, q3, krep, vrep, snew, kvt)

    @pl.when(i < n_prompt)
    def _():
        _shift_caches(i, xs_ref.shape[0], ckc_ref, cvc_ref, kq_ref, vq_ref, kvt)
        _prompt_step(i, i == n_prompt - 1, xp_ref, pscale_ref, yp_ref, kp_ref, vp_ref, up_ref,
                     win_ref, wout_ref, wpool_ref, qw_ref, kw_ref, z_ref, hbuf, qbuf, kbuf, vbuf, uext, mixed, sink2, eye4, maskt)

    @pl.when(i >= n_prompt)
    def _():
        _sample_step(i - n_prompt, i == pl.num_programs(0) - 1, sinks_ref, xs_ref, pscale_ref, ck_ref, cv_ref, sp_ref,
                     ys_ref, pq_ref, wout_ref, wpool_ref, zs_ref, q3, krep, vrep, snew, kvt, o3, mixed_s)


def _fused_call(sinks, xp, xs, nw, win, qn, kn, wpool, pscale, wout, ck, cv, sp):
    seq, nb = xp.shape[0], xs.shape[0]
    n_prompt, n_sample = seq // TB, nb // BB
    assert nb == CB * n_prompt
    const2 = lambda i: (0, 0)
    const3 = lambda i: (0, 0, 0)
    pblk = lambda i: (jnp.minimum(i, n_prompt - 1), 0)
    cblk = lambda i: (jnp.minimum(i, n_prompt - 1), 0, 0)
    sidx = lambda i: jnp.clip(i - n_prompt, 0, n_sample - 1)
    chunk3 = lambda i: (sidx(i), 0, 0)
    mid3 = lambda i: (0, sidx(i), 0)
    once = pl.Buffered(1)
    return pl.pallas_call(
        functools.partial(_fused_kernel, n_prompt),
        grid=(n_prompt + n_sample,),
        in_specs=[
            pl.BlockSpec(memory_space=pltpu.SMEM),
            pl.BlockSpec((TB, D_MODEL), pblk),
            pl.BlockSpec((nb, 1, D_MODEL), const3, pipeline_mode=once),
            pl.BlockSpec((1, D_MODEL), const2),
            pl.BlockSpec(memory_space=pl.ANY),
            pl.BlockSpec((1, HEAD_DIM), const2),
            pl.BlockSpec((1, HEAD_DIM), const2),
            pl.BlockSpec((4, LANES, LANES), const3, pipeline_mode=once),
            pl.BlockSpec((1, D_POOL), const2),
            pl.BlockSpec(memory_space=pl.ANY),
            pl.BlockSpec((BB, LANES, WINDOW), chunk3),
            pl.BlockSpec((BB, LANES, WINDOW), chunk3),
            pl.BlockSpec((POOL_STATE, BB, D_POOL), mid3),
            pl.BlockSpec((CB, LANES, WINDOW), cblk),
            pl.BlockSpec((CB, LANES, WINDOW), cblk),
        ],
        out_specs=[
            pl.BlockSpec((TB, D_MODEL), pblk),
            pl.BlockSpec((LANES, WINDOW), const2),
            pl.BlockSpec((LANES, WINDOW), const2),
            pl.BlockSpec((POOL_STATE, 1, D_POOL), const3),
            pl.BlockSpec((nb, 1, D_MODEL), const3, pipeline_mode=once),
            pl.BlockSpec((CB, LANES, WINDOW), cblk),
            pl.BlockSpec((CB, LANES, WINDOW), cblk),
            pl.BlockSpec((POOL_STATE, BB, D_POOL), mid3),
        ],
        out_shape=[
            jax.ShapeDtypeStruct((seq, D_MODEL), F32),
            jax.ShapeDtypeStruct((LANES, WINDOW), F32),
            jax.ShapeDtypeStruct((LANES, WINDOW), F32),
            jax.ShapeDtypeStruct((POOL_STATE, 1, D_POOL), F32),
            jax.ShapeDtypeStruct((nb, 1, D_MODEL), F32),
            jax.ShapeDtypeStruct((nb, LANES, WINDOW), F32),
            jax.ShapeDtypeStruct((nb, LANES, WINDOW), F32),
            jax.ShapeDtypeStruct((POOL_STATE, nb, D_POOL), F32),
        ],
        scratch_shapes=[
            pltpu.VMEM((D_MODEL, D_IN_PROJ), BF16),
            pltpu.VMEM((D_MODEL, D_MODEL), BF16),
            pltpu.VMEM((2, 2 * LANES, 2 * LANES), BF16),
            pltpu.VMEM((1, LANES), F32),
            pltpu.VMEM((1, LANES), F32),
            pltpu.VMEM((2, D_MODEL, WCHUNK), F32),
            pltpu.SemaphoreType.DMA((2,)),
            pltpu.VMEM((TB, D_IN_PROJ), F32),
            pltpu.VMEM((TB, D_MODEL), BF16),
            pltpu.VMEM((2, NSB, GROUP * BLOCK, LANES), BF16),
            pltpu.VMEM((2, TB + BLOCK, LANES), BF16),
            pltpu.VMEM((2, TB + BLOCK, LANES), BF16),
            pltpu.VMEM((TB + 16, D_POOL), F32),
            pltpu.VMEM((TB, D_MODEL), BF16),
            pltpu.VMEM((N_HEADS // 2, 1, LANES), F32),
            pltpu.VMEM((GROUP * BLOCK, LANES), BF16),
            pltpu.VMEM((2, 2 * BLOCK, LANES), BF16),
            pltpu.VMEM((nb, D_IN_PROJ), F32),
            pltpu.VMEM((nb * QROWS, LANES), F32),
            pltpu.VMEM((nb * QROWS, LANES), F32),
            pltpu.VMEM((nb * QROWS, LANES), F32),
            pltpu.VMEM((nb * QROWS, LANES), F32),
            pltpu.VMEM((2, LANES, nb), F32),
            pltpu.VMEM((BB * QROWS, LANES), F32),
            pltpu.VMEM((nb, D_MODEL), BF16),
        ],
        compiler_params=pltpu.CompilerParams(dimension_semantics=("arbitrary",), vmem_limit_bytes=VMEM_LIMIT),
        name="hybrid_step",
    )(sinks, xp, xs, nw, win, qn, kn, wpool, pscale, wout, ck, cv, sp, ck, cv)


def _cache_in(c):
    nb = c.shape[0]
    return jnp.transpose(c, (0, 2, 3, 1)).reshape(nb, LANES, WINDOW)


def _cache_out(c):
    nb = c.shape[0]
    return jnp.transpose(c.reshape(nb, N_KV_HEADS, HEAD_DIM, WINDOW), (0, 3, 1, 2))


def kernel(x_prompt, x_sample, cache_k, cache_v, state_pool, norm_w, w_in, q_norm_w, k_norm_w, sinks, w_pool,
           pool_scale, w_out):
    depth = norm_w.shape[0]
    assert depth == 1 and x_prompt.shape[0] == 1 and x_sample.shape[1] == 1
    seq = x_prompt.shape[1]
    nb = x_sample.shape[0]
    assert seq % TB == 0 and nb % BB == 0 and nb == LANES

    yp, kp, vp, up, ys, kq, vq, pq = _fused_call(
        sinks[0], x_prompt[0], x_sample, norm_w, w_in[0], q_norm_w, k_norm_w, w_pool[0], pool_scale, w_out[0],
        _cache_in(cache_k[0]), _cache_in(cache_v[0]), jnp.transpose(state_pool[0], (1, 0, 2)))
    return (yp[None], ys, _cache_out(kp[None])[None], _cache_out(vp[None])[None],
            jnp.transpose(up, (1, 0, 2))[None],
            _cache_out(kq)[None], _cache_out(vq)[None], jnp.transpose(pq, (1, 0, 2))[None])
```

```python
import functools

import jax
import jax.numpy as jnp
from jax import lax
from jax.experimental import pallas as pl
from jax.experimental.pallas import tpu as pltpu

D_MODEL = 1024
HEAD_DIM = 64
N_HEADS = 8
N_KV_HEADS = 2
GROUP = 4
WINDOW = 128
BLOCK = 128
POOL_WINDOWS = (2, 4, 8, 16)
POOL_STATE = 15
D_ATTN = 512
D_POOL = 512
D_IN_PROJ = 2304
EPS = 1e-6
NEG_INF = -1e30
LOG2E = 1.4426950408889634

O_K = 512
O_V = 640
O_GA = 768
O_U = 1280
O_GP = 1792

LANES = 128
TB = 512
NSB = TB // BLOCK
BB = 16
CB = 4
QROWS = 16
WCHUNK = 256
VMEM_LIMIT = 58 * 1024 * 1024

F32 = jnp.float32
BF16 = jnp.bfloat16
_NT = (((1,), (1,)), ((), ()))


def _silu_half(h):
    return h + h * jnp.tanh(h)


def _lo_mask():
    return lax.broadcasted_iota(jnp.int32, (1, LANES), 1) < HEAD_DIM


def _pair_rms(zs, lo, w2):
    sq = zs * zs
    s_lo = jnp.sum(jnp.where(lo, sq, 0.0), axis=-1, keepdims=True)
    s_hi = jnp.sum(jnp.where(lo, 0.0, sq), axis=-1, keepdims=True)
    r = lax.rsqrt(jnp.where(lo, s_lo, s_hi) * (1.0 / HEAD_DIM) + EPS)
    return zs * r * w2


def _row_scale(x):
    return lax.rsqrt(jnp.mean(x * x, axis=-1, keepdims=True) + EPS)


def _project(x, win):
    return jnp.dot(x.astype(BF16), win, preferred_element_type=F32) * _row_scale(x)


def _weight_chunks():
    chunks = []
    for c0 in range(0, D_IN_PROJ, WCHUNK):
        gate = O_GA <= c0 < O_U or c0 >= O_GP
        chunks.append((0, c0, 0.5 if gate else 1.0))
    chunks += [(1, c0, 1.0) for c0 in range(0, D_MODEL, WCHUNK)]
    return chunks


def _load_weights(win_hbm, wout_hbm, wpool_f32, nw_ref, qn_ref, kn_ref, stage, sem, win_bf, wout_bf, wpool_bf,
                  qw_ref, kw_ref):
    chunks = _weight_chunks()
    srcs, dsts = (win_hbm, wout_hbm), (win_bf, wout_bf)
    nw_tile = jnp.concatenate([jnp.broadcast_to(nw_ref[:, t * LANES:(t + 1) * LANES], (LANES, LANES)).T
                               for t in range(D_MODEL // LANES)], axis=0)
    nw_rows = jnp.concatenate([nw_tile] * (WCHUNK // LANES), axis=1)

    def copy(k):
        which, c0, _ = chunks[k]
        return pltpu.make_async_copy(srcs[which].at[:, pl.ds(c0, WCHUNK)], stage.at[k % 2], sem.at[k % 2])

    copy(0).start()
    copy(1).start()
    for k, (which, c0, scale) in enumerate(chunks):
        copy(k).wait()
        w = stage[k % 2]
        if which == 0:
            w = w * nw_rows if scale == 1.0 else w * (scale * nw_rows)
        dsts[which][:, c0:c0 + WCHUNK] = w.astype(BF16)
        if k + 2 < len(chunks):
            copy(k + 2).start()
    wpool_bf[...] = jnp.zeros(wpool_bf.shape, BF16)
    for g in range(len(POOL_WINDOWS)):
        d0 = (g % 2) * LANES
        wpool_bf[g // 2, d0:d0 + LANES, d0:d0 + LANES] = wpool_f32[g].astype(BF16)
    qn = qn_ref[...] * (HEAD_DIM ** -0.5 * LOG2E)
    qw_ref[...] = jnp.concatenate([qn, qn], axis=1)
    kw_ref[...] = jnp.concatenate([kn_ref[...], kn_ref[...]], axis=1)


def _prompt_init(sinks_ref, kbuf, vbuf, uext, sink2, eye4, maskt):
    kbuf[:, 0:BLOCK, :] = jnp.zeros((2, BLOCK, LANES), BF16)
    vbuf[:, 0:BLOCK, :] = jnp.zeros((2, BLOCK, LANES), BF16)
    uext[0:16, :] = jnp.zeros((16, D_POOL), F32)
    lo = _lo_mask()
    for jj in range(N_HEADS // 2):
        sink2[jj] = jnp.where(lo, sinks_ref[2 * jj], sinks_ref[2 * jj + 1]) * LOG2E
    rr = lax.broadcasted_iota(jnp.int32, (GROUP * BLOCK, LANES), 0)
    ll = lax.broadcasted_iota(jnp.int32, (GROUP * BLOCK, LANES), 1)
    eye4[...] = jnp.where((rr & (BLOCK - 1)) == ll, 1.0, 0.0).astype(BF16)
    cc = lax.broadcasted_iota(jnp.int32, (2 * BLOCK, LANES), 0)
    qr = lax.broadcasted_iota(jnp.int32, (2 * BLOCK, LANES), 1)
    band = (cc >= qr) & (cc <= qr + WINDOW)
    maskt[0] = jnp.where(band & (cc >= BLOCK), 0.0, NEG_INF).astype(BF16)
    maskt[1] = jnp.where(band, 0.0, NEG_INF).astype(BF16)


def _prompt_step(i, last, x_ref, pscale_ref, y_ref, knew_ref, vnew_ref, unew_ref,
                 win_ref, wout_ref, wpool_ref, qw_ref, kw_ref, z_ref, hbuf, qbuf, kbuf, vbuf, uext, mixed, sink2, eye4, maskt):
    lo = _lo_mask()
    hbuf[...] = x_ref[...].astype(BF16)
    rs = _row_scale(x_ref[...])

    def proj(cols, rows=slice(None)):
        return jnp.dot(hbuf[rows, :], win_ref[:, cols], preferred_element_type=F32) * rs[rows]

    z_ref[:, O_U:] = proj(slice(O_U, D_IN_PROJ))
    uext[16:, :] = z_ref[:, O_U:O_U + D_POOL]
    pos16 = i * TB + lax.broadcasted_iota(jnp.int32, (16, LANES), 0)
    ds = []
    for g, w in enumerate(POOL_WINDOWS):
        acc = uext[:, g * LANES:(g + 1) * LANES]
        sh = 1
        while sh < w:
            acc = acc + pltpu.roll(acc, sh, axis=0)
            sh *= 2
        ug = z_ref[:, O_U + g * LANES:O_U + (g + 1) * LANES]
        cnt = jnp.minimum(w, pos16 + 1).astype(F32)
        d_first = acc[16:32, :] / cnt - ug[0:16, :]
        d_rest = acc[32:, :] * (1.0 / w) - ug[16:, :]
        ds.append(jnp.concatenate([d_first, d_rest], axis=0).astype(BF16))
    uext[0:16, :] = uext[TB:TB + 16, :]

    z_ref[:, :O_K] = proj(slice(0, O_K))

    for pr in range(len(POOL_WINDOWS) // 2):
        cols = slice(pr * 2 * LANES, (pr + 1) * 2 * LANES)
        d2 = jnp.concatenate(ds[2 * pr:2 * pr + 2], axis=1)
        po = jnp.dot(d2, wpool_ref[pr], preferred_element_type=F32) * pscale_ref[:, cols]
        gp = z_ref[:, O_GP + pr * 2 * LANES:O_GP + (pr + 1) * 2 * LANES]
        mixed[:, D_ATTN + pr * 2 * LANES:D_ATTN + (pr + 1) * 2 * LANES] = (po * _silu_half(gp)).astype(BF16)

    for j in range(4):
        qhat = _pair_rms(z_ref[:, j * LANES:(j + 1) * LANES], lo, qw_ref[...])
        q_lo = jnp.where(lo, qhat, 0.0).astype(BF16)
        q_hi = jnp.where(lo, 0.0, qhat).astype(BF16)
        c, g0 = j // 2, 2 * (j % 2)
        for sb in range(NSB):
            rows = slice(sb * BLOCK, (sb + 1) * BLOCK)
            qbuf[c, sb, g0 * BLOCK:(g0 + 1) * BLOCK, :] = q_lo[rows]
            qbuf[c, sb, (g0 + 1) * BLOCK:(g0 + 2) * BLOCK, :] = q_hi[rows]

    half = TB // 2
    for r0 in (0, half):
        z_ref[r0:r0 + half, O_K:O_GA] = proj(slice(O_K, O_GA), slice(r0, r0 + half))
    z_ref[:, O_GA:O_U] = proj(slice(O_GA, O_U))

    khat = _pair_rms(z_ref[:, O_K:O_K + LANES], lo, kw_ref[...])
    kr = pltpu.roll(khat, HEAD_DIM, axis=1)
    kbuf[0, BLOCK:, :] = jnp.where(lo, khat, kr).astype(BF16)
    kbuf[1, BLOCK:, :] = jnp.where(lo, kr, khat).astype(BF16)
    vz = z_ref[:, O_V:O_V + LANES]
    vr = pltpu.roll(vz, HEAD_DIM, axis=1)
    vbuf[0, BLOCK:, :] = jnp.where(lo, vz, vr).astype(BF16)
    vbuf[1, BLOCK:, :] = jnp.where(lo, vr, vz).astype(BF16)

    first = jnp.where(i > 0, 1, 0)
    for sb in range(NSB):
        rows = slice(sb * BLOCK, (sb + 1) * BLOCK)
        mcols = maskt[first] if sb == 0 else maskt[1]
        ps, ms, ls, vals = [], [], [], []
        for c in range(2):
            keys = kbuf[c, sb * BLOCK:(sb + 2) * BLOCK, :]
            vals.append(vbuf[c, sb * BLOCK:(sb + 2) * BLOCK, :])
            s = lax.dot_general(jnp.concatenate([qbuf[c, sb], eye4[...]], axis=1),
                                jnp.concatenate([keys, mcols], axis=1), _NT,
                                preferred_element_type=F32)
            for g in range(GROUP):
                sg = s[g * BLOCK:(g + 1) * BLOCK, :]
                m = jnp.max(sg, axis=-1, keepdims=True)
                p = jnp.exp2(sg - m)
                ps.append(p.astype(BF16))
                ms.append(m)
                ls.append(jnp.sum(p, axis=-1, keepdims=True))
        pcat = jnp.concatenate([jnp.concatenate([ps[g], ps[GROUP + g]], axis=1) for g in range(GROUP)], axis=0)
        zero = jnp.zeros_like(vals[0])
        vcat = jnp.concatenate([jnp.concatenate([vals[0], zero], axis=1),
                                jnp.concatenate([zero, vals[1]], axis=1)], axis=0)
        o = jnp.dot(pcat, vcat, preferred_element_type=F32)
        for c in range(2):
            for jj in range(2):
                ev, od = 2 * jj, 2 * jj + 1
                oc = slice(c * LANES, (c + 1) * LANES)
                slab = jnp.where(lo, o[ev * BLOCK:(ev + 1) * BLOCK, oc], o[od * BLOCK:(od + 1) * BLOCK, oc])
                l = (jnp.where(lo, ls[c * GROUP + ev], ls[c * GROUP + od])
                     + jnp.exp2(sink2[2 * c + jj] - jnp.where(lo, ms[c * GROUP + ev], ms[c * GROUP + od])))
                col = (2 * c + jj) * LANES
                ga = z_ref[rows, O_GA + col:O_GA + col + LANES]
                mixed[rows, col:col + LANES] = (slab * (1.0 / l) * _silu_half(ga)).astype(BF16)

    kbuf[:, 0:BLOCK, :] = kbuf[:, TB:TB + BLOCK, :]
    vbuf[:, 0:BLOCK, :] = vbuf[:, TB:TB + BLOCK, :]

    y_ref[...] = x_ref[...] + jnp.dot(mixed[...], wout_ref[...], preferred_element_type=F32)

    @pl.when(last)
    def _():
        unew_ref[:, 0, :] = z_ref[TB - POOL_STATE:TB, O_U:O_U + D_POOL]
        tail = slice(TB - WINDOW, TB)
        knew_ref[...] = _pair_rms(z_ref[tail, O_K:O_K + LANES], lo, kw_ref[...]).T
        vnew_ref[...] = z_ref[tail, O_V:O_V + LANES].T


def _sample_init(x_ref, win_ref, qw_ref, kw_ref, z_ref, q3, krep, vrep, snew, kvt):
    nb = x_ref.shape[0]
    lo = _lo_mask()
    z_ref[...] = _project(x_ref[:, 0, :], win_ref[...])
    khat = _pair_rms(z_ref[:, O_K:O_K + LANES], lo, kw_ref[...])
    vnew = z_ref[:, O_V:O_V + LANES]
    kvt[0] = khat.T
    kvt[1] = vnew.T
    q3[...] = jnp.zeros(q3.shape, F32)
    krep[...] = jnp.zeros(krep.shape, F32)
    vrep[...] = jnp.zeros(vrep.shape, F32)
    for j in range(4):
        qhat = _pair_rms(z_ref[:, j * LANES:(j + 1) * LANES], lo, qw_ref[...])
        qrot = pltpu.roll(qhat, HEAD_DIM, axis=1)
        grp_lo = j < 2
        for half in range(2):
            h = 2 * j + half
            src = qhat if (half == 0) == grp_lo else qrot
            q3[pl.ds(h, nb, stride=QROWS), :] = jnp.where(lo if grp_lo else ~lo, src, 0.0)
            krep[pl.ds(h, nb, stride=QROWS), :] = khat
            vrep[pl.ds(h, nb, stride=QROWS), :] = vnew
    s_new = jnp.sum(q3[...] * krep[...], axis=-1, keepdims=True)
    snew[...] = jnp.broadcast_to(s_new, snew.shape)


def _shift_caches(i, nb, ck_ref, cv_ref, knew_ref, vnew_ref, kvt):
    shift = lax.rem(nb - i * CB, nb)
    newest = lax.broadcasted_iota(jnp.int32, (LANES, WINDOW), 1) == WINDOW - 1
    for src, dst, which in ((ck_ref, knew_ref, 0), (cv_ref, vnew_ref, 1)):
        cols = pltpu.roll(kvt[which], shift, axis=1)
        shifted = pltpu.roll(src[...], WINDOW - 1, axis=2)
        for bb in range(CB):
            dst[bb] = jnp.where(newest, cols[:, bb:bb + 1], shifted[bb])


def _sample_step(j, last, sinks_ref, x_ref, pscale_ref, ck_ref, cv_ref, sp_ref, y_ref, spnew_ref,
                 wout_ref, wpool_ref, z_ref, q3, krep, vrep, snew, kvt, o3, mixed):
    nb = x_ref.shape[0]
    lo = _lo_mask()
    row0 = pl.multiple_of(j * BB, BB)
    rows = pl.ds(row0, BB)
    qrows = pl.ds(pl.multiple_of(j * (BB * QROWS), BB * QROWS), BB * QROWS)
    rid = lax.broadcasted_iota(jnp.int32, (1, QROWS, 1), 1)
    sink3 = jnp.zeros((1, QROWS, 1), F32)
    for h in range(N_HEADS):
        sink3 = jnp.where(rid == h, sinks_ref[h] * LOG2E, sink3)

    q = q3[qrows, :].reshape(BB, QROWS, LANES)
    s_new = snew[qrows, :].reshape(BB, QROWS, LANES)
    kt = ck_ref[...]
    vt = cv_ref[...]
    s = lax.dot_general(q.astype(BF16), kt.astype(BF16), (((2,), (1,)), ((0,), (0,))), preferred_element_type=F32)
    m = jnp.maximum(jnp.maximum(jnp.max(s, axis=-1, keepdims=True), s_new), sink3)
    p = jnp.exp2(s - m)
    p_new = jnp.exp2(s_new - m)
    l = jnp.sum(p, axis=-1, keepdims=True) + p_new + jnp.exp2(sink3 - m)
    o = lax.dot_general(p.astype(BF16), vt.astype(BF16), (((2,), (2,)), ((0,), (0,))), preferred_element_type=F32)
    o = (o + p_new * vrep[qrows, :].reshape(BB, QROWS, LANES)) / l
    o3[...] = o.reshape(BB * QROWS, LANES)

    for jj in range(4):
        grp_lo = jj < 2
        oh = [o3[pl.ds(2 * jj + half, BB, stride=QROWS), :] for half in range(2)]
        a = oh[0] if grp_lo else pltpu.roll(oh[0], HEAD_DIM, axis=1)
        b = pltpu.roll(oh[1], HEAD_DIM, axis=1) if grp_lo else oh[1]
        ga = z_ref[rows, O_GA + jj * LANES:O_GA + (jj + 1) * LANES]
        mixed[rows, jj * LANES:(jj + 1) * LANES] = (jnp.where(lo, a, b) * _silu_half(ga)).astype(BF16)

    u = z_ref[rows, O_U:O_U + D_POOL]
    spnew_ref[0:POOL_STATE - 1] = sp_ref[1:POOL_STATE]
    spnew_ref[POOL_STATE - 1] = u
    ds = []
    for g, w in enumerate(POOL_WINDOWS):
        cols = slice(g * LANES, (g + 1) * LANES)
        ug = u[:, cols]
        win_sum = ug
        for r in range(POOL_STATE - (w - 1), POOL_STATE):
            win_sum = win_sum + sp_ref[r, :, cols]
        ds.append((win_sum * (1.0 / w) - ug).astype(BF16))
    for pr in range(len(POOL_WINDOWS) // 2):
        cols = slice(pr * 2 * LANES, (pr + 1) * 2 * LANES)
        d2 = jnp.concatenate(ds[2 * pr:2 * pr + 2], axis=1)
        po = jnp.dot(d2, wpool_ref[pr], preferred_element_type=F32) * pscale_ref[:, cols]
        gp = z_ref[rows, O_GP + pr * 2 * LANES:O_GP + (pr + 1) * 2 * LANES]
        mixed[rows, D_ATTN + pr * 2 * LANES:D_ATTN + (pr + 1) * 2 * LANES] = (po * _silu_half(gp)).astype(BF16)

    @pl.when(last)
    def _():
        y_ref[:, 0, :] = x_ref[:, 0, :] + jnp.dot(mixed[...], wout_ref[...], preferred_element_type=F32)


def _fused_kernel(n_prompt, sinks_ref, xp_ref, xs_ref, nw_ref, win_hbm, qn_ref, kn_ref, wpool_f32, pscale_ref, wout_hbm,
                  ck_ref, cv_ref, sp_ref, ckc_ref, cvc_ref,
                  yp_ref, kp_ref, vp_ref, up_ref, ys_ref, kq_ref, vq_ref, pq_ref,
                  win_ref, wout_ref, wpool_ref, qw_ref, kw_ref, stage, sem,
                  z_ref, hbuf, qbuf, kbuf, vbuf, uext, mixed, sink2, eye4, maskt,
                  zs_ref, q3, krep, vrep, snew, kvt, o3, mixed_s):
    i = pl.program_id(0)

    @pl.when(i == 0)
    def _():
        _load_weights(win_hbm, wout_hbm, wpool_f32, nw_ref, qn_ref, kn_ref, stage, sem, win_ref, wout_ref, wpool_ref,
                      qw_ref, kw_ref)
        _prompt_init(sinks_ref, kbuf, vbuf, uext, sink2, eye4, maskt)
        _sample_init(xs_ref, win_ref, qw_ref, kw_ref, zs_ref, q3, krep, vrep, snew, kvt)

    @pl.when(i < n_prompt)
    def _():
        _shift_caches(i, xs_ref.shape[0], ckc_ref, cvc_ref, kq_ref, vq_ref, kvt)
        _prompt_step(i, i == n_prompt - 1, xp_ref, pscale_ref, yp_ref, kp_ref, vp_ref, up_ref,
                     win_ref, wout_ref, wpool_ref, qw_ref, kw_ref, z_ref, hbuf, qbuf, kbuf, vbuf, uext, mixed, sink2, eye4, maskt)

    @pl.when(i >= n_prompt)
    def _():
        _sample_step(i - n_prompt, i == pl.num_programs(0) - 1, sinks_ref, xs_ref, pscale_ref, ck_ref, cv_ref, sp_ref,
                     ys_ref, pq_ref, wout_ref, wpool_ref, zs_ref, q3, krep, vrep, snew, kvt, o3, mixed_s)


def _fused_call(sinks, xp, xs, nw, win, qn, kn, wpool, pscale, wout, ck, cv, sp):
    seq, nb = xp.shape[0], xs.shape[0]
    n_prompt, n_sample = seq // TB, nb // BB
    assert nb == CB * n_prompt
    const2 = lambda i: (0, 0)
    const3 = lambda i: (0, 0, 0)
    pblk = lambda i: (jnp.minimum(i, n_prompt - 1), 0)
    cblk = lambda i: (jnp.minimum(i, n_prompt - 1), 0, 0)
    sidx = lambda i: jnp.clip(i - n_prompt, 0, n_sample - 1)
    chunk3 = lambda i: (sidx(i), 0, 0)
    mid3 = lambda i: (0, sidx(i), 0)
    once = pl.Buffered(1)
    return pl.pallas_call(
        functools.partial(_fused_kernel, n_prompt),
        grid=(n_prompt + n_sample,),
        in_specs=[
            pl.BlockSpec(memory_space=pltpu.SMEM),
            pl.BlockSpec((TB, D_MODEL), pblk),
            pl.BlockSpec((nb, 1, D_MODEL), const3, pipeline_mode=once),
            pl.BlockSpec((1, D_MODEL), const2),
            pl.BlockSpec(memory_space=pl.ANY),
            pl.BlockSpec((1, HEAD_DIM), const2),
            pl.BlockSpec((1, HEAD_DIM), const2),
            pl.BlockSpec((4, LANES, LANES), const3, pipeline_mode=once),
            pl.BlockSpec((1, D_POOL), const2),
            pl.BlockSpec(memory_space=pl.ANY),
            pl.BlockSpec((BB, LANES, WINDOW), chunk3),
            pl.BlockSpec((BB, LANES, WINDOW), chunk3),
            pl.BlockSpec((POOL_STATE, BB, D_POOL), mid3),
            pl.BlockSpec((CB, LANES, WINDOW), cblk),
            pl.BlockSpec((CB, LANES, WINDOW), cblk),
        ],
        out_specs=[
            pl.BlockSpec((TB, D_MODEL), pblk),
            pl.BlockSpec((LANES, WINDOW), const2),
            pl.BlockSpec((LANES, WINDOW), const2),
            pl.BlockSpec((POOL_STATE, 1, D_POOL), const3),
            pl.BlockSpec((nb, 1, D_MODEL), const3, pipeline_mode=once),
            pl.BlockSpec((CB, LANES, WINDOW), cblk),
            pl.BlockSpec((CB, LANES, WINDOW), cblk),
            pl.BlockSpec((POOL_STATE, BB, D_POOL), mid3),
        ],
        out_shape=[
            jax.ShapeDtypeStruct((seq, D_MODEL), F32),
            jax.ShapeDtypeStruct((LANES, WINDOW), F32),
            jax.ShapeDtypeStruct((LANES, WINDOW), F32),
            jax.ShapeDtypeStruct((POOL_STATE, 1, D_POOL), F32),
            jax.ShapeDtypeStruct((nb, 1, D_MODEL), F32),
            jax.ShapeDtypeStruct((nb, LANES, WINDOW), F32),
            jax.ShapeDtypeStruct((nb, LANES, WINDOW), F32),
            jax.ShapeDtypeStruct((POOL_STATE, nb, D_POOL), F32),
        ],
        scratch_shapes=[
            pltpu.VMEM((D_MODEL, D_IN_PROJ), BF16),
            pltpu.VMEM((D_MODEL, D_MODEL), BF16),
            pltpu.VMEM((2, 2 * LANES, 2 * LANES), BF16),
            pltpu.VMEM((1, LANES), F32),
            pltpu.VMEM((1, LANES), F32),
            pltpu.VMEM((2, D_MODEL, WCHUNK), F32),
            pltpu.SemaphoreType.DMA((2,)),
            pltpu.VMEM((TB, D_IN_PROJ), F32),
            pltpu.VMEM((TB, D_MODEL), BF16),
            pltpu.VMEM((2, NSB, GROUP * BLOCK, LANES), BF16),
            pltpu.VMEM((2, TB + BLOCK, LANES), BF16),
            pltpu.VMEM((2, TB + BLOCK, LANES), BF16),
            pltpu.VMEM((TB + 16, D_POOL), F32),
            pltpu.VMEM((TB, D_MODEL), BF16),
            pltpu.VMEM((N_HEADS // 2, 1, LANES), F32),
            pltpu.VMEM((GROUP * BLOCK, LANES), BF16),
            pltpu.VMEM((2, 2 * BLOCK, LANES), BF16),
            pltpu.VMEM((nb, D_IN_PROJ), F32),
            pltpu.VMEM((nb * QROWS, LANES), F32),
            pltpu.VMEM((nb * QROWS, LANES), F32),
            pltpu.VMEM((nb * QROWS, LANES), F32),
            pltpu.VMEM((nb * QROWS, LANES), F32),
            pltpu.VMEM((2, LANES, nb), F32),
            pltpu.VMEM((BB * QROWS, LANES), F32),
            pltpu.VMEM((nb, D_MODEL), BF16),
        ],
        compiler_params=pltpu.CompilerParams(dimension_semantics=("arbitrary",), vmem_limit_bytes=VMEM_LIMIT),
        name="hybrid_step",
    )(sinks, xp, xs, nw, win, qn, kn, wpool, pscale, wout, ck, cv, sp, ck, cv)


def _cache_in(c):
    nb = c.shape[0]
    return jnp.transpose(c, (0, 2, 3, 1)).reshape(nb, LANES, WINDOW)


def _cache_out(c):
    nb = c.shape[0]
    return jnp.transpose(c.reshape(nb, N_KV_HEADS, HEAD_DIM, WINDOW), (0, 3, 1, 2))


def kernel(x_prompt, x_sample, cache_k, cache_v, state_pool, norm_w, w_in, q_norm_w, k_norm_w, sinks, w_pool,
           pool_scale, w_out):
    depth = norm_w.shape[0]
    assert depth == 1 and x_prompt.shape[0] == 1 and x_sample.shape[1] == 1
    seq = x_prompt.shape[1]
    nb = x_sample.shape[0]
    assert seq % TB == 0 and nb % BB == 0 and nb == LANES

    yp, kp, vp, up, ys, kq, vq, pq = _fused_call(
        sinks[0], x_prompt[0], x_sample, norm_w, w_in[0], q_norm_w, k_norm_w, w_pool[0], pool_scale, w_out[0],
        _cache_in(cache_k[0]), _cache_in(cache_v[0]), jnp.transpose(state_pool[0], (1, 0, 2)))
    return (yp[None], ys, _cache_out(kp[None])[None], _cache_out(vp[None])[None],
            jnp.transpose(up, (1, 0, 2))[None],
            _cache_out(kq)[None], _cache_out(vq)[None], jnp.transpose(pq, (1, 0, 2))[None])
```

```python
import functools

import jax
import jax.numpy as jnp
from jax import lax
from jax.experimental import pallas as pl
from jax.experimental.pallas import tpu as pltpu

D_MODEL = 1024
HEAD_DIM = 64
N_HEADS = 8
N_KV_HEADS = 2
GROUP = 4
WINDOW = 128
BLOCK = 128
POOL_WINDOWS = (2, 4, 8, 16)
POOL_STATE = 15
D_ATTN = 512
D_POOL = 512
D_IN_PROJ = 2304
EPS = 1e-6
NEG_INF = -1e30
LOG2E = 1.4426950408889634

O_K = 512
O_V = 640
O_GA = 768
O_U = 1280
O_GP = 1792

LANES = 128
TB = 512
NSB = TB // BLOCK
BB = 16
CB = 4
QROWS = 16
WCHUNK = 256
VMEM_LIMIT = 58 * 1024 * 1024

F32 = jnp.float32
BF16 = jnp.bfloat16
_NT = (((1,), (1,)), ((), ()))


def _silu_half(h):
    return h + h * jnp.tanh(h)


def _lo_mask():
    return lax.broadcasted_iota(jnp.int32, (1, LANES), 1) < HEAD_DIM


def _pair_rms(zs, lo, w2):
    sq = zs * zs
    s_lo = jnp.sum(jnp.where(lo, sq, 0.0), axis=-1, keepdims=True)
    s_hi = jnp.sum(jnp.where(lo, 0.0, sq), axis=-1, keepdims=True)
    r = lax.rsqrt(jnp.where(lo, s_lo, s_hi) * (1.0 / HEAD_DIM) + EPS)
    return zs * r * w2


def _row_scale(x):
    return lax.rsqrt(jnp.mean(x * x, axis=-1, keepdims=True) + EPS)


def _project(x, win):
    return jnp.dot(x.astype(BF16), win, preferred_element_type=F32) * _row_scale(x)


def _weight_chunks():
    chunks = []
    for c0 in range(0, D_IN_PROJ, WCHUNK):
        gate = O_GA <= c0 < O_U or c0 >= O_GP
        chunks.append((0, c0, 0.5 if gate else 1.0))
    chunks += [(1, c0, 1.0) for c0 in range(0, D_MODEL, WCHUNK)]
    return chunks


def _load_weights(win_hbm, wout_hbm, wpool_f32, nw_ref, qn_ref, kn_ref, stage, sem, win_bf, wout_bf, wpool_bf,
                  qw_ref, kw_ref):
    chunks = _weight_chunks()
    srcs, dsts = (win_hbm, wout_hbm), (win_bf, wout_bf)
    nw_tile = jnp.concatenate([jnp.broadcast_to(nw_ref[:, t * LANES:(t + 1) * LANES], (LANES, LANES)).T
                               for t in range(D_MODEL // LANES)], axis=0)
    nw_rows = jnp.concatenate([nw_tile] * (WCHUNK // LANES), axis=1)

    def copy(k):
        which, c0, _ = chunks[k]
        return pltpu.make_async_copy(srcs[which].at[:, pl.ds(c0, WCHUNK)], stage.at[k % 2], sem.at[k % 2])

    copy(0).start()
    copy(1).start()
    for k, (which, c0, scale) in enumerate(chunks):
        copy(k).wait()
        w = stage[k % 2]
        if which == 0:
            w = w * nw_rows if scale == 1.0 else w * (scale * nw_rows)
        dsts[which][:, c0:c0 + WCHUNK] = w.astype(BF16)
        if k + 2 < len(chunks):
            copy(k + 2).start()
    wpool_bf[...] = jnp.zeros(wpool_bf.shape, BF16)
    for g in range(len(POOL_WINDOWS)):
        d0 = (g % 2) * LANES
        wpool_bf[g // 2, d0:d0 + LANES, d0:d0 + LANES] = wpool_f32[g].astype(BF16)
    qn = qn_ref[...] * (HEAD_DIM ** -0.5 * LOG2E)
    qw_ref[...] = jnp.concatenate([qn, qn], axis=1)
    kw_ref[...] = jnp.concatenate([kn_ref[...], kn_ref[...]], axis=1)


def _prompt_init(sinks_ref, kbuf, vbuf, uext, sink2, eye4, maskt):
    kbuf[:, 0:BLOCK, :] = jnp.zeros((2, BLOCK, LANES), BF16)
    vbuf[:, 0:BLOCK, :] = jnp.zeros((2, BLOCK, LANES), BF16)
    uext[0:16, :] = jnp.zeros((16, D_POOL), F32)
    lo = _lo_mask()
    for jj in range(N_HEADS // 2):
        sink2[jj] = jnp.where(lo, sinks_ref[2 * jj], sinks_ref[2 * jj + 1]) * LOG2E
    rr = lax.broadcasted_iota(jnp.int32, (GROUP * BLOCK, LANES), 0)
    ll = lax.broadcasted_iota(jnp.int32, (GROUP * BLOCK, LANES), 1)
    eye4[...] = jnp.where((rr & (BLOCK - 1)) == ll, 1.0, 0.0).astype(BF16)
    cc = lax.broadcasted_iota(jnp.int32, (2 * BLOCK, LANES), 0)
    qr = lax.broadcasted_iota(jnp.int32, (2 * BLOCK, LANES), 1)
    band = (cc >= qr) & (cc <= qr + WINDOW)
    maskt[0] = jnp.where(band & (cc >= BLOCK), 0.0, NEG_INF).astype(BF16)
    maskt[1] = jnp.where(band, 0.0, NEG_INF).astype(BF16)


def _prompt_step(i, last, x_ref, pscale_ref, y_ref, knew_ref, vnew_ref, unew_ref,
                 win_ref, wout_ref, wpool_ref, qw_ref, kw_ref, z_ref, hbuf, qbuf, kbuf, vbuf, uext, mixed, sink2, eye4, maskt):
    lo = _lo_mask()
    hbuf[...] = x_ref[...].astype(BF16)
    rs = _row_scale(x_ref[...])

    def proj(cols, rows=slice(None)):
        return jnp.dot(hbuf[rows, :], win_ref[:, cols], preferred_element_type=F32) * rs[rows]

    z_ref[:, O_U:] = proj(slice(O_U, D_IN_PROJ))
    uext[16:, :] = z_ref[:, O_U:O_U + D_POOL]
    pos16 = i * TB + lax.broadcasted_iota(jnp.int32, (16, LANES), 0)
    ds = []
    for g, w in enumerate(POOL_WINDOWS):
        acc = uext[:, g * LANES:(g + 1) * LANES]
        sh = 1
        while sh < w:
            acc = acc + pltpu.roll(acc, sh, axis=0)
            sh *= 2
        ug = z_ref[:, O_U + g * LANES:O_U + (g + 1) * LANES]
        cnt = jnp.minimum(w, pos16 + 1).astype(F32)
        d_first = acc[16:32, :] / cnt - ug[0:16, :]
        d_rest = acc[32:, :] * (1.0 / w) - ug[16:, :]
        ds.append(jnp.concatenate([d_first, d_rest], axis=0).astype(BF16))
    uext[0:16, :] = uext[TB:TB + 16, :]

    z_ref[:, :O_K] = proj(slice(0, O_K))

    for pr in range(len(POOL_WINDOWS) // 2):
        cols = slice(pr * 2 * LANES, (pr + 1) * 2 * LANES)
        d2 = jnp.concatenate(ds[2 * pr:2 * pr + 2], axis=1)
        po = jnp.dot(d2, wpool_ref[pr], preferred_element_type=F32) * pscale_ref[:, cols]
        gp = z_ref[:, O_GP + pr * 2 * LANES:O_GP + (pr + 1) * 2 * LANES]
        mixed[:, D_ATTN + pr * 2 * LANES:D_ATTN + (pr + 1) * 2 * LANES] = (po * _silu_half(gp)).astype(BF16)

    for j in range(4):
        qhat = _pair_rms(z_ref[:, j * LANES:(j + 1) * LANES], lo, qw_ref[...])
        q_lo = jnp.where(lo, qhat, 0.0).astype(BF16)
        q_hi = jnp.where(lo, 0.0, qhat).astype(BF16)
        c, g0 = j // 2, 2 * (j % 2)
        for sb in range(NSB):
            rows = slice(sb * BLOCK, (sb + 1) * BLOCK)
            qbuf[c, sb, g0 * BLOCK:(g0 + 1) * BLOCK, :] = q_lo[rows]
            qbuf[c, sb, (g0 + 1) * BLOCK:(g0 + 2) * BLOCK, :] = q_hi[rows]

    half = TB // 2
    for r0 in (0, half):
        z_ref[r0:r0 + half, O_K:O_GA] = proj(slice(O_K, O_GA), slice(r0, r0 + half))
    z_ref[:, O_GA:O_U] = proj(slice(O_GA, O_U))

    khat = _pair_rms(z_ref[:, O_K:O_K + LANES], lo, kw_ref[...])
    kr = pltpu.roll(khat, HEAD_DIM, axis=1)
    kbuf[0, BLOCK:, :] = jnp.where(lo, khat, kr).astype(BF16)
    kbuf[1, BLOCK:, :] = jnp.where(lo, kr, khat).astype(BF16)
    vz = z_ref[:, O_V:O_V + LANES]
    vr = pltpu.roll(vz, HEAD_DIM, axis=1)
    vbuf[0, BLOCK:, :] = jnp.where(lo, vz, vr).astype(BF16)
    vbuf[1, BLOCK:, :] = jnp.where(lo, vr, vz).astype(BF16)

    first = jnp.where(i > 0, 1, 0)
    for sb in range(NSB):
        rows = slice(sb * BLOCK, (sb + 1) * BLOCK)
        mcols = maskt[first] if sb == 0 else maskt[1]
        for c in range(2):
            keys = kbuf[c, sb * BLOCK:(sb + 2) * BLOCK, :]
            vals = vbuf[c, sb * BLOCK:(sb + 2) * BLOCK, :]
            kcat = jnp.concatenate([keys, mcols], axis=1)
            for jj in range(2):
                pr = slice(jj * 2 * BLOCK, (jj + 1) * 2 * BLOCK)
                s = lax.dot_general(jnp.concatenate([qbuf[c, sb, pr, :], eye4[0:2 * BLOCK, :]], axis=1), kcat, _NT,
                                    preferred_element_type=F32)
                ps, ms, ls = [], [], []
                for g in range(2):
                    sg = s[g * BLOCK:(g + 1) * BLOCK, :]
                    m = jnp.max(sg, axis=-1, keepdims=True)
                    p = jnp.exp2(sg - m)
                    ps.append(p.astype(BF16))
                    ms.append(m)
                    ls.append(jnp.sum(p, axis=-1, keepdims=True))
                o = jnp.dot(jnp.concatenate(ps, axis=0), vals, preferred_element_type=F32)
                slab = jnp.where(lo, o[0:BLOCK, :], o[BLOCK:, :])
                l = jnp.where(lo, ls[0], ls[1]) + jnp.exp2(sink2[2 * c + jj] - jnp.where(lo, ms[0], ms[1]))
                col = (2 * c + jj) * LANES
                ga = z_ref[rows, O_GA + col:O_GA + col + LANES]
                mixed[rows, col:col + LANES] = (slab * (1.0 / l) * _silu_half(ga)).astype(BF16)

    kbuf[:, 0:BLOCK, :] = kbuf[:, TB:TB + BLOCK, :]
    vbuf[:, 0:BLOCK, :] = vbuf[:, TB:TB + BLOCK, :]

    y_ref[...] = x_ref[...] + jnp.dot(mixed[...], wout_ref[...], preferred_element_type=F32)

    @pl.when(last)
    def _():
        unew_ref[:, 0, :] = z_ref[TB - POOL_STATE:TB, O_U:O_U + D_POOL]
        tail = slice(TB - WINDOW, TB)
        knew_ref[...] = _pair_rms(z_ref[tail, O_K:O_K + LANES], lo, kw_ref[...]).T
        vnew_ref[...] = z_ref[tail, O_V:O_V + LANES].T


def _sample_init(x_ref, win_ref, qw_ref, kw_ref, z_ref, q3, krep, vrep, snew, kvt):
    nb = x_ref.shape[0]
    lo = _lo_mask()
    z_ref[...] = _project(x_ref[:, 0, :], win_ref[...])
    khat = _pair_rms(z_ref[:, O_K:O_K + LANES], lo, kw_ref[...])
    vnew = z_ref[:, O_V:O_V + LANES]
    kvt[0] = khat.T
    kvt[1] = vnew.T
    q3[...] = jnp.zeros(q3.shape, F32)
    krep[...] = jnp.zeros(krep.shape, F32)
    vrep[...] = jnp.zeros(vrep.shape, F32)
    for j in range(4):
        qhat = _pair_rms(z_ref[:, j * LANES:(j + 1) * LANES], lo, qw_ref[...])
        qrot = pltpu.roll(qhat, HEAD_DIM, axis=1)
        grp_lo = j < 2
        for half in range(2):
            h = 2 * j + half
            src = qhat if (half == 0) == grp_lo else qrot
            q3[pl.ds(h, nb, stride=QROWS), :] = jnp.where(lo if grp_lo else ~lo, src, 0.0)
            krep[pl.ds(h, nb, stride=QROWS), :] = khat
            vrep[pl.ds(h, nb, stride=QROWS), :] = vnew
    s_new = jnp.sum(q3[...] * krep[...], axis=-1, keepdims=True)
    snew[...] = jnp.broadcast_to(s_new, snew.shape)


def _shift_caches(i, nb, ck_ref, cv_ref, knew_ref, vnew_ref, kvt):
    shift = lax.rem(nb - i * CB, nb)
    newest = lax.broadcasted_iota(jnp.int32, (LANES, WINDOW), 1) == WINDOW - 1
    for src, dst, which in ((ck_ref, knew_ref, 0), (cv_ref, vnew_ref, 1)):
        cols = pltpu.roll(kvt[which], shift, axis=1)
        shifted = pltpu.roll(src[...], WINDOW - 1, axis=2)
        for bb in range(CB):
            dst[bb] = jnp.where(newest, cols[:, bb:bb + 1], shifted[bb])


def _sample_step(j, last, sinks_ref, x_ref, pscale_ref, ck_ref, cv_ref, sp_ref, y_ref, spnew_ref,
                 wout_ref, wpool_ref, z_ref, q3, krep, vrep, snew, kvt, o3, mixed):
    nb = x_ref.shape[0]
    lo = _lo_mask()
    row0 = pl.multiple_of(j * BB, BB)
    rows = pl.ds(row0, BB)
    qrows = pl.ds(pl.multiple_of(j * (BB * QROWS), BB * QROWS), BB * QROWS)
    rid = lax.broadcasted_iota(jnp.int32, (1, QROWS, 1), 1)
    sink3 = jnp.zeros((1, QROWS, 1), F32)
    for h in range(N_HEADS):
        sink3 = jnp.where(rid == h, sinks_ref[h] * LOG2E, sink3)

    q = q3[qrows, :].reshape(BB, QROWS, LANES)
    s_new = snew[qrows, :].reshape(BB, QROWS, LANES)
    kt = ck_ref[...]
    vt = cv_ref[...]
    s = lax.dot_general(q.astype(BF16), kt.astype(BF16), (((2,), (1,)), ((0,), (0,))), preferred_element_type=F32)
    m = jnp.maximum(jnp.maximum(jnp.max(s, axis=-1, keepdims=True), s_new), sink3)
    p = jnp.exp2(s - m)
    p_new = jnp.exp2(s_new - m)
    l = jnp.sum(p, axis=-1, keepdims=True) + p_new + jnp.exp2(sink3 - m)
    o = lax.dot_general(p.astype(BF16), vt.astype(BF16), (((2,), (2,)), ((0,), (0,))), preferred_element_type=F32)
    o = (o + p_new * vrep[qrows, :].reshape(BB, QROWS, LANES)) / l
    o3[...] = o.reshape(BB * QROWS, LANES)

    for jj in range(4):
        grp_lo = jj < 2
        oh = [o3[pl.ds(2 * jj + half, BB, stride=QROWS), :] for half in range(2)]
        a = oh[0] if grp_lo else pltpu.roll(oh[0], HEAD_DIM, axis=1)
        b = pltpu.roll(oh[1], HEAD_DIM, axis=1) if grp_lo else oh[1]
        ga = z_ref[rows, O_GA + jj * LANES:O_GA + (jj + 1) * LANES]
        mixed[rows, jj * LANES:(jj + 1) * LANES] = (jnp.where(lo, a, b) * _silu_half(ga)).astype(BF16)

    u = z_ref[rows, O_U:O_U + D_POOL]
    spnew_ref[0:POOL_STATE - 1] = sp_ref[1:POOL_STATE]
    spnew_ref[POOL_STATE - 1] = u
    ds = []
    for g, w in enumerate(POOL_WINDOWS):
        cols = slice(g * LANES, (g + 1) * LANES)
        ug = u[:, cols]
        win_sum = ug
        for r in range(POOL_STATE - (w - 1), POOL_STATE):
            win_sum = win_sum + sp_ref[r, :, cols]
        ds.append((win_sum * (1.0 / w) - ug).astype(BF16))
    for pr in range(len(POOL_WINDOWS) // 2):
        cols = slice(pr * 2 * LANES, (pr + 1) * 2 * LANES)
        d2 = jnp.concatenate(ds[2 * pr:2 * pr + 2], axis=1)
        po = jnp.dot(d2, wpool_ref[pr], preferred_element_type=F32) * pscale_ref[:, cols]
        gp = z_ref[rows, O_GP + pr * 2 * LANES:O_GP + (pr + 1) * 2 * LANES]
        mixed[rows, D_ATTN + pr * 2 * LANES:D_ATTN + (pr + 1) * 2 * LANES] = (po * _silu_half(gp)).astype(BF16)

    @pl.when(last)
    def _():
        y_ref[:, 0, :] = x_ref[:, 0, :] + jnp.dot(mixed[...], wout_ref[...], preferred_element_type=F32)


def _fused_kernel(n_prompt, sinks_ref, xp_ref, xs_ref, nw_ref, win_hbm, qn_ref, kn_ref, wpool_f32, pscale_ref, wout_hbm,
                  ck_ref, cv_ref, sp_ref, ckc_ref, cvc_ref,
                  yp_ref, kp_ref, vp_ref, up_ref, ys_ref, kq_ref, vq_ref, pq_ref,
                  win_ref, wout_ref, wpool_ref, qw_ref, kw_ref, stage, sem,
                  z_ref, hbuf, qbuf, kbuf, vbuf, uext, mixed, sink2, eye4, maskt,
                  zs_ref, q3, krep, vrep, snew, kvt, o3, mixed_s):
    i = pl.program_id(0)

    @pl.when(i == 0)
    def _():
        _load_weights(win_hbm, wout_hbm, wpool_f32, nw_ref, qn_ref, kn_ref, stage, sem, win_ref, wout_ref, wpool_ref,
                      qw_ref, kw_ref)
        _prompt_init(sinks_ref, kbuf, vbuf, uext, sink2, eye4, maskt)
        _sample_init(xs_ref, win_ref, qw_ref, kw_ref, zs_ref, q3, krep, vrep, snew, kvt)

    @pl.when(i < n_prompt)
    def _():
        _shift_caches(i, xs_ref.shape[0], ckc_ref, cvc_ref, kq_ref, vq_ref, kvt)
        _prompt_step(i, i == n_prompt - 1, xp_ref, pscale_ref, yp_ref, kp_ref, vp_ref, up_ref,
                     win_ref, wout_ref, wpool_ref, qw_ref, kw_ref, z_ref, hbuf, qbuf, kbuf, vbuf, uext, mixed, sink2, eye4, maskt)

    @pl.when(i >= n_prompt)
    def _():
        _sample_step(i - n_prompt, i == pl.num_programs(0) - 1, sinks_ref, xs_ref, pscale_ref, ck_ref, cv_ref, sp_ref,
                     ys_ref, pq_ref, wout_ref, wpool_ref, zs_ref, q3, krep, vrep, snew, kvt, o3, mixed_s)


def _fused_call(sinks, xp, xs, nw, win, qn, kn, wpool, pscale, wout, ck, cv, sp):
    seq, nb = xp.shape[0], xs.shape[0]
    n_prompt, n_sample = seq // TB, nb // BB
    assert nb == CB * n_prompt
    const2 = lambda i: (0, 0)
    const3 = lambda i: (0, 0, 0)
    pblk = lambda i: (jnp.minimum(i, n_prompt - 1), 0)
    cblk = lambda i: (jnp.minimum(i, n_prompt - 1), 0, 0)
    sidx = lambda i: jnp.clip(i - n_prompt, 0, n_sample - 1)
    chunk3 = lambda i: (sidx(i), 0, 0)
    mid3 = lambda i: (0, sidx(i), 0)
    once = pl.Buffered(1)
    return pl.pallas_call(
        functools.partial(_fused_kernel, n_prompt),
        grid=(n_prompt + n_sample,),
        in_specs=[
            pl.BlockSpec(memory_space=pltpu.SMEM),
            pl.BlockSpec((TB, D_MODEL), pblk),
            pl.BlockSpec((nb, 1, D_MODEL), const3, pipeline_mode=once),
            pl.BlockSpec((1, D_MODEL), const2),
            pl.BlockSpec(memory_space=pl.ANY),
            pl.BlockSpec((1, HEAD_DIM), const2),
            pl.BlockSpec((1, HEAD_DIM), const2),
            pl.BlockSpec((4, LANES, LANES), const3, pipeline_mode=once),
            pl.BlockSpec((1, D_POOL), const2),
            pl.BlockSpec(memory_space=pl.ANY),
            pl.BlockSpec((BB, LANES, WINDOW), chunk3),
            pl.BlockSpec((BB, LANES, WINDOW), chunk3),
            pl.BlockSpec((POOL_STATE, BB, D_POOL), mid3),
            pl.BlockSpec((CB, LANES, WINDOW), cblk),
            pl.BlockSpec((CB, LANES, WINDOW), cblk),
        ],
        out_specs=[
            pl.BlockSpec((TB, D_MODEL), pblk),
            pl.BlockSpec((LANES, WINDOW), const2),
            pl.BlockSpec((LANES, WINDOW), const2),
            pl.BlockSpec((POOL_STATE, 1, D_POOL), const3),
            pl.BlockSpec((nb, 1, D_MODEL), const3, pipeline_mode=once),
            pl.BlockSpec((CB, LANES, WINDOW), cblk),
            pl.BlockSpec((CB, LANES, WINDOW), cblk),
            pl.BlockSpec((POOL_STATE, BB, D_POOL), mid3),
        ],
        out_shape=[
            jax.ShapeDtypeStruct((seq, D_MODEL), F32),
            jax.ShapeDtypeStruct((LANES, WINDOW), F32),
            jax.ShapeDtypeStruct((LANES, WINDOW), F32),
            jax.ShapeDtypeStruct((POOL_STATE, 1, D_POOL), F32),
            jax.ShapeDtypeStruct((nb, 1, D_MODEL), F32),
            jax.ShapeDtypeStruct((nb, LANES, WINDOW), F32),
            jax.ShapeDtypeStruct((nb, LANES, WINDOW), F32),
            jax.ShapeDtypeStruct((POOL_STATE, nb, D_POOL), F32),
        ],
        scratch_shapes=[
            pltpu.VMEM((D_MODEL, D_IN_PROJ), BF16),
            pltpu.VMEM((D_MODEL, D_MODEL), BF16),
            pltpu.VMEM((2, 2 * LANES, 2 * LANES), BF16),
            pltpu.VMEM((1, LANES), F32),
            pltpu.VMEM((1, LANES), F32),
            pltpu.VMEM((2, D_MODEL, WCHUNK), F32),
            pltpu.SemaphoreType.DMA((2,)),
            pltpu.VMEM((TB, D_IN_PROJ), F32),
            pltpu.VMEM((TB, D_MODEL), BF16),
            pltpu.VMEM((2, NSB, GROUP * BLOCK, LANES), BF16),
            pltpu.VMEM((2, TB + BLOCK, LANES), BF16),
            pltpu.VMEM((2, TB + BLOCK, LANES), BF16),
            pltpu.VMEM((TB + 16, D_POOL), F32),
            pltpu.VMEM((TB, D_MODEL), BF16),
            pltpu.VMEM((N_HEADS // 2, 1, LANES), F32),
            pltpu.VMEM((GROUP * BLOCK, LANES), BF16),
            pltpu.VMEM((2, 2 * BLOCK, LANES), BF16),
            pltpu.VMEM((nb, D_IN_PROJ), F32),
            pltpu.VMEM((nb * QROWS, LANES), F32),
            pltpu.VMEM((nb * QROWS, LANES), F32),
            pltpu.VMEM((nb * QROWS, LANES), F32),
            pltpu.VMEM((nb * QROWS, LANES), F32),
            pltpu.VMEM((2, LANES, nb), F32),
            pltpu.VMEM((BB * QROWS, LANES), F32),
            pltpu.VMEM((nb, D_MODEL), BF16),
        ],
        compiler_params=pltpu.CompilerParams(dimension_semantics=("arbitrary",), vmem_limit_bytes=VMEM_LIMIT),
        name="hybrid_step",
    )(sinks, xp, xs, nw, win, qn, kn, wpool, pscale, wout, ck, cv, sp, ck, cv)


def _cache_in(c):
    nb = c.shape[0]
    return jnp.transpose(c, (0, 2, 3, 1)).reshape(nb, LANES, WINDOW)


def _cache_out(c):
    nb = c.shape[0]
    return jnp.transpose(c.reshape(nb, N_KV_HEADS, HEAD_DIM, WINDOW), (0, 3, 1, 2))


def kernel(x_prompt, x_sample, cache_k, cache_v, state_pool, norm_w, w_in, q_norm_w, k_norm_w, sinks, w_pool,
           pool_scale, w_out):
    depth = norm_w.shape[0]
    assert depth == 1 and x_prompt.shape[0] == 1 and x_sample.shape[1] == 1
    seq = x_prompt.shape[1]
    nb = x_sample.shape[0]
    assert seq % TB == 0 and nb % BB == 0 and nb == LANES

    yp, kp, vp, up, ys, kq, vq, pq = _fused_call(
        sinks[0], x_prompt[0], x_sample, norm_w, w_in[0], q_norm_w, k_norm_w, w_pool[0], pool_scale, w_out[0],
        _cache_in(cache_k[0]), _cache_in(cache_v[0]), jnp.transpose(state_pool[0], (1, 0, 2)))
    return (yp[None], ys, _cache_out(kp[None])[None], _cache_out(vp[None])[None],
            jnp.transpose(up, (1, 0, 2))[None],
            _cache_out(kq)[None], _cache_out(vq)[None], jnp.transpose(pq, (1, 0, 2))[None])
```

```python
import functools

import jax
import jax.numpy as jnp
from jax import lax
from jax.experimental import pallas as pl
from jax.experimental.pallas import tpu as pltpu

D_MODEL = 1024
HEAD_DIM = 64
N_HEADS = 8
N_KV_HEADS = 2
GROUP = 4
WINDOW = 128
BLOCK = 128
POOL_WINDOWS = (2, 4, 8, 16)
POOL_STATE = 15
D_ATTN = 512
D_POOL = 512
D_IN_PROJ = 2304
EPS = 1e-6
NEG_INF = -1e30
LOG2E = 1.4426950408889634

O_K = 512
O_V = 640
O_GA = 768
O_U = 1280
O_GP = 1792

LANES = 128
TB = 512
NSB = TB // BLOCK
BB = 32
CB = 4
QROWS = 16
WCHUNK = 128
VMEM_LIMIT = 58 * 1024 * 1024

F32 = jnp.float32
BF16 = jnp.bfloat16
_NT = (((1,), (1,)), ((), ()))


def _silu_half(h):
    return h + h * jnp.tanh(h)


def _lo_mask():
    return lax.broadcasted_iota(jnp.int32, (1, LANES), 1) < HEAD_DIM


def _pair_rms(zs, lo, w2):
    sq = zs * zs
    s_lo = jnp.sum(jnp.where(lo, sq, 0.0), axis=-1, keepdims=True)
    s_hi = jnp.sum(jnp.where(lo, 0.0, sq), axis=-1, keepdims=True)
    r = lax.rsqrt(jnp.where(lo, s_lo, s_hi) * (1.0 / HEAD_DIM) + EPS)
    return zs * r * w2


def _row_scale(x):
    return lax.rsqrt(jnp.mean(x * x, axis=-1, keepdims=True) + EPS)


def _project(x, win):
    return jnp.dot(x.astype(BF16), win, preferred_element_type=F32) * _row_scale(x)


def _weight_chunks():
    chunks = []
    for c0 in range(0, D_IN_PROJ, WCHUNK):
        gate = O_GA <= c0 < O_U or c0 >= O_GP
        chunks.append((0, c0, 0.5 if gate else 1.0))
    chunks += [(1, c0, 1.0) for c0 in range(0, D_MODEL, WCHUNK)]
    return chunks


def _load_weights(win_hbm, wout_hbm, wpool_f32, nw_ref, qn_ref, kn_ref, stage, sem, win_bf, wout_bf, wpool_bf,
                  qw_ref, kw_ref):
    chunks = _weight_chunks()
    srcs, dsts = (win_hbm, wout_hbm), (win_bf, wout_bf)
    nw_tile = jnp.concatenate([jnp.broadcast_to(nw_ref[:, t * LANES:(t + 1) * LANES], (LANES, LANES)).T
                               for t in range(D_MODEL // LANES)], axis=0)
    nw_rows = jnp.concatenate([nw_tile] * (WCHUNK // LANES), axis=1)

    def copy(k):
        which, c0, _ = chunks[k]
        return pltpu.make_async_copy(srcs[which].at[:, pl.ds(c0, WCHUNK)], stage.at[k % 2], sem.at[k % 2])

    copy(0).start()
    copy(1).start()
    for k, (which, c0, scale) in enumerate(chunks):
        copy(k).wait()
        w = stage[k % 2]
        if which == 0:
            w = w * nw_rows if scale == 1.0 else w * (scale * nw_rows)
        dsts[which][:, c0:c0 + WCHUNK] = w.astype(BF16)
        if k + 2 < len(chunks):
            copy(k + 2).start()
    wpool_bf[...] = jnp.zeros(wpool_bf.shape, BF16)
    for g in range(len(POOL_WINDOWS)):
        d0 = (g % 2) * LANES
        wpool_bf[g // 2, d0:d0 + LANES, d0:d0 + LANES] = wpool_f32[g].astype(BF16)
    qn = qn_ref[...] * (HEAD_DIM ** -0.5 * LOG2E)
    qw_ref[...] = jnp.concatenate([qn, qn], axis=1)
    kw_ref[...] = jnp.concatenate([kn_ref[...], kn_ref[...]], axis=1)


def _prompt_init(sinks_ref, kbuf, vbuf, uext, sink2, eye4, maskt):
    kbuf[:, 0:BLOCK, :] = jnp.zeros((2, BLOCK, LANES), BF16)
    vbuf[:, 0:BLOCK, :] = jnp.zeros((2, BLOCK, LANES), BF16)
    uext[0:16, :] = jnp.zeros((16, D_POOL), F32)
    lo = _lo_mask()
    for jj in range(N_HEADS // 2):
        sink2[jj] = jnp.where(lo, sinks_ref[2 * jj], sinks_ref[2 * jj + 1]) * LOG2E
    rr = lax.broadcasted_iota(jnp.int32, (GROUP * BLOCK, LANES), 0)
    ll = lax.broadcasted_iota(jnp.int32, (GROUP * BLOCK, LANES), 1)
    eye4[...] = jnp.where((rr & (BLOCK - 1)) == ll, 1.0, 0.0).astype(BF16)
    cc = lax.broadcasted_iota(jnp.int32, (2 * BLOCK, LANES), 0)
    qr = lax.broadcasted_iota(jnp.int32, (2 * BLOCK, LANES), 1)
    band = (cc >= qr) & (cc <= qr + WINDOW)
    maskt[0] = jnp.where(band & (cc >= BLOCK), 0.0, NEG_INF).astype(BF16)
    maskt[1] = jnp.where(band, 0.0, NEG_INF).astype(BF16)


def _prompt_step(i, last, x_ref, pscale_ref, y_ref, knew_ref, vnew_ref, unew_ref,
                 win_ref, wout_ref, wpool_ref, qw_ref, kw_ref, z_ref, hbuf, qbuf, kbuf, vbuf, uext, mixed, sink2, eye4, maskt):
    lo = _lo_mask()
    hbuf[...] = x_ref[...].astype(BF16)
    rs = _row_scale(x_ref[...])

    def proj(cols, rows=slice(None)):
        return jnp.dot(hbuf[rows, :], win_ref[:, cols], preferred_element_type=F32) * rs[rows]

    z_ref[:, O_U:] = proj(slice(O_U, D_IN_PROJ))
    uext[16:, :] = z_ref[:, O_U:O_U + D_POOL]
    pos16 = i * TB + lax.broadcasted_iota(jnp.int32, (16, LANES), 0)
    ds = []
    for g, w in enumerate(POOL_WINDOWS):
        acc = uext[:, g * LANES:(g + 1) * LANES]
        sh = 1
        while sh < w:
            acc = acc + pltpu.roll(acc, sh, axis=0)
            sh *= 2
        ug = z_ref[:, O_U + g * LANES:O_U + (g + 1) * LANES]
        cnt = jnp.minimum(w, pos16 + 1).astype(F32)
        d_first = acc[16:32, :] / cnt - ug[0:16, :]
        d_rest = acc[32:, :] * (1.0 / w) - ug[16:, :]
        ds.append(jnp.concatenate([d_first, d_rest], axis=0).astype(BF16))
    uext[0:16, :] = uext[TB:TB + 16, :]

    z_ref[:, :O_K] = proj(slice(0, O_K))

    for pr in range(len(POOL_WINDOWS) // 2):
        cols = slice(pr * 2 * LANES, (pr + 1) * 2 * LANES)
        d2 = jnp.concatenate(ds[2 * pr:2 * pr + 2], axis=1)
        po = jnp.dot(d2, wpool_ref[pr], preferred_element_type=F32) * pscale_ref[:, cols]
        gp = z_ref[:, O_GP + pr * 2 * LANES:O_GP + (pr + 1) * 2 * LANES]
        mixed[:, D_ATTN + pr * 2 * LANES:D_ATTN + (pr + 1) * 2 * LANES] = (po * _silu_half(gp)).astype(BF16)

    for j in range(4):
        qhat = _pair_rms(z_ref[:, j * LANES:(j + 1) * LANES], lo, qw_ref[...])
        q_lo = jnp.where(lo, qhat, 0.0).astype(BF16)
        q_hi = jnp.where(lo, 0.0, qhat).astype(BF16)
        c, g0 = j // 2, 2 * (j % 2)
        for sb in range(NSB):
            rows = slice(sb * BLOCK, (sb + 1) * BLOCK)
            qbuf[c, sb, g0 * BLOCK:(g0 + 1) * BLOCK, :] = q_lo[rows]
            qbuf[c, sb, (g0 + 1) * BLOCK:(g0 + 2) * BLOCK, :] = q_hi[rows]

    half = TB // 2
    for r0 in (0, half):
        z_ref[r0:r0 + half, O_K:O_GA] = proj(slice(O_K, O_GA), slice(r0, r0 + half))
    z_ref[:, O_GA:O_U] = proj(slice(O_GA, O_U))

    khat = _pair_rms(z_ref[:, O_K:O_K + LANES], lo, kw_ref[...])
    kr = pltpu.roll(khat, HEAD_DIM, axis=1)
    kbuf[0, BLOCK:, :] = jnp.where(lo, khat, kr).astype(BF16)
    kbuf[1, BLOCK:, :] = jnp.where(lo, kr, khat).astype(BF16)
    vz = z_ref[:, O_V:O_V + LANES]
    vr = pltpu.roll(vz, HEAD_DIM, axis=1)
    vbuf[0, BLOCK:, :] = jnp.where(lo, vz, vr).astype(BF16)
    vbuf[1, BLOCK:, :] = jnp.where(lo, vr, vz).astype(BF16)

    first = jnp.where(i > 0, 1, 0)
    for sb in range(NSB):
        rows = slice(sb * BLOCK, (sb + 1) * BLOCK)
        mcols = maskt[first] if sb == 0 else maskt[1]
        for c in range(2):
            keys = kbuf[c, sb * BLOCK:(sb + 2) * BLOCK, :]
            vals = vbuf[c, sb * BLOCK:(sb + 2) * BLOCK, :]
            s = lax.dot_general(jnp.concatenate([qbuf[c, sb], eye4[...]], axis=1),
                                jnp.concatenate([keys, mcols], axis=1), _NT,
                                preferred_element_type=F32)
            ps, ms, ls = [], [], []
            for g in range(GROUP):
                sg = s[g * BLOCK:(g + 1) * BLOCK, :]
                m = jnp.max(sg, axis=-1, keepdims=True)
                p = jnp.exp2(sg - m)
                ps.append(p.astype(BF16))
                ms.append(m)
                ls.append(jnp.sum(p, axis=-1, keepdims=True))
            o = jnp.dot(jnp.concatenate(ps, axis=0), vals, preferred_element_type=F32)
            for jj in range(2):
                ev, od = 2 * jj, 2 * jj + 1
                slab = jnp.where(lo, o[ev * BLOCK:(ev + 1) * BLOCK, :], o[od * BLOCK:(od + 1) * BLOCK, :])
                l = jnp.where(lo, ls[ev], ls[od]) + jnp.exp2(sink2[2 * c + jj] - jnp.where(lo, ms[ev], ms[od]))
                col = (2 * c + jj) * LANES
                ga = z_ref[rows, O_GA + col:O_GA + col + LANES]
                mixed[rows, col:col + LANES] = (slab * (1.0 / l) * _silu_half(ga)).astype(BF16)

    kbuf[:, 0:BLOCK, :] = kbuf[:, TB:TB + BLOCK, :]
    vbuf[:, 0:BLOCK, :] = vbuf[:, TB:TB + BLOCK, :]

    y_ref[...] = x_ref[...] + jnp.dot(mixed[...], wout_ref[...], preferred_element_type=F32)

    @pl.when(last)
    def _():
        unew_ref[:, 0, :] = z_ref[TB - POOL_STATE:TB, O_U:O_U + D_POOL]
        tail = slice(TB - WINDOW, TB)
        knew_ref[...] = _pair_rms(z_ref[tail, O_K:O_K + LANES], lo, kw_ref[...]).T
        vnew_ref[...] = z_ref[tail, O_V:O_V + LANES].T


def _sample_init(x_ref, win_ref, qw_ref, kw_ref, z_ref, q3, krep, vrep, snew, kvt):
    nb = x_ref.shape[0]
    lo = _lo_mask()
    z_ref[...] = _project(x_ref[:, 0, :], win_ref[...])
    khat = _pair_rms(z_ref[:, O_K:O_K + LANES], lo, kw_ref[...])
    vnew = z_ref[:, O_V:O_V + LANES]
    kvt[0] = khat.T
    kvt[1] = vnew.T
    q3[...] = jnp.zeros(q3.shape, F32)
    krep[...] = jnp.zeros(krep.shape, F32)
    vrep[...] = jnp.zeros(vrep.shape, F32)
    for j in range(4):
        qhat = _pair_rms(z_ref[:, j * LANES:(j + 1) * LANES], lo, qw_ref[...])
        qrot = pltpu.roll(qhat, HEAD_DIM, axis=1)
        grp_lo = j < 2
        for half in range(2):
            h = 2 * j + half
            src = qhat if (half == 0) == grp_lo else qrot
            q3[pl.ds(h, nb, stride=QROWS), :] = jnp.where(lo if grp_lo else ~lo, src, 0.0)
            krep[pl.ds(h, nb, stride=QROWS), :] = khat
            vrep[pl.ds(h, nb, stride=QROWS), :] = vnew
    s_new = jnp.sum(q3[...] * krep[...], axis=-1, keepdims=True)
    snew[...] = jnp.broadcast_to(s_new, snew.shape)


def _shift_caches(i, nb, ck_ref, cv_ref, knew_ref, vnew_ref, kvt):
    shift = lax.rem(nb - i * CB, nb)
    newest = lax.broadcasted_iota(jnp.int32, (LANES, WINDOW), 1) == WINDOW - 1
    for src, dst, which in ((ck_ref, knew_ref, 0), (cv_ref, vnew_ref, 1)):
        cols = pltpu.roll(kvt[which], shift, axis=1)
        shifted = pltpu.roll(src[...], WINDOW - 1, axis=2)
        for bb in range(CB):
            dst[bb] = jnp.where(newest, cols[:, bb:bb + 1], shifted[bb])


def _sample_step(j, last, sinks_ref, x_ref, pscale_ref, ck_ref, cv_ref, sp_ref, y_ref, spnew_ref,
                 wout_ref, wpool_ref, z_ref, q3, krep, vrep, snew, kvt, o3, mixed):
    nb = x_ref.shape[0]
    lo = _lo_mask()
    row0 = pl.multiple_of(j * BB, BB)
    rows = pl.ds(row0, BB)
    qrows = pl.ds(pl.multiple_of(j * (BB * QROWS), BB * QROWS), BB * QROWS)
    rid = lax.broadcasted_iota(jnp.int32, (1, QROWS, 1), 1)
    sink3 = jnp.zeros((1, QROWS, 1), F32)
    for h in range(N_HEADS):
        sink3 = jnp.where(rid == h, sinks_ref[h] * LOG2E, sink3)

    q = q3[qrows, :].reshape(BB, QROWS, LANES)
    s_new = snew[qrows, :].reshape(BB, QROWS, LANES)
    kt = ck_ref[...]
    vt = cv_ref[...]
    s = lax.dot_general(q.astype(BF16), kt.astype(BF16), (((2,), (1,)), ((0,), (0,))), preferred_element_type=F32)
    m = jnp.maximum(jnp.maximum(jnp.max(s, axis=-1, keepdims=True), s_new), sink3)
    p = jnp.exp2(s - m)
    p_new = jnp.exp2(s_new - m)
    l = jnp.sum(p, axis=-1, keepdims=True) + p_new + jnp.exp2(sink3 - m)
    o = lax.dot_general(p.astype(BF16), vt.astype(BF16), (((2,), (2,)), ((0,), (0,))), preferred_element_type=F32)
    o = (o + p_new * vrep[qrows, :].reshape(BB, QROWS, LANES)) / l
    o3[...] = o.reshape(BB * QROWS, LANES)

    for jj in range(4):
        grp_lo = jj < 2
        oh = [o3[pl.ds(2 * jj + half, BB, stride=QROWS), :] for half in range(2)]
        a = oh[0] if grp_lo else pltpu.roll(oh[0], HEAD_DIM, axis=1)
        b = pltpu.roll(oh[1], HEAD_DIM, axis=1) if grp_lo else oh[1]
        ga = z_ref[rows, O_GA + jj * LANES:O_GA + (jj + 1) * LANES]
        mixed[rows, jj * LANES:(jj + 1) * LANES] = (jnp.where(lo, a, b) * _silu_half(ga)).astype(BF16)

    u = z_ref[rows, O_U:O_U + D_POOL]
    spnew_ref[0:POOL_STATE - 1] = sp_ref[1:POOL_STATE]
    spnew_ref[POOL_STATE - 1] = u
    ds = []
    for g, w in enumerate(POOL_WINDOWS):
        cols = slice(g * LANES, (g + 1) * LANES)
        ug = u[:, cols]
        win_sum = ug
        for r in range(POOL_STATE - (w - 1), POOL_STATE):
            win_sum = win_sum + sp_ref[r, :, cols]
        ds.append((win_sum * (1.0 / w) - ug).astype(BF16))
    for pr in range(len(POOL_WINDOWS) // 2):
        cols = slice(pr * 2 * LANES, (pr + 1) * 2 * LANES)
        d2 = jnp.concatenate(ds[2 * pr:2 * pr + 2], axis=1)
        po = jnp.dot(d2, wpool_ref[pr], preferred_element_type=F32) * pscale_ref[:, cols]
        gp = z_ref[rows, O_GP + pr * 2 * LANES:O_GP + (pr + 1) * 2 * LANES]
        mixed[rows, D_ATTN + pr * 2 * LANES:D_ATTN + (pr + 1) * 2 * LANES] = (po * _silu_half(gp)).astype(BF16)

    @pl.when(last)
    def _():
        y_ref[:, 0, :] = x_ref[:, 0, :] + jnp.dot(mixed[...], wout_ref[...], preferred_element_type=F32)


def _fused_kernel(n_prompt, sinks_ref, xp_ref, xs_ref, nw_ref, win_hbm, qn_ref, kn_ref, wpool_f32, pscale_ref, wout_hbm,
                  ck_ref, cv_ref, sp_ref, ckc_ref, cvc_ref,
                  yp_ref, kp_ref, vp_ref, up_ref, ys_ref, kq_ref, vq_ref, pq_ref,
                  win_ref, wout_ref, wpool_ref, qw_ref, kw_ref, stage, sem,
                  z_ref, hbuf, qbuf, kbuf, vbuf, uext, mixed, sink2, eye4, maskt,
                  zs_ref, q3, krep, vrep, snew, kvt, o3, mixed_s):
    i = pl.program_id(0)

    @pl.when(i == 0)
    def _():
        _load_weights(win_hbm, wout_hbm, wpool_f32, nw_ref, qn_ref, kn_ref, stage, sem, win_ref, wout_ref, wpool_ref,
                      qw_ref, kw_ref)
        _prompt_init(sinks_ref, kbuf, vbuf, uext, sink2, eye4, maskt)
        _sample_init(xs_ref, win_ref, qw_ref, kw_ref, zs_ref, q3, krep, vrep, snew, kvt)

    @pl.when(i < n_prompt)
    def _():
        _shift_caches(i, xs_ref.shape[0], ckc_ref, cvc_ref, kq_ref, vq_ref, kvt)
        _prompt_step(i, i == n_prompt - 1, xp_ref, pscale_ref, yp_ref, kp_ref, vp_ref, up_ref,
                     win_ref, wout_ref, wpool_ref, qw_ref, kw_ref, z_ref, hbuf, qbuf, kbuf, vbuf, uext, mixed, sink2, eye4, maskt)

    @pl.when(i >= n_prompt)
    def _():
        _sample_step(i - n_prompt, i == pl.num_programs(0) - 1, sinks_ref, xs_ref, pscale_ref, ck_ref, cv_ref, sp_ref,
                     ys_ref, pq_ref, wout_ref, wpool_ref, zs_ref, q3, krep, vrep, snew, kvt, o3, mixed_s)


def _fused_call(sinks, xp, xs, nw, win, qn, kn, wpool, pscale, wout, ck, cv, sp):
    seq, nb = xp.shape[0], xs.shape[0]
    n_prompt, n_sample = seq // TB, nb // BB
    assert nb == CB * n_prompt
    const2 = lambda i: (0, 0)
    const3 = lambda i: (0, 0, 0)
    pblk = lambda i: (jnp.minimum(i, n_prompt - 1), 0)
    cblk = lambda i: (jnp.minimum(i, n_prompt - 1), 0, 0)
    sidx = lambda i: jnp.clip(i - n_prompt, 0, n_sample - 1)
    chunk3 = lambda i: (sidx(i), 0, 0)
    mid3 = lambda i: (0, sidx(i), 0)
    once = pl.Buffered(1)
    return pl.pallas_call(
        functools.partial(_fused_kernel, n_prompt),
        grid=(n_prompt + n_sample,),
        in_specs=[
            pl.BlockSpec(memory_space=pltpu.SMEM),
            pl.BlockSpec((TB, D_MODEL), pblk),
            pl.BlockSpec((nb, 1, D_MODEL), const3, pipeline_mode=once),
            pl.BlockSpec((1, D_MODEL), const2),
            pl.BlockSpec(memory_space=pl.ANY),
            pl.BlockSpec((1, HEAD_DIM), const2),
            pl.BlockSpec((1, HEAD_DIM), const2),
            pl.BlockSpec((4, LANES, LANES), const3, pipeline_mode=once),
            pl.BlockSpec((1, D_POOL), const2),
            pl.BlockSpec(memory_space=pl.ANY),
            pl.BlockSpec((BB, LANES, WINDOW), chunk3),
            pl.BlockSpec((BB, LANES, WINDOW), chunk3),
            pl.BlockSpec((POOL_STATE, BB, D_POOL), mid3),
            pl.BlockSpec((CB, LANES, WINDOW), cblk),
            pl.BlockSpec((CB, LANES, WINDOW), cblk),
        ],
        out_specs=[
            pl.BlockSpec((TB, D_MODEL), pblk),
            pl.BlockSpec((LANES, WINDOW), const2),
            pl.BlockSpec((LANES, WINDOW), const2),
            pl.BlockSpec((POOL_STATE, 1, D_POOL), const3),
            pl.BlockSpec((nb, 1, D_MODEL), const3, pipeline_mode=once),
            pl.BlockSpec((CB, LANES, WINDOW), cblk),
            pl.BlockSpec((CB, LANES, WINDOW), cblk),
            pl.BlockSpec((POOL_STATE, BB, D_POOL), mid3),
        ],
        out_shape=[
            jax.ShapeDtypeStruct((seq, D_MODEL), F32),
            jax.ShapeDtypeStruct((LANES, WINDOW), F32),
            jax.ShapeDtypeStruct((LANES, WINDOW), F32),
            jax.ShapeDtypeStruct((POOL_STATE, 1, D_POOL), F32),
            jax.ShapeDtypeStruct((nb, 1, D_MODEL), F32),
            jax.ShapeDtypeStruct((nb, LANES, WINDOW), F32),
            jax.ShapeDtypeStruct((nb, LANES, WINDOW), F32),
            jax.ShapeDtypeStruct((POOL_STATE, nb, D_POOL), F32),
        ],
        scratch_shapes=[
            pltpu.VMEM((D_MODEL, D_IN_PROJ), BF16),
            pltpu.VMEM((D_MODEL, D_MODEL), BF16),
            pltpu.VMEM((2, 2 * LANES, 2 * LANES), BF16),
            pltpu.VMEM((1, LANES), F32),
            pltpu.VMEM((1, LANES), F32),
            pltpu.VMEM((2, D_MODEL, WCHUNK), F32),
            pltpu.SemaphoreType.DMA((2,)),
            pltpu.VMEM((TB, D_IN_PROJ), F32),
            pltpu.VMEM((TB, D_MODEL), BF16),
            pltpu.VMEM((2, NSB, GROUP * BLOCK, LANES), BF16),
            pltpu.VMEM((2, TB + BLOCK, LANES), BF16),
            pltpu.VMEM((2, TB + BLOCK, LANES), BF16),
            pltpu.VMEM((TB + 16, D_POOL), F32),
            pltpu.VMEM((TB, D_MODEL), BF16),
            pltpu.VMEM((N_HEADS // 2, 1, LANES), F32),
            pltpu.VMEM((GROUP * BLOCK, LANES), BF16),
            pltpu.VMEM((2, 2 * BLOCK, LANES), BF16),
            pltpu.VMEM((nb, D_IN_PROJ), F32),
            pltpu.VMEM((nb * QROWS, LANES), F32),
            pltpu.VMEM((nb * QROWS, LANES), F32),
            pltpu.VMEM((nb * QROWS, LANES), F32),
            pltpu.VMEM((nb * QROWS, LANES), F32),
            pltpu.VMEM((2, LANES, nb), F32),
            pltpu.VMEM((BB * QROWS, LANES), F32),
            pltpu.VMEM((nb, D_MODEL), BF16),
        ],
        compiler_params=pltpu.CompilerParams(dimension_semantics=("arbitrary",), vmem_limit_bytes=VMEM_LIMIT),
        name="hybrid_step",
    )(sinks, xp, xs, nw, win, qn, kn, wpool, pscale, wout, ck, cv, sp, ck, cv)


def _cache_in(c):
    nb = c.shape[0]
    return jnp.transpose(c, (0, 2, 3, 1)).reshape(nb, LANES, WINDOW)


def _cache_out(c):
    nb = c.shape[0]
    return jnp.transpose(c.reshape(nb, N_KV_HEADS, HEAD_DIM, WINDOW), (0, 3, 1, 2))


def kernel(x_prompt, x_sample, cache_k, cache_v, state_pool, norm_w, w_in, q_norm_w, k_norm_w, sinks, w_pool,
           pool_scale, w_out):
    depth = norm_w.shape[0]
    assert depth == 1 and x_prompt.shape[0] == 1 and x_sample.shape[1] == 1
    seq = x_prompt.shape[1]
    nb = x_sample.shape[0]
    assert seq % TB == 0 and nb % BB == 0 and nb == LANES

    yp, kp, vp, up, ys, kq, vq, pq = _fused_call(
        sinks[0], x_prompt[0], x_sample, norm_w, w_in[0], q_norm_w, k_norm_w, w_pool[0], pool_scale, w_out[0],
        _cache_in(cache_k[0]), _cache_in(cache_v[0]), jnp.transpose(state_pool[0], (1, 0, 2)))
    return (yp[None], ys, _cache_out(kp[None])[None], _cache_out(vp[None])[None],
            jnp.transpose(up, (1, 0, 2))[None],
            _cache_out(kq)[None], _cache_out(vq)[None], jnp.transpose(pq, (1, 0, 2))[None])
```

```python
import functools

import jax
import jax.numpy as jnp
from jax import lax
from jax.experimental import pallas as pl
from jax.experimental.pallas import tpu as pltpu

D_MODEL = 1024
HEAD_DIM = 64
N_HEADS = 8
N_KV_HEADS = 2
GROUP = 4
WINDOW = 128
BLOCK = 128
POOL_WINDOWS = (2, 4, 8, 16)
POOL_STATE = 15
D_ATTN = 512
D_POOL = 512
D_IN_PROJ = 2304
EPS = 1e-6
NEG_INF = -1e30
LOG2E = 1.4426950408889634

O_K = 512
O_V = 640
O_GA = 768
O_U = 1280
O_GP = 1792

LANES = 128
TB = 512
NSB = TB // BLOCK
BB = 32
CB = 4
QROWS = 16
WCHUNK = 256
VMEM_LIMIT = 58 * 1024 * 1024

F32 = jnp.float32
BF16 = jnp.bfloat16
_NT = (((1,), (1,)), ((), ()))


def _silu_half(h):
    return h + h * jnp.tanh(h)


def _lo_mask():
    return lax.broadcasted_iota(jnp.int32, (1, LANES), 1) < HEAD_DIM


def _pair_rms(zs, lo, w2):
    sq = zs * zs
    s_lo = jnp.sum(jnp.where(lo, sq, 0.0), axis=-1, keepdims=True)
    s_hi = jnp.sum(jnp.where(lo, 0.0, sq), axis=-1, keepdims=True)
    r = lax.rsqrt(jnp.where(lo, s_lo, s_hi) * (1.0 / HEAD_DIM) + EPS)
    return zs * r * w2


def _row_scale(x):
    return lax.rsqrt(jnp.mean(x * x, axis=-1, keepdims=True) + EPS)


def _project(x, win):
    return jnp.dot(x.astype(BF16), win, preferred_element_type=F32) * _row_scale(x)


def _weight_chunks():
    chunks = []
    for c0 in range(0, D_IN_PROJ, WCHUNK):
        gate = O_GA <= c0 < O_U or c0 >= O_GP
        chunks.append((0, c0, 0.5 if gate else 1.0))
    chunks += [(1, c0, 1.0) for c0 in range(0, D_MODEL, WCHUNK)]
    return chunks


def _load_weights(win_hbm, wout_hbm, wpool_f32, nw_ref, qn_ref, kn_ref, stage, sem, win_bf, wout_bf, wpool_bf,
                  qw_ref, kw_ref):
    chunks = _weight_chunks()
    srcs, dsts = (win_hbm, wout_hbm), (win_bf, wout_bf)
    nw_tile = jnp.concatenate([jnp.broadcast_to(nw_ref[:, t * LANES:(t + 1) * LANES], (LANES, LANES)).T
                               for t in range(D_MODEL // LANES)], axis=0)
    nw_rows = jnp.concatenate([nw_tile] * (WCHUNK // LANES), axis=1)

    def copy(k):
        which, c0, _ = chunks[k]
        return pltpu.make_async_copy(srcs[which].at[:, pl.ds(c0, WCHUNK)], stage.at[k % 2], sem.at[k % 2])

    copy(0).start()
    copy(1).start()
    for k, (which, c0, scale) in enumerate(chunks):
        copy(k).wait()
        w = stage[k % 2]
        if which == 0:
            w = w * nw_rows if scale == 1.0 else w * (scale * nw_rows)
        dsts[which][:, c0:c0 + WCHUNK] = w.astype(BF16)
        if k + 2 < len(chunks):
            copy(k + 2).start()
    wpool_bf[...] = jnp.zeros(wpool_bf.shape, BF16)
    for g in range(len(POOL_WINDOWS)):
        d0 = (g % 2) * LANES
        wpool_bf[g // 2, d0:d0 + LANES, d0:d0 + LANES] = wpool_f32[g].astype(BF16)
    qn = qn_ref[...] * (HEAD_DIM ** -0.5 * LOG2E)
    qw_ref[...] = jnp.concatenate([qn, qn], axis=1)
    kw_ref[...] = jnp.concatenate([kn_ref[...], kn_ref[...]], axis=1)


def _prompt_init(sinks_ref, kbuf, vbuf, uext, sink2, eye4, maskt):
    kbuf[:, 0:BLOCK, :] = jnp.zeros((2, BLOCK, LANES), BF16)
    vbuf[:, 0:BLOCK, :] = jnp.zeros((2, BLOCK, LANES), BF16)
    uext[0:16, :] = jnp.zeros((16, D_POOL), F32)
    lo = _lo_mask()
    for jj in range(N_HEADS // 2):
        sink2[jj] = jnp.where(lo, sinks_ref[2 * jj], sinks_ref[2 * jj + 1]) * LOG2E
    rr = lax.broadcasted_iota(jnp.int32, (GROUP * BLOCK, LANES), 0)
    ll = lax.broadcasted_iota(jnp.int32, (GROUP * BLOCK, LANES), 1)
    eye4[...] = jnp.where((rr & (BLOCK - 1)) == ll, 1.0, 0.0).astype(BF16)
    cc = lax.broadcasted_iota(jnp.int32, (2 * BLOCK, LANES), 0)
    qr = lax.broadcasted_iota(jnp.int32, (2 * BLOCK, LANES), 1)
    band = (cc >= qr) & (cc <= qr + WINDOW)
    maskt[0] = jnp.where(band & (cc >= BLOCK), 0.0, NEG_INF).astype(BF16)
    maskt[1] = jnp.where(band, 0.0, NEG_INF).astype(BF16)


def _prompt_step(i, last, x_ref, pscale_ref, y_ref, knew_ref, vnew_ref, unew_ref,
                 win_ref, wout_ref, wpool_ref, qw_ref, kw_ref, z_ref, hbuf, qbuf, kbuf, vbuf, uext, mixed, sink2, eye4, maskt):
    lo = _lo_mask()
    hbuf[...] = x_ref[...].astype(BF16)
    rs = _row_scale(x_ref[...])

    def proj(cols, rows=slice(None)):
        return jnp.dot(hbuf[rows, :], win_ref[:, cols], preferred_element_type=F32) * rs[rows]

    z_ref[:, O_U:] = proj(slice(O_U, D_IN_PROJ))
    uext[16:, :] = z_ref[:, O_U:O_U + D_POOL]
    pos16 = i * TB + lax.broadcasted_iota(jnp.int32, (16, LANES), 0)
    ds = []
    for g, w in enumerate(POOL_WINDOWS):
        acc = uext[:, g * LANES:(g + 1) * LANES]
        sh = 1
        while sh < w:
            acc = acc + pltpu.roll(acc, sh, axis=0)
            sh *= 2
        ug = z_ref[:, O_U + g * LANES:O_U + (g + 1) * LANES]
        cnt = jnp.minimum(w, pos16 + 1).astype(F32)
        d_first = acc[16:32, :] / cnt - ug[0:16, :]
        d_rest = acc[32:, :] * (1.0 / w) - ug[16:, :]
        ds.append(jnp.concatenate([d_first, d_rest], axis=0).astype(BF16))
    uext[0:16, :] = uext[TB:TB + 16, :]

    z_ref[:, :O_K] = proj(slice(0, O_K))

    for pr in range(len(POOL_WINDOWS) // 2):
        cols = slice(pr * 2 * LANES, (pr + 1) * 2 * LANES)
        d2 = jnp.concatenate(ds[2 * pr:2 * pr + 2], axis=1)
        po = jnp.dot(d2, wpool_ref[pr], preferred_element_type=F32) * pscale_ref[:, cols]
        gp = z_ref[:, O_GP + pr * 2 * LANES:O_GP + (pr + 1) * 2 * LANES]
        mixed[:, D_ATTN + pr * 2 * LANES:D_ATTN + (pr + 1) * 2 * LANES] = (po * _silu_half(gp)).astype(BF16)

    for j in range(4):
        qhat = _pair_rms(z_ref[:, j * LANES:(j + 1) * LANES], lo, qw_ref[...])
        q_lo = jnp.where(lo, qhat, 0.0).astype(BF16)
        q_hi = jnp.where(lo, 0.0, qhat).astype(BF16)
        c, g0 = j // 2, 2 * (j % 2)
        for sb in range(NSB):
            rows = slice(sb * BLOCK, (sb + 1) * BLOCK)
            qbuf[c, sb, g0 * BLOCK:(g0 + 1) * BLOCK, :] = q_lo[rows]
            qbuf[c, sb, (g0 + 1) * BLOCK:(g0 + 2) * BLOCK, :] = q_hi[rows]

    half = TB // 2
    for r0 in (0, half):
        z_ref[r0:r0 + half, O_K:O_GA] = proj(slice(O_K, O_GA), slice(r0, r0 + half))
    z_ref[:, O_GA:O_U] = proj(slice(O_GA, O_U))

    khat = _pair_rms(z_ref[:, O_K:O_K + LANES], lo, kw_ref[...])
    kr = pltpu.roll(khat, HEAD_DIM, axis=1)
    kbuf[0, BLOCK:, :] = jnp.where(lo, khat, kr).astype(BF16)
    kbuf[1, BLOCK:, :] = jnp.where(lo, kr, khat).astype(BF16)
    vz = z_ref[:, O_V:O_V + LANES]
    vr = pltpu.roll(vz, HEAD_DIM, axis=1)
    vbuf[0, BLOCK:, :] = jnp.where(lo, vz, vr).astype(BF16)
    vbuf[1, BLOCK:, :] = jnp.where(lo, vr, vz).astype(BF16)

    first = jnp.where(i > 0, 1, 0)
    for sb in range(NSB):
        rows = slice(sb * BLOCK, (sb + 1) * BLOCK)
        mcols = maskt[first] if sb == 0 else maskt[1]
        for c in range(2):
            keys = kbuf[c, sb * BLOCK:(sb + 2) * BLOCK, :]
            vals = vbuf[c, sb * BLOCK:(sb + 2) * BLOCK, :]
            s = lax.dot_general(jnp.concatenate([qbuf[c, sb], eye4[...]], axis=1),
                                jnp.concatenate([keys, mcols], axis=1), _NT,
                                preferred_element_type=F32)
            ps, ms, ls = [], [], []
            for g in range(GROUP):
                sg = s[g * BLOCK:(g + 1) * BLOCK, :]
                m = jnp.max(sg, axis=-1, keepdims=True)
                p = jnp.exp2(sg - m)
                ps.append(p.astype(BF16))
                ms.append(m)
                ls.append(jnp.sum(p, axis=-1, keepdims=True))
            o = jnp.dot(jnp.concatenate(ps, axis=0), vals, preferred_element_type=F32)
            for jj in range(2):
                ev, od = 2 * jj, 2 * jj + 1
                slab = jnp.where(lo, o[ev * BLOCK:(ev + 1) * BLOCK, :], o[od * BLOCK:(od + 1) * BLOCK, :])
                l = jnp.where(lo, ls[ev], ls[od]) + jnp.exp2(sink2[2 * c + jj] - jnp.where(lo, ms[ev], ms[od]))
                col = (2 * c + jj) * LANES
                ga = z_ref[rows, O_GA + col:O_GA + col + LANES]
                mixed[rows, col:col + LANES] = (slab * (1.0 / l) * _silu_half(ga)).astype(BF16)

    kbuf[:, 0:BLOCK, :] = kbuf[:, TB:TB + BLOCK, :]
    vbuf[:, 0:BLOCK, :] = vbuf[:, TB:TB + BLOCK, :]

    y_ref[...] = x_ref[...] + jnp.dot(mixed[...], wout_ref[...], preferred_element_type=F32)

    @pl.when(last)
    def _():
        unew_ref[:, 0, :] = z_ref[TB - POOL_STATE:TB, O_U:O_U + D_POOL]
        tail = slice(TB - WINDOW, TB)
        knew_ref[...] = _pair_rms(z_ref[tail, O_K:O_K + LANES], lo, kw_ref[...]).T
        vnew_ref[...] = z_ref[tail, O_V:O_V + LANES].T


def _sample_init(x_ref, win_ref, qw_ref, kw_ref, z_ref, q3, krep, vrep, snew, kvt):
    nb = x_ref.shape[0]
    lo = _lo_mask()
    z_ref[...] = _project(x_ref[:, 0, :], win_ref[...])
    khat = _pair_rms(z_ref[:, O_K:O_K + LANES], lo, kw_ref[...])
    vnew = z_ref[:, O_V:O_V + LANES]
    kvt[0] = khat.T
    kvt[1] = vnew.T
    q3[...] = jnp.zeros(q3.shape, F32)
    krep[...] = jnp.zeros(krep.shape, F32)
    vrep[...] = jnp.zeros(vrep.shape, F32)
    for j in range(4):
        qhat = _pair_rms(z_ref[:, j * LANES:(j + 1) * LANES], lo, qw_ref[...])
        qrot = pltpu.roll(qhat, HEAD_DIM, axis=1)
        grp_lo = j < 2
        for half in range(2):
            h = 2 * j + half
            src = qhat if (half == 0) == grp_lo else qrot
            q3[pl.ds(h, nb, stride=QROWS), :] = jnp.where(lo if grp_lo else ~lo, src, 0.0)
            krep[pl.ds(h, nb, stride=QROWS), :] = khat
            vrep[pl.ds(h, nb, stride=QROWS), :] = vnew
    s_new = jnp.sum(q3[...] * krep[...], axis=-1, keepdims=True)
    snew[...] = jnp.broadcast_to(s_new, snew.shape)


def _shift_caches(i, nb, ck_ref, cv_ref, knew_ref, vnew_ref, kvt):
    shift = lax.rem(nb - i * CB, nb)
    newest = lax.broadcasted_iota(jnp.int32, (LANES, WINDOW), 1) == WINDOW - 1
    for src, dst, which in ((ck_ref, knew_ref, 0), (cv_ref, vnew_ref, 1)):
        cols = pltpu.roll(kvt[which], shift, axis=1)
        shifted = pltpu.roll(src[...], WINDOW - 1, axis=2)
        for bb in range(CB):
            dst[bb] = jnp.where(newest, cols[:, bb:bb + 1], shifted[bb])


def _sample_step(j, last, sinks_ref, x_ref, pscale_ref, ck_ref, cv_ref, sp_ref, y_ref, spnew_ref,
                 wout_ref, wpool_ref, z_ref, q3, krep, vrep, snew, kvt, o3, mixed):
    nb = x_ref.shape[0]
    lo = _lo_mask()
    row0 = pl.multiple_of(j * BB, BB)
    rows = pl.ds(row0, BB)
    qrows = pl.ds(pl.multiple_of(j * (BB * QROWS), BB * QROWS), BB * QROWS)
    rid = lax.broadcasted_iota(jnp.int32, (1, QROWS, 1), 1)
    sink3 = jnp.zeros((1, QROWS, 1), F32)
    for h in range(N_HEADS):
        sink3 = jnp.where(rid == h, sinks_ref[h] * LOG2E, sink3)

    q = q3[qrows, :].reshape(BB, QROWS, LANES)
    s_new = snew[qrows, :].reshape(BB, QROWS, LANES)
    kt = ck_ref[...]
    vt = cv_ref[...]
    s = lax.dot_general(q.astype(BF16), kt.astype(BF16), (((2,), (1,)), ((0,), (0,))), preferred_element_type=F32)
    m = jnp.maximum(jnp.maximum(jnp.max(s, axis=-1, keepdims=True), s_new), sink3)
    p = jnp.exp2(s - m)
    p_new = jnp.exp2(s_new - m)
    l = jnp.sum(p, axis=-1, keepdims=True) + p_new + jnp.exp2(sink3 - m)
    o = lax.dot_general(p.astype(BF16), vt.astype(BF16), (((2,), (2,)), ((0,), (0,))), preferred_element_type=F32)
    o = (o + p_new * vrep[qrows, :].reshape(BB, QROWS, LANES)) / l
    o3[...] = o.reshape(BB * QROWS, LANES)

    for jj in range(4):
        grp_lo = jj < 2
        oh = [o3[pl.ds(2 * jj + half, BB, stride=QROWS), :] for half in range(2)]
        a = oh[0] if grp_lo else pltpu.roll(oh[0], HEAD_DIM, axis=1)
        b = pltpu.roll(oh[1], HEAD_DIM, axis=1) if grp_lo else oh[1]
        ga = z_ref[rows, O_GA + jj * LANES:O_GA + (jj + 1) * LANES]
        mixed[rows, jj * LANES:(jj + 1) * LANES] = (jnp.where(lo, a, b) * _silu_half(ga)).astype(BF16)

    u = z_ref[rows, O_U:O_U + D_POOL]
    spnew_ref[0:POOL_STATE - 1] = sp_ref[1:POOL_STATE]
    spnew_ref[POOL_STATE - 1] = u
    ds = []
    for g, w in enumerate(POOL_WINDOWS):
        cols = slice(g * LANES, (g + 1) * LANES)
        ug = u[:, cols]
        win_sum = ug
        for r in range(POOL_STATE - (w - 1), POOL_STATE):
            win_sum = win_sum + sp_ref[r, :, cols]
        ds.append((win_sum * (1.0 / w) - ug).astype(BF16))
    for pr in range(len(POOL_WINDOWS) // 2):
        cols = slice(pr * 2 * LANES, (pr + 1) * 2 * LANES)
        d2 = jnp.concatenate(ds[2 * pr:2 * pr + 2], axis=1)
        po = jnp.dot(d2, wpool_ref[pr], preferred_element_type=F32) * pscale_ref[:, cols]
        gp = z_ref[rows, O_GP + pr * 2 * LANES:O_GP + (pr + 1) * 2 * LANES]
        mixed[rows, D_ATTN + pr * 2 * LANES:D_ATTN + (pr + 1) * 2 * LANES] = (po * _silu_half(gp)).astype(BF16)

    @pl.when(last)
    def _():
        y_ref[:, 0, :] = x_ref[:, 0, :] + jnp.dot(mixed[...], wout_ref[...], preferred_element_type=F32)


def _fused_kernel(n_prompt, sinks_ref, xp_ref, xs_ref, nw_ref, win_hbm, qn_ref, kn_ref, wpool_f32, pscale_ref, wout_hbm,
                  ck_ref, cv_ref, sp_ref, ckc_ref, cvc_ref,
                  yp_ref, kp_ref, vp_ref, up_ref, ys_ref, kq_ref, vq_ref, pq_ref,
                  win_ref, wout_ref, wpool_ref, qw_ref, kw_ref, stage, sem,
                  z_ref, hbuf, qbuf, kbuf, vbuf, uext, mixed, sink2, eye4, maskt,
                  zs_ref, q3, krep, vrep, snew, kvt, o3, mixed_s):
    i = pl.program_id(0)

    @pl.when(i == 0)
    def _():
        _load_weights(win_hbm, wout_hbm, wpool_f32, nw_ref, qn_ref, kn_ref, stage, sem, win_ref, wout_ref, wpool_ref,
                      qw_ref, kw_ref)
        _prompt_init(sinks_ref, kbuf, vbuf, uext, sink2, eye4, maskt)
        _sample_init(xs_ref, win_ref, qw_ref, kw_ref, zs_ref, q3, krep, vrep, snew, kvt)

    @pl.when(i < n_prompt)
    def _():
        _shift_caches(i, xs_ref.shape[0], ckc_ref, cvc_ref, kq_ref, vq_ref, kvt)
        _prompt_step(i, i == n_prompt - 1, xp_ref, pscale_ref, yp_ref, kp_ref, vp_ref, up_ref,
                     win_ref, wout_ref, wpool_ref, qw_ref, kw_ref, z_ref, hbuf, qbuf, kbuf, vbuf, uext, mixed, sink2, eye4, maskt)

    @pl.when(i >= n_prompt)
    def _():
        _sample_step(i - n_prompt, i == pl.num_programs(0) - 1, sinks_ref, xs_ref, pscale_ref, ck_ref, cv_ref, sp_ref,
                     ys_ref, pq_ref, wout_ref, wpool_ref, zs_ref, q3, krep, vrep, snew, kvt, o3, mixed_s)


def _fused_call(sinks, xp, xs, nw, win, qn, kn, wpool, pscale, wout, ck, cv, sp):
    seq, nb = xp.shape[0], xs.shape[0]
    n_prompt, n_sample = seq // TB, nb // BB
    assert nb == CB * n_prompt
    const2 = lambda i: (0, 0)
    const3 = lambda i: (0, 0, 0)
    pblk = lambda i: (jnp.minimum(i, n_prompt - 1), 0)
    cblk = lambda i: (jnp.minimum(i, n_prompt - 1), 0, 0)
    sidx = lambda i: jnp.clip(i - n_prompt, 0, n_sample - 1)
    chunk3 = lambda i: (sidx(i), 0, 0)
    mid3 = lambda i: (0, sidx(i), 0)
    once = pl.Buffered(1)
    return pl.pallas_call(
        functools.partial(_fused_kernel, n_prompt),
        grid=(n_prompt + n_sample,),
        in_specs=[
            pl.BlockSpec(memory_space=pltpu.SMEM),
            pl.BlockSpec((TB, D_MODEL), pblk),
            pl.BlockSpec((nb, 1, D_MODEL), const3, pipeline_mode=once),
            pl.BlockSpec((1, D_MODEL), const2),
            pl.BlockSpec(memory_space=pl.ANY),
            pl.BlockSpec((1, HEAD_DIM), const2),
            pl.BlockSpec((1, HEAD_DIM), const2),
            pl.BlockSpec((4, LANES, LANES), const3, pipeline_mode=once),
            pl.BlockSpec((1, D_POOL), const2),
            pl.BlockSpec(memory_space=pl.ANY),
            pl.BlockSpec((BB, LANES, WINDOW), chunk3),
            pl.BlockSpec((BB, LANES, WINDOW), chunk3),
            pl.BlockSpec((POOL_STATE, BB, D_POOL), mid3),
            pl.BlockSpec((CB, LANES, WINDOW), cblk),
            pl.BlockSpec((CB, LANES, WINDOW), cblk),
        ],
        out_specs=[
            pl.BlockSpec((TB, D_MODEL), pblk),
            pl.BlockSpec((LANES, WINDOW), const2),
            pl.BlockSpec((LANES, WINDOW), const2),
            pl.BlockSpec((POOL_STATE, 1, D_POOL), const3),
            pl.BlockSpec((nb, 1, D_MODEL), const3, pipeline_mode=once),
            pl.BlockSpec((CB, LANES, WINDOW), cblk),
            pl.BlockSpec((CB, LANES, WINDOW), cblk),
            pl.BlockSpec((POOL_STATE, BB, D_POOL), mid3),
        ],
        out_shape=[
            jax.ShapeDtypeStruct((seq, D_MODEL), F32),
            jax.ShapeDtypeStruct((LANES, WINDOW), F32),
            jax.ShapeDtypeStruct((LANES, WINDOW), F32),
            jax.ShapeDtypeStruct((POOL_STATE, 1, D_POOL), F32),
            jax.ShapeDtypeStruct((nb, 1, D_MODEL), F32),
            jax.ShapeDtypeStruct((nb, LANES, WINDOW), F32),
            jax.ShapeDtypeStruct((nb, LANES, WINDOW), F32),
            jax.ShapeDtypeStruct((POOL_STATE, nb, D_POOL), F32),
        ],
        scratch_shapes=[
            pltpu.VMEM((D_MODEL, D_IN_PROJ), BF16),
            pltpu.VMEM((D_MODEL, D_MODEL), BF16),
            pltpu.VMEM((2, 2 * LANES, 2 * LANES), BF16),
            pltpu.VMEM((1, LANES), F32),
            pltpu.VMEM((1, LANES), F32),
            pltpu.VMEM((2, D_MODEL, WCHUNK), F32),
            pltpu.SemaphoreType.DMA((2,)),
            pltpu.VMEM((TB, D_IN_PROJ), F32),
            pltpu.VMEM((TB, D_MODEL), BF16),
            pltpu.VMEM((2, NSB, GROUP * BLOCK, LANES), BF16),
            pltpu.VMEM((2, TB + BLOCK, LANES), BF16),
            pltpu.VMEM((2, TB + BLOCK, LANES), BF16),
            pltpu.VMEM((TB + 16, D_POOL), F32),
            pltpu.VMEM((TB, D_MODEL), BF16),
            pltpu.VMEM((N_HEADS // 2, 1, LANES), F32),
            pltpu.VMEM((GROUP * BLOCK, LANES), BF16),
            pltpu.VMEM((2, 2 * BLOCK, LANES), BF16),
            pltpu.VMEM((nb, D_IN_PROJ), F32),
            pltpu.VMEM((nb * QROWS, LANES), F32),
            pltpu.VMEM((nb * QROWS, LANES), F32),
            pltpu.VMEM((nb * QROWS, LANES), F32),
            pltpu.VMEM((nb * QROWS, LANES), F32),
            pltpu.VMEM((2, LANES, nb), F32),
            pltpu.VMEM((BB * QROWS, LANES), F32),
            pltpu.VMEM((nb, D_MODEL), BF16),
        ],
        compiler_params=pltpu.CompilerParams(dimension_semantics=("arbitrary",), vmem_limit_bytes=VMEM_LIMIT),
        name="hybrid_step",
    )(sinks, xp, xs, nw, win, qn, kn, wpool, pscale, wout, ck, cv, sp, ck, cv)


def _cache_in(c):
    nb = c.shape[0]
    return jnp.transpose(c, (0, 2, 3, 1)).reshape(nb, LANES, WINDOW)


def _cache_out(c):
    nb = c.shape[0]
    return jnp.transpose(c.reshape(nb, N_KV_HEADS, HEAD_DIM, WINDOW), (0, 3, 1, 2))


def kernel(x_prompt, x_sample, cache_k, cache_v, state_pool, norm_w, w_in, q_norm_w, k_norm_w, sinks, w_pool,
           pool_scale, w_out):
    depth = norm_w.shape[0]
    assert depth == 1 and x_prompt.shape[0] == 1 and x_sample.shape[1] == 1
    seq = x_prompt.shape[1]
    nb = x_sample.shape[0]
    assert seq % TB == 0 and nb % BB == 0 and nb == LANES

    yp, kp, vp, up, ys, kq, vq, pq = _fused_call(
        sinks[0], x_prompt[0], x_sample, norm_w, w_in[0], q_norm_w, k_norm_w, w_pool[0], pool_scale, w_out[0],
        _cache_in(cache_k[0]), _cache_in(cache_v[0]), jnp.transpose(state_pool[0], (1, 0, 2)))
    return (yp[None], ys, _cache_out(kp[None])[None], _cache_out(vp[None])[None],
            jnp.transpose(up, (1, 0, 2))[None],
            _cache_out(kq)[None], _cache_out(vq)[None], jnp.transpose(pq, (1, 0, 2))[None])
```

```python
import functools

import jax
import jax.numpy as jnp
from jax import lax
from jax.experimental import pallas as pl
from jax.experimental.pallas import tpu as pltpu

D_MODEL = 1024
HEAD_DIM = 64
N_HEADS = 8
N_KV_HEADS = 2
GROUP = 4
WINDOW = 128
BLOCK = 128
POOL_WINDOWS = (2, 4, 8, 16)
POOL_STATE = 15
D_ATTN = 512
D_POOL = 512
D_IN_PROJ = 2304
EPS = 1e-6
NEG_INF = -1e30
LOG2E = 1.4426950408889634

O_K = 512
O_V = 640
O_GA = 768
O_U = 1280
O_GP = 1792

LANES = 128
TB = 512
NSB = TB // BLOCK
BB = 32
CB = 4
QROWS = 16
WROWS = 128
VMEM_LIMIT = 58 * 1024 * 1024

F32 = jnp.float32
BF16 = jnp.bfloat16
_NT = (((1,), (1,)), ((), ()))


def _silu_half(h):
    return h + h * jnp.tanh(h)


def _lo_mask():
    return lax.broadcasted_iota(jnp.int32, (1, LANES), 1) < HEAD_DIM


def _pair_rms(zs, lo, w2):
    sq = zs * zs
    s_lo = jnp.sum(jnp.where(lo, sq, 0.0), axis=-1, keepdims=True)
    s_hi = jnp.sum(jnp.where(lo, 0.0, sq), axis=-1, keepdims=True)
    r = lax.rsqrt(jnp.where(lo, s_lo, s_hi) * (1.0 / HEAD_DIM) + EPS)
    return zs * r * w2


def _row_scale(x):
    return lax.rsqrt(jnp.mean(x * x, axis=-1, keepdims=True) + EPS)


def _project(x, win):
    return jnp.dot(x.astype(BF16), win, preferred_element_type=F32) * _row_scale(x)


def _weight_chunks():
    return [(which, r0) for which in (0, 1) for r0 in range(0, D_MODEL, WROWS)]


def _load_weights(win_hbm, wout_hbm, wpool_f32, nw_ref, qn_ref, kn_ref, stage, sem, win_bf, wout_bf, wpool_bf,
                  qw_ref, kw_ref):
    chunks = _weight_chunks()
    srcs, dsts, widths = (win_hbm, wout_hbm), (win_bf, wout_bf), (D_IN_PROJ, D_MODEL)
    nw_tile = jnp.concatenate([jnp.broadcast_to(nw_ref[:, t * LANES:(t + 1) * LANES], (LANES, LANES)).T
                               for t in range(D_MODEL // LANES)], axis=0)
    col = lax.broadcasted_iota(jnp.int32, (1, D_IN_PROJ), 1)
    gate_scale = jnp.where(((col >= O_GA) & (col < O_U)) | (col >= O_GP), 0.5, 1.0)

    def copy(k):
        which, r0 = chunks[k]
        return pltpu.make_async_copy(srcs[which].at[pl.ds(r0, WROWS), :],
                                     stage.at[k % 2, :, pl.ds(0, widths[which])], sem.at[k % 2])

    copy(0).start()
    copy(1).start()
    for k, (which, r0) in enumerate(chunks):
        copy(k).wait()
        w = stage[k % 2, :, 0:widths[which]]
        if which == 0:
            w = w * nw_tile[r0:r0 + WROWS, 0:1] * gate_scale
        dsts[which][r0:r0 + WROWS, :] = w.astype(BF16)
        if k + 2 < len(chunks):
            copy(k + 2).start()
    wpool_bf[...] = jnp.zeros(wpool_bf.shape, BF16)
    for g in range(len(POOL_WINDOWS)):
        d0 = (g % 2) * LANES
        wpool_bf[g // 2, d0:d0 + LANES, d0:d0 + LANES] = wpool_f32[g].astype(BF16)
    qn = qn_ref[...] * (HEAD_DIM ** -0.5 * LOG2E)
    qw_ref[...] = jnp.concatenate([qn, qn], axis=1)
    kw_ref[...] = jnp.concatenate([kn_ref[...], kn_ref[...]], axis=1)


def _prompt_init(sinks_ref, kbuf, vbuf, uext, sink2, eye4, maskt):
    kbuf[:, 0:BLOCK, :] = jnp.zeros((2, BLOCK, LANES), BF16)
    vbuf[:, 0:BLOCK, :] = jnp.zeros((2, BLOCK, LANES), BF16)
    uext[0:16, :] = jnp.zeros((16, D_POOL), F32)
    lo = _lo_mask()
    for jj in range(N_HEADS // 2):
        sink2[jj] = jnp.where(lo, sinks_ref[2 * jj], sinks_ref[2 * jj + 1]) * LOG2E
    rr = lax.broadcasted_iota(jnp.int32, (GROUP * BLOCK, LANES), 0)
    ll = lax.broadcasted_iota(jnp.int32, (GROUP * BLOCK, LANES), 1)
    eye4[...] = jnp.where((rr & (BLOCK - 1)) == ll, 1.0, 0.0).astype(BF16)
    cc = lax.broadcasted_iota(jnp.int32, (2 * BLOCK, LANES), 0)
    qr = lax.broadcasted_iota(jnp.int32, (2 * BLOCK, LANES), 1)
    band = (cc >= qr) & (cc <= qr + WINDOW)
    maskt[0] = jnp.where(band & (cc >= BLOCK), 0.0, NEG_INF).astype(BF16)
    maskt[1] = jnp.where(band, 0.0, NEG_INF).astype(BF16)


def _prompt_step(i, last, x_ref, pscale_ref, y_ref, knew_ref, vnew_ref, unew_ref,
                 win_ref, wout_ref, wpool_ref, qw_ref, kw_ref, z_ref, hbuf, qbuf, kbuf, vbuf, uext, mixed, sink2, eye4, maskt):
    lo = _lo_mask()
    hbuf[...] = x_ref[...].astype(BF16)
    rs = _row_scale(x_ref[...])

    def proj(cols, rows=slice(None)):
        return jnp.dot(hbuf[rows, :], win_ref[:, cols], preferred_element_type=F32) * rs[rows]

    z_ref[:, O_U:] = proj(slice(O_U, D_IN_PROJ))
    uext[16:, :] = z_ref[:, O_U:O_U + D_POOL]
    pos16 = i * TB + lax.broadcasted_iota(jnp.int32, (16, LANES), 0)
    ds = []
    for g, w in enumerate(POOL_WINDOWS):
        acc = uext[:, g * LANES:(g + 1) * LANES]
        sh = 1
        while sh < w:
            acc = acc + pltpu.roll(acc, sh, axis=0)
            sh *= 2
        ug = z_ref[:, O_U + g * LANES:O_U + (g + 1) * LANES]
        cnt = jnp.minimum(w, pos16 + 1).astype(F32)
        d_first = acc[16:32, :] / cnt - ug[0:16, :]
        d_rest = acc[32:, :] * (1.0 / w) - ug[16:, :]
        ds.append(jnp.concatenate([d_first, d_rest], axis=0).astype(BF16))
    uext[0:16, :] = uext[TB:TB + 16, :]

    z_ref[:, :O_K] = proj(slice(0, O_K))

    for pr in range(len(POOL_WINDOWS) // 2):
        cols = slice(pr * 2 * LANES, (pr + 1) * 2 * LANES)
        d2 = jnp.concatenate(ds[2 * pr:2 * pr + 2], axis=1)
        po = jnp.dot(d2, wpool_ref[pr], preferred_element_type=F32) * pscale_ref[:, cols]
        gp = z_ref[:, O_GP + pr * 2 * LANES:O_GP + (pr + 1) * 2 * LANES]
        mixed[:, D_ATTN + pr * 2 * LANES:D_ATTN + (pr + 1) * 2 * LANES] = (po * _silu_half(gp)).astype(BF16)

    for j in range(4):
        qhat = _pair_rms(z_ref[:, j * LANES:(j + 1) * LANES], lo, qw_ref[...])
        q_lo = jnp.where(lo, qhat, 0.0).astype(BF16)
        q_hi = jnp.where(lo, 0.0, qhat).astype(BF16)
        c, g0 = j // 2, 2 * (j % 2)
        for sb in range(NSB):
            rows = slice(sb * BLOCK, (sb + 1) * BLOCK)
            qbuf[c, sb, g0 * BLOCK:(g0 + 1) * BLOCK, :] = q_lo[rows]
            qbuf[c, sb, (g0 + 1) * BLOCK:(g0 + 2) * BLOCK, :] = q_hi[rows]

    half = TB // 2
    for r0 in (0, half):
        z_ref[r0:r0 + half, O_K:O_GA] = proj(slice(O_K, O_GA), slice(r0, r0 + half))
    z_ref[:, O_GA:O_U] = proj(slice(O_GA, O_U))

    khat = _pair_rms(z_ref[:, O_K:O_K + LANES], lo, kw_ref[...])
    kr = pltpu.roll(khat, HEAD_DIM, axis=1)
    kbuf[0, BLOCK:, :] = jnp.where(lo, khat, kr).astype(BF16)
    kbuf[1, BLOCK:, :] = jnp.where(lo, kr, khat).astype(BF16)
    vz = z_ref[:, O_V:O_V + LANES]
    vr = pltpu.roll(vz, HEAD_DIM, axis=1)
    vbuf[0, BLOCK:, :] = jnp.where(lo, vz, vr).astype(BF16)
    vbuf[1, BLOCK:, :] = jnp.where(lo, vr, vz).astype(BF16)

    first = jnp.where(i > 0, 1, 0)
    for sb in range(NSB):
        rows = slice(sb * BLOCK, (sb + 1) * BLOCK)
        mcols = maskt[first] if sb == 0 else maskt[1]
        for c in range(2):
            keys = kbuf[c, sb * BLOCK:(sb + 2) * BLOCK, :]
            vals = vbuf[c, sb * BLOCK:(sb + 2) * BLOCK, :]
            s = lax.dot_general(jnp.concatenate([qbuf[c, sb], eye4[...]], axis=1),
                                jnp.concatenate([keys, mcols], axis=1), _NT,
                                preferred_element_type=F32)
            ps, ms, ls = [], [], []
            for g in range(GROUP):
                sg = s[g * BLOCK:(g + 1) * BLOCK, :]
                m = jnp.max(sg, axis=-1, keepdims=True)
                p = jnp.exp2(sg - m)
                ps.append(p.astype(BF16))
                ms.append(m)
                ls.append(jnp.sum(p, axis=-1, keepdims=True))
            o = jnp.dot(jnp.concatenate(ps, axis=0), vals, preferred_element_type=F32)
            for jj in range(2):
                ev, od = 2 * jj, 2 * jj + 1
                slab = jnp.where(lo, o[ev * BLOCK:(ev + 1) * BLOCK, :], o[od * BLOCK:(od + 1) * BLOCK, :])
                l = jnp.where(lo, ls[ev], ls[od]) + jnp.exp2(sink2[2 * c + jj] - jnp.where(lo, ms[ev], ms[od]))
                col = (2 * c + jj) * LANES
                ga = z_ref[rows, O_GA + col:O_GA + col + LANES]
                mixed[rows, col:col + LANES] = (slab * (1.0 / l) * _silu_half(ga)).astype(BF16)

    kbuf[:, 0:BLOCK, :] = kbuf[:, TB:TB + BLOCK, :]
    vbuf[:, 0:BLOCK, :] = vbuf[:, TB:TB + BLOCK, :]

    y_ref[...] = x_ref[...] + jnp.dot(mixed[...], wout_ref[...], preferred_element_type=F32)

    @pl.when(last)
    def _():
        unew_ref[:, 0, :] = z_ref[TB - POOL_STATE:TB, O_U:O_U + D_POOL]
        tail = slice(TB - WINDOW, TB)
        knew_ref[...] = _pair_rms(z_ref[tail, O_K:O_K + LANES], lo, kw_ref[...]).T
        vnew_ref[...] = z_ref[tail, O_V:O_V + LANES].T


def _sample_init(x_ref, win_ref, qw_ref, kw_ref, z_ref, q3, krep, vrep, snew, kvt):
    nb = x_ref.shape[0]
    lo = _lo_mask()
    z_ref[...] = _project(x_ref[:, 0, :], win_ref[...])
    khat = _pair_rms(z_ref[:, O_K:O_K + LANES], lo, kw_ref[...])
    vnew = z_ref[:, O_V:O_V + LANES]
    kvt[0] = khat.T
    kvt[1] = vnew.T
    q3[...] = jnp.zeros(q3.shape, F32)
    krep[...] = jnp.zeros(krep.shape, F32)
    vrep[...] = jnp.zeros(vrep.shape, F32)
    for j in range(4):
        qhat = _pair_rms(z_ref[:, j * LANES:(j + 1) * LANES], lo, qw_ref[...])
        qrot = pltpu.roll(qhat, HEAD_DIM, axis=1)
        grp_lo = j < 2
        for half in range(2):
            h = 2 * j + half
            src = qhat if (half == 0) == grp_lo else qrot
            q3[pl.ds(h, nb, stride=QROWS), :] = jnp.where(lo if grp_lo else ~lo, src, 0.0)
            krep[pl.ds(h, nb, stride=QROWS), :] = khat
            vrep[pl.ds(h, nb, stride=QROWS), :] = vnew
    s_new = jnp.sum(q3[...] * krep[...], axis=-1, keepdims=True)
    snew[...] = jnp.broadcast_to(s_new, snew.shape)


def _shift_caches(i, nb, ck_ref, cv_ref, knew_ref, vnew_ref, kvt):
    shift = lax.rem(nb - i * CB, nb)
    newest = lax.broadcasted_iota(jnp.int32, (LANES, WINDOW), 1) == WINDOW - 1
    for src, dst, which in ((ck_ref, knew_ref, 0), (cv_ref, vnew_ref, 1)):
        cols = pltpu.roll(kvt[which], shift, axis=1)
        shifted = pltpu.roll(src[...], WINDOW - 1, axis=2)
        for bb in range(CB):
            dst[bb] = jnp.where(newest, cols[:, bb:bb + 1], shifted[bb])


def _sample_step(j, last, sinks_ref, x_ref, pscale_ref, ck_ref, cv_ref, sp_ref, y_ref, spnew_ref,
                 wout_ref, wpool_ref, z_ref, q3, krep, vrep, snew, kvt, o3, mixed):
    nb = x_ref.shape[0]
    lo = _lo_mask()
    row0 = pl.multiple_of(j * BB, BB)
    rows = pl.ds(row0, BB)
    qrows = pl.ds(pl.multiple_of(j * (BB * QROWS), BB * QROWS), BB * QROWS)
    rid = lax.broadcasted_iota(jnp.int32, (1, QROWS, 1), 1)
    sink3 = jnp.zeros((1, QROWS, 1), F32)
    for h in range(N_HEADS):
        sink3 = jnp.where(rid == h, sinks_ref[h] * LOG2E, sink3)

    q = q3[qrows, :].reshape(BB, QROWS, LANES)
    s_new = snew[qrows, :].reshape(BB, QROWS, LANES)
    kt = ck_ref[...]
    vt = cv_ref[...]
    s = lax.dot_general(q.astype(BF16), kt.astype(BF16), (((2,), (1,)), ((0,), (0,))), preferred_element_type=F32)
    m = jnp.maximum(jnp.maximum(jnp.max(s, axis=-1, keepdims=True), s_new), sink3)
    p = jnp.exp2(s - m)
    p_new = jnp.exp2(s_new - m)
    l = jnp.sum(p, axis=-1, keepdims=True) + p_new + jnp.exp2(sink3 - m)
    o = lax.dot_general(p.astype(BF16), vt.astype(BF16), (((2,), (2,)), ((0,), (0,))), preferred_element_type=F32)
    o = (o + p_new * vrep[qrows, :].reshape(BB, QROWS, LANES)) / l
    o3[...] = o.reshape(BB * QROWS, LANES)

    for jj in range(4):
        grp_lo = jj < 2
        oh = [o3[pl.ds(2 * jj + half, BB, stride=QROWS), :] for half in range(2)]
        a = oh[0] if grp_lo else pltpu.roll(oh[0], HEAD_DIM, axis=1)
        b = pltpu.roll(oh[1], HEAD_DIM, axis=1) if grp_lo else oh[1]
        ga = z_ref[rows, O_GA + jj * LANES:O_GA + (jj + 1) * LANES]
        mixed[rows, jj * LANES:(jj + 1) * LANES] = (jnp.where(lo, a, b) * _silu_half(ga)).astype(BF16)

    u = z_ref[rows, O_U:O_U + D_POOL]
    spnew_ref[0:POOL_STATE - 1] = sp_ref[1:POOL_STATE]
    spnew_ref[POOL_STATE - 1] = u
    ds = []
    for g, w in enumerate(POOL_WINDOWS):
        cols = slice(g * LANES, (g + 1) * LANES)
        ug = u[:, cols]
        win_sum = ug
        for r in range(POOL_STATE - (w - 1), POOL_STATE):
            win_sum = win_sum + sp_ref[r, :, cols]
        ds.append((win_sum * (1.0 / w) - ug).astype(BF16))
    for pr in range(len(POOL_WINDOWS) // 2):
        cols = slice(pr * 2 * LANES, (pr + 1) * 2 * LANES)
        d2 = jnp.concatenate(ds[2 * pr:2 * pr + 2], axis=1)
        po = jnp.dot(d2, wpool_ref[pr], preferred_element_type=F32) * pscale_ref[:, cols]
        gp = z_ref[rows, O_GP + pr * 2 * LANES:O_GP + (pr + 1) * 2 * LANES]
        mixed[rows, D_ATTN + pr * 2 * LANES:D_ATTN + (pr + 1) * 2 * LANES] = (po * _silu_half(gp)).astype(BF16)

    @pl.when(last)
    def _():
        y_ref[:, 0, :] = x_ref[:, 0, :] + jnp.dot(mixed[...], wout_ref[...], preferred_element_type=F32)


def _fused_kernel(n_prompt, sinks_ref, xp_ref, xs_ref, nw_ref, win_hbm, qn_ref, kn_ref, wpool_f32, pscale_ref, wout_hbm,
                  ck_ref, cv_ref, sp_ref, ckc_ref, cvc_ref,
                  yp_ref, kp_ref, vp_ref, up_ref, ys_ref, kq_ref, vq_ref, pq_ref,
                  win_ref, wout_ref, wpool_ref, qw_ref, kw_ref, stage, sem,
                  z_ref, hbuf, qbuf, kbuf, vbuf, uext, mixed, sink2, eye4, maskt,
                  zs_ref, q3, krep, vrep, snew, kvt, o3, mixed_s):
    i = pl.program_id(0)

    @pl.when(i == 0)
    def _():
        _load_weights(win_hbm, wout_hbm, wpool_f32, nw_ref, qn_ref, kn_ref, stage, sem, win_ref, wout_ref, wpool_ref,
                      qw_ref, kw_ref)
        _prompt_init(sinks_ref, kbuf, vbuf, uext, sink2, eye4, maskt)
        _sample_init(xs_ref, win_ref, qw_ref, kw_ref, zs_ref, q3, krep, vrep, snew, kvt)

    @pl.when(i < n_prompt)
    def _():
        _shift_caches(i, xs_ref.shape[0], ckc_ref, cvc_ref, kq_ref, vq_ref, kvt)
        _prompt_step(i, i == n_prompt - 1, xp_ref, pscale_ref, yp_ref, kp_ref, vp_ref, up_ref,
                     win_ref, wout_ref, wpool_ref, qw_ref, kw_ref, z_ref, hbuf, qbuf, kbuf, vbuf, uext, mixed, sink2, eye4, maskt)

    @pl.when(i >= n_prompt)
    def _():
        _sample_step(i - n_prompt, i == pl.num_programs(0) - 1, sinks_ref, xs_ref, pscale_ref, ck_ref, cv_ref, sp_ref,
                     ys_ref, pq_ref, wout_ref, wpool_ref, zs_ref, q3, krep, vrep, snew, kvt, o3, mixed_s)


def _fused_call(sinks, xp, xs, nw, win, qn, kn, wpool, pscale, wout, ck, cv, sp):
    seq, nb = xp.shape[0], xs.shape[0]
    n_prompt, n_sample = seq // TB, nb // BB
    assert nb == CB * n_prompt
    const2 = lambda i: (0, 0)
    const3 = lambda i: (0, 0, 0)
    pblk = lambda i: (jnp.minimum(i, n_prompt - 1), 0)
    cblk = lambda i: (jnp.minimum(i, n_prompt - 1), 0, 0)
    sidx = lambda i: jnp.clip(i - n_prompt, 0, n_sample - 1)
    chunk3 = lambda i: (sidx(i), 0, 0)
    mid3 = lambda i: (0, sidx(i), 0)
    once = pl.Buffered(1)
    return pl.pallas_call(
        functools.partial(_fused_kernel, n_prompt),
        grid=(n_prompt + n_sample,),
        in_specs=[
            pl.BlockSpec(memory_space=pltpu.SMEM),
            pl.BlockSpec((TB, D_MODEL), pblk),
            pl.BlockSpec((nb, 1, D_MODEL), const3, pipeline_mode=once),
            pl.BlockSpec((1, D_MODEL), const2),
            pl.BlockSpec(memory_space=pl.ANY),
            pl.BlockSpec((1, HEAD_DIM), const2),
            pl.BlockSpec((1, HEAD_DIM), const2),
            pl.BlockSpec((4, LANES, LANES), const3, pipeline_mode=once),
            pl.BlockSpec((1, D_POOL), const2),
            pl.BlockSpec(memory_space=pl.ANY),
            pl.BlockSpec((BB, LANES, WINDOW), chunk3),
            pl.BlockSpec((BB, LANES, WINDOW), chunk3),
            pl.BlockSpec((POOL_STATE, BB, D_POOL), mid3),
            pl.BlockSpec((CB, LANES, WINDOW), cblk),
            pl.BlockSpec((CB, LANES, WINDOW), cblk),
        ],
        out_specs=[
            pl.BlockSpec((TB, D_MODEL), pblk),
            pl.BlockSpec((LANES, WINDOW), const2),
            pl.BlockSpec((LANES, WINDOW), const2),
            pl.BlockSpec((POOL_STATE, 1, D_POOL), const3),
            pl.BlockSpec((nb, 1, D_MODEL), const3, pipeline_mode=once),
            pl.BlockSpec((CB, LANES, WINDOW), cblk),
            pl.BlockSpec((CB, LANES, WINDOW), cblk),
            pl.BlockSpec((POOL_STATE, BB, D_POOL), mid3),
        ],
        out_shape=[
            jax.ShapeDtypeStruct((seq, D_MODEL), F32),
            jax.ShapeDtypeStruct((LANES, WINDOW), F32),
            jax.ShapeDtypeStruct((LANES, WINDOW), F32),
            jax.ShapeDtypeStruct((POOL_STATE, 1, D_POOL), F32),
            jax.ShapeDtypeStruct((nb, 1, D_MODEL), F32),
            jax.ShapeDtypeStruct((nb, LANES, WINDOW), F32),
            jax.ShapeDtypeStruct((nb, LANES, WINDOW), F32),
            jax.ShapeDtypeStruct((POOL_STATE, nb, D_POOL), F32),
        ],
        scratch_shapes=[
            pltpu.VMEM((D_MODEL, D_IN_PROJ), BF16),
            pltpu.VMEM((D_MODEL, D_MODEL), BF16),
            pltpu.VMEM((2, 2 * LANES, 2 * LANES), BF16),
            pltpu.VMEM((1, LANES), F32),
            pltpu.VMEM((1, LANES), F32),
            pltpu.VMEM((2, WROWS, D_IN_PROJ), F32),
            pltpu.SemaphoreType.DMA((2,)),
            pltpu.VMEM((TB, D_IN_PROJ), F32),
            pltpu.VMEM((TB, D_MODEL), BF16),
            pltpu.VMEM((2, NSB, GROUP * BLOCK, LANES), BF16),
            pltpu.VMEM((2, TB + BLOCK, LANES), BF16),
            pltpu.VMEM((2, TB + BLOCK, LANES), BF16),
            pltpu.VMEM((TB + 16, D_POOL), F32),
            pltpu.VMEM((TB, D_MODEL), BF16),
            pltpu.VMEM((N_HEADS // 2, 1, LANES), F32),
            pltpu.VMEM((GROUP * BLOCK, LANES), BF16),
            pltpu.VMEM((2, 2 * BLOCK, LANES), BF16),
            pltpu.VMEM((nb, D_IN_PROJ), F32),
            pltpu.VMEM((nb * QROWS, LANES), F32),
            pltpu.VMEM((nb * QROWS, LANES), F32),
            pltpu.VMEM((nb * QROWS, LANES), F32),
            pltpu.VMEM((nb * QROWS, LANES), F32),
            pltpu.VMEM((2, LANES, nb), F32),
            pltpu.VMEM((BB * QROWS, LANES), F32),
            pltpu.VMEM((nb, D_MODEL), BF16),
        ],
        compiler_params=pltpu.CompilerParams(dimension_semantics=("arbitrary",), vmem_limit_bytes=VMEM_LIMIT),
        name="hybrid_step",
    )(sinks, xp, xs, nw, win, qn, kn, wpool, pscale, wout, ck, cv, sp, ck, cv)


def _cache_in(c):
    nb = c.shape[0]
    return jnp.transpose(c, (0, 2, 3, 1)).reshape(nb, LANES, WINDOW)


def _cache_out(c):
    nb = c.shape[0]
    return jnp.transpose(c.reshape(nb, N_KV_HEADS, HEAD_DIM, WINDOW), (0, 3, 1, 2))


def kernel(x_prompt, x_sample, cache_k, cache_v, state_pool, norm_w, w_in, q_norm_w, k_norm_w, sinks, w_pool,
           pool_scale, w_out):
    depth = norm_w.shape[0]
    assert depth == 1 and x_prompt.shape[0] == 1 and x_sample.shape[1] == 1
    seq = x_prompt.shape[1]
    nb = x_sample.shape[0]
    assert seq % TB == 0 and nb % BB == 0 and nb == LANES

    yp, kp, vp, up, ys, kq, vq, pq = _fused_call(
        sinks[0], x_prompt[0], x_sample, norm_w, w_in[0], q_norm_w, k_norm_w, w_pool[0], pool_scale, w_out[0],
        _cache_in(cache_k[0]), _cache_in(cache_v[0]), jnp.transpose(state_pool[0], (1, 0, 2)))
    return (yp[None], ys, _cache_out(kp[None])[None], _cache_out(vp[None])[None],
            jnp.transpose(up, (1, 0, 2))[None],
            _cache_out(kq)[None], _cache_out(vq)[None], jnp.transpose(pq, (1, 0, 2))[None])
```

```python
import functools

import jax
import jax.numpy as jnp
from jax import lax
from jax.experimental import pallas as pl
from jax.experimental.pallas import tpu as pltpu

D_MODEL = 1024
HEAD_DIM = 64
N_HEADS = 8
N_KV_HEADS = 2
GROUP = 4
WINDOW = 128
BLOCK = 128
POOL_WINDOWS = (2, 4, 8, 16)
POOL_STATE = 15
D_ATTN = 512
D_POOL = 512
D_IN_PROJ = 2304
EPS = 1e-6
NEG_INF = -1e30
LOG2E = 1.4426950408889634

O_K = 512
O_V = 640
O_GA = 768
O_U = 1280
O_GP = 1792

LANES = 128
TB = 512
NSB = TB // BLOCK
BB = 32
CB = 4
QROWS = 16
WCHUNK = 256
NSTAGE = 4
VMEM_LIMIT = 58 * 1024 * 1024

F32 = jnp.float32
BF16 = jnp.bfloat16
_NT = (((1,), (1,)), ((), ()))


def _silu_half(h):
    return h + h * jnp.tanh(h)


def _lo_mask():
    return lax.broadcasted_iota(jnp.int32, (1, LANES), 1) < HEAD_DIM


def _pair_rms(zs, lo, w2):
    sq = zs * zs
    s_lo = jnp.sum(jnp.where(lo, sq, 0.0), axis=-1, keepdims=True)
    s_hi = jnp.sum(jnp.where(lo, 0.0, sq), axis=-1, keepdims=True)
    r = lax.rsqrt(jnp.where(lo, s_lo, s_hi) * (1.0 / HEAD_DIM) + EPS)
    return zs * r * w2


def _row_scale(x):
    return lax.rsqrt(jnp.mean(x * x, axis=-1, keepdims=True) + EPS)


def _project(x, win):
    return jnp.dot(x.astype(BF16), win, preferred_element_type=F32) * _row_scale(x)


def _weight_chunks():
    chunks = []
    for c0 in range(0, D_IN_PROJ, WCHUNK):
        gate = O_GA <= c0 < O_U or c0 >= O_GP
        chunks.append((0, c0, 0.5 if gate else 1.0))
    chunks += [(1, c0, 1.0) for c0 in range(0, D_MODEL, WCHUNK)]
    return chunks


def _load_weights(win_hbm, wout_hbm, wpool_f32, nw_ref, qn_ref, kn_ref, stage, sem, win_bf, wout_bf, wpool_bf,
                  qw_ref, kw_ref):
    chunks = _weight_chunks()
    srcs, dsts = (win_hbm, wout_hbm), (win_bf, wout_bf)
    nw_tile = jnp.concatenate([jnp.broadcast_to(nw_ref[:, t * LANES:(t + 1) * LANES], (LANES, LANES)).T
                               for t in range(D_MODEL // LANES)], axis=0)
    nw_rows = jnp.concatenate([nw_tile] * (WCHUNK // LANES), axis=1)

    def copy(k):
        which, c0, _ = chunks[k]
        return pltpu.make_async_copy(srcs[which].at[:, pl.ds(c0, WCHUNK)], stage.at[k % NSTAGE], sem.at[k % NSTAGE])

    for k in range(NSTAGE):
        copy(k).start()
    for k, (which, c0, scale) in enumerate(chunks):
        copy(k).wait()
        w = stage[k % NSTAGE]
        if which == 0:
            w = w * nw_rows if scale == 1.0 else w * (scale * nw_rows)
        dsts[which][:, c0:c0 + WCHUNK] = w.astype(BF16)
        if k + NSTAGE < len(chunks):
            copy(k + NSTAGE).start()
    wpool_bf[...] = jnp.zeros(wpool_bf.shape, BF16)
    for g in range(len(POOL_WINDOWS)):
        d0 = (g % 2) * LANES
        wpool_bf[g // 2, d0:d0 + LANES, d0:d0 + LANES] = wpool_f32[g].astype(BF16)
    qn = qn_ref[...] * (HEAD_DIM ** -0.5 * LOG2E)
    qw_ref[...] = jnp.concatenate([qn, qn], axis=1)
    kw_ref[...] = jnp.concatenate([kn_ref[...], kn_ref[...]], axis=1)


def _prompt_init(sinks_ref, kbuf, vbuf, uext, sink2, eye4, maskt):
    kbuf[:, 0:BLOCK, :] = jnp.zeros((2, BLOCK, LANES), BF16)
    vbuf[:, 0:BLOCK, :] = jnp.zeros((2, BLOCK, LANES), BF16)
    uext[0:16, :] = jnp.zeros((16, D_POOL), F32)
    lo = _lo_mask()
    for jj in range(N_HEADS // 2):
        sink2[jj] = jnp.where(lo, sinks_ref[2 * jj], sinks_ref[2 * jj + 1]) * LOG2E
    rr = lax.broadcasted_iota(jnp.int32, (GROUP * BLOCK, LANES), 0)
    ll = lax.broadcasted_iota(jnp.int32, (GROUP * BLOCK, LANES), 1)
    eye4[...] = jnp.where((rr & (BLOCK - 1)) == ll, 1.0, 0.0).astype(BF16)
    cc = lax.broadcasted_iota(jnp.int32, (2 * BLOCK, LANES), 0)
    qr = lax.broadcasted_iota(jnp.int32, (2 * BLOCK, LANES), 1)
    band = (cc >= qr) & (cc <= qr + WINDOW)
    maskt[0] = jnp.where(band & (cc >= BLOCK), 0.0, NEG_INF).astype(BF16)
    maskt[1] = jnp.where(band, 0.0, NEG_INF).astype(BF16)


def _prompt_step(i, last, x_ref, pscale_ref, y_ref, knew_ref, vnew_ref, unew_ref,
                 win_ref, wout_ref, wpool_ref, qw_ref, kw_ref, z_ref, hbuf, qbuf, kbuf, vbuf, uext, mixed, sink2, eye4, maskt):
    lo = _lo_mask()
    hbuf[...] = x_ref[...].astype(BF16)
    rs = _row_scale(x_ref[...])

    def proj(cols, rows=slice(None)):
        return jnp.dot(hbuf[rows, :], win_ref[:, cols], preferred_element_type=F32) * rs[rows]

    z_ref[:, O_U:] = proj(slice(O_U, D_IN_PROJ))
    uext[16:, :] = z_ref[:, O_U:O_U + D_POOL]
    pos16 = i * TB + lax.broadcasted_iota(jnp.int32, (16, LANES), 0)
    ds = []
    for g, w in enumerate(POOL_WINDOWS):
        acc = uext[:, g * LANES:(g + 1) * LANES]
        sh = 1
        while sh < w:
            acc = acc + pltpu.roll(acc, sh, axis=0)
            sh *= 2
        ug = z_ref[:, O_U + g * LANES:O_U + (g + 1) * LANES]
        cnt = jnp.minimum(w, pos16 + 1).astype(F32)
        d_first = acc[16:32, :] / cnt - ug[0:16, :]
        d_rest = acc[32:, :] * (1.0 / w) - ug[16:, :]
        ds.append(jnp.concatenate([d_first, d_rest], axis=0).astype(BF16))
    uext[0:16, :] = uext[TB:TB + 16, :]

    z_ref[:, :O_K] = proj(slice(0, O_K))

    for pr in range(len(POOL_WINDOWS) // 2):
        cols = slice(pr * 2 * LANES, (pr + 1) * 2 * LANES)
        d2 = jnp.concatenate(ds[2 * pr:2 * pr + 2], axis=1)
        po = jnp.dot(d2, wpool_ref[pr], preferred_element_type=F32) * pscale_ref[:, cols]
        gp = z_ref[:, O_GP + pr * 2 * LANES:O_GP + (pr + 1) * 2 * LANES]
        mixed[:, D_ATTN + pr * 2 * LANES:D_ATTN + (pr + 1) * 2 * LANES] = (po * _silu_half(gp)).astype(BF16)

    for j in range(4):
        qhat = _pair_rms(z_ref[:, j * LANES:(j + 1) * LANES], lo, qw_ref[...])
        q_lo = jnp.where(lo, qhat, 0.0).astype(BF16)
        q_hi = jnp.where(lo, 0.0, qhat).astype(BF16)
        c, g0 = j // 2, 2 * (j % 2)
        for sb in range(NSB):
            rows = slice(sb * BLOCK, (sb + 1) * BLOCK)
            qbuf[c, sb, g0 * BLOCK:(g0 + 1) * BLOCK, :] = q_lo[rows]
            qbuf[c, sb, (g0 + 1) * BLOCK:(g0 + 2) * BLOCK, :] = q_hi[rows]

    half = TB // 2
    for r0 in (0, half):
        z_ref[r0:r0 + half, O_K:O_GA] = proj(slice(O_K, O_GA), slice(r0, r0 + half))
    z_ref[:, O_GA:O_U] = proj(slice(O_GA, O_U))

    khat = _pair_rms(z_ref[:, O_K:O_K + LANES], lo, kw_ref[...])
    kr = pltpu.roll(khat, HEAD_DIM, axis=1)
    kbuf[0, BLOCK:, :] = jnp.where(lo, khat, kr).astype(BF16)
    kbuf[1, BLOCK:, :] = jnp.where(lo, kr, khat).astype(BF16)
    vz = z_ref[:, O_V:O_V + LANES]
    vr = pltpu.roll(vz, HEAD_DIM, axis=1)
    vbuf[0, BLOCK:, :] = jnp.where(lo, vz, vr).astype(BF16)
    vbuf[1, BLOCK:, :] = jnp.where(lo, vr, vz).astype(BF16)

    first = jnp.where(i > 0, 1, 0)
    for sb in range(NSB):
        rows = slice(sb * BLOCK, (sb + 1) * BLOCK)
        mcols = maskt[first] if sb == 0 else maskt[1]
        for c in range(2):
            keys = kbuf[c, sb * BLOCK:(sb + 2) * BLOCK, :]
            vals = vbuf[c, sb * BLOCK:(sb + 2) * BLOCK, :]
            s = lax.dot_general(jnp.concatenate([qbuf[c, sb], eye4[...]], axis=1),
                                jnp.concatenate([keys, mcols], axis=1), _NT,
                                preferred_element_type=F32)
            ps, ms, ls = [], [], []
            for g in range(GROUP):
                sg = s[g * BLOCK:(g + 1) * BLOCK, :]
                m = jnp.max(sg, axis=-1, keepdims=True)
                p = jnp.exp2(sg - m)
                ps.append(p.astype(BF16))
                ms.append(m)
                ls.append(jnp.sum(p, axis=-1, keepdims=True))
            o = jnp.dot(jnp.concatenate(ps, axis=0), vals, preferred_element_type=F32)
            for jj in range(2):
                ev, od = 2 * jj, 2 * jj + 1
                slab = jnp.where(lo, o[ev * BLOCK:(ev + 1) * BLOCK, :], o[od * BLOCK:(od + 1) * BLOCK, :])
                l = jnp.where(lo, ls[ev], ls[od]) + jnp.exp2(sink2[2 * c + jj] - jnp.where(lo, ms[ev], ms[od]))
                col = (2 * c + jj) * LANES
                ga = z_ref[rows, O_GA + col:O_GA + col + LANES]
                mixed[rows, col:col + LANES] = (slab * (1.0 / l) * _silu_half(ga)).astype(BF16)

    kbuf[:, 0:BLOCK, :] = kbuf[:, TB:TB + BLOCK, :]
    vbuf[:, 0:BLOCK, :] = vbuf[:, TB:TB + BLOCK, :]

    y_ref[...] = x_ref[...] + jnp.dot(mixed[...], wout_ref[...], preferred_element_type=F32)

    @pl.when(last)
    def _():
        unew_ref[:, 0, :] = z_ref[TB - POOL_STATE:TB, O_U:O_U + D_POOL]
        tail = slice(TB - WINDOW, TB)
        knew_ref[...] = _pair_rms(z_ref[tail, O_K:O_K + LANES], lo, kw_ref[...]).T
        vnew_ref[...] = z_ref[tail, O_V:O_V + LANES].T


def _sample_init(x_ref, win_ref, qw_ref, kw_ref, z_ref, q3, krep, vrep, snew, kvt):
    nb = x_ref.shape[0]
    lo = _lo_mask()
    z_ref[...] = _project(x_ref[:, 0, :], win_ref[...])
    khat = _pair_rms(z_ref[:, O_K:O_K + LANES], lo, kw_ref[...])
    vnew = z_ref[:, O_V:O_V + LANES]
    kvt[0] = khat.T
    kvt[1] = vnew.T
    q3[...] = jnp.zeros(q3.shape, F32)
    krep[...] = jnp.zeros(krep.shape, F32)
    vrep[...] = jnp.zeros(vrep.shape, F32)
    for j in range(4):
        qhat = _pair_rms(z_ref[:, j * LANES:(j + 1) * LANES], lo, qw_ref[...])
        qrot = pltpu.roll(qhat, HEAD_DIM, axis=1)
        grp_lo = j < 2
        for half in range(2):
            h = 2 * j + half
            src = qhat if (half == 0) == grp_lo else qrot
            q3[pl.ds(h, nb, stride=QROWS), :] = jnp.where(lo if grp_lo else ~lo, src, 0.0)
            krep[pl.ds(h, nb, stride=QROWS), :] = khat
            vrep[pl.ds(h, nb, stride=QROWS), :] = vnew
    s_new = jnp.sum(q3[...] * krep[...], axis=-1, keepdims=True)
    snew[...] = jnp.broadcast_to(s_new, snew.shape)


def _shift_caches(i, nb, ck_ref, cv_ref, knew_ref, vnew_ref, kvt):
    shift = lax.rem(nb - i * CB, nb)
    newest = lax.broadcasted_iota(jnp.int32, (LANES, WINDOW), 1) == WINDOW - 1
    for src, dst, which in ((ck_ref, knew_ref, 0), (cv_ref, vnew_ref, 1)):
        cols = pltpu.roll(kvt[which], shift, axis=1)
        shifted = pltpu.roll(src[...], WINDOW - 1, axis=2)
        for bb in range(CB):
            dst[bb] = jnp.where(newest, cols[:, bb:bb + 1], shifted[bb])


def _sample_step(j, last, sinks_ref, x_ref, pscale_ref, ck_ref, cv_ref, sp_ref, y_ref, spnew_ref,
                 wout_ref, wpool_ref, z_ref, q3, krep, vrep, snew, kvt, o3, mixed):
    nb = x_ref.shape[0]
    lo = _lo_mask()
    row0 = pl.multiple_of(j * BB, BB)
    rows = pl.ds(row0, BB)
    qrows = pl.ds(pl.multiple_of(j * (BB * QROWS), BB * QROWS), BB * QROWS)
    rid = lax.broadcasted_iota(jnp.int32, (1, QROWS, 1), 1)
    sink3 = jnp.zeros((1, QROWS, 1), F32)
    for h in range(N_HEADS):
        sink3 = jnp.where(rid == h, sinks_ref[h] * LOG2E, sink3)

    q = q3[qrows, :].reshape(BB, QROWS, LANES)
    s_new = snew[qrows, :].reshape(BB, QROWS, LANES)
    kt = ck_ref[...]
    vt = cv_ref[...]
    s = lax.dot_general(q.astype(BF16), kt.astype(BF16), (((2,), (1,)), ((0,), (0,))), preferred_element_type=F32)
    m = jnp.maximum(jnp.maximum(jnp.max(s, axis=-1, keepdims=True), s_new), sink3)
    p = jnp.exp2(s - m)
    p_new = jnp.exp2(s_new - m)
    l = jnp.sum(p, axis=-1, keepdims=True) + p_new + jnp.exp2(sink3 - m)
    o = lax.dot_general(p.astype(BF16), vt.astype(BF16), (((2,), (2,)), ((0,), (0,))), preferred_element_type=F32)
    o = (o + p_new * vrep[qrows, :].reshape(BB, QROWS, LANES)) / l
    o3[...] = o.reshape(BB * QROWS, LANES)

    for jj in range(4):
        grp_lo = jj < 2
        oh = [o3[pl.ds(2 * jj + half, BB, stride=QROWS), :] for half in range(2)]
        a = oh[0] if grp_lo else pltpu.roll(oh[0], HEAD_DIM, axis=1)
        b = pltpu.roll(oh[1], HEAD_DIM, axis=1) if grp_lo else oh[1]
        ga = z_ref[rows, O_GA + jj * LANES:O_GA + (jj + 1) * LANES]
        mixed[rows, jj * LANES:(jj + 1) * LANES] = (jnp.where(lo, a, b) * _silu_half(ga)).astype(BF16)

    u = z_ref[rows, O_U:O_U + D_POOL]
    spnew_ref[0:POOL_STATE - 1] = sp_ref[1:POOL_STATE]
    spnew_ref[POOL_STATE - 1] = u
    ds = []
    for g, w in enumerate(POOL_WINDOWS):
        cols = slice(g * LANES, (g + 1) * LANES)
        ug = u[:, cols]
        win_sum = ug
        for r in range(POOL_STATE - (w - 1), POOL_STATE):
            win_sum = win_sum + sp_ref[r, :, cols]
        ds.append((win_sum * (1.0 / w) - ug).astype(BF16))
    for pr in range(len(POOL_WINDOWS) // 2):
        cols = slice(pr * 2 * LANES, (pr + 1) * 2 * LANES)
        d2 = jnp.concatenate(ds[2 * pr:2 * pr + 2], axis=1)
        po = jnp.dot(d2, wpool_ref[pr], preferred_element_type=F32) * pscale_ref[:, cols]
        gp = z_ref[rows, O_GP + pr * 2 * LANES:O_GP + (pr + 1) * 2 * LANES]
        mixed[rows, D_ATTN + pr * 2 * LANES:D_ATTN + (pr + 1) * 2 * LANES] = (po * _silu_half(gp)).astype(BF16)

    @pl.when(last)
    def _():
        y_ref[:, 0, :] = x_ref[:, 0, :] + jnp.dot(mixed[...], wout_ref[...], preferred_element_type=F32)


def _fused_kernel(n_prompt, sinks_ref, xp_ref, xs_ref, nw_ref, win_hbm, qn_ref, kn_ref, wpool_f32, pscale_ref, wout_hbm,
                  ck_ref, cv_ref, sp_ref, ckc_ref, cvc_ref,
                  yp_ref, kp_ref, vp_ref, up_ref, ys_ref, kq_ref, vq_ref, pq_ref,
                  win_ref, wout_ref, wpool_ref, qw_ref, kw_ref, stage, sem,
                  z_ref, hbuf, qbuf, kbuf, vbuf, uext, mixed, sink2, eye4, maskt,
                  zs_ref, q3, krep, vrep, snew, kvt, o3, mixed_s):
    i = pl.program_id(0)

    @pl.when(i == 0)
    def _():
        _load_weights(win_hbm, wout_hbm, wpool_f32, nw_ref, qn_ref, kn_ref, stage, sem, win_ref, wout_ref, wpool_ref,
                      qw_ref, kw_ref)
        _prompt_init(sinks_ref, kbuf, vbuf, uext, sink2, eye4, maskt)
        _sample_init(xs_ref, win_ref, qw_ref, kw_ref, zs_ref, q3, krep, vrep, snew, kvt)

    @pl.when(i < n_prompt)
    def _():
        _shift_caches(i, xs_ref.shape[0], ckc_ref, cvc_ref, kq_ref, vq_ref, kvt)
        _prompt_step(i, i == n_prompt - 1, xp_ref, pscale_ref, yp_ref, kp_ref, vp_ref, up_ref,
                     win_ref, wout_ref, wpool_ref, qw_ref, kw_ref, z_ref, hbuf, qbuf, kbuf, vbuf, uext, mixed, sink2, eye4, maskt)

    @pl.when(i >= n_prompt)
    def _():
        _sample_step(i - n_prompt, i == pl.num_programs(0) - 1, sinks_ref, xs_ref, pscale_ref, ck_ref, cv_ref, sp_ref,
                     ys_ref, pq_ref, wout_ref, wpool_ref, zs_ref, q3, krep, vrep, snew, kvt, o3, mixed_s)


def _fused_call(sinks, xp, xs, nw, win, qn, kn, wpool, pscale, wout, ck, cv, sp):
    seq, nb = xp.shape[0], xs.shape[0]
    n_prompt, n_sample = seq // TB, nb // BB
    assert nb == CB * n_prompt
    const2 = lambda i: (0, 0)
    const3 = lambda i: (0, 0, 0)
    pblk = lambda i: (jnp.minimum(i, n_prompt - 1), 0)
    cblk = lambda i: (jnp.minimum(i, n_prompt - 1), 0, 0)
    sidx = lambda i: jnp.clip(i - n_prompt, 0, n_sample - 1)
    chunk3 = lambda i: (sidx(i), 0, 0)
    mid3 = lambda i: (0, sidx(i), 0)
    once = pl.Buffered(1)
    return pl.pallas_call(
        functools.partial(_fused_kernel, n_prompt),
        grid=(n_prompt + n_sample,),
        in_specs=[
            pl.BlockSpec(memory_space=pltpu.SMEM),
            pl.BlockSpec((TB, D_MODEL), pblk),
            pl.BlockSpec((nb, 1, D_MODEL), const3, pipeline_mode=once),
            pl.BlockSpec((1, D_MODEL), const2),
            pl.BlockSpec(memory_space=pl.ANY),
            pl.BlockSpec((1, HEAD_DIM), const2),
            pl.BlockSpec((1, HEAD_DIM), const2),
            pl.BlockSpec((4, LANES, LANES), const3, pipeline_mode=once),
            pl.BlockSpec((1, D_POOL), const2),
            pl.BlockSpec(memory_space=pl.ANY),
            pl.BlockSpec((BB, LANES, WINDOW), chunk3),
            pl.BlockSpec((BB, LANES, WINDOW), chunk3),
            pl.BlockSpec((POOL_STATE, BB, D_POOL), mid3),
            pl.BlockSpec((CB, LANES, WINDOW), cblk),
            pl.BlockSpec((CB, LANES, WINDOW), cblk),
        ],
        out_specs=[
            pl.BlockSpec((TB, D_MODEL), pblk),
            pl.BlockSpec((LANES, WINDOW), const2),
            pl.BlockSpec((LANES, WINDOW), const2),
            pl.BlockSpec((POOL_STATE, 1, D_POOL), const3),
            pl.BlockSpec((nb, 1, D_MODEL), const3, pipeline_mode=once),
            pl.BlockSpec((CB, LANES, WINDOW), cblk),
            pl.BlockSpec((CB, LANES, WINDOW), cblk),
            pl.BlockSpec((POOL_STATE, BB, D_POOL), mid3),
        ],
        out_shape=[
            jax.ShapeDtypeStruct((seq, D_MODEL), F32),
            jax.ShapeDtypeStruct((LANES, WINDOW), F32),
            jax.ShapeDtypeStruct((LANES, WINDOW), F32),
            jax.ShapeDtypeStruct((POOL_STATE, 1, D_POOL), F32),
            jax.ShapeDtypeStruct((nb, 1, D_MODEL), F32),
            jax.ShapeDtypeStruct((nb, LANES, WINDOW), F32),
            jax.ShapeDtypeStruct((nb, LANES, WINDOW), F32),
            jax.ShapeDtypeStruct((POOL_STATE, nb, D_POOL), F32),
        ],
        scratch_shapes=[
            pltpu.VMEM((D_MODEL, D_IN_PROJ), BF16),
            pltpu.VMEM((D_MODEL, D_MODEL), BF16),
            pltpu.VMEM((2, 2 * LANES, 2 * LANES), BF16),
            pltpu.VMEM((1, LANES), F32),
            pltpu.VMEM((1, LANES), F32),
            pltpu.VMEM((NSTAGE, D_MODEL, WCHUNK), F32),
            pltpu.SemaphoreType.DMA((NSTAGE,)),
            pltpu.VMEM((TB, D_IN_PROJ), F32),
            pltpu.VMEM((TB, D_MODEL), BF16),
            pltpu.VMEM((2, NSB, GROUP * BLOCK, LANES), BF16),
            pltpu.VMEM((2, TB + BLOCK, LANES), BF16),
            pltpu.VMEM((2, TB + BLOCK, LANES), BF16),
            pltpu.VMEM((TB + 16, D_POOL), F32),
            pltpu.VMEM((TB, D_MODEL), BF16),
            pltpu.VMEM((N_HEADS // 2, 1, LANES), F32),
            pltpu.VMEM((GROUP * BLOCK, LANES), BF16),
            pltpu.VMEM((2, 2 * BLOCK, LANES), BF16),
            pltpu.VMEM((nb, D_IN_PROJ), F32),
            pltpu.VMEM((nb * QROWS, LANES), F32),
            pltpu.VMEM((nb * QROWS, LANES), F32),
            pltpu.VMEM((nb * QROWS, LANES), F32),
            pltpu.VMEM((nb * QROWS, LANES), F32),
            pltpu.VMEM((2, LANES, nb), F32),
            pltpu.VMEM((BB * QROWS, LANES), F32),
            pltpu.VMEM((nb, D_MODEL), BF16),
        ],
        compiler_params=pltpu.CompilerParams(dimension_semantics=("arbitrary",), vmem_limit_bytes=VMEM_LIMIT),
        name="hybrid_step",
    )(sinks, xp, xs, nw, win, qn, kn, wpool, pscale, wout, ck, cv, sp, ck, cv)


def _cache_in(c):
    nb = c.shape[0]
    return jnp.transpose(c, (0, 2, 3, 1)).reshape(nb, LANES, WINDOW)


def _cache_out(c):
    nb = c.shape[0]
    return jnp.transpose(c.reshape(nb, N_KV_HEADS, HEAD_DIM, WINDOW), (0, 3, 1, 2))


def kernel(x_prompt, x_sample, cache_k, cache_v, state_pool, norm_w, w_in, q_norm_w, k_norm_w, sinks, w_pool,
           pool_scale, w_out):
    depth = norm_w.shape[0]
    assert depth == 1 and x_prompt.shape[0] == 1 and x_sample.shape[1] == 1
    seq = x_prompt.shape[1]
    nb = x_sample.shape[0]
    assert seq % TB == 0 and nb % BB == 0 and nb == LANES

    yp, kp, vp, up, ys, kq, vq, pq = _fused_call(
        sinks[0], x_prompt[0], x_sample, norm_w, w_in[0], q_norm_w, k_norm_w, w_pool[0], pool_scale, w_out[0],
        _cache_in(cache_k[0]), _cache_in(cache_v[0]), jnp.transpose(state_pool[0], (1, 0, 2)))
    return (yp[None], ys, _cache_out(kp[None])[None], _cache_out(vp[None])[None],
            jnp.transpose(up, (1, 0, 2))[None],
            _cache_out(kq)[None], _cache_out(vq)[None], jnp.transpose(pq, (1, 0, 2))[None])
```

```python
import functools

import jax
import jax.numpy as jnp
from jax import lax
from jax.experimental import pallas as pl
from jax.experimental.pallas import tpu as pltpu

D_MODEL = 1024
HEAD_DIM = 64
N_HEADS = 8
N_KV_HEADS = 2
GROUP = 4
WINDOW = 128
BLOCK = 128
POOL_WINDOWS = (2, 4, 8, 16)
POOL_STATE = 15
D_ATTN = 512
D_POOL = 512
D_IN_PROJ = 2304
EPS = 1e-6
NEG_INF = -1e30
LOG2E = 1.4426950408889634

O_K = 512
O_V = 640
O_GA = 768
O_U = 1280
O_GP = 1792

LANES = 128
TB = 512
NSB = TB // BLOCK
BB = 32
CB = 4
QROWS = 16
WCHUNK = 256
NSTAGE = 6
VMEM_LIMIT = 58 * 1024 * 1024

F32 = jnp.float32
BF16 = jnp.bfloat16
_NT = (((1,), (1,)), ((), ()))


def _silu_half(h):
    return h + h * jnp.tanh(h)


def _lo_mask():
    return lax.broadcasted_iota(jnp.int32, (1, LANES), 1) < HEAD_DIM


def _pair_rms(zs, lo, w2):
    sq = zs * zs
    s_lo = jnp.sum(jnp.where(lo, sq, 0.0), axis=-1, keepdims=True)
    s_hi = jnp.sum(jnp.where(lo, 0.0, sq), axis=-1, keepdims=True)
    r = lax.rsqrt(jnp.where(lo, s_lo, s_hi) * (1.0 / HEAD_DIM) + EPS)
    return zs * r * w2


def _row_scale(x):
    return lax.rsqrt(jnp.mean(x * x, axis=-1, keepdims=True) + EPS)


def _project(x, win):
    return jnp.dot(x.astype(BF16), win, preferred_element_type=F32) * _row_scale(x)


def _weight_chunks():
    chunks = []
    for c0 in range(0, D_IN_PROJ, WCHUNK):
        gate = O_GA <= c0 < O_U or c0 >= O_GP
        chunks.append((0, c0, 0.5 if gate else 1.0))
    chunks += [(1, c0, 1.0) for c0 in range(0, D_MODEL, WCHUNK)]
    return chunks


def _load_weights(win_hbm, wout_hbm, wpool_f32, nw_ref, qn_ref, kn_ref, stage, sem, win_bf, wout_bf, wpool_bf,
                  qw_ref, kw_ref):
    chunks = _weight_chunks()
    srcs, dsts = (win_hbm, wout_hbm), (win_bf, wout_bf)

    def copy(k):
        which, c0, _ = chunks[k]
        return pltpu.make_async_copy(srcs[which].at[:, pl.ds(c0, WCHUNK)], stage.at[k % NSTAGE], sem.at[k % NSTAGE])

    for k in range(NSTAGE):
        copy(k).start()
    nw_tile = jnp.concatenate([jnp.broadcast_to(nw_ref[:, t * LANES:(t + 1) * LANES], (LANES, LANES)).T
                               for t in range(D_MODEL // LANES)], axis=0)
    nw_rows = jnp.concatenate([nw_tile] * (WCHUNK // LANES), axis=1)
    for k, (which, c0, scale) in enumerate(chunks):
        copy(k).wait()
        w = stage[k % NSTAGE]
        if which == 0:
            w = w * nw_rows if scale == 1.0 else w * (scale * nw_rows)
        dsts[which][:, c0:c0 + WCHUNK] = w.astype(BF16)
        if k + NSTAGE < len(chunks):
            copy(k + NSTAGE).start()
    wpool_bf[...] = jnp.zeros(wpool_bf.shape, BF16)
    for g in range(len(POOL_WINDOWS)):
        d0 = (g % 2) * LANES
        wpool_bf[g // 2, d0:d0 + LANES, d0:d0 + LANES] = wpool_f32[g].astype(BF16)
    qn = qn_ref[...] * (HEAD_DIM ** -0.5 * LOG2E)
    qw_ref[...] = jnp.concatenate([qn, qn], axis=1)
    kw_ref[...] = jnp.concatenate([kn_ref[...], kn_ref[...]], axis=1)


def _prompt_init(sinks_ref, kbuf, vbuf, uext, sink2, eye4, maskt):
    kbuf[:, 0:BLOCK, :] = jnp.zeros((2, BLOCK, LANES), BF16)
    vbuf[:, 0:BLOCK, :] = jnp.zeros((2, BLOCK, LANES), BF16)
    uext[0:16, :] = jnp.zeros((16, D_POOL), F32)
    lo = _lo_mask()
    for jj in range(N_HEADS // 2):
        sink2[jj] = jnp.where(lo, sinks_ref[2 * jj], sinks_ref[2 * jj + 1]) * LOG2E
    rr = lax.broadcasted_iota(jnp.int32, (GROUP * BLOCK, LANES), 0)
    ll = lax.broadcasted_iota(jnp.int32, (GROUP * BLOCK, LANES), 1)
    eye4[...] = jnp.where((rr & (BLOCK - 1)) == ll, 1.0, 0.0).astype(BF16)
    cc = lax.broadcasted_iota(jnp.int32, (2 * BLOCK, LANES), 0)
    qr = lax.broadcasted_iota(jnp.int32, (2 * BLOCK, LANES), 1)
    band = (cc >= qr) & (cc <= qr + WINDOW)
    maskt[0] = jnp.where(band & (cc >= BLOCK), 0.0, NEG_INF).astype(BF16)
    maskt[1] = jnp.where(band, 0.0, NEG_INF).astype(BF16)


def _prompt_step(i, last, x_ref, pscale_ref, y_ref, knew_ref, vnew_ref, unew_ref,
                 win_ref, wout_ref, wpool_ref, qw_ref, kw_ref, z_ref, hbuf, qbuf, kbuf, vbuf, uext, mixed, sink2, eye4, maskt):
    lo = _lo_mask()
    hbuf[...] = x_ref[...].astype(BF16)
    rs = _row_scale(x_ref[...])

    def proj(cols, rows=slice(None)):
        return jnp.dot(hbuf[rows, :], win_ref[:, cols], preferred_element_type=F32) * rs[rows]

    z_ref[:, O_U:] = proj(slice(O_U, D_IN_PROJ))
    uext[16:, :] = z_ref[:, O_U:O_U + D_POOL]
    pos16 = i * TB + lax.broadcasted_iota(jnp.int32, (16, LANES), 0)
    ds = []
    for g, w in enumerate(POOL_WINDOWS):
        acc = uext[:, g * LANES:(g + 1) * LANES]
        sh = 1
        while sh < w:
            acc = acc + pltpu.roll(acc, sh, axis=0)
            sh *= 2
        ug = z_ref[:, O_U + g * LANES:O_U + (g + 1) * LANES]
        cnt = jnp.minimum(w, pos16 + 1).astype(F32)
        d_first = acc[16:32, :] / cnt - ug[0:16, :]
        d_rest = acc[32:, :] * (1.0 / w) - ug[16:, :]
        ds.append(jnp.concatenate([d_first, d_rest], axis=0).astype(BF16))
    uext[0:16, :] = uext[TB:TB + 16, :]

    z_ref[:, :O_K] = proj(slice(0, O_K))

    for pr in range(len(POOL_WINDOWS) // 2):
        cols = slice(pr * 2 * LANES, (pr + 1) * 2 * LANES)
        d2 = jnp.concatenate(ds[2 * pr:2 * pr + 2], axis=1)
        po = jnp.dot(d2, wpool_ref[pr], preferred_element_type=F32) * pscale_ref[:, cols]
        gp = z_ref[:, O_GP + pr * 2 * LANES:O_GP + (pr + 1) * 2 * LANES]
        mixed[:, D_ATTN + pr * 2 * LANES:D_ATTN + (pr + 1) * 2 * LANES] = (po * _silu_half(gp)).astype(BF16)

    for j in range(4):
        qhat = _pair_rms(z_ref[:, j * LANES:(j + 1) * LANES], lo, qw_ref[...])
        q_lo = jnp.where(lo, qhat, 0.0).astype(BF16)
        q_hi = jnp.where(lo, 0.0, qhat).astype(BF16)
        c, g0 = j // 2, 2 * (j % 2)
        for sb in range(NSB):
            rows = slice(sb * BLOCK, (sb + 1) * BLOCK)
            qbuf[c, sb, g0 * BLOCK:(g0 + 1) * BLOCK, :] = q_lo[rows]
            qbuf[c, sb, (g0 + 1) * BLOCK:(g0 + 2) * BLOCK, :] = q_hi[rows]

    half = TB // 2
    for r0 in (0, half):
        z_ref[r0:r0 + half, O_K:O_GA] = proj(slice(O_K, O_GA), slice(r0, r0 + half))
    z_ref[:, O_GA:O_U] = proj(slice(O_GA, O_U))

    khat = _pair_rms(z_ref[:, O_K:O_K + LANES], lo, kw_ref[...])
    kr = pltpu.roll(khat, HEAD_DIM, axis=1)
    kbuf[0, BLOCK:, :] = jnp.where(lo, khat, kr).astype(BF16)
    kbuf[1, BLOCK:, :] = jnp.where(lo, kr, khat).astype(BF16)
    vz = z_ref[:, O_V:O_V + LANES]
    vr = pltpu.roll(vz, HEAD_DIM, axis=1)
    vbuf[0, BLOCK:, :] = jnp.where(lo, vz, vr).astype(BF16)
    vbuf[1, BLOCK:, :] = jnp.where(lo, vr, vz).astype(BF16)

    first = jnp.where(i > 0, 1, 0)
    for sb in range(NSB):
        rows = slice(sb * BLOCK, (sb + 1) * BLOCK)
        mcols = maskt[first] if sb == 0 else maskt[1]
        for c in range(2):
            keys = kbuf[c, sb * BLOCK:(sb + 2) * BLOCK, :]
            vals = vbuf[c, sb * BLOCK:(sb + 2) * BLOCK, :]
            s = lax.dot_general(jnp.concatenate([qbuf[c, sb], eye4[...]], axis=1),
                                jnp.concatenate([keys, mcols], axis=1), _NT,
                                preferred_element_type=F32)
            ps, ms, ls = [], [], []
            for g in range(GROUP):
                sg = s[g * BLOCK:(g + 1) * BLOCK, :]
                m = jnp.max(sg, axis=-1, keepdims=True)
                p = jnp.exp2(sg - m)
                ps.append(p.astype(BF16))
                ms.append(m)
                ls.append(jnp.sum(p, axis=-1, keepdims=True))
            o = jnp.dot(jnp.concatenate(ps, axis=0), vals, preferred_element_type=F32)
            for jj in range(2):
                ev, od = 2 * jj, 2 * jj + 1
                slab = jnp.where(lo, o[ev * BLOCK:(ev + 1) * BLOCK, :], o[od * BLOCK:(od + 1) * BLOCK, :])
                l = jnp.where(lo, ls[ev], ls[od]) + jnp.exp2(sink2[2 * c + jj] - jnp.where(lo, ms[ev], ms[od]))
                col = (2 * c + jj) * LANES
                ga = z_ref[rows, O_GA + col:O_GA + col + LANES]
                mixed[rows, col:col + LANES] = (slab * (1.0 / l) * _silu_half(ga)).astype(BF16)

    kbuf[:, 0:BLOCK, :] = kbuf[:, TB:TB + BLOCK, :]
    vbuf[:, 0:BLOCK, :] = vbuf[:, TB:TB + BLOCK, :]

    y_ref[...] = x_ref[...] + jnp.dot(mixed[...], wout_ref[...], preferred_element_type=F32)

    @pl.when(last)
    def _():
        unew_ref[:, 0, :] = z_ref[TB - POOL_STATE:TB, O_U:O_U + D_POOL]
        tail = slice(TB - WINDOW, TB)
        knew_ref[...] = _pair_rms(z_ref[tail, O_K:O_K + LANES], lo, kw_ref[...]).T
        vnew_ref[...] = z_ref[tail, O_V:O_V + LANES].T


def _sample_init(x_ref, win_ref, qw_ref, kw_ref, z_ref, q3, krep, vrep, snew, kvt):
    nb = x_ref.shape[0]
    lo = _lo_mask()
    z_ref[...] = _project(x_ref[:, 0, :], win_ref[...])
    khat = _pair_rms(z_ref[:, O_K:O_K + LANES], lo, kw_ref[...])
    vnew = z_ref[:, O_V:O_V + LANES]
    kvt[0] = khat.T
    kvt[1] = vnew.T
    q3[...] = jnp.zeros(q3.shape, F32)
    krep[...] = jnp.zeros(krep.shape, F32)
    vrep[...] = jnp.zeros(vrep.shape, F32)
    for j in range(4):
        qhat = _pair_rms(z_ref[:, j * LANES:(j + 1) * LANES], lo, qw_ref[...])
        qrot = pltpu.roll(qhat, HEAD_DIM, axis=1)
        grp_lo = j < 2
        for half in range(2):
            h = 2 * j + half
            src = qhat if (half == 0) == grp_lo else qrot
            q3[pl.ds(h, nb, stride=QROWS), :] = jnp.where(lo if grp_lo else ~lo, src, 0.0)
            krep[pl.ds(h, nb, stride=QROWS), :] = khat
            vrep[pl.ds(h, nb, stride=QROWS), :] = vnew
    s_new = jnp.sum(q3[...] * krep[...], axis=-1, keepdims=True)
    snew[...] = jnp.broadcast_to(s_new, snew.shape)


def _shift_caches(i, nb, ck_ref, cv_ref, knew_ref, vnew_ref, kvt):
    shift = lax.rem(nb - i * CB, nb)
    newest = lax.broadcasted_iota(jnp.int32, (LANES, WINDOW), 1) == WINDOW - 1
    for src, dst, which in ((ck_ref, knew_ref, 0), (cv_ref, vnew_ref, 1)):
        cols = pltpu.roll(kvt[which], shift, axis=1)
        shifted = pltpu.roll(src[...], WINDOW - 1, axis=2)
        for bb in range(CB):
            dst[bb] = jnp.where(newest, cols[:, bb:bb + 1], shifted[bb])


def _sample_step(j, last, sinks_ref, x_ref, pscale_ref, ck_ref, cv_ref, sp_ref, y_ref, spnew_ref,
                 wout_ref, wpool_ref, z_ref, q3, krep, vrep, snew, kvt, o3, mixed):
    nb = x_ref.shape[0]
    lo = _lo_mask()
    row0 = pl.multiple_of(j * BB, BB)
    rows = pl.ds(row0, BB)
    qrows = pl.ds(pl.multiple_of(j * (BB * QROWS), BB * QROWS), BB * QROWS)
    rid = lax.broadcasted_iota(jnp.int32, (1, QROWS, 1), 1)
    sink3 = jnp.zeros((1, QROWS, 1), F32)
    for h in range(N_HEADS):
        sink3 = jnp.where(rid == h, sinks_ref[h] * LOG2E, sink3)

    q = q3[qrows, :].reshape(BB, QROWS, LANES)
    s_new = snew[qrows, :].reshape(BB, QROWS, LANES)
    kt = ck_ref[...]
    vt = cv_ref[...]
    s = lax.dot_general(q.astype(BF16), kt.astype(BF16), (((2,), (1,)), ((0,), (0,))), preferred_element_type=F32)
    m = jnp.maximum(jnp.maximum(jnp.max(s, axis=-1, keepdims=True), s_new), sink3)
    p = jnp.exp2(s - m)
    p_new = jnp.exp2(s_new - m)
    l = jnp.sum(p, axis=-1, keepdims=True) + p_new + jnp.exp2(sink3 - m)
    o = lax.dot_general(p.astype(BF16), vt.astype(BF16), (((2,), (2,)), ((0,), (0,))), preferred_element_type=F32)
    o = (o + p_new * vrep[qrows, :].reshape(BB, QROWS, LANES)) / l
    o3[...] = o.reshape(BB * QROWS, LANES)

    for jj in range(4):
        grp_lo = jj < 2
        oh = [o3[pl.ds(2 * jj + half, BB, stride=QROWS), :] for half in range(2)]
        a = oh[0] if grp_lo else pltpu.roll(oh[0], HEAD_DIM, axis=1)
        b = pltpu.roll(oh[1], HEAD_DIM, axis=1) if grp_lo else oh[1]
        ga = z_ref[rows, O_GA + jj * LANES:O_GA + (jj + 1) * LANES]
        mixed[rows, jj * LANES:(jj + 1) * LANES] = (jnp.where(lo, a, b) * _silu_half(ga)).astype(BF16)

    u = z_ref[rows, O_U:O_U + D_POOL]
    spnew_ref[0:POOL_STATE - 1] = sp_ref[1:POOL_STATE]
    spnew_ref[POOL_STATE - 1] = u
    ds = []
    for g, w in enumerate(POOL_WINDOWS):
        cols = slice(g * LANES, (g + 1) * LANES)
        ug = u[:, cols]
        win_sum = ug
        for r in range(POOL_STATE - (w - 1), POOL_STATE):
            win_sum = win_sum + sp_ref[r, :, cols]
        ds.append((win_sum * (1.0 / w) - ug).astype(BF16))
    for pr in range(len(POOL_WINDOWS) // 2):
        cols = slice(pr * 2 * LANES, (pr + 1) * 2 * LANES)
        d2 = jnp.concatenate(ds[2 * pr:2 * pr + 2], axis=1)
        po = jnp.dot(d2, wpool_ref[pr], preferred_element_type=F32) * pscale_ref[:, cols]
        gp = z_ref[rows, O_GP + pr * 2 * LANES:O_GP + (pr + 1) * 2 * LANES]
        mixed[rows, D_ATTN + pr * 2 * LANES:D_ATTN + (pr + 1) * 2 * LANES] = (po * _silu_half(gp)).astype(BF16)

    @pl.when(last)
    def _():
        y_ref[:, 0, :] = x_ref[:, 0, :] + jnp.dot(mixed[...], wout_ref[...], preferred_element_type=F32)


def _fused_kernel(n_prompt, sinks_ref, xp_ref, xs_ref, nw_ref, win_hbm, qn_ref, kn_ref, wpool_f32, pscale_ref, wout_hbm,
                  ck_ref, cv_ref, sp_ref, ckc_ref, cvc_ref,
                  yp_ref, kp_ref, vp_ref, up_ref, ys_ref, kq_ref, vq_ref, pq_ref,
                  win_ref, wout_ref, wpool_ref, qw_ref, kw_ref, stage, sem,
                  z_ref, hbuf, qbuf, kbuf, vbuf, uext, mixed, sink2, eye4, maskt,
                  zs_ref, q3, krep, vrep, snew, kvt, o3, mixed_s):
    i = pl.program_id(0)

    @pl.when(i == 0)
    def _():
        _load_weights(win_hbm, wout_hbm, wpool_f32, nw_ref, qn_ref, kn_ref, stage, sem, win_ref, wout_ref, wpool_ref,
                      qw_ref, kw_ref)
        _prompt_init(sinks_ref, kbuf, vbuf, uext, sink2, eye4, maskt)
        _sample_init(xs_ref, win_ref, qw_ref, kw_ref, zs_ref, q3, krep, vrep, snew, kvt)

    @pl.when(i < n_prompt)
    def _():
        _shift_caches(i, xs_ref.shape[0], ckc_ref, cvc_ref, kq_ref, vq_ref, kvt)
        _prompt_step(i, i == n_prompt - 1, xp_ref, pscale_ref, yp_ref, kp_ref, vp_ref, up_ref,
                     win_ref, wout_ref, wpool_ref, qw_ref, kw_ref, z_ref, hbuf, qbuf, kbuf, vbuf, uext, mixed, sink2, eye4, maskt)

    @pl.when(i >= n_prompt)
    def _():
        _sample_step(i - n_prompt, i == pl.num_programs(0) - 1, sinks_ref, xs_ref, pscale_ref, ck_ref, cv_ref, sp_ref,
                     ys_ref, pq_ref, wout_ref, wpool_ref, zs_ref, q3, krep, vrep, snew, kvt, o3, mixed_s)


def _fused_call(sinks, xp, xs, nw, win, qn, kn, wpool, pscale, wout, ck, cv, sp):
    seq, nb = xp.shape[0], xs.shape[0]
    n_prompt, n_sample = seq // TB, nb // BB
    assert nb == CB * n_prompt
    const2 = lambda i: (0, 0)
    const3 = lambda i: (0, 0, 0)
    pblk = lambda i: (jnp.minimum(i, n_prompt - 1), 0)
    cblk = lambda i: (jnp.minimum(i, n_prompt - 1), 0, 0)
    sidx = lambda i: jnp.clip(i - n_prompt, 0, n_sample - 1)
    chunk3 = lambda i: (sidx(i), 0, 0)
    mid3 = lambda i: (0, sidx(i), 0)
    once = pl.Buffered(1)
    return pl.pallas_call(
        functools.partial(_fused_kernel, n_prompt),
        grid=(n_prompt + n_sample,),
        in_specs=[
            pl.BlockSpec(memory_space=pltpu.SMEM),
            pl.BlockSpec((TB, D_MODEL), pblk),
            pl.BlockSpec((nb, 1, D_MODEL), const3, pipeline_mode=once),
            pl.BlockSpec((1, D_MODEL), const2),
            pl.BlockSpec(memory_space=pl.ANY),
            pl.BlockSpec((1, HEAD_DIM), const2),
            pl.BlockSpec((1, HEAD_DIM), const2),
            pl.BlockSpec((4, LANES, LANES), const3, pipeline_mode=once),
            pl.BlockSpec((1, D_POOL), const2),
            pl.BlockSpec(memory_space=pl.ANY),
            pl.BlockSpec((BB, LANES, WINDOW), chunk3),
            pl.BlockSpec((BB, LANES, WINDOW), chunk3),
            pl.BlockSpec((POOL_STATE, BB, D_POOL), mid3),
            pl.BlockSpec((CB, LANES, WINDOW), cblk),
            pl.BlockSpec((CB, LANES, WINDOW), cblk),
        ],
        out_specs=[
            pl.BlockSpec((TB, D_MODEL), pblk),
            pl.BlockSpec((LANES, WINDOW), const2),
            pl.BlockSpec((LANES, WINDOW), const2),
            pl.BlockSpec((POOL_STATE, 1, D_POOL), const3),
            pl.BlockSpec((nb, 1, D_MODEL), const3, pipeline_mode=once),
            pl.BlockSpec((CB, LANES, WINDOW), cblk),
            pl.BlockSpec((CB, LANES, WINDOW), cblk),
            pl.BlockSpec((POOL_STATE, BB, D_POOL), mid3),
        ],
        out_shape=[
            jax.ShapeDtypeStruct((seq, D_MODEL), F32),
            jax.ShapeDtypeStruct((LANES, WINDOW), F32),
            jax.ShapeDtypeStruct((LANES, WINDOW), F32),
            jax.ShapeDtypeStruct((POOL_STATE, 1, D_POOL), F32),
            jax.ShapeDtypeStruct((nb, 1, D_MODEL), F32),
            jax.ShapeDtypeStruct((nb, LANES, WINDOW), F32),
            jax.ShapeDtypeStruct((nb, LANES, WINDOW), F32),
            jax.ShapeDtypeStruct((POOL_STATE, nb, D_POOL), F32),
        ],
        scratch_shapes=[
            pltpu.VMEM((D_MODEL, D_IN_PROJ), BF16),
            pltpu.VMEM((D_MODEL, D_MODEL), BF16),
            pltpu.VMEM((2, 2 * LANES, 2 * LANES), BF16),
            pltpu.VMEM((1, LANES), F32),
            pltpu.VMEM((1, LANES), F32),
            pltpu.VMEM((NSTAGE, D_MODEL, WCHUNK), F32),
            pltpu.SemaphoreType.DMA((NSTAGE,)),
            pltpu.VMEM((TB, D_IN_PROJ), F32),
            pltpu.VMEM((TB, D_MODEL), BF16),
            pltpu.VMEM((2, NSB, GROUP * BLOCK, LANES), BF16),
            pltpu.VMEM((2, TB + BLOCK, LANES), BF16),
            pltpu.VMEM((2, TB + BLOCK, LANES), BF16),
            pltpu.VMEM((TB + 16, D_POOL), F32),
            pltpu.VMEM((TB, D_MODEL), BF16),
            pltpu.VMEM((N_HEADS // 2, 1, LANES), F32),
            pltpu.VMEM((GROUP * BLOCK, LANES), BF16),
            pltpu.VMEM((2, 2 * BLOCK, LANES), BF16),
            pltpu.VMEM((nb, D_IN_PROJ), F32),
            pltpu.VMEM((nb * QROWS, LANES), F32),
            pltpu.VMEM((nb * QROWS, LANES), F32),
            pltpu.VMEM((nb * QROWS, LANES), F32),
            pltpu.VMEM((nb * QROWS, LANES), F32),
            pltpu.VMEM((2, LANES, nb), F32),
            pltpu.VMEM((BB * QROWS, LANES), F32),
            pltpu.VMEM((nb, D_MODEL), BF16),
        ],
        compiler_params=pltpu.CompilerParams(dimension_semantics=("arbitrary",), vmem_limit_bytes=VMEM_LIMIT),
        name="hybrid_step",
    )(sinks, xp, xs, nw, win, qn, kn, wpool, pscale, wout, ck, cv, sp, ck, cv)


def _cache_in(c):
    nb = c.shape[0]
    return jnp.transpose(c, (0, 2, 3, 1)).reshape(nb, LANES, WINDOW)


def _cache_out(c):
    nb = c.shape[0]
    return jnp.transpose(c.reshape(nb, N_KV_HEADS, HEAD_DIM, WINDOW), (0, 3, 1, 2))


def kernel(x_prompt, x_sample, cache_k, cache_v, state_pool, norm_w, w_in, q_norm_w, k_norm_w, sinks, w_pool,
           pool_scale, w_out):
    depth = norm_w.shape[0]
    assert depth == 1 and x_prompt.shape[0] == 1 and x_sample.shape[1] == 1
    seq = x_prompt.shape[1]
    nb = x_sample.shape[0]
    assert seq % TB == 0 and nb % BB == 0 and nb == LANES

    yp, kp, vp, up, ys, kq, vq, pq = _fused_call(
        sinks[0], x_prompt[0], x_sample, norm_w, w_in[0], q_norm_w, k_norm_w, w_pool[0], pool_scale, w_out[0],
        _cache_in(cache_k[0]), _cache_in(cache_v[0]), jnp.transpose(state_pool[0], (1, 0, 2)))
    return (yp[None], ys, _cache_out(kp[None])[None], _cache_out(vp[None])[None],
            jnp.transpose(up, (1, 0, 2))[None],
            _cache_out(kq)[None], _cache_out(vq)[None], jnp.transpose(pq, (1, 0, 2))[None])
```

```python
import functools

import jax
import jax.numpy as jnp
from jax import lax
from jax.experimental import pallas as pl
from jax.experimental.pallas import tpu as pltpu

D_MODEL = 1024
HEAD_DIM = 64
N_HEADS = 8
N_KV_HEADS = 2
GROUP = 4
WINDOW = 128
BLOCK = 128
POOL_WINDOWS = (2, 4, 8, 16)
POOL_STATE = 15
D_ATTN = 512
D_POOL = 512
D_IN_PROJ = 2304
EPS = 1e-6
NEG_INF = -1e30
LOG2E = 1.4426950408889634

O_K = 512
O_V = 640
O_GA = 768
O_U = 1280
O_GP = 1792

LANES = 128
TB = 512
NSB = TB // BLOCK
BB = 32
CB = 4
QROWS = 16
WCHUNK = 256
NSTAGE = 8
VMEM_LIMIT = 58 * 1024 * 1024

F32 = jnp.float32
BF16 = jnp.bfloat16
_NT = (((1,), (1,)), ((), ()))


def _silu_half(h):
    return h + h * jnp.tanh(h)


def _lo_mask():
    return lax.broadcasted_iota(jnp.int32, (1, LANES), 1) < HEAD_DIM


def _pair_rms(zs, lo, w2):
    sq = zs * zs
    s_lo = jnp.sum(jnp.where(lo, sq, 0.0), axis=-1, keepdims=True)
    s_hi = jnp.sum(jnp.where(lo, 0.0, sq), axis=-1, keepdims=True)
    r = lax.rsqrt(jnp.where(lo, s_lo, s_hi) * (1.0 / HEAD_DIM) + EPS)
    return zs * r * w2


def _row_scale(x):
    return lax.rsqrt(jnp.mean(x * x, axis=-1, keepdims=True) + EPS)


def _project(x, win):
    return jnp.dot(x.astype(BF16), win, preferred_element_type=F32) * _row_scale(x)


def _weight_chunks():
    chunks = []
    for c0 in range(0, D_IN_PROJ, WCHUNK):
        gate = O_GA <= c0 < O_U or c0 >= O_GP
        chunks.append((0, c0, 0.5 if gate else 1.0))
    chunks += [(1, c0, 1.0) for c0 in range(0, D_MODEL, WCHUNK)]
    return chunks


def _load_weights(win_hbm, wout_hbm, wpool_f32, nw_ref, qn_ref, kn_ref, stage, sem, win_bf, wout_bf, wpool_bf,
                  qw_ref, kw_ref):
    chunks = _weight_chunks()
    srcs, dsts = (win_hbm, wout_hbm), (win_bf, wout_bf)

    def copy(k):
        which, c0, _ = chunks[k]
        return pltpu.make_async_copy(srcs[which].at[:, pl.ds(c0, WCHUNK)], stage.at[k % NSTAGE], sem.at[k % NSTAGE])

    for k in range(NSTAGE):
        copy(k).start()
    nw_tile = jnp.concatenate([jnp.broadcast_to(nw_ref[:, t * LANES:(t + 1) * LANES], (LANES, LANES)).T
                               for t in range(D_MODEL // LANES)], axis=0)
    nw_rows = jnp.concatenate([nw_tile] * (WCHUNK // LANES), axis=1)
    for k, (which, c0, scale) in enumerate(chunks):
        copy(k).wait()
        w = stage[k % NSTAGE]
        if which == 0:
            w = w * nw_rows if scale == 1.0 else w * (scale * nw_rows)
        dsts[which][:, c0:c0 + WCHUNK] = w.astype(BF16)
        if k + NSTAGE < len(chunks):
            copy(k + NSTAGE).start()
    wpool_bf[...] = jnp.zeros(wpool_bf.shape, BF16)
    for g in range(len(POOL_WINDOWS)):
        d0 = (g % 2) * LANES
        wpool_bf[g // 2, d0:d0 + LANES, d0:d0 + LANES] = wpool_f32[g].astype(BF16)
    qn = qn_ref[...] * (HEAD_DIM ** -0.5 * LOG2E)
    qw_ref[...] = jnp.concatenate([qn, qn], axis=1)
    kw_ref[...] = jnp.concatenate([kn_ref[...], kn_ref[...]], axis=1)


def _prompt_init(sinks_ref, kbuf, vbuf, uext, sink2, eye4, maskt):
    kbuf[:, 0:BLOCK, :] = jnp.zeros((2, BLOCK, LANES), BF16)
    vbuf[:, 0:BLOCK, :] = jnp.zeros((2, BLOCK, LANES), BF16)
    uext[0:16, :] = jnp.zeros((16, D_POOL), F32)
    lo = _lo_mask()
    for jj in range(N_HEADS // 2):
        sink2[jj] = jnp.where(lo, sinks_ref[2 * jj], sinks_ref[2 * jj + 1]) * LOG2E
    rr = lax.broadcasted_iota(jnp.int32, (GROUP * BLOCK, LANES), 0)
    ll = lax.broadcasted_iota(jnp.int32, (GROUP * BLOCK, LANES), 1)
    eye4[...] = jnp.where((rr & (BLOCK - 1)) == ll, 1.0, 0.0).astype(BF16)
    cc = lax.broadcasted_iota(jnp.int32, (2 * BLOCK, LANES), 0)
    qr = lax.broadcasted_iota(jnp.int32, (2 * BLOCK, LANES), 1)
    band = (cc >= qr) & (cc <= qr + WINDOW)
    maskt[0] = jnp.where(band & (cc >= BLOCK), 0.0, NEG_INF).astype(BF16)
    maskt[1] = jnp.where(band, 0.0, NEG_INF).astype(BF16)


def _prompt_step(i, last, x_ref, pscale_ref, y_ref, knew_ref, vnew_ref, unew_ref,
                 win_ref, wout_ref, wpool_ref, qw_ref, kw_ref, z_ref, hbuf, qbuf, kbuf, vbuf, uext, mixed, sink2, eye4, maskt):
    lo = _lo_mask()
    hbuf[...] = x_ref[...].astype(BF16)
    rs = _row_scale(x_ref[...])

    def proj(cols, rows=slice(None)):
        return jnp.dot(hbuf[rows, :], win_ref[:, cols], preferred_element_type=F32) * rs[rows]

    z_ref[:, O_U:] = proj(slice(O_U, D_IN_PROJ))
    uext[16:, :] = z_ref[:, O_U:O_U + D_POOL]
    pos16 = i * TB + lax.broadcasted_iota(jnp.int32, (16, LANES), 0)
    ds = []
    for g, w in enumerate(POOL_WINDOWS):
        acc = uext[:, g * LANES:(g + 1) * LANES]
        sh = 1
        while sh < w:
            acc = acc + pltpu.roll(acc, sh, axis=0)
            sh *= 2
        ug = z_ref[:, O_U + g * LANES:O_U + (g + 1) * LANES]
        cnt = jnp.minimum(w, pos16 + 1).astype(F32)
        d_first = acc[16:32, :] / cnt - ug[0:16, :]
        d_rest = acc[32:, :] * (1.0 / w) - ug[16:, :]
        ds.append(jnp.concatenate([d_first, d_rest], axis=0).astype(BF16))
    uext[0:16, :] = uext[TB:TB + 16, :]

    z_ref[:, :O_K] = proj(slice(0, O_K))

    for pr in range(len(POOL_WINDOWS) // 2):
        cols = slice(pr * 2 * LANES, (pr + 1) * 2 * LANES)
        d2 = jnp.concatenate(ds[2 * pr:2 * pr + 2], axis=1)
        po = jnp.dot(d2, wpool_ref[pr], preferred_element_type=F32) * pscale_ref[:, cols]
        gp = z_ref[:, O_GP + pr * 2 * LANES:O_GP + (pr + 1) * 2 * LANES]
        mixed[:, D_ATTN + pr * 2 * LANES:D_ATTN + (pr + 1) * 2 * LANES] = (po * _silu_half(gp)).astype(BF16)

    for j in range(4):
        qhat = _pair_rms(z_ref[:, j * LANES:(j + 1) * LANES], lo, qw_ref[...])
        q_lo = jnp.where(lo, qhat, 0.0).astype(BF16)
        q_hi = jnp.where(lo, 0.0, qhat).astype(BF16)
        c, g0 = j // 2, 2 * (j % 2)
        for sb in range(NSB):
            rows = slice(sb * BLOCK, (sb + 1) * BLOCK)
            qbuf[c, sb, g0 * BLOCK:(g0 + 1) * BLOCK, :] = q_lo[rows]
            qbuf[c, sb, (g0 + 1) * BLOCK:(g0 + 2) * BLOCK, :] = q_hi[rows]

    half = TB // 2
    for r0 in (0, half):
        z_ref[r0:r0 + half, O_K:O_GA] = proj(slice(O_K, O_GA), slice(r0, r0 + half))
    z_ref[:, O_GA:O_U] = proj(slice(O_GA, O_U))

    khat = _pair_rms(z_ref[:, O_K:O_K + LANES], lo, kw_ref[...])
    kr = pltpu.roll(khat, HEAD_DIM, axis=1)
    kbuf[0, BLOCK:, :] = jnp.where(lo, khat, kr).astype(BF16)
    kbuf[1, BLOCK:, :] = jnp.where(lo, kr, khat).astype(BF16)
    vz = z_ref[:, O_V:O_V + LANES]
    vr = pltpu.roll(vz, HEAD_DIM, axis=1)
    vbuf[0, BLOCK:, :] = jnp.where(lo, vz, vr).astype(BF16)
    vbuf[1, BLOCK:, :] = jnp.where(lo, vr, vz).astype(BF16)

    first = jnp.where(i > 0, 1, 0)
    for sb in range(NSB):
        rows = slice(sb * BLOCK, (sb + 1) * BLOCK)
        mcols = maskt[first] if sb == 0 else maskt[1]
        for c in range(2):
            keys = kbuf[c, sb * BLOCK:(sb + 2) * BLOCK, :]
            vals = vbuf[c, sb * BLOCK:(sb + 2) * BLOCK, :]
            s = lax.dot_general(jnp.concatenate([qbuf[c, sb], eye4[...]], axis=1),
                                jnp.concatenate([keys, mcols], axis=1), _NT,
                                preferred_element_type=F32)
            ps, ms, ls = [], [], []
            for g in range(GROUP):
                sg = s[g * BLOCK:(g + 1) * BLOCK, :]
                m = jnp.max(sg, axis=-1, keepdims=True)
                p = jnp.exp2(sg - m)
                ps.append(p.astype(BF16))
                ms.append(m)
                ls.append(jnp.sum(p, axis=-1, keepdims=True))
            o = jnp.dot(jnp.concatenate(ps, axis=0), vals, preferred_element_type=F32)
            for jj in range(2):
                ev, od = 2 * jj, 2 * jj + 1
                slab = jnp.where(lo, o[ev * BLOCK:(ev + 1) * BLOCK, :], o[od * BLOCK:(od + 1) * BLOCK, :])
                l = jnp.where(lo, ls[ev], ls[od]) + jnp.exp2(sink2[2 * c + jj] - jnp.where(lo, ms[ev], ms[od]))
                col = (2 * c + jj) * LANES
                ga = z_ref[rows, O_GA + col:O_GA + col + LANES]
                mixed[rows, col:col + LANES] = (slab * (1.0 / l) * _silu_half(ga)).astype(BF16)

    kbuf[:, 0:BLOCK, :] = kbuf[:, TB:TB + BLOCK, :]
    vbuf[:, 0:BLOCK, :] = vbuf[:, TB:TB + BLOCK, :]

    y_ref[...] = x_ref[...] + jnp.dot(mixed[...], wout_ref[...], preferred_element_type=F32)

    @pl.when(last)
    def _():
        unew_ref[:, 0, :] = z_ref[TB - POOL_STATE:TB, O_U:O_U + D_POOL]
        tail = slice(TB - WINDOW, TB)
        knew_ref[...] = _pair_rms(z_ref[tail, O_K:O_K + LANES], lo, kw_ref[...]).T
        vnew_ref[...] = z_ref[tail, O_V:O_V + LANES].T


def _sample_init(x_ref, win_ref, qw_ref, kw_ref, z_ref, q3, krep, vrep, snew, kvt):
    nb = x_ref.shape[0]
    lo = _lo_mask()
    z_ref[...] = _project(x_ref[:, 0, :], win_ref[...])
    khat = _pair_rms(z_ref[:, O_K:O_K + LANES], lo, kw_ref[...])
    vnew = z_ref[:, O_V:O_V + LANES]
    kvt[0] = khat.T
    kvt[1] = vnew.T
    q3[...] = jnp.zeros(q3.shape, F32)
    krep[...] = jnp.zeros(krep.shape, F32)
    vrep[...] = jnp.zeros(vrep.shape, F32)
    for j in range(4):
        qhat = _pair_rms(z_ref[:, j * LANES:(j + 1) * LANES], lo, qw_ref[...])
        qrot = pltpu.roll(qhat, HEAD_DIM, axis=1)
        grp_lo = j < 2
        for half in range(2):
            h = 2 * j + half
            src = qhat if (half == 0) == grp_lo else qrot
            q3[pl.ds(h, nb, stride=QROWS), :] = jnp.where(lo if grp_lo else ~lo, src, 0.0)
            krep[pl.ds(h, nb, stride=QROWS), :] = khat
            vrep[pl.ds(h, nb, stride=QROWS), :] = vnew
    s_new = jnp.sum(q3[...] * krep[...], axis=-1, keepdims=True)
    snew[...] = jnp.broadcast_to(s_new, snew.shape)


def _shift_caches(i, nb, ck_ref, cv_ref, knew_ref, vnew_ref, kvt):
    shift = lax.rem(nb - i * CB, nb)
    newest = lax.broadcasted_iota(jnp.int32, (LANES, WINDOW), 1) == WINDOW - 1
    for src, dst, which in ((ck_ref, knew_ref, 0), (cv_ref, vnew_ref, 1)):
        cols = pltpu.roll(kvt[which], shift, axis=1)
        shifted = pltpu.roll(src[...], WINDOW - 1, axis=2)
        for bb in range(CB):
            dst[bb] = jnp.where(newest, cols[:, bb:bb + 1], shifted[bb])


def _sample_step(j, last, sinks_ref, x_ref, pscale_ref, ck_ref, cv_ref, sp_ref, y_ref, spnew_ref,
                 wout_ref, wpool_ref, z_ref, q3, krep, vrep, snew, kvt, o3, mixed):
    nb = x_ref.shape[0]
    lo = _lo_mask()
    row0 = pl.multiple_of(j * BB, BB)
    rows = pl.ds(row0, BB)
    qrows = pl.ds(pl.multiple_of(j * (BB * QROWS), BB * QROWS), BB * QROWS)
    rid = lax.broadcasted_iota(jnp.int32, (1, QROWS, 1), 1)
    sink3 = jnp.zeros((1, QROWS, 1), F32)
    for h in range(N_HEADS):
        sink3 = jnp.where(rid == h, sinks_ref[h] * LOG2E, sink3)

    q = q3[qrows, :].reshape(BB, QROWS, LANES)
    s_new = snew[qrows, :].reshape(BB, QROWS, LANES)
    kt = ck_ref[...]
    vt = cv_ref[...]
    s = lax.dot_general(q.astype(BF16), kt.astype(BF16), (((2,), (1,)), ((0,), (0,))), preferred_element_type=F32)
    m = jnp.maximum(jnp.maximum(jnp.max(s, axis=-1, keepdims=True), s_new), sink3)
    p = jnp.exp2(s - m)
    p_new = jnp.exp2(s_new - m)
    l = jnp.sum(p, axis=-1, keepdims=True) + p_new + jnp.exp2(sink3 - m)
    o = lax.dot_general(p.astype(BF16), vt.astype(BF16), (((2,), (2,)), ((0,), (0,))), preferred_element_type=F32)
    o = (o + p_new * vrep[qrows, :].reshape(BB, QROWS, LANES)) / l
    o3[...] = o.reshape(BB * QROWS, LANES)

    for jj in range(4):
        grp_lo = jj < 2
        oh = [o3[pl.ds(2 * jj + half, BB, stride=QROWS), :] for half in range(2)]
        a = oh[0] if grp_lo else pltpu.roll(oh[0], HEAD_DIM, axis=1)
        b = pltpu.roll(oh[1], HEAD_DIM, axis=1) if grp_lo else oh[1]
        ga = z_ref[rows, O_GA + jj * LANES:O_GA + (jj + 1) * LANES]
        mixed[rows, jj * LANES:(jj + 1) * LANES] = (jnp.where(lo, a, b) * _silu_half(ga)).astype(BF16)

    u = z_ref[rows, O_U:O_U + D_POOL]
    spnew_ref[0:POOL_STATE - 1] = sp_ref[1:POOL_STATE]
    spnew_ref[POOL_STATE - 1] = u
    ds = []
    for g, w in enumerate(POOL_WINDOWS):
        cols = slice(g * LANES, (g + 1) * LANES)
        ug = u[:, cols]
        win_sum = ug
        for r in range(POOL_STATE - (w - 1), POOL_STATE):
            win_sum = win_sum + sp_ref[r, :, cols]
        ds.append((win_sum * (1.0 / w) - ug).astype(BF16))
    for pr in range(len(POOL_WINDOWS) // 2):
        cols = slice(pr * 2 * LANES, (pr + 1) * 2 * LANES)
        d2 = jnp.concatenate(ds[2 * pr:2 * pr + 2], axis=1)
        po = jnp.dot(d2, wpool_ref[pr], preferred_element_type=F32) * pscale_ref[:, cols]
        gp = z_ref[rows, O_GP + pr * 2 * LANES:O_GP + (pr + 1) * 2 * LANES]
        mixed[rows, D_ATTN + pr * 2 * LANES:D_ATTN + (pr + 1) * 2 * LANES] = (po * _silu_half(gp)).astype(BF16)

    @pl.when(last)
    def _():
        y_ref[:, 0, :] = x_ref[:, 0, :] + jnp.dot(mixed[...], wout_ref[...], preferred_element_type=F32)


def _fused_kernel(n_prompt, sinks_ref, xp_ref, xs_ref, nw_ref, win_hbm, qn_ref, kn_ref, wpool_f32, pscale_ref, wout_hbm,
                  ck_ref, cv_ref, sp_ref, ckc_ref, cvc_ref,
                  yp_ref, kp_ref, vp_ref, up_ref, ys_ref, kq_ref, vq_ref, pq_ref,
                  win_ref, wout_ref, wpool_ref, qw_ref, kw_ref, stage, sem,
                  z_ref, hbuf, qbuf, kbuf, vbuf, uext, mixed, sink2, eye4, maskt,
                  zs_ref, q3, krep, vrep, snew, kvt, o3, mixed_s):
    i = pl.program_id(0)

    @pl.when(i == 0)
    def _():
        _load_weights(win_hbm, wout_hbm, wpool_f32, nw_ref, qn_ref, kn_ref, stage, sem, win_ref, wout_ref, wpool_ref,
                      qw_ref, kw_ref)
        _prompt_init(sinks_ref, kbuf, vbuf, uext, sink2, eye4, maskt)
        _sample_init(xs_ref, win_ref, qw_ref, kw_ref, zs_ref, q3, krep, vrep, snew, kvt)

    @pl.when(i < n_prompt)
    def _():
        _shift_caches(i, xs_ref.shape[0], ckc_ref, cvc_ref, kq_ref, vq_ref, kvt)
        _prompt_step(i, i == n_prompt - 1, xp_ref, pscale_ref, yp_ref, kp_ref, vp_ref, up_ref,
                     win_ref, wout_ref, wpool_ref, qw_ref, kw_ref, z_ref, hbuf, qbuf, kbuf, vbuf, uext, mixed, sink2, eye4, maskt)

    @pl.when(i >= n_prompt)
    def _():
        _sample_step(i - n_prompt, i == pl.num_programs(0) - 1, sinks_ref, xs_ref, pscale_ref, ck_ref, cv_ref, sp_ref,
                     ys_ref, pq_ref, wout_ref, wpool_ref, zs_ref, q3, krep, vrep, snew, kvt, o3, mixed_s)


def _fused_call(sinks, xp, xs, nw, win, qn, kn, wpool, pscale, wout, ck, cv, sp):
    seq, nb = xp.shape[0], xs.shape[0]
    n_prompt, n_sample = seq // TB, nb // BB
    assert nb == CB * n_prompt
    const2 = lambda i: (0, 0)
    const3 = lambda i: (0, 0, 0)
    pblk = lambda i: (jnp.minimum(i, n_prompt - 1), 0)
    cblk = lambda i: (jnp.minimum(i, n_prompt - 1), 0, 0)
    sidx = lambda i: jnp.clip(i - n_prompt, 0, n_sample - 1)
    chunk3 = lambda i: (sidx(i), 0, 0)
    mid3 = lambda i: (0, sidx(i), 0)
    once = pl.Buffered(1)
    return pl.pallas_call(
        functools.partial(_fused_kernel, n_prompt),
        grid=(n_prompt + n_sample,),
        in_specs=[
            pl.BlockSpec(memory_space=pltpu.SMEM),
            pl.BlockSpec((TB, D_MODEL), pblk),
            pl.BlockSpec((nb, 1, D_MODEL), const3, pipeline_mode=once),
            pl.BlockSpec((1, D_MODEL), const2),
            pl.BlockSpec(memory_space=pl.ANY),
            pl.BlockSpec((1, HEAD_DIM), const2),
            pl.BlockSpec((1, HEAD_DIM), const2),
            pl.BlockSpec((4, LANES, LANES), const3, pipeline_mode=once),
            pl.BlockSpec((1, D_POOL), const2),
            pl.BlockSpec(memory_space=pl.ANY),
            pl.BlockSpec((BB, LANES, WINDOW), chunk3),
            pl.BlockSpec((BB, LANES, WINDOW), chunk3),
            pl.BlockSpec((POOL_STATE, BB, D_POOL), mid3),
            pl.BlockSpec((CB, LANES, WINDOW), cblk),
            pl.BlockSpec((CB, LANES, WINDOW), cblk),
        ],
        out_specs=[
            pl.BlockSpec((TB, D_MODEL), pblk),
            pl.BlockSpec((LANES, WINDOW), const2),
            pl.BlockSpec((LANES, WINDOW), const2),
            pl.BlockSpec((POOL_STATE, 1, D_POOL), const3),
            pl.BlockSpec((nb, 1, D_MODEL), const3, pipeline_mode=once),
            pl.BlockSpec((CB, LANES, WINDOW), cblk),
            pl.BlockSpec((CB, LANES, WINDOW), cblk),
            pl.BlockSpec((POOL_STATE, BB, D_POOL), mid3),
        ],
        out_shape=[
            jax.ShapeDtypeStruct((seq, D_MODEL), F32),
            jax.ShapeDtypeStruct((LANES, WINDOW), F32),
            jax.ShapeDtypeStruct((LANES, WINDOW), F32),
            jax.ShapeDtypeStruct((POOL_STATE, 1, D_POOL), F32),
            jax.ShapeDtypeStruct((nb, 1, D_MODEL), F32),
            jax.ShapeDtypeStruct((nb, LANES, WINDOW), F32),
            jax.ShapeDtypeStruct((nb, LANES, WINDOW), F32),
            jax.ShapeDtypeStruct((POOL_STATE, nb, D_POOL), F32),
        ],
        scratch_shapes=[
            pltpu.VMEM((D_MODEL, D_IN_PROJ), BF16),
            pltpu.VMEM((D_MODEL, D_MODEL), BF16),
            pltpu.VMEM((2, 2 * LANES, 2 * LANES), BF16),
            pltpu.VMEM((1, LANES), F32),
            pltpu.VMEM((1, LANES), F32),
            pltpu.VMEM((NSTAGE, D_MODEL, WCHUNK), F32),
            pltpu.SemaphoreType.DMA((NSTAGE,)),
            pltpu.VMEM((TB, D_IN_PROJ), F32),
            pltpu.VMEM((TB, D_MODEL), BF16),
            pltpu.VMEM((2, NSB, GROUP * BLOCK, LANES), BF16),
            pltpu.VMEM((2, TB + BLOCK, LANES), BF16),
            pltpu.VMEM((2, TB + BLOCK, LANES), BF16),
            pltpu.VMEM((TB + 16, D_POOL), F32),
            pltpu.VMEM((TB, D_MODEL), BF16),
            pltpu.VMEM((N_HEADS // 2, 1, LANES), F32),
            pltpu.VMEM((GROUP * BLOCK, LANES), BF16),
            pltpu.VMEM((2, 2 * BLOCK, LANES), BF16),
            pltpu.VMEM((nb, D_IN_PROJ), F32),
            pltpu.VMEM((nb * QROWS, LANES), F32),
            pltpu.VMEM((nb * QROWS, LANES), F32),
            pltpu.VMEM((nb * QROWS, LANES), F32),
            pltpu.VMEM((nb * QROWS, LANES), F32),
            pltpu.VMEM((2, LANES, nb), F32),
            pltpu.VMEM((BB * QROWS, LANES), F32),
            pltpu.VMEM((nb, D_MODEL), BF16),
        ],
        compiler_params=pltpu.CompilerParams(dimension_semantics=("arbitrary",), vmem_limit_bytes=VMEM_LIMIT),
        name="hybrid_step",
    )(sinks, xp, xs, nw, win, qn, kn, wpool, pscale, wout, ck, cv, sp, ck, cv)


def _cache_in(c):
    nb = c.shape[0]
    return jnp.transpose(c, (0, 2, 3, 1)).reshape(nb, LANES, WINDOW)


def _cache_out(c):
    nb = c.shape[0]
    return jnp.transpose(c.reshape(nb, N_KV_HEADS, HEAD_DIM, WINDOW), (0, 3, 1, 2))


def kernel(x_prompt, x_sample, cache_k, cache_v, state_pool, norm_w, w_in, q_norm_w, k_norm_w, sinks, w_pool,
           pool_scale, w_out):
    depth = norm_w.shape[0]
    assert depth == 1 and x_prompt.shape[0] == 1 and x_sample.shape[1] == 1
    seq = x_prompt.shape[1]
    nb = x_sample.shape[0]
    assert seq % TB == 0 and nb % BB == 0 and nb == LANES

    yp, kp, vp, up, ys, kq, vq, pq = _fused_call(
        sinks[0], x_prompt[0], x_sample, norm_w, w_in[0], q_norm_w, k_norm_w, w_pool[0], pool_scale, w_out[0],
        _cache_in(cache_k[0]), _cache_in(cache_v[0]), jnp.transpose(state_pool[0], (1, 0, 2)))
    return (yp[None], ys, _cache_out(kp[None])[None], _cache_out(vp[None])[None],
            jnp.transpose(up, (1, 0, 2))[None],
            _cache_out(kq)[None], _cache_out(vq)[None], jnp.transpose(pq, (1, 0, 2))[None])
```

```python
import functools

import jax
import jax.numpy as jnp
from jax import lax
from jax.experimental import pallas as pl
from jax.experimental.pallas import tpu as pltpu

D_MODEL = 1024
HEAD_DIM = 64
N_HEADS = 8
N_KV_HEADS = 2
GROUP = 4
WINDOW = 128
BLOCK = 128
POOL_WINDOWS = (2, 4, 8, 16)
POOL_STATE = 15
D_ATTN = 512
D_POOL = 512
D_IN_PROJ = 2304
EPS = 1e-6
NEG_INF = -1e30
LOG2E = 1.4426950408889634

O_K = 512
O_V = 640
O_GA = 768
O_U = 1280
O_GP = 1792

LANES = 128
TB = 512
NSB = TB // BLOCK
BB = 32
CB = 4
QROWS = 16
WCHUNK = 256
NSTAGE = 6
VMEM_LIMIT = 58 * 1024 * 1024

F32 = jnp.float32
BF16 = jnp.bfloat16
_NT = (((1,), (1,)), ((), ()))


def _silu_half(h):
    return h + h * jnp.tanh(h)


def _lo_mask():
    return lax.broadcasted_iota(jnp.int32, (1, LANES), 1) < HEAD_DIM


def _pair_rms(zs, lo, w2):
    sq = zs * zs
    s_lo = jnp.sum(jnp.where(lo, sq, 0.0), axis=-1, keepdims=True)
    s_hi = jnp.sum(jnp.where(lo, 0.0, sq), axis=-1, keepdims=True)
    r = lax.rsqrt(jnp.where(lo, s_lo, s_hi) * (1.0 / HEAD_DIM) + EPS)
    return zs * r * w2


def _row_scale(x):
    return lax.rsqrt(jnp.mean(x * x, axis=-1, keepdims=True) + EPS)


def _project(x, win):
    return jnp.dot(x.astype(BF16), win, preferred_element_type=F32) * _row_scale(x)


def _weight_chunks():
    chunks = []
    for c0 in range(0, D_IN_PROJ, WCHUNK):
        gate = O_GA <= c0 < O_U or c0 >= O_GP
        chunks.append((0, c0, 0.5 if gate else 1.0))
    chunks += [(1, c0, 1.0) for c0 in range(0, D_MODEL, WCHUNK)]
    return chunks


def _load_weights(win_hbm, wout_hbm, wpool_f32, nw_ref, qn_ref, kn_ref, stage, sem, win_bf, wout_bf, wpool_bf,
                  qw_ref, kw_ref):
    chunks = _weight_chunks()
    srcs, dsts = (win_hbm, wout_hbm), (win_bf, wout_bf)

    def copy(k):
        which, c0, _ = chunks[k]
        return pltpu.make_async_copy(srcs[which].at[:, pl.ds(c0, WCHUNK)], stage.at[k % NSTAGE], sem.at[k % NSTAGE])

    for k in range(NSTAGE):
        copy(k).start()
    nw_tile = jnp.concatenate([jnp.broadcast_to(nw_ref[:, t * LANES:(t + 1) * LANES], (LANES, LANES)).T
                               for t in range(D_MODEL // LANES)], axis=0)
    nw_rows = jnp.concatenate([nw_tile] * (WCHUNK // LANES), axis=1)

    def convert(ks):
        for k in ks:
            which, c0, scale = chunks[k]
            copy(k).wait()
            w = stage[k % NSTAGE]
            if which == 0:
                w = w * nw_rows if scale == 1.0 else w * (scale * nw_rows)
            dsts[which][:, c0:c0 + WCHUNK] = w.astype(BF16)
            if k + NSTAGE < len(chunks):
                copy(k + NSTAGE).start()

    n_in = sum(1 for which, _, _ in chunks if which == 0)
    convert(range(n_in))
    wpool_bf[...] = jnp.zeros(wpool_bf.shape, BF16)
    for g in range(len(POOL_WINDOWS)):
        d0 = (g % 2) * LANES
        wpool_bf[g // 2, d0:d0 + LANES, d0:d0 + LANES] = wpool_f32[g].astype(BF16)
    qn = qn_ref[...] * (HEAD_DIM ** -0.5 * LOG2E)
    qw_ref[...] = jnp.concatenate([qn, qn], axis=1)
    kw_ref[...] = jnp.concatenate([kn_ref[...], kn_ref[...]], axis=1)
    return lambda: convert(range(n_in, len(chunks)))


def _prompt_init(sinks_ref, kbuf, vbuf, uext, sink2, eye4, maskt):
    kbuf[:, 0:BLOCK, :] = jnp.zeros((2, BLOCK, LANES), BF16)
    vbuf[:, 0:BLOCK, :] = jnp.zeros((2, BLOCK, LANES), BF16)
    uext[0:16, :] = jnp.zeros((16, D_POOL), F32)
    lo = _lo_mask()
    for jj in range(N_HEADS // 2):
        sink2[jj] = jnp.where(lo, sinks_ref[2 * jj], sinks_ref[2 * jj + 1]) * LOG2E
    rr = lax.broadcasted_iota(jnp.int32, (GROUP * BLOCK, LANES), 0)
    ll = lax.broadcasted_iota(jnp.int32, (GROUP * BLOCK, LANES), 1)
    eye4[...] = jnp.where((rr & (BLOCK - 1)) == ll, 1.0, 0.0).astype(BF16)
    cc = lax.broadcasted_iota(jnp.int32, (2 * BLOCK, LANES), 0)
    qr = lax.broadcasted_iota(jnp.int32, (2 * BLOCK, LANES), 1)
    band = (cc >= qr) & (cc <= qr + WINDOW)
    maskt[0] = jnp.where(band & (cc >= BLOCK), 0.0, NEG_INF).astype(BF16)
    maskt[1] = jnp.where(band, 0.0, NEG_INF).astype(BF16)


def _prompt_step(i, last, x_ref, pscale_ref, y_ref, knew_ref, vnew_ref, unew_ref,
                 win_ref, wout_ref, wpool_ref, qw_ref, kw_ref, z_ref, hbuf, qbuf, kbuf, vbuf, uext, mixed, sink2, eye4, maskt):
    lo = _lo_mask()
    hbuf[...] = x_ref[...].astype(BF16)
    rs = _row_scale(x_ref[...])

    def proj(cols, rows=slice(None)):
        return jnp.dot(hbuf[rows, :], win_ref[:, cols], preferred_element_type=F32) * rs[rows]

    z_ref[:, O_U:] = proj(slice(O_U, D_IN_PROJ))
    uext[16:, :] = z_ref[:, O_U:O_U + D_POOL]
    pos16 = i * TB + lax.broadcasted_iota(jnp.int32, (16, LANES), 0)
    ds = []
    for g, w in enumerate(POOL_WINDOWS):
        acc = uext[:, g * LANES:(g + 1) * LANES]
        sh = 1
        while sh < w:
            acc = acc + pltpu.roll(acc, sh, axis=0)
            sh *= 2
        ug = z_ref[:, O_U + g * LANES:O_U + (g + 1) * LANES]
        cnt = jnp.minimum(w, pos16 + 1).astype(F32)
        d_first = acc[16:32, :] / cnt - ug[0:16, :]
        d_rest = acc[32:, :] * (1.0 / w) - ug[16:, :]
        ds.append(jnp.concatenate([d_first, d_rest], axis=0).astype(BF16))
    uext[0:16, :] = uext[TB:TB + 16, :]

    z_ref[:, :O_K] = proj(slice(0, O_K))

    for pr in range(len(POOL_WINDOWS) // 2):
        cols = slice(pr * 2 * LANES, (pr + 1) * 2 * LANES)
        d2 = jnp.concatenate(ds[2 * pr:2 * pr + 2], axis=1)
        po = jnp.dot(d2, wpool_ref[pr], preferred_element_type=F32) * pscale_ref[:, cols]
        gp = z_ref[:, O_GP + pr * 2 * LANES:O_GP + (pr + 1) * 2 * LANES]
        mixed[:, D_ATTN + pr * 2 * LANES:D_ATTN + (pr + 1) * 2 * LANES] = (po * _silu_half(gp)).astype(BF16)

    for j in range(4):
        qhat = _pair_rms(z_ref[:, j * LANES:(j + 1) * LANES], lo, qw_ref[...])
        q_lo = jnp.where(lo, qhat, 0.0).astype(BF16)
        q_hi = jnp.where(lo, 0.0, qhat).astype(BF16)
        c, g0 = j // 2, 2 * (j % 2)
        for sb in range(NSB):
            rows = slice(sb * BLOCK, (sb + 1) * BLOCK)
            qbuf[c, sb, g0 * BLOCK:(g0 + 1) * BLOCK, :] = q_lo[rows]
            qbuf[c, sb, (g0 + 1) * BLOCK:(g0 + 2) * BLOCK, :] = q_hi[rows]

    half = TB // 2
    for r0 in (0, half):
        z_ref[r0:r0 + half, O_K:O_GA] = proj(slice(O_K, O_GA), slice(r0, r0 + half))
    z_ref[:, O_GA:O_U] = proj(slice(O_GA, O_U))

    khat = _pair_rms(z_ref[:, O_K:O_K + LANES], lo, kw_ref[...])
    kr = pltpu.roll(khat, HEAD_DIM, axis=1)
    kbuf[0, BLOCK:, :] = jnp.where(lo, khat, kr).astype(BF16)
    kbuf[1, BLOCK:, :] = jnp.where(lo, kr, khat).astype(BF16)
    vz = z_ref[:, O_V:O_V + LANES]
    vr = pltpu.roll(vz, HEAD_DIM, axis=1)
    vbuf[0, BLOCK:, :] = jnp.where(lo, vz, vr).astype(BF16)
    vbuf[1, BLOCK:, :] = jnp.where(lo, vr, vz).astype(BF16)

    first = jnp.where(i > 0, 1, 0)
    for sb in range(NSB):
        rows = slice(sb * BLOCK, (sb + 1) * BLOCK)
        mcols = maskt[first] if sb == 0 else maskt[1]
        for c in range(2):
            keys = kbuf[c, sb * BLOCK:(sb + 2) * BLOCK, :]
            vals = vbuf[c, sb * BLOCK:(sb + 2) * BLOCK, :]
            s = lax.dot_general(jnp.concatenate([qbuf[c, sb], eye4[...]], axis=1),
                                jnp.concatenate([keys, mcols], axis=1), _NT,
                                preferred_element_type=F32)
            ps, ms, ls = [], [], []
            for g in range(GROUP):
                sg = s[g * BLOCK:(g + 1) * BLOCK, :]
                m = jnp.max(sg, axis=-1, keepdims=True)
                p = jnp.exp2(sg - m)
                ps.append(p.astype(BF16))
                ms.append(m)
                ls.append(jnp.sum(p, axis=-1, keepdims=True))
            o = jnp.dot(jnp.concatenate(ps, axis=0), vals, preferred_element_type=F32)
            for jj in range(2):
                ev, od = 2 * jj, 2 * jj + 1
                slab = jnp.where(lo, o[ev * BLOCK:(ev + 1) * BLOCK, :], o[od * BLOCK:(od + 1) * BLOCK, :])
                l = jnp.where(lo, ls[ev], ls[od]) + jnp.exp2(sink2[2 * c + jj] - jnp.where(lo, ms[ev], ms[od]))
                col = (2 * c + jj) * LANES
                ga = z_ref[rows, O_GA + col:O_GA + col + LANES]
                mixed[rows, col:col + LANES] = (slab * (1.0 / l) * _silu_half(ga)).astype(BF16)

    kbuf[:, 0:BLOCK, :] = kbuf[:, TB:TB + BLOCK, :]
    vbuf[:, 0:BLOCK, :] = vbuf[:, TB:TB + BLOCK, :]

    y_ref[...] = x_ref[...] + jnp.dot(mixed[...], wout_ref[...], preferred_element_type=F32)

    @pl.when(last)
    def _():
        unew_ref[:, 0, :] = z_ref[TB - POOL_STATE:TB, O_U:O_U + D_POOL]
        tail = slice(TB - WINDOW, TB)
        knew_ref[...] = _pair_rms(z_ref[tail, O_K:O_K + LANES], lo, kw_ref[...]).T
        vnew_ref[...] = z_ref[tail, O_V:O_V + LANES].T


def _sample_init(x_ref, win_ref, qw_ref, kw_ref, z_ref, q3, krep, vrep, snew, kvt):
    nb = x_ref.shape[0]
    lo = _lo_mask()
    z_ref[...] = _project(x_ref[:, 0, :], win_ref[...])
    khat = _pair_rms(z_ref[:, O_K:O_K + LANES], lo, kw_ref[...])
    vnew = z_ref[:, O_V:O_V + LANES]
    kvt[0] = khat.T
    kvt[1] = vnew.T
    q3[...] = jnp.zeros(q3.shape, F32)
    krep[...] = jnp.zeros(krep.shape, F32)
    vrep[...] = jnp.zeros(vrep.shape, F32)
    for j in range(4):
        qhat = _pair_rms(z_ref[:, j * LANES:(j + 1) * LANES], lo, qw_ref[...])
        qrot = pltpu.roll(qhat, HEAD_DIM, axis=1)
        grp_lo = j < 2
        for half in range(2):
            h = 2 * j + half
            src = qhat if (half == 0) == grp_lo else qrot
            q3[pl.ds(h, nb, stride=QROWS), :] = jnp.where(lo if grp_lo else ~lo, src, 0.0)
            krep[pl.ds(h, nb, stride=QROWS), :] = khat
            vrep[pl.ds(h, nb, stride=QROWS), :] = vnew
    s_new = jnp.sum(q3[...] * krep[...], axis=-1, keepdims=True)
    snew[...] = jnp.broadcast_to(s_new, snew.shape)


def _shift_caches(i, nb, ck_ref, cv_ref, knew_ref, vnew_ref, kvt):
    shift = lax.rem(nb - i * CB, nb)
    newest = lax.broadcasted_iota(jnp.int32, (LANES, WINDOW), 1) == WINDOW - 1
    for src, dst, which in ((ck_ref, knew_ref, 0), (cv_ref, vnew_ref, 1)):
        cols = pltpu.roll(kvt[which], shift, axis=1)
        shifted = pltpu.roll(src[...], WINDOW - 1, axis=2)
        for bb in range(CB):
            dst[bb] = jnp.where(newest, cols[:, bb:bb + 1], shifted[bb])


def _sample_step(j, last, sinks_ref, x_ref, pscale_ref, ck_ref, cv_ref, sp_ref, y_ref, spnew_ref,
                 wout_ref, wpool_ref, z_ref, q3, krep, vrep, snew, kvt, o3, mixed):
    nb = x_ref.shape[0]
    lo = _lo_mask()
    row0 = pl.multiple_of(j * BB, BB)
    rows = pl.ds(row0, BB)
    qrows = pl.ds(pl.multiple_of(j * (BB * QROWS), BB * QROWS), BB * QROWS)
    rid = lax.broadcasted_iota(jnp.int32, (1, QROWS, 1), 1)
    sink3 = jnp.zeros((1, QROWS, 1), F32)
    for h in range(N_HEADS):
        sink3 = jnp.where(rid == h, sinks_ref[h] * LOG2E, sink3)

    q = q3[qrows, :].reshape(BB, QROWS, LANES)
    s_new = snew[qrows, :].reshape(BB, QROWS, LANES)
    kt = ck_ref[...]
    vt = cv_ref[...]
    s = lax.dot_general(q.astype(BF16), kt.astype(BF16), (((2,), (1,)), ((0,), (0,))), preferred_element_type=F32)
    m = jnp.maximum(jnp.maximum(jnp.max(s, axis=-1, keepdims=True), s_new), sink3)
    p = jnp.exp2(s - m)
    p_new = jnp.exp2(s_new - m)
    l = jnp.sum(p, axis=-1, keepdims=True) + p_new + jnp.exp2(sink3 - m)
    o = lax.dot_general(p.astype(BF16), vt.astype(BF16), (((2,), (2,)), ((0,), (0,))), preferred_element_type=F32)
    o = (o + p_new * vrep[qrows, :].reshape(BB, QROWS, LANES)) / l
    o3[...] = o.reshape(BB * QROWS, LANES)

    for jj in range(4):
        grp_lo = jj < 2
        oh = [o3[pl.ds(2 * jj + half, BB, stride=QROWS), :] for half in range(2)]
        a = oh[0] if grp_lo else pltpu.roll(oh[0], HEAD_DIM, axis=1)
        b = pltpu.roll(oh[1], HEAD_DIM, axis=1) if grp_lo else oh[1]
        ga = z_ref[rows, O_GA + jj * LANES:O_GA + (jj + 1) * LANES]
        mixed[rows, jj * LANES:(jj + 1) * LANES] = (jnp.where(lo, a, b) * _silu_half(ga)).astype(BF16)

    u = z_ref[rows, O_U:O_U + D_POOL]
    spnew_ref[0:POOL_STATE - 1] = sp_ref[1:POOL_STATE]
    spnew_ref[POOL_STATE - 1] = u
    ds = []
    for g, w in enumerate(POOL_WINDOWS):
        cols = slice(g * LANES, (g + 1) * LANES)
        ug = u[:, cols]
        win_sum = ug
        for r in range(POOL_STATE - (w - 1), POOL_STATE):
            win_sum = win_sum + sp_ref[r, :, cols]
        ds.append((win_sum * (1.0 / w) - ug).astype(BF16))
    for pr in range(len(POOL_WINDOWS) // 2):
        cols = slice(pr * 2 * LANES, (pr + 1) * 2 * LANES)
        d2 = jnp.concatenate(ds[2 * pr:2 * pr + 2], axis=1)
        po = jnp.dot(d2, wpool_ref[pr], preferred_element_type=F32) * pscale_ref[:, cols]
        gp = z_ref[rows, O_GP + pr * 2 * LANES:O_GP + (pr + 1) * 2 * LANES]
        mixed[rows, D_ATTN + pr * 2 * LANES:D_ATTN + (pr + 1) * 2 * LANES] = (po * _silu_half(gp)).astype(BF16)

    @pl.when(last)
    def _():
        y_ref[:, 0, :] = x_ref[:, 0, :] + jnp.dot(mixed[...], wout_ref[...], preferred_element_type=F32)


def _fused_kernel(n_prompt, sinks_ref, xp_ref, xs_ref, nw_ref, win_hbm, qn_ref, kn_ref, wpool_f32, pscale_ref, wout_hbm,
                  ck_ref, cv_ref, sp_ref, ckc_ref, cvc_ref,
                  yp_ref, kp_ref, vp_ref, up_ref, ys_ref, kq_ref, vq_ref, pq_ref,
                  win_ref, wout_ref, wpool_ref, qw_ref, kw_ref, stage, sem,
                  z_ref, hbuf, qbuf, kbuf, vbuf, uext, mixed, sink2, eye4, maskt,
                  zs_ref, q3, krep, vrep, snew, kvt, o3, mixed_s):
    i = pl.program_id(0)

    @pl.when(i == 0)
    def _():
        convert_wout = _load_weights(win_hbm, wout_hbm, wpool_f32, nw_ref, qn_ref, kn_ref, stage, sem,
                                     win_ref, wout_ref, wpool_ref, qw_ref, kw_ref)
        _prompt_init(sinks_ref, kbuf, vbuf, uext, sink2, eye4, maskt)
        _sample_init(xs_ref, win_ref, qw_ref, kw_ref, zs_ref, q3, krep, vrep, snew, kvt)
        convert_wout()

    @pl.when(i < n_prompt)
    def _():
        _shift_caches(i, xs_ref.shape[0], ckc_ref, cvc_ref, kq_ref, vq_ref, kvt)
        _prompt_step(i, i == n_prompt - 1, xp_ref, pscale_ref, yp_ref, kp_ref, vp_ref, up_ref,
                     win_ref, wout_ref, wpool_ref, qw_ref, kw_ref, z_ref, hbuf, qbuf, kbuf, vbuf, uext, mixed, sink2, eye4, maskt)

    @pl.when(i >= n_prompt)
    def _():
        _sample_step(i - n_prompt, i == pl.num_programs(0) - 1, sinks_ref, xs_ref, pscale_ref, ck_ref, cv_ref, sp_ref,
                     ys_ref, pq_ref, wout_ref, wpool_ref, zs_ref, q3, krep, vrep, snew, kvt, o3, mixed_s)


def _fused_call(sinks, xp, xs, nw, win, qn, kn, wpool, pscale, wout, ck, cv, sp):
    seq, nb = xp.shape[0], xs.shape[0]
    n_prompt, n_sample = seq // TB, nb // BB
    assert nb == CB * n_prompt
    const2 = lambda i: (0, 0)
    const3 = lambda i: (0, 0, 0)
    pblk = lambda i: (jnp.minimum(i, n_prompt - 1), 0)
    cblk = lambda i: (jnp.minimum(i, n_prompt - 1), 0, 0)
    sidx = lambda i: jnp.clip(i - n_prompt, 0, n_sample - 1)
    chunk3 = lambda i: (sidx(i), 0, 0)
    mid3 = lambda i: (0, sidx(i), 0)
    once = pl.Buffered(1)
    return pl.pallas_call(
        functools.partial(_fused_kernel, n_prompt),
        grid=(n_prompt + n_sample,),
        in_specs=[
            pl.BlockSpec(memory_space=pltpu.SMEM),
            pl.BlockSpec((TB, D_MODEL), pblk),
            pl.BlockSpec((nb, 1, D_MODEL), const3, pipeline_mode=once),
            pl.BlockSpec((1, D_MODEL), const2),
            pl.BlockSpec(memory_space=pl.ANY),
            pl.BlockSpec((1, HEAD_DIM), const2),
            pl.BlockSpec((1, HEAD_DIM), const2),
            pl.BlockSpec((4, LANES, LANES), const3, pipeline_mode=once),
            pl.BlockSpec((1, D_POOL), const2),
            pl.BlockSpec(memory_space=pl.ANY),
            pl.BlockSpec((BB, LANES, WINDOW), chunk3),
            pl.BlockSpec((BB, LANES, WINDOW), chunk3),
            pl.BlockSpec((POOL_STATE, BB, D_POOL), mid3),
            pl.BlockSpec((CB, LANES, WINDOW), cblk),
            pl.BlockSpec((CB, LANES, WINDOW), cblk),
        ],
        out_specs=[
            pl.BlockSpec((TB, D_MODEL), pblk),
            pl.BlockSpec((LANES, WINDOW), const2),
            pl.BlockSpec((LANES, WINDOW), const2),
            pl.BlockSpec((POOL_STATE, 1, D_POOL), const3),
            pl.BlockSpec((nb, 1, D_MODEL), const3, pipeline_mode=once),
            pl.BlockSpec((CB, LANES, WINDOW), cblk),
            pl.BlockSpec((CB, LANES, WINDOW), cblk),
            pl.BlockSpec((POOL_STATE, BB, D_POOL), mid3),
        ],
        out_shape=[
            jax.ShapeDtypeStruct((seq, D_MODEL), F32),
            jax.ShapeDtypeStruct((LANES, WINDOW), F32),
            jax.ShapeDtypeStruct((LANES, WINDOW), F32),
            jax.ShapeDtypeStruct((POOL_STATE, 1, D_POOL), F32),
            jax.ShapeDtypeStruct((nb, 1, D_MODEL), F32),
            jax.ShapeDtypeStruct((nb, LANES, WINDOW), F32),
            jax.ShapeDtypeStruct((nb, LANES, WINDOW), F32),
            jax.ShapeDtypeStruct((POOL_STATE, nb, D_POOL), F32),
        ],
        scratch_shapes=[
            pltpu.VMEM((D_MODEL, D_IN_PROJ), BF16),
            pltpu.VMEM((D_MODEL, D_MODEL), BF16),
            pltpu.VMEM((2, 2 * LANES, 2 * LANES), BF16),
            pltpu.VMEM((1, LANES), F32),
            pltpu.VMEM((1, LANES), F32),
            pltpu.VMEM((NSTAGE, D_MODEL, WCHUNK), F32),
            pltpu.SemaphoreType.DMA((NSTAGE,)),
            pltpu.VMEM((TB, D_IN_PROJ), F32),
            pltpu.VMEM((TB, D_MODEL), BF16),
            pltpu.VMEM((2, NSB, GROUP * BLOCK, LANES), BF16),
            pltpu.VMEM((2, TB + BLOCK, LANES), BF16),
            pltpu.VMEM((2, TB + BLOCK, LANES), BF16),
            pltpu.VMEM((TB + 16, D_POOL), F32),
            pltpu.VMEM((TB, D_MODEL), BF16),
            pltpu.VMEM((N_HEADS // 2, 1, LANES), F32),
            pltpu.VMEM((GROUP * BLOCK, LANES), BF16),
            pltpu.VMEM((2, 2 * BLOCK, LANES), BF16),
            pltpu.VMEM((nb, D_IN_PROJ), F32),
            pltpu.VMEM((nb * QROWS, LANES), F32),
            pltpu.VMEM((nb * QROWS, LANES), F32),
            pltpu.VMEM((nb * QROWS, LANES), F32),
            pltpu.VMEM((nb * QROWS, LANES), F32),
            pltpu.VMEM((2, LANES, nb), F32),
            pltpu.VMEM((BB * QROWS, LANES), F32),
            pltpu.VMEM((nb, D_MODEL), BF16),
        ],
        compiler_params=pltpu.CompilerParams(dimension_semantics=("arbitrary",), vmem_limit_bytes=VMEM_LIMIT),
        name="hybrid_step",
    )(sinks, xp, xs, nw, win, qn, kn, wpool, pscale, wout, ck, cv, sp, ck, cv)


def _cache_in(c):
    nb = c.shape[0]
    return jnp.transpose(c, (0, 2, 3, 1)).reshape(nb, LANES, WINDOW)


def _cache_out(c):
    nb = c.shape[0]
    return jnp.transpose(c.reshape(nb, N_KV_HEADS, HEAD_DIM, WINDOW), (0, 3, 1, 2))


def kernel(x_prompt, x_sample, cache_k, cache_v, state_pool, norm_w, w_in, q_norm_w, k_norm_w, sinks, w_pool,
           pool_scale, w_out):
    depth = norm_w.shape[0]
    assert depth == 1 and x_prompt.shape[0] == 1 and x_sample.shape[1] == 1
    seq = x_prompt.shape[1]
    nb = x_sample.shape[0]
    assert seq % TB == 0 and nb % BB == 0 and nb == LANES

    yp, kp, vp, up, ys, kq, vq, pq = _fused_call(
        sinks[0], x_prompt[0], x_sample, norm_w, w_in[0], q_norm_w, k_norm_w, w_pool[0], pool_scale, w_out[0],
        _cache_in(cache_k[0]), _cache_in(cache_v[0]), jnp.transpose(state_pool[0], (1, 0, 2)))
    return (yp[None], ys, _cache_out(kp[None])[None], _cache_out(vp[None])[None],
            jnp.transpose(up, (1, 0, 2))[None],
            _cache_out(kq)[None], _cache_out(vq)[None], jnp.transpose(pq, (1, 0, 2))[None])
```

```python
import functools

import jax
import jax.numpy as jnp
from jax import lax
from jax.experimental import pallas as pl
from jax.experimental.pallas import tpu as pltpu

D_MODEL = 1024
HEAD_DIM = 64
N_HEADS = 8
N_KV_HEADS = 2
GROUP = 4
WINDOW = 128
BLOCK = 128
POOL_WINDOWS = (2, 4, 8, 16)
POOL_STATE = 15
D_ATTN = 512
D_POOL = 512
D_IN_PROJ = 2304
EPS = 1e-6
NEG_INF = -1e30
LOG2E = 1.4426950408889634

O_K = 512
O_V = 640
O_GA = 768
O_U = 1280
O_GP = 1792

LANES = 128
TB = 512
NSB = TB // BLOCK
BB = 32
CB = 4
QROWS = 16
WCHUNK = 256
NSTAGE = 6
VMEM_LIMIT = 58 * 1024 * 1024

F32 = jnp.float32
BF16 = jnp.bfloat16
_NT = (((1,), (1,)), ((), ()))


def _silu_half(h):
    return h + h * jnp.tanh(h)


def _lo_mask():
    return lax.broadcasted_iota(jnp.int32, (1, LANES), 1) < HEAD_DIM


def _pair_rms(zs, lo, w2):
    sq = zs * zs
    s_lo = jnp.sum(jnp.where(lo, sq, 0.0), axis=-1, keepdims=True)
    s_hi = jnp.sum(jnp.where(lo, 0.0, sq), axis=-1, keepdims=True)
    r = lax.rsqrt(jnp.where(lo, s_lo, s_hi) * (1.0 / HEAD_DIM) + EPS)
    return zs * r * w2


def _row_scale(x):
    return lax.rsqrt(jnp.mean(x * x, axis=-1, keepdims=True) + EPS)


def _project(x, win):
    return jnp.dot(x.astype(BF16), win, preferred_element_type=F32) * _row_scale(x)


def _weight_chunks():
    chunks = []
    for c0 in range(0, D_IN_PROJ, WCHUNK):
        gate = O_GA <= c0 < O_U or c0 >= O_GP
        chunks.append((0, c0, 0.5 if gate else 1.0))
    chunks += [(1, c0, 1.0) for c0 in range(0, D_MODEL, WCHUNK)]
    return chunks


def _load_weights(win_hbm, wout_hbm, wpool_f32, nw_ref, qn_ref, kn_ref, stage, sem, win_bf, wout_bf, wpool_bf,
                  qw_ref, kw_ref):
    chunks = _weight_chunks()
    srcs, dsts = (win_hbm, wout_hbm), (win_bf, wout_bf)

    def copy(k):
        which, c0, _ = chunks[k]
        return pltpu.make_async_copy(srcs[which].at[:, pl.ds(c0, WCHUNK)], stage.at[k % NSTAGE], sem.at[k % NSTAGE])

    for k in range(NSTAGE):
        copy(k).start()
    nw_tile = jnp.concatenate([jnp.broadcast_to(nw_ref[:, t * LANES:(t + 1) * LANES], (LANES, LANES)).T
                               for t in range(D_MODEL // LANES)], axis=0)
    nw_rows = jnp.concatenate([nw_tile] * (WCHUNK // LANES), axis=1)

    def convert(ks):
        for k in ks:
            which, c0, scale = chunks[k]
            copy(k).wait()
            w = stage[k % NSTAGE]
            if which == 0:
                w = w * nw_rows if scale == 1.0 else w * (scale * nw_rows)
            dsts[which][:, c0:c0 + WCHUNK] = w.astype(BF16)
            if k + NSTAGE < len(chunks):
                copy(k + NSTAGE).start()

    n_in = sum(1 for which, _, _ in chunks if which == 0)
    convert(range(n_in))
    wpool_bf[...] = jnp.zeros(wpool_bf.shape, BF16)
    for g in range(len(POOL_WINDOWS)):
        d0 = (g % 2) * LANES
        wpool_bf[g // 2, d0:d0 + LANES, d0:d0 + LANES] = wpool_f32[g].astype(BF16)
    qn = qn_ref[...] * (HEAD_DIM ** -0.5 * LOG2E)
    qw_ref[...] = jnp.concatenate([qn, qn], axis=1)
    kw_ref[...] = jnp.concatenate([kn_ref[...], kn_ref[...]], axis=1)
    return lambda: convert(range(n_in, len(chunks)))


def _prompt_init(sinks_ref, kbuf, vbuf, uext, sink2, eye4, maskt):
    kbuf[:, 0:BLOCK, :] = jnp.zeros((2, BLOCK, LANES), BF16)
    vbuf[:, 0:BLOCK, :] = jnp.zeros((2, BLOCK, LANES), BF16)
    uext[0:16, :] = jnp.zeros((16, D_POOL), F32)
    lo = _lo_mask()
    for jj in range(N_HEADS // 2):
        sink2[jj] = jnp.where(lo, sinks_ref[2 * jj], sinks_ref[2 * jj + 1]) * LOG2E
    rr = lax.broadcasted_iota(jnp.int32, (GROUP * BLOCK, LANES), 0)
    ll = lax.broadcasted_iota(jnp.int32, (GROUP * BLOCK, LANES), 1)
    eye4[...] = jnp.where((rr & (BLOCK - 1)) == ll, 1.0, 0.0).astype(BF16)
    cc = lax.broadcasted_iota(jnp.int32, (2 * BLOCK, LANES), 0)
    qr = lax.broadcasted_iota(jnp.int32, (2 * BLOCK, LANES), 1)
    band = (cc >= qr) & (cc <= qr + WINDOW)
    maskt[0] = jnp.where(band & (cc >= BLOCK), 0.0, NEG_INF).astype(BF16)
    maskt[1] = jnp.where(band, 0.0, NEG_INF).astype(BF16)


def _prompt_step(i, last, x_ref, pscale_ref, y_ref, knew_ref, vnew_ref, unew_ref,
                 win_ref, wout_ref, wpool_ref, qw_ref, kw_ref, z_ref, hbuf, qbuf, kbuf, vbuf, uext, mixed, sink2, eye4, maskt,
                 side_work):
    lo = _lo_mask()
    hbuf[...] = x_ref[...].astype(BF16)
    rs = _row_scale(x_ref[...])

    def proj(cols, rows=slice(None)):
        return jnp.dot(hbuf[rows, :], win_ref[:, cols], preferred_element_type=F32) * rs[rows]

    z_ref[:, O_U:] = proj(slice(O_U, D_IN_PROJ))
    uext[16:, :] = z_ref[:, O_U:O_U + D_POOL]
    pos16 = i * TB + lax.broadcasted_iota(jnp.int32, (16, LANES), 0)
    ds = []
    for g, w in enumerate(POOL_WINDOWS):
        acc = uext[:, g * LANES:(g + 1) * LANES]
        sh = 1
        while sh < w:
            acc = acc + pltpu.roll(acc, sh, axis=0)
            sh *= 2
        ug = z_ref[:, O_U + g * LANES:O_U + (g + 1) * LANES]
        cnt = jnp.minimum(w, pos16 + 1).astype(F32)
        d_first = acc[16:32, :] / cnt - ug[0:16, :]
        d_rest = acc[32:, :] * (1.0 / w) - ug[16:, :]
        ds.append(jnp.concatenate([d_first, d_rest], axis=0).astype(BF16))
    uext[0:16, :] = uext[TB:TB + 16, :]

    z_ref[:, :O_K] = proj(slice(0, O_K))

    for pr in range(len(POOL_WINDOWS) // 2):
        cols = slice(pr * 2 * LANES, (pr + 1) * 2 * LANES)
        d2 = jnp.concatenate(ds[2 * pr:2 * pr + 2], axis=1)
        po = jnp.dot(d2, wpool_ref[pr], preferred_element_type=F32) * pscale_ref[:, cols]
        gp = z_ref[:, O_GP + pr * 2 * LANES:O_GP + (pr + 1) * 2 * LANES]
        mixed[:, D_ATTN + pr * 2 * LANES:D_ATTN + (pr + 1) * 2 * LANES] = (po * _silu_half(gp)).astype(BF16)

    for j in range(4):
        qhat = _pair_rms(z_ref[:, j * LANES:(j + 1) * LANES], lo, qw_ref[...])
        q_lo = jnp.where(lo, qhat, 0.0).astype(BF16)
        q_hi = jnp.where(lo, 0.0, qhat).astype(BF16)
        c, g0 = j // 2, 2 * (j % 2)
        for sb in range(NSB):
            rows = slice(sb * BLOCK, (sb + 1) * BLOCK)
            qbuf[c, sb, g0 * BLOCK:(g0 + 1) * BLOCK, :] = q_lo[rows]
            qbuf[c, sb, (g0 + 1) * BLOCK:(g0 + 2) * BLOCK, :] = q_hi[rows]

    half = TB // 2
    for r0 in (0, half):
        z_ref[r0:r0 + half, O_K:O_GA] = proj(slice(O_K, O_GA), slice(r0, r0 + half))
    z_ref[:, O_GA:O_U] = proj(slice(O_GA, O_U))

    khat = _pair_rms(z_ref[:, O_K:O_K + LANES], lo, kw_ref[...])
    kr = pltpu.roll(khat, HEAD_DIM, axis=1)
    kbuf[0, BLOCK:, :] = jnp.where(lo, khat, kr).astype(BF16)
    kbuf[1, BLOCK:, :] = jnp.where(lo, kr, khat).astype(BF16)
    vz = z_ref[:, O_V:O_V + LANES]
    vr = pltpu.roll(vz, HEAD_DIM, axis=1)
    vbuf[0, BLOCK:, :] = jnp.where(lo, vz, vr).astype(BF16)
    vbuf[1, BLOCK:, :] = jnp.where(lo, vr, vz).astype(BF16)

    first = jnp.where(i > 0, 1, 0)
    for sb in range(NSB):
        rows = slice(sb * BLOCK, (sb + 1) * BLOCK)
        mcols = maskt[first] if sb == 0 else maskt[1]
        for c in range(2):
            keys = kbuf[c, sb * BLOCK:(sb + 2) * BLOCK, :]
            vals = vbuf[c, sb * BLOCK:(sb + 2) * BLOCK, :]
            s = lax.dot_general(jnp.concatenate([qbuf[c, sb], eye4[...]], axis=1),
                                jnp.concatenate([keys, mcols], axis=1), _NT,
                                preferred_element_type=F32)
            ps, ms, ls = [], [], []
            for g in range(GROUP):
                sg = s[g * BLOCK:(g + 1) * BLOCK, :]
                m = jnp.max(sg, axis=-1, keepdims=True)
                p = jnp.exp2(sg - m)
                ps.append(p.astype(BF16))
                ms.append(m)
                ls.append(jnp.sum(p, axis=-1, keepdims=True))
            o = jnp.dot(jnp.concatenate(ps, axis=0), vals, preferred_element_type=F32)
            for jj in range(2):
                ev, od = 2 * jj, 2 * jj + 1
                slab = jnp.where(lo, o[ev * BLOCK:(ev + 1) * BLOCK, :], o[od * BLOCK:(od + 1) * BLOCK, :])
                l = jnp.where(lo, ls[ev], ls[od]) + jnp.exp2(sink2[2 * c + jj] - jnp.where(lo, ms[ev], ms[od]))
                col = (2 * c + jj) * LANES
                ga = z_ref[rows, O_GA + col:O_GA + col + LANES]
                mixed[rows, col:col + LANES] = (slab * (1.0 / l) * _silu_half(ga)).astype(BF16)

    kbuf[:, 0:BLOCK, :] = kbuf[:, TB:TB + BLOCK, :]
    vbuf[:, 0:BLOCK, :] = vbuf[:, TB:TB + BLOCK, :]

    y_ref[...] = x_ref[...] + jnp.dot(mixed[...], wout_ref[...], preferred_element_type=F32)
    side_work()

    @pl.when(last)
    def _():
        unew_ref[:, 0, :] = z_ref[TB - POOL_STATE:TB, O_U:O_U + D_POOL]
        tail = slice(TB - WINDOW, TB)
        knew_ref[...] = _pair_rms(z_ref[tail, O_K:O_K + LANES], lo, kw_ref[...]).T
        vnew_ref[...] = z_ref[tail, O_V:O_V + LANES].T


def _sample_init(x_ref, win_ref, qw_ref, kw_ref, z_ref, q3, krep, vrep, snew, kvt):
    nb = x_ref.shape[0]
    lo = _lo_mask()
    z_ref[...] = _project(x_ref[:, 0, :], win_ref[...])
    khat = _pair_rms(z_ref[:, O_K:O_K + LANES], lo, kw_ref[...])
    vnew = z_ref[:, O_V:O_V + LANES]
    kvt[0] = khat.T
    kvt[1] = vnew.T
    q3[...] = jnp.zeros(q3.shape, F32)
    krep[...] = jnp.zeros(krep.shape, F32)
    vrep[...] = jnp.zeros(vrep.shape, F32)
    for j in range(4):
        qhat = _pair_rms(z_ref[:, j * LANES:(j + 1) * LANES], lo, qw_ref[...])
        qrot = pltpu.roll(qhat, HEAD_DIM, axis=1)
        grp_lo = j < 2
        for half in range(2):
            h = 2 * j + half
            src = qhat if (half == 0) == grp_lo else qrot
            q3[pl.ds(h, nb, stride=QROWS), :] = jnp.where(lo if grp_lo else ~lo, src, 0.0)
            krep[pl.ds(h, nb, stride=QROWS), :] = khat
            vrep[pl.ds(h, nb, stride=QROWS), :] = vnew
    s_new = jnp.sum(q3[...] * krep[...], axis=-1, keepdims=True)
    snew[...] = jnp.broadcast_to(s_new, snew.shape)


def _shift_caches(i, nb, ck_ref, cv_ref, knew_ref, vnew_ref, kvt):
    shift = lax.rem(nb - i * CB, nb)
    newest = lax.broadcasted_iota(jnp.int32, (LANES, WINDOW), 1) == WINDOW - 1
    for src, dst, which in ((ck_ref, knew_ref, 0), (cv_ref, vnew_ref, 1)):
        cols = pltpu.roll(kvt[which], shift, axis=1)
        shifted = pltpu.roll(src[...], WINDOW - 1, axis=2)
        for bb in range(CB):
            dst[bb] = jnp.where(newest, cols[:, bb:bb + 1], shifted[bb])


def _sample_attend(i, sinks_ref, ck_ref, cv_ref, q3, vrep, snew, oall):
    qrows = pl.ds(pl.multiple_of(i * (CB * QROWS), CB * QROWS), CB * QROWS)
    rid = lax.broadcasted_iota(jnp.int32, (1, QROWS, 1), 1)
    sink3 = jnp.zeros((1, QROWS, 1), F32)
    for h in range(N_HEADS):
        sink3 = jnp.where(rid == h, sinks_ref[h] * LOG2E, sink3)

    q = q3[qrows, :].reshape(CB, QROWS, LANES)
    s_new = snew[qrows, :].reshape(CB, QROWS, LANES)
    kt = ck_ref[...]
    vt = cv_ref[...]
    s = lax.dot_general(q.astype(BF16), kt.astype(BF16), (((2,), (1,)), ((0,), (0,))), preferred_element_type=F32)
    m = jnp.maximum(jnp.maximum(jnp.max(s, axis=-1, keepdims=True), s_new), sink3)
    p = jnp.exp2(s - m)
    p_new = jnp.exp2(s_new - m)
    l = jnp.sum(p, axis=-1, keepdims=True) + p_new + jnp.exp2(sink3 - m)

    def finish():
        o = lax.dot_general(p.astype(BF16), vt.astype(BF16), (((2,), (2,)), ((0,), (0,))), preferred_element_type=F32)
        o = (o + p_new * vrep[qrows, :].reshape(CB, QROWS, LANES)) / l
        oall[qrows, :] = o.reshape(CB * QROWS, LANES)

    return finish


def _sample_step(j, last, x_ref, pscale_ref, sp_ref, y_ref, spnew_ref, wout_ref, wpool_ref, z_ref, oall, o3, mixed):
    lo = _lo_mask()
    row0 = pl.multiple_of(j * BB, BB)
    rows = pl.ds(row0, BB)
    o3[...] = oall[pl.ds(pl.multiple_of(j * (BB * QROWS), BB * QROWS), BB * QROWS), :]

    for jj in range(4):
        grp_lo = jj < 2
        oh = [o3[pl.ds(2 * jj + half, BB, stride=QROWS), :] for half in range(2)]
        a = oh[0] if grp_lo else pltpu.roll(oh[0], HEAD_DIM, axis=1)
        b = pltpu.roll(oh[1], HEAD_DIM, axis=1) if grp_lo else oh[1]
        ga = z_ref[rows, O_GA + jj * LANES:O_GA + (jj + 1) * LANES]
        mixed[rows, jj * LANES:(jj + 1) * LANES] = (jnp.where(lo, a, b) * _silu_half(ga)).astype(BF16)

    u = z_ref[rows, O_U:O_U + D_POOL]
    spnew_ref[0:POOL_STATE - 1] = sp_ref[1:POOL_STATE]
    spnew_ref[POOL_STATE - 1] = u
    ds = []
    for g, w in enumerate(POOL_WINDOWS):
        cols = slice(g * LANES, (g + 1) * LANES)
        ug = u[:, cols]
        win_sum = ug
        for r in range(POOL_STATE - (w - 1), POOL_STATE):
            win_sum = win_sum + sp_ref[r, :, cols]
        ds.append((win_sum * (1.0 / w) - ug).astype(BF16))
    for pr in range(len(POOL_WINDOWS) // 2):
        cols = slice(pr * 2 * LANES, (pr + 1) * 2 * LANES)
        d2 = jnp.concatenate(ds[2 * pr:2 * pr + 2], axis=1)
        po = jnp.dot(d2, wpool_ref[pr], preferred_element_type=F32) * pscale_ref[:, cols]
        gp = z_ref[rows, O_GP + pr * 2 * LANES:O_GP + (pr + 1) * 2 * LANES]
        mixed[rows, D_ATTN + pr * 2 * LANES:D_ATTN + (pr + 1) * 2 * LANES] = (po * _silu_half(gp)).astype(BF16)

    @pl.when(last)
    def _():
        y_ref[:, 0, :] = x_ref[:, 0, :] + jnp.dot(mixed[...], wout_ref[...], preferred_element_type=F32)


def _fused_kernel(n_prompt, sinks_ref, xp_ref, xs_ref, nw_ref, win_hbm, qn_ref, kn_ref, wpool_f32, pscale_ref, wout_hbm,
                  sp_ref, ckc_ref, cvc_ref,
                  yp_ref, kp_ref, vp_ref, up_ref, ys_ref, kq_ref, vq_ref, pq_ref,
                  win_ref, wout_ref, wpool_ref, qw_ref, kw_ref, stage, sem,
                  z_ref, hbuf, qbuf, kbuf, vbuf, uext, mixed, sink2, eye4, maskt,
                  zs_ref, q3, krep, vrep, snew, kvt, oall, o3, mixed_s):
    i = pl.program_id(0)

    @pl.when(i == 0)
    def _():
        convert_wout = _load_weights(win_hbm, wout_hbm, wpool_f32, nw_ref, qn_ref, kn_ref, stage, sem,
                                     win_ref, wout_ref, wpool_ref, qw_ref, kw_ref)
        _prompt_init(sinks_ref, kbuf, vbuf, uext, sink2, eye4, maskt)
        _sample_init(xs_ref, win_ref, qw_ref, kw_ref, zs_ref, q3, krep, vrep, snew, kvt)
        convert_wout()

    @pl.when(i < n_prompt)
    def _():
        _shift_caches(i, xs_ref.shape[0], ckc_ref, cvc_ref, kq_ref, vq_ref, kvt)
        attend = _sample_attend(i, sinks_ref, ckc_ref, cvc_ref, q3, vrep, snew, oall)
        _prompt_step(i, i == n_prompt - 1, xp_ref, pscale_ref, yp_ref, kp_ref, vp_ref, up_ref,
                     win_ref, wout_ref, wpool_ref, qw_ref, kw_ref, z_ref, hbuf, qbuf, kbuf, vbuf, uext, mixed, sink2, eye4, maskt,
                     attend)

    @pl.when(i >= n_prompt)
    def _():
        _sample_step(i - n_prompt, i == pl.num_programs(0) - 1, xs_ref, pscale_ref, sp_ref,
                     ys_ref, pq_ref, wout_ref, wpool_ref, zs_ref, oall, o3, mixed_s)


def _fused_call(sinks, xp, xs, nw, win, qn, kn, wpool, pscale, wout, ck, cv, sp):
    seq, nb = xp.shape[0], xs.shape[0]
    n_prompt, n_sample = seq // TB, nb // BB
    assert nb == CB * n_prompt
    const2 = lambda i: (0, 0)
    const3 = lambda i: (0, 0, 0)
    pblk = lambda i: (jnp.minimum(i, n_prompt - 1), 0)
    cblk = lambda i: (jnp.minimum(i, n_prompt - 1), 0, 0)
    sidx = lambda i: jnp.clip(i - n_prompt, 0, n_sample - 1)
    mid3 = lambda i: (0, sidx(i), 0)
    once = pl.Buffered(1)
    return pl.pallas_call(
        functools.partial(_fused_kernel, n_prompt),
        grid=(n_prompt + n_sample,),
        in_specs=[
            pl.BlockSpec(memory_space=pltpu.SMEM),
            pl.BlockSpec((TB, D_MODEL), pblk),
            pl.BlockSpec((nb, 1, D_MODEL), const3, pipeline_mode=once),
            pl.BlockSpec((1, D_MODEL), const2),
            pl.BlockSpec(memory_space=pl.ANY),
            pl.BlockSpec((1, HEAD_DIM), const2),
            pl.BlockSpec((1, HEAD_DIM), const2),
            pl.BlockSpec((4, LANES, LANES), const3, pipeline_mode=once),
            pl.BlockSpec((1, D_POOL), const2),
            pl.BlockSpec(memory_space=pl.ANY),
            pl.BlockSpec((POOL_STATE, BB, D_POOL), mid3),
            pl.BlockSpec((CB, LANES, WINDOW), cblk),
            pl.BlockSpec((CB, LANES, WINDOW), cblk),
        ],
        out_specs=[
            pl.BlockSpec((TB, D_MODEL), pblk),
            pl.BlockSpec((LANES, WINDOW), const2),
            pl.BlockSpec((LANES, WINDOW), const2),
            pl.BlockSpec((POOL_STATE, 1, D_POOL), const3),
            pl.BlockSpec((nb, 1, D_MODEL), const3, pipeline_mode=once),
            pl.BlockSpec((CB, LANES, WINDOW), cblk),
            pl.BlockSpec((CB, LANES, WINDOW), cblk),
            pl.BlockSpec((POOL_STATE, BB, D_POOL), mid3),
        ],
        out_shape=[
            jax.ShapeDtypeStruct((seq, D_MODEL), F32),
            jax.ShapeDtypeStruct((LANES, WINDOW), F32),
            jax.ShapeDtypeStruct((LANES, WINDOW), F32),
            jax.ShapeDtypeStruct((POOL_STATE, 1, D_POOL), F32),
            jax.ShapeDtypeStruct((nb, 1, D_MODEL), F32),
            jax.ShapeDtypeStruct((nb, LANES, WINDOW), F32),
            jax.ShapeDtypeStruct((nb, LANES, WINDOW), F32),
            jax.ShapeDtypeStruct((POOL_STATE, nb, D_POOL), F32),
        ],
        scratch_shapes=[
            pltpu.VMEM((D_MODEL, D_IN_PROJ), BF16),
            pltpu.VMEM((D_MODEL, D_MODEL), BF16),
            pltpu.VMEM((2, 2 * LANES, 2 * LANES), BF16),
            pltpu.VMEM((1, LANES), F32),
            pltpu.VMEM((1, LANES), F32),
            pltpu.VMEM((NSTAGE, D_MODEL, WCHUNK), F32),
            pltpu.SemaphoreType.DMA((NSTAGE,)),
            pltpu.VMEM((TB, D_IN_PROJ), F32),
            pltpu.VMEM((TB, D_MODEL), BF16),
            pltpu.VMEM((2, NSB, GROUP * BLOCK, LANES), BF16),
            pltpu.VMEM((2, TB + BLOCK, LANES), BF16),
            pltpu.VMEM((2, TB + BLOCK, LANES), BF16),
            pltpu.VMEM((TB + 16, D_POOL), F32),
            pltpu.VMEM((TB, D_MODEL), BF16),
            pltpu.VMEM((N_HEADS // 2, 1, LANES), F32),
            pltpu.VMEM((GROUP * BLOCK, LANES), BF16),
            pltpu.VMEM((2, 2 * BLOCK, LANES), BF16),
            pltpu.VMEM((nb, D_IN_PROJ), F32),
            pltpu.VMEM((nb * QROWS, LANES), F32),
            pltpu.VMEM((nb * QROWS, LANES), F32),
            pltpu.VMEM((nb * QROWS, LANES), F32),
            pltpu.VMEM((nb * QROWS, LANES), F32),
            pltpu.VMEM((2, LANES, nb), F32),
            pltpu.VMEM((nb * QROWS, LANES), F32),
            pltpu.VMEM((BB * QROWS, LANES), F32),
            pltpu.VMEM((nb, D_MODEL), BF16),
        ],
        compiler_params=pltpu.CompilerParams(dimension_semantics=("arbitrary",), vmem_limit_bytes=VMEM_LIMIT),
        name="hybrid_step",
    )(sinks, xp, xs, nw, win, qn, kn, wpool, pscale, wout, sp, ck, cv)


def _cache_in(c):
    nb = c.shape[0]
    return jnp.transpose(c, (0, 2, 3, 1)).reshape(nb, LANES, WINDOW)


def _cache_out(c):
    nb = c.shape[0]
    return jnp.transpose(c.reshape(nb, N_KV_HEADS, HEAD_DIM, WINDOW), (0, 3, 1, 2))


def kernel(x_prompt, x_sample, cache_k, cache_v, state_pool, norm_w, w_in, q_norm_w, k_norm_w, sinks, w_pool,
           pool_scale, w_out):
    depth = norm_w.shape[0]
    assert depth == 1 and x_prompt.shape[0] == 1 and x_sample.shape[1] == 1
    seq = x_prompt.shape[1]
    nb = x_sample.shape[0]
    assert seq % TB == 0 and nb % BB == 0 and nb == LANES

    yp, kp, vp, up, ys, kq, vq, pq = _fused_call(
        sinks[0], x_prompt[0], x_sample, norm_w, w_in[0], q_norm_w, k_norm_w, w_pool[0], pool_scale, w_out[0],
        _cache_in(cache_k[0]), _cache_in(cache_v[0]), jnp.transpose(state_pool[0], (1, 0, 2)))
    return (yp[None], ys, _cache_out(kp[None])[None], _cache_out(vp[None])[None],
            jnp.transpose(up, (1, 0, 2))[None],
            _cache_out(kq)[None], _cache_out(vq)[None], jnp.transpose(pq, (1, 0, 2))[None])
```

```python
import functools

import jax
import jax.numpy as jnp
from jax import lax
from jax.experimental import pallas as pl
from jax.experimental.pallas import tpu as pltpu

D_MODEL = 1024
HEAD_DIM = 64
N_HEADS = 8
N_KV_HEADS = 2
GROUP = 4
WINDOW = 128
BLOCK = 128
POOL_WINDOWS = (2, 4, 8, 16)
POOL_STATE = 15
D_ATTN = 512
D_POOL = 512
D_IN_PROJ = 2304
EPS = 1e-6
NEG_INF = -1e30
LOG2E = 1.4426950408889634

O_K = 512
O_V = 640
O_GA = 768
O_U = 1280
O_GP = 1792

LANES = 128
TB = 512
NSB = TB // BLOCK
BB = 64
CB = 4
QROWS = 16
WCHUNK = 256
NSTAGE = 6
VMEM_LIMIT = 58 * 1024 * 1024

F32 = jnp.float32
BF16 = jnp.bfloat16
_NT = (((1,), (1,)), ((), ()))


def _silu_half(h):
    return h + h * jnp.tanh(h)


def _lo_mask():
    return lax.broadcasted_iota(jnp.int32, (1, LANES), 1) < HEAD_DIM


def _pair_rms(zs, lo, w2):
    sq = zs * zs
    s_lo = jnp.sum(jnp.where(lo, sq, 0.0), axis=-1, keepdims=True)
    s_hi = jnp.sum(jnp.where(lo, 0.0, sq), axis=-1, keepdims=True)
    r = lax.rsqrt(jnp.where(lo, s_lo, s_hi) * (1.0 / HEAD_DIM) + EPS)
    return zs * r * w2


def _row_scale(x):
    return lax.rsqrt(jnp.mean(x * x, axis=-1, keepdims=True) + EPS)


def _project(x, win):
    return jnp.dot(x.astype(BF16), win, preferred_element_type=F32) * _row_scale(x)


def _weight_chunks():
    chunks = []
    for c0 in range(0, D_IN_PROJ, WCHUNK):
        gate = O_GA <= c0 < O_U or c0 >= O_GP
        chunks.append((0, c0, 0.5 if gate else 1.0))
    chunks += [(1, c0, 1.0) for c0 in range(0, D_MODEL, WCHUNK)]
    return chunks


def _load_weights(win_hbm, wout_hbm, wpool_f32, nw_ref, qn_ref, kn_ref, stage, sem, win_bf, wout_bf, wpool_bf,
                  qw_ref, kw_ref):
    chunks = _weight_chunks()
    srcs, dsts = (win_hbm, wout_hbm), (win_bf, wout_bf)

    def copy(k):
        which, c0, _ = chunks[k]
        return pltpu.make_async_copy(srcs[which].at[:, pl.ds(c0, WCHUNK)], stage.at[k % NSTAGE], sem.at[k % NSTAGE])

    for k in range(NSTAGE):
        copy(k).start()
    nw_tile = jnp.concatenate([jnp.broadcast_to(nw_ref[:, t * LANES:(t + 1) * LANES], (LANES, LANES)).T
                               for t in range(D_MODEL // LANES)], axis=0)
    nw_rows = jnp.concatenate([nw_tile] * (WCHUNK // LANES), axis=1)

    def convert(ks):
        for k in ks:
            which, c0, scale = chunks[k]
            copy(k).wait()
            w = stage[k % NSTAGE]
            if which == 0:
                w = w * nw_rows if scale == 1.0 else w * (scale * nw_rows)
            dsts[which][:, c0:c0 + WCHUNK] = w.astype(BF16)
            if k + NSTAGE < len(chunks):
                copy(k + NSTAGE).start()

    n_in = sum(1 for which, _, _ in chunks if which == 0)
    convert(range(n_in))
    wpool_bf[...] = jnp.zeros(wpool_bf.shape, BF16)
    for g in range(len(POOL_WINDOWS)):
        d0 = (g % 2) * LANES
        wpool_bf[g // 2, d0:d0 + LANES, d0:d0 + LANES] = wpool_f32[g].astype(BF16)
    qn = qn_ref[...] * (HEAD_DIM ** -0.5 * LOG2E)
    qw_ref[...] = jnp.concatenate([qn, qn], axis=1)
    kw_ref[...] = jnp.concatenate([kn_ref[...], kn_ref[...]], axis=1)
    return lambda: convert(range(n_in, len(chunks)))


def _prompt_init(sinks_ref, kbuf, vbuf, uext, sink2, eye4, maskt):
    kbuf[:, 0:BLOCK, :] = jnp.zeros((2, BLOCK, LANES), BF16)
    vbuf[:, 0:BLOCK, :] = jnp.zeros((2, BLOCK, LANES), BF16)
    uext[0:16, :] = jnp.zeros((16, D_POOL), F32)
    lo = _lo_mask()
    for jj in range(N_HEADS // 2):
        sink2[jj] = jnp.where(lo, sinks_ref[2 * jj], sinks_ref[2 * jj + 1]) * LOG2E
    rr = lax.broadcasted_iota(jnp.int32, (GROUP * BLOCK, LANES), 0)
    ll = lax.broadcasted_iota(jnp.int32, (GROUP * BLOCK, LANES), 1)
    eye4[...] = jnp.where((rr & (BLOCK - 1)) == ll, 1.0, 0.0).astype(BF16)
    cc = lax.broadcasted_iota(jnp.int32, (2 * BLOCK, LANES), 0)
    qr = lax.broadcasted_iota(jnp.int32, (2 * BLOCK, LANES), 1)
    band = (cc >= qr) & (cc <= qr + WINDOW)
    maskt[0] = jnp.where(band & (cc >= BLOCK), 0.0, NEG_INF).astype(BF16)
    maskt[1] = jnp.where(band, 0.0, NEG_INF).astype(BF16)


def _prompt_step(i, last, x_ref, pscale_ref, y_ref, knew_ref, vnew_ref, unew_ref,
                 win_ref, wout_ref, wpool_ref, qw_ref, kw_ref, z_ref, hbuf, qbuf, kbuf, vbuf, uext, mixed, sink2, eye4, maskt,
                 side_work):
    lo = _lo_mask()
    hbuf[...] = x_ref[...].astype(BF16)
    rs = _row_scale(x_ref[...])

    def proj(cols, rows=slice(None)):
        return jnp.dot(hbuf[rows, :], win_ref[:, cols], preferred_element_type=F32) * rs[rows]

    z_ref[:, O_U:] = proj(slice(O_U, D_IN_PROJ))
    uext[16:, :] = z_ref[:, O_U:O_U + D_POOL]
    pos16 = i * TB + lax.broadcasted_iota(jnp.int32, (16, LANES), 0)
    ds = []
    for g, w in enumerate(POOL_WINDOWS):
        acc = uext[:, g * LANES:(g + 1) * LANES]
        sh = 1
        while sh < w:
            acc = acc + pltpu.roll(acc, sh, axis=0)
            sh *= 2
        ug = z_ref[:, O_U + g * LANES:O_U + (g + 1) * LANES]
        cnt = jnp.minimum(w, pos16 + 1).astype(F32)
        d_first = acc[16:32, :] / cnt - ug[0:16, :]
        d_rest = acc[32:, :] * (1.0 / w) - ug[16:, :]
        ds.append(jnp.concatenate([d_first, d_rest], axis=0).astype(BF16))
    uext[0:16, :] = uext[TB:TB + 16, :]

    z_ref[:, :O_K] = proj(slice(0, O_K))

    for pr in range(len(POOL_WINDOWS) // 2):
        cols = slice(pr * 2 * LANES, (pr + 1) * 2 * LANES)
        d2 = jnp.concatenate(ds[2 * pr:2 * pr + 2], axis=1)
        po = jnp.dot(d2, wpool_ref[pr], preferred_element_type=F32) * pscale_ref[:, cols]
        gp = z_ref[:, O_GP + pr * 2 * LANES:O_GP + (pr + 1) * 2 * LANES]
        mixed[:, D_ATTN + pr * 2 * LANES:D_ATTN + (pr + 1) * 2 * LANES] = (po * _silu_half(gp)).astype(BF16)

    for j in range(4):
        qhat = _pair_rms(z_ref[:, j * LANES:(j + 1) * LANES], lo, qw_ref[...])
        q_lo = jnp.where(lo, qhat, 0.0).astype(BF16)
        q_hi = jnp.where(lo, 0.0, qhat).astype(BF16)
        c, g0 = j // 2, 2 * (j % 2)
        for sb in range(NSB):
            rows = slice(sb * BLOCK, (sb + 1) * BLOCK)
            qbuf[c, sb, g0 * BLOCK:(g0 + 1) * BLOCK, :] = q_lo[rows]
            qbuf[c, sb, (g0 + 1) * BLOCK:(g0 + 2) * BLOCK, :] = q_hi[rows]

    half = TB // 2
    for r0 in (0, half):
        z_ref[r0:r0 + half, O_K:O_GA] = proj(slice(O_K, O_GA), slice(r0, r0 + half))
    z_ref[:, O_GA:O_U] = proj(slice(O_GA, O_U))

    khat = _pair_rms(z_ref[:, O_K:O_K + LANES], lo, kw_ref[...])
    kr = pltpu.roll(khat, HEAD_DIM, axis=1)
    kbuf[0, BLOCK:, :] = jnp.where(lo, khat, kr).astype(BF16)
    kbuf[1, BLOCK:, :] = jnp.where(lo, kr, khat).astype(BF16)
    vz = z_ref[:, O_V:O_V + LANES]
    vr = pltpu.roll(vz, HEAD_DIM, axis=1)
    vbuf[0, BLOCK:, :] = jnp.where(lo, vz, vr).astype(BF16)
    vbuf[1, BLOCK:, :] = jnp.where(lo, vr, vz).astype(BF16)

    first = jnp.where(i > 0, 1, 0)
    for sb in range(NSB):
        rows = slice(sb * BLOCK, (sb + 1) * BLOCK)
        mcols = maskt[first] if sb == 0 else maskt[1]
        for c in range(2):
            keys = kbuf[c, sb * BLOCK:(sb + 2) * BLOCK, :]
            vals = vbuf[c, sb * BLOCK:(sb + 2) * BLOCK, :]
            s = lax.dot_general(jnp.concatenate([qbuf[c, sb], eye4[...]], axis=1),
                                jnp.concatenate([keys, mcols], axis=1), _NT,
                                preferred_element_type=F32)
            ps, ms, ls = [], [], []
            for g in range(GROUP):
                sg = s[g * BLOCK:(g + 1) * BLOCK, :]
                m = jnp.max(sg, axis=-1, keepdims=True)
                p = jnp.exp2(sg - m)
                ps.append(p.astype(BF16))
                ms.append(m)
                ls.append(jnp.sum(p, axis=-1, keepdims=True))
            o = jnp.dot(jnp.concatenate(ps, axis=0), vals, preferred_element_type=F32)
            for jj in range(2):
                ev, od = 2 * jj, 2 * jj + 1
                slab = jnp.where(lo, o[ev * BLOCK:(ev + 1) * BLOCK, :], o[od * BLOCK:(od + 1) * BLOCK, :])
                l = jnp.where(lo, ls[ev], ls[od]) + jnp.exp2(sink2[2 * c + jj] - jnp.where(lo, ms[ev], ms[od]))
                col = (2 * c + jj) * LANES
                ga = z_ref[rows, O_GA + col:O_GA + col + LANES]
                mixed[rows, col:col + LANES] = (slab * (1.0 / l) * _silu_half(ga)).astype(BF16)

    kbuf[:, 0:BLOCK, :] = kbuf[:, TB:TB + BLOCK, :]
    vbuf[:, 0:BLOCK, :] = vbuf[:, TB:TB + BLOCK, :]

    y_ref[...] = x_ref[...] + jnp.dot(mixed[...], wout_ref[...], preferred_element_type=F32)
    side_work()

    @pl.when(last)
    def _():
        unew_ref[:, 0, :] = z_ref[TB - POOL_STATE:TB, O_U:O_U + D_POOL]
        tail = slice(TB - WINDOW, TB)
        knew_ref[...] = _pair_rms(z_ref[tail, O_K:O_K + LANES], lo, kw_ref[...]).T
        vnew_ref[...] = z_ref[tail, O_V:O_V + LANES].T


def _sample_init(x_ref, win_ref, qw_ref, kw_ref, z_ref, q3, krep, vrep, snew, kvt):
    nb = x_ref.shape[0]
    lo = _lo_mask()
    z_ref[...] = _project(x_ref[:, 0, :], win_ref[...])
    khat = _pair_rms(z_ref[:, O_K:O_K + LANES], lo, kw_ref[...])
    vnew = z_ref[:, O_V:O_V + LANES]
    kvt[0] = khat.T
    kvt[1] = vnew.T
    q3[...] = jnp.zeros(q3.shape, F32)
    krep[...] = jnp.zeros(krep.shape, F32)
    vrep[...] = jnp.zeros(vrep.shape, F32)
    for j in range(4):
        qhat = _pair_rms(z_ref[:, j * LANES:(j + 1) * LANES], lo, qw_ref[...])
        qrot = pltpu.roll(qhat, HEAD_DIM, axis=1)
        grp_lo = j < 2
        for half in range(2):
            h = 2 * j + half
            src = qhat if (half == 0) == grp_lo else qrot
            q3[pl.ds(h, nb, stride=QROWS), :] = jnp.where(lo if grp_lo else ~lo, src, 0.0)
            krep[pl.ds(h, nb, stride=QROWS), :] = khat
            vrep[pl.ds(h, nb, stride=QROWS), :] = vnew
    s_new = jnp.sum(q3[...] * krep[...], axis=-1, keepdims=True)
    snew[...] = jnp.broadcast_to(s_new, snew.shape)


def _shift_caches(i, nb, ck_ref, cv_ref, knew_ref, vnew_ref, kvt):
    shift = lax.rem(nb - i * CB, nb)
    newest = lax.broadcasted_iota(jnp.int32, (LANES, WINDOW), 1) == WINDOW - 1
    for src, dst, which in ((ck_ref, knew_ref, 0), (cv_ref, vnew_ref, 1)):
        cols = pltpu.roll(kvt[which], shift, axis=1)
        shifted = pltpu.roll(src[...], WINDOW - 1, axis=2)
        for bb in range(CB):
            dst[bb] = jnp.where(newest, cols[:, bb:bb + 1], shifted[bb])


def _sample_attend(i, sinks_ref, ck_ref, cv_ref, q3, vrep, snew, oall):
    qrows = pl.ds(pl.multiple_of(i * (CB * QROWS), CB * QROWS), CB * QROWS)
    rid = lax.broadcasted_iota(jnp.int32, (1, QROWS, 1), 1)
    sink3 = jnp.zeros((1, QROWS, 1), F32)
    for h in range(N_HEADS):
        sink3 = jnp.where(rid == h, sinks_ref[h] * LOG2E, sink3)

    q = q3[qrows, :].reshape(CB, QROWS, LANES)
    s_new = snew[qrows, :].reshape(CB, QROWS, LANES)
    kt = ck_ref[...]
    vt = cv_ref[...]
    s = lax.dot_general(q.astype(BF16), kt.astype(BF16), (((2,), (1,)), ((0,), (0,))), preferred_element_type=F32)
    m = jnp.maximum(jnp.maximum(jnp.max(s, axis=-1, keepdims=True), s_new), sink3)
    p = jnp.exp2(s - m)
    p_new = jnp.exp2(s_new - m)
    l = jnp.sum(p, axis=-1, keepdims=True) + p_new + jnp.exp2(sink3 - m)

    def finish():
        o = lax.dot_general(p.astype(BF16), vt.astype(BF16), (((2,), (2,)), ((0,), (0,))), preferred_element_type=F32)
        o = (o + p_new * vrep[qrows, :].reshape(CB, QROWS, LANES)) / l
        oall[qrows, :] = o.reshape(CB * QROWS, LANES)

    return finish


def _sample_step(j, last, x_ref, pscale_ref, sp_ref, y_ref, spnew_ref, wout_ref, wpool_ref, z_ref, oall, o3, mixed):
    lo = _lo_mask()
    row0 = pl.multiple_of(j * BB, BB)
    rows = pl.ds(row0, BB)
    o3[...] = oall[pl.ds(pl.multiple_of(j * (BB * QROWS), BB * QROWS), BB * QROWS), :]

    for jj in range(4):
        grp_lo = jj < 2
        oh = [o3[pl.ds(2 * jj + half, BB, stride=QROWS), :] for half in range(2)]
        a = oh[0] if grp_lo else pltpu.roll(oh[0], HEAD_DIM, axis=1)
        b = pltpu.roll(oh[1], HEAD_DIM, axis=1) if grp_lo else oh[1]
        ga = z_ref[rows, O_GA + jj * LANES:O_GA + (jj + 1) * LANES]
        mixed[rows, jj * LANES:(jj + 1) * LANES] = (jnp.where(lo, a, b) * _silu_half(ga)).astype(BF16)

    u = z_ref[rows, O_U:O_U + D_POOL]
    spnew_ref[0:POOL_STATE - 1] = sp_ref[1:POOL_STATE]
    spnew_ref[POOL_STATE - 1] = u
    ds = []
    for g, w in enumerate(POOL_WINDOWS):
        cols = slice(g * LANES, (g + 1) * LANES)
        ug = u[:, cols]
        win_sum = ug
        for r in range(POOL_STATE - (w - 1), POOL_STATE):
            win_sum = win_sum + sp_ref[r, :, cols]
        ds.append((win_sum * (1.0 / w) - ug).astype(BF16))
    for pr in range(len(POOL_WINDOWS) // 2):
        cols = slice(pr * 2 * LANES, (pr + 1) * 2 * LANES)
        d2 = jnp.concatenate(ds[2 * pr:2 * pr + 2], axis=1)
        po = jnp.dot(d2, wpool_ref[pr], preferred_element_type=F32) * pscale_ref[:, cols]
        gp = z_ref[rows, O_GP + pr * 2 * LANES:O_GP + (pr + 1) * 2 * LANES]
        mixed[rows, D_ATTN + pr * 2 * LANES:D_ATTN + (pr + 1) * 2 * LANES] = (po * _silu_half(gp)).astype(BF16)

    @pl.when(last)
    def _():
        y_ref[:, 0, :] = x_ref[:, 0, :] + jnp.dot(mixed[...], wout_ref[...], preferred_element_type=F32)


def _fused_kernel(n_prompt, sinks_ref, xp_ref, xs_ref, nw_ref, win_hbm, qn_ref, kn_ref, wpool_f32, pscale_ref, wout_hbm,
                  sp_ref, ckc_ref, cvc_ref,
                  yp_ref, kp_ref, vp_ref, up_ref, ys_ref, kq_ref, vq_ref, pq_ref,
                  win_ref, wout_ref, wpool_ref, qw_ref, kw_ref, stage, sem,
                  z_ref, hbuf, qbuf, kbuf, vbuf, uext, mixed, sink2, eye4, maskt,
                  zs_ref, q3, krep, vrep, snew, kvt, oall, o3, mixed_s):
    i = pl.program_id(0)

    @pl.when(i == 0)
    def _():
        convert_wout = _load_weights(win_hbm, wout_hbm, wpool_f32, nw_ref, qn_ref, kn_ref, stage, sem,
                                     win_ref, wout_ref, wpool_ref, qw_ref, kw_ref)
        _prompt_init(sinks_ref, kbuf, vbuf, uext, sink2, eye4, maskt)
        _sample_init(xs_ref, win_ref, qw_ref, kw_ref, zs_ref, q3, krep, vrep, snew, kvt)
        convert_wout()

    @pl.when(i < n_prompt)
    def _():
        _shift_caches(i, xs_ref.shape[0], ckc_ref, cvc_ref, kq_ref, vq_ref, kvt)
        attend = _sample_attend(i, sinks_ref, ckc_ref, cvc_ref, q3, vrep, snew, oall)
        _prompt_step(i, i == n_prompt - 1, xp_ref, pscale_ref, yp_ref, kp_ref, vp_ref, up_ref,
                     win_ref, wout_ref, wpool_ref, qw_ref, kw_ref, z_ref, hbuf, qbuf, kbuf, vbuf, uext, mixed, sink2, eye4, maskt,
                     attend)

    @pl.when(i >= n_prompt)
    def _():
        _sample_step(i - n_prompt, i == pl.num_programs(0) - 1, xs_ref, pscale_ref, sp_ref,
                     ys_ref, pq_ref, wout_ref, wpool_ref, zs_ref, oall, o3, mixed_s)


def _fused_call(sinks, xp, xs, nw, win, qn, kn, wpool, pscale, wout, ck, cv, sp):
    seq, nb = xp.shape[0], xs.shape[0]
    n_prompt, n_sample = seq // TB, nb // BB
    assert nb == CB * n_prompt
    const2 = lambda i: (0, 0)
    const3 = lambda i: (0, 0, 0)
    pblk = lambda i: (jnp.minimum(i, n_prompt - 1), 0)
    cblk = lambda i: (jnp.minimum(i, n_prompt - 1), 0, 0)
    sidx = lambda i: jnp.clip(i - n_prompt, 0, n_sample - 1)
    mid3 = lambda i: (0, sidx(i), 0)
    once = pl.Buffered(1)
    return pl.pallas_call(
        functools.partial(_fused_kernel, n_prompt),
        grid=(n_prompt + n_sample,),
        in_specs=[
            pl.BlockSpec(memory_space=pltpu.SMEM),
            pl.BlockSpec((TB, D_MODEL), pblk),
            pl.BlockSpec((nb, 1, D_MODEL), const3, pipeline_mode=once),
            pl.BlockSpec((1, D_MODEL), const2),
            pl.BlockSpec(memory_space=pl.ANY),
            pl.BlockSpec((1, HEAD_DIM), const2),
            pl.BlockSpec((1, HEAD_DIM), const2),
            pl.BlockSpec((4, LANES, LANES), const3, pipeline_mode=once),
            pl.BlockSpec((1, D_POOL), const2),
            pl.BlockSpec(memory_space=pl.ANY),
            pl.BlockSpec((POOL_STATE, BB, D_POOL), mid3),
            pl.BlockSpec((CB, LANES, WINDOW), cblk),
            pl.BlockSpec((CB, LANES, WINDOW), cblk),
        ],
        out_specs=[
            pl.BlockSpec((TB, D_MODEL), pblk),
            pl.BlockSpec((LANES, WINDOW), const2),
            pl.BlockSpec((LANES, WINDOW), const2),
            pl.BlockSpec((POOL_STATE, 1, D_POOL), const3),
            pl.BlockSpec((nb, 1, D_MODEL), const3, pipeline_mode=once),
            pl.BlockSpec((CB, LANES, WINDOW), cblk),
            pl.BlockSpec((CB, LANES, WINDOW), cblk),
            pl.BlockSpec((POOL_STATE, BB, D_POOL), mid3),
        ],
        out_shape=[
            jax.ShapeDtypeStruct((seq, D_MODEL), F32),
            jax.ShapeDtypeStruct((LANES, WINDOW), F32),
            jax.ShapeDtypeStruct((LANES, WINDOW), F32),
            jax.ShapeDtypeStruct((POOL_STATE, 1, D_POOL), F32),
            jax.ShapeDtypeStruct((nb, 1, D_MODEL), F32),
            jax.ShapeDtypeStruct((nb, LANES, WINDOW), F32),
            jax.ShapeDtypeStruct((nb, LANES, WINDOW), F32),
            jax.ShapeDtypeStruct((POOL_STATE, nb, D_POOL), F32),
        ],
        scratch_shapes=[
            pltpu.VMEM((D_MODEL, D_IN_PROJ), BF16),
            pltpu.VMEM((D_MODEL, D_MODEL), BF16),
            pltpu.VMEM((2, 2 * LANES, 2 * LANES), BF16),
            pltpu.VMEM((1, LANES), F32),
            pltpu.VMEM((1, LANES), F32),
            pltpu.VMEM((NSTAGE, D_MODEL, WCHUNK), F32),
            pltpu.SemaphoreType.DMA((NSTAGE,)),
            pltpu.VMEM((TB, D_IN_PROJ), F32),
            pltpu.VMEM((TB, D_MODEL), BF16),
            pltpu.VMEM((2, NSB, GROUP * BLOCK, LANES), BF16),
            pltpu.VMEM((2, TB + BLOCK, LANES), BF16),
            pltpu.VMEM((2, TB + BLOCK, LANES), BF16),
            pltpu.VMEM((TB + 16, D_POOL), F32),
            pltpu.VMEM((TB, D_MODEL), BF16),
            pltpu.VMEM((N_HEADS // 2, 1, LANES), F32),
            pltpu.VMEM((GROUP * BLOCK, LANES), BF16),
            pltpu.VMEM((2, 2 * BLOCK, LANES), BF16),
            pltpu.VMEM((nb, D_IN_PROJ), F32),
            pltpu.VMEM((nb * QROWS, LANES), F32),
            pltpu.VMEM((nb * QROWS, LANES), F32),
            pltpu.VMEM((nb * QROWS, LANES), F32),
            pltpu.VMEM((nb * QROWS, LANES), F32),
            pltpu.VMEM((2, LANES, nb), F32),
            pltpu.VMEM((nb * QROWS, LANES), F32),
            pltpu.VMEM((BB * QROWS, LANES), F32),
            pltpu.VMEM((nb, D_MODEL), BF16),
        ],
        compiler_params=pltpu.CompilerParams(dimension_semantics=("arbitrary",), vmem_limit_bytes=VMEM_LIMIT),
        name="hybrid_step",
    )(sinks, xp, xs, nw, win, qn, kn, wpool, pscale, wout, sp, ck, cv)


def _cache_in(c):
    nb = c.shape[0]
    return jnp.transpose(c, (0, 2, 3, 1)).reshape(nb, LANES, WINDOW)


def _cache_out(c):
    nb = c.shape[0]
    return jnp.transpose(c.reshape(nb, N_KV_HEADS, HEAD_DIM, WINDOW), (0, 3, 1, 2))


def kernel(x_prompt, x_sample, cache_k, cache_v, state_pool, norm_w, w_in, q_norm_w, k_norm_w, sinks, w_pool,
           pool_scale, w_out):
    depth = norm_w.shape[0]
    assert depth == 1 and x_prompt.shape[0] == 1 and x_sample.shape[1] == 1
    seq = x_prompt.shape[1]
    nb = x_sample.shape[0]
    assert seq % TB == 0 and nb % BB == 0 and nb == LANES

    yp, kp, vp, up, ys, kq, vq, pq = _fused_call(
        sinks[0], x_prompt[0], x_sample, norm_w, w_in[0], q_norm_w, k_norm_w, w_pool[0], pool_scale, w_out[0],
        _cache_in(cache_k[0]), _cache_in(cache_v[0]), jnp.transpose(state_pool[0], (1, 0, 2)))
    return (yp[None], ys, _cache_out(kp[None])[None], _cache_out(vp[None])[None],
            jnp.transpose(up, (1, 0, 2))[None],
            _cache_out(kq)[None], _cache_out(vq)[None], jnp.transpose(pq, (1, 0, 2))[None])
```

```python
import functools

import jax
import jax.numpy as jnp
from jax import lax
from jax.experimental import pallas as pl
from jax.experimental.pallas import tpu as pltpu

D_MODEL = 1024
HEAD_DIM = 64
N_HEADS = 8
N_KV_HEADS = 2
GROUP = 4
WINDOW = 128
BLOCK = 128
POOL_WINDOWS = (2, 4, 8, 16)
POOL_STATE = 15
D_ATTN = 512
D_POOL = 512
D_IN_PROJ = 2304
EPS = 1e-6
NEG_INF = -1e30
LOG2E = 1.4426950408889634

O_K = 512
O_V = 640
O_GA = 768
O_U = 1280
O_GP = 1792

LANES = 128
TB = 512
NSB = TB // BLOCK
BB = 128
CB = 4
QROWS = 16
WCHUNK = 256
NSTAGE = 6
VMEM_LIMIT = 58 * 1024 * 1024

F32 = jnp.float32
BF16 = jnp.bfloat16
_NT = (((1,), (1,)), ((), ()))


def _silu_half(h):
    return h + h * jnp.tanh(h)


def _lo_mask():
    return lax.broadcasted_iota(jnp.int32, (1, LANES), 1) < HEAD_DIM


def _pair_rms(zs, lo, w2):
    sq = zs * zs
    s_lo = jnp.sum(jnp.where(lo, sq, 0.0), axis=-1, keepdims=True)
    s_hi = jnp.sum(jnp.where(lo, 0.0, sq), axis=-1, keepdims=True)
    r = lax.rsqrt(jnp.where(lo, s_lo, s_hi) * (1.0 / HEAD_DIM) + EPS)
    return zs * r * w2


def _row_scale(x):
    return lax.rsqrt(jnp.mean(x * x, axis=-1, keepdims=True) + EPS)


def _project(x, win):
    return jnp.dot(x.astype(BF16), win, preferred_element_type=F32) * _row_scale(x)


def _weight_chunks():
    chunks = []
    for c0 in range(0, D_IN_PROJ, WCHUNK):
        gate = O_GA <= c0 < O_U or c0 >= O_GP
        chunks.append((0, c0, 0.5 if gate else 1.0))
    chunks += [(1, c0, 1.0) for c0 in range(0, D_MODEL, WCHUNK)]
    return chunks


def _load_weights(win_hbm, wout_hbm, wpool_f32, nw_ref, qn_ref, kn_ref, stage, sem, win_bf, wout_bf, wpool_bf,
                  qw_ref, kw_ref):
    chunks = _weight_chunks()
    srcs, dsts = (win_hbm, wout_hbm), (win_bf, wout_bf)

    def copy(k):
        which, c0, _ = chunks[k]
        return pltpu.make_async_copy(srcs[which].at[:, pl.ds(c0, WCHUNK)], stage.at[k % NSTAGE], sem.at[k % NSTAGE])

    for k in range(NSTAGE):
        copy(k).start()
    nw_tile = jnp.concatenate([jnp.broadcast_to(nw_ref[:, t * LANES:(t + 1) * LANES], (LANES, LANES)).T
                               for t in range(D_MODEL // LANES)], axis=0)
    nw_rows = jnp.concatenate([nw_tile] * (WCHUNK // LANES), axis=1)

    def convert(ks):
        for k in ks:
            which, c0, scale = chunks[k]
            copy(k).wait()
            w = stage[k % NSTAGE]
            if which == 0:
                w = w * nw_rows if scale == 1.0 else w * (scale * nw_rows)
            dsts[which][:, c0:c0 + WCHUNK] = w.astype(BF16)
            if k + NSTAGE < len(chunks):
                copy(k + NSTAGE).start()

    n_in = sum(1 for which, _, _ in chunks if which == 0)
    convert(range(n_in))
    wpool_bf[...] = jnp.zeros(wpool_bf.shape, BF16)
    for g in range(len(POOL_WINDOWS)):
        d0 = (g % 2) * LANES
        wpool_bf[g // 2, d0:d0 + LANES, d0:d0 + LANES] = wpool_f32[g].astype(BF16)
    qn = qn_ref[...] * (HEAD_DIM ** -0.5 * LOG2E)
    qw_ref[...] = jnp.concatenate([qn, qn], axis=1)
    kw_ref[...] = jnp.concatenate([kn_ref[...], kn_ref[...]], axis=1)
    return lambda: convert(range(n_in, len(chunks)))


def _prompt_init(sinks_ref, kbuf, vbuf, uext, sink2, eye4, maskt):
    kbuf[:, 0:BLOCK, :] = jnp.zeros((2, BLOCK, LANES), BF16)
    vbuf[:, 0:BLOCK, :] = jnp.zeros((2, BLOCK, LANES), BF16)
    uext[0:16, :] = jnp.zeros((16, D_POOL), F32)
    lo = _lo_mask()
    for jj in range(N_HEADS // 2):
        sink2[jj] = jnp.where(lo, sinks_ref[2 * jj], sinks_ref[2 * jj + 1]) * LOG2E
    rr = lax.broadcasted_iota(jnp.int32, (GROUP * BLOCK, LANES), 0)
    ll = lax.broadcasted_iota(jnp.int32, (GROUP * BLOCK, LANES), 1)
    eye4[...] = jnp.where((rr & (BLOCK - 1)) == ll, 1.0, 0.0).astype(BF16)
    cc = lax.broadcasted_iota(jnp.int32, (2 * BLOCK, LANES), 0)
    qr = lax.broadcasted_iota(jnp.int32, (2 * BLOCK, LANES), 1)
    band = (cc >= qr) & (cc <= qr + WINDOW)
    maskt[0] = jnp.where(band & (cc >= BLOCK), 0.0, NEG_INF).astype(BF16)
    maskt[1] = jnp.where(band, 0.0, NEG_INF).astype(BF16)


def _prompt_step(i, last, x_ref, pscale_ref, y_ref, knew_ref, vnew_ref, unew_ref,
                 win_ref, wout_ref, wpool_ref, qw_ref, kw_ref, z_ref, hbuf, qbuf, kbuf, vbuf, uext, mixed, sink2, eye4, maskt,
                 side_work):
    lo = _lo_mask()
    hbuf[...] = x_ref[...].astype(BF16)
    rs = _row_scale(x_ref[...])

    def proj(cols, rows=slice(None)):
        return jnp.dot(hbuf[rows, :], win_ref[:, cols], preferred_element_type=F32) * rs[rows]

    z_ref[:, O_U:] = proj(slice(O_U, D_IN_PROJ))
    uext[16:, :] = z_ref[:, O_U:O_U + D_POOL]
    pos16 = i * TB + lax.broadcasted_iota(jnp.int32, (16, LANES), 0)
    ds = []
    for g, w in enumerate(POOL_WINDOWS):
        acc = uext[:, g * LANES:(g + 1) * LANES]
        sh = 1
        while sh < w:
            acc = acc + pltpu.roll(acc, sh, axis=0)
            sh *= 2
        ug = z_ref[:, O_U + g * LANES:O_U + (g + 1) * LANES]
        cnt = jnp.minimum(w, pos16 + 1).astype(F32)
        d_first = acc[16:32, :] / cnt - ug[0:16, :]
        d_rest = acc[32:, :] * (1.0 / w) - ug[16:, :]
        ds.append(jnp.concatenate([d_first, d_rest], axis=0).astype(BF16))
    uext[0:16, :] = uext[TB:TB + 16, :]

    z_ref[:, :O_K] = proj(slice(0, O_K))

    for pr in range(len(POOL_WINDOWS) // 2):
        cols = slice(pr * 2 * LANES, (pr + 1) * 2 * LANES)
        d2 = jnp.concatenate(ds[2 * pr:2 * pr + 2], axis=1)
        po = jnp.dot(d2, wpool_ref[pr], preferred_element_type=F32) * pscale_ref[:, cols]
        gp = z_ref[:, O_GP + pr * 2 * LANES:O_GP + (pr + 1) * 2 * LANES]
        mixed[:, D_ATTN + pr * 2 * LANES:D_ATTN + (pr + 1) * 2 * LANES] = (po * _silu_half(gp)).astype(BF16)

    for j in range(4):
        qhat = _pair_rms(z_ref[:, j * LANES:(j + 1) * LANES], lo, qw_ref[...])
        q_lo = jnp.where(lo, qhat, 0.0).astype(BF16)
        q_hi = jnp.where(lo, 0.0, qhat).astype(BF16)
        c, g0 = j // 2, 2 * (j % 2)
        for sb in range(NSB):
            rows = slice(sb * BLOCK, (sb + 1) * BLOCK)
            qbuf[c, sb, g0 * BLOCK:(g0 + 1) * BLOCK, :] = q_lo[rows]
            qbuf[c, sb, (g0 + 1) * BLOCK:(g0 + 2) * BLOCK, :] = q_hi[rows]

    half = TB // 2
    for r0 in (0, half):
        z_ref[r0:r0 + half, O_K:O_GA] = proj(slice(O_K, O_GA), slice(r0, r0 + half))
    z_ref[:, O_GA:O_U] = proj(slice(O_GA, O_U))

    khat = _pair_rms(z_ref[:, O_K:O_K + LANES], lo, kw_ref[...])
    kr = pltpu.roll(khat, HEAD_DIM, axis=1)
    kbuf[0, BLOCK:, :] = jnp.where(lo, khat, kr).astype(BF16)
    kbuf[1, BLOCK:, :] = jnp.where(lo, kr, khat).astype(BF16)
    vz = z_ref[:, O_V:O_V + LANES]
    vr = pltpu.roll(vz, HEAD_DIM, axis=1)
    vbuf[0, BLOCK:, :] = jnp.where(lo, vz, vr).astype(BF16)
    vbuf[1, BLOCK:, :] = jnp.where(lo, vr, vz).astype(BF16)

    first = jnp.where(i > 0, 1, 0)
    for sb in range(NSB):
        rows = slice(sb * BLOCK, (sb + 1) * BLOCK)
        mcols = maskt[first] if sb == 0 else maskt[1]
        for c in range(2):
            keys = kbuf[c, sb * BLOCK:(sb + 2) * BLOCK, :]
            vals = vbuf[c, sb * BLOCK:(sb + 2) * BLOCK, :]
            s = lax.dot_general(jnp.concatenate([qbuf[c, sb], eye4[...]], axis=1),
                                jnp.concatenate([keys, mcols], axis=1), _NT,
                                preferred_element_type=F32)
            ps, ms, ls = [], [], []
            for g in range(GROUP):
                sg = s[g * BLOCK:(g + 1) * BLOCK, :]
                m = jnp.max(sg, axis=-1, keepdims=True)
                p = jnp.exp2(sg - m)
                ps.append(p.astype(BF16))
                ms.append(m)
                ls.append(jnp.sum(p, axis=-1, keepdims=True))
            o = jnp.dot(jnp.concatenate(ps, axis=0), vals, preferred_element_type=F32)
            for jj in range(2):
                ev, od = 2 * jj, 2 * jj + 1
                slab = jnp.where(lo, o[ev * BLOCK:(ev + 1) * BLOCK, :], o[od * BLOCK:(od + 1) * BLOCK, :])
                l = jnp.where(lo, ls[ev], ls[od]) + jnp.exp2(sink2[2 * c + jj] - jnp.where(lo, ms[ev], ms[od]))
                col = (2 * c + jj) * LANES
                ga = z_ref[rows, O_GA + col:O_GA + col + LANES]
                mixed[rows, col:col + LANES] = (slab * (1.0 / l) * _silu_half(ga)).astype(BF16)

    kbuf[:, 0:BLOCK, :] = kbuf[:, TB:TB + BLOCK, :]
    vbuf[:, 0:BLOCK, :] = vbuf[:, TB:TB + BLOCK, :]

    y_ref[...] = x_ref[...] + jnp.dot(mixed[...], wout_ref[...], preferred_element_type=F32)
    side_work()

    @pl.when(last)
    def _():
        unew_ref[:, 0, :] = z_ref[TB - POOL_STATE:TB, O_U:O_U + D_POOL]
        tail = slice(TB - WINDOW, TB)
        knew_ref[...] = _pair_rms(z_ref[tail, O_K:O_K + LANES], lo, kw_ref[...]).T
        vnew_ref[...] = z_ref[tail, O_V:O_V + LANES].T


def _sample_init(x_ref, win_ref, qw_ref, kw_ref, z_ref, q3, krep, vrep, snew, kvt):
    nb = x_ref.shape[0]
    lo = _lo_mask()
    z_ref[...] = _project(x_ref[:, 0, :], win_ref[...])
    khat = _pair_rms(z_ref[:, O_K:O_K + LANES], lo, kw_ref[...])
    vnew = z_ref[:, O_V:O_V + LANES]
    kvt[0] = khat.T
    kvt[1] = vnew.T
    q3[...] = jnp.zeros(q3.shape, F32)
    krep[...] = jnp.zeros(krep.shape, F32)
    vrep[...] = jnp.zeros(vrep.shape, F32)
    for j in range(4):
        qhat = _pair_rms(z_ref[:, j * LANES:(j + 1) * LANES], lo, qw_ref[...])
        qrot = pltpu.roll(qhat, HEAD_DIM, axis=1)
        grp_lo = j < 2
        for half in range(2):
            h = 2 * j + half
            src = qhat if (half == 0) == grp_lo else qrot
            q3[pl.ds(h, nb, stride=QROWS), :] = jnp.where(lo if grp_lo else ~lo, src, 0.0)
            krep[pl.ds(h, nb, stride=QROWS), :] = khat
            vrep[pl.ds(h, nb, stride=QROWS), :] = vnew
    s_new = jnp.sum(q3[...] * krep[...], axis=-1, keepdims=True)
    snew[...] = jnp.broadcast_to(s_new, snew.shape)


def _shift_caches(i, nb, ck_ref, cv_ref, knew_ref, vnew_ref, kvt):
    shift = lax.rem(nb - i * CB, nb)
    newest = lax.broadcasted_iota(jnp.int32, (LANES, WINDOW), 1) == WINDOW - 1
    for src, dst, which in ((ck_ref, knew_ref, 0), (cv_ref, vnew_ref, 1)):
        cols = pltpu.roll(kvt[which], shift, axis=1)
        shifted = pltpu.roll(src[...], WINDOW - 1, axis=2)
        for bb in range(CB):
            dst[bb] = jnp.where(newest, cols[:, bb:bb + 1], shifted[bb])


def _sample_attend(i, sinks_ref, ck_ref, cv_ref, q3, vrep, snew, oall):
    qrows = pl.ds(pl.multiple_of(i * (CB * QROWS), CB * QROWS), CB * QROWS)
    rid = lax.broadcasted_iota(jnp.int32, (1, QROWS, 1), 1)
    sink3 = jnp.zeros((1, QROWS, 1), F32)
    for h in range(N_HEADS):
        sink3 = jnp.where(rid == h, sinks_ref[h] * LOG2E, sink3)

    q = q3[qrows, :].reshape(CB, QROWS, LANES)
    s_new = snew[qrows, :].reshape(CB, QROWS, LANES)
    kt = ck_ref[...]
    vt = cv_ref[...]
    s = lax.dot_general(q.astype(BF16), kt.astype(BF16), (((2,), (1,)), ((0,), (0,))), preferred_element_type=F32)
    m = jnp.maximum(jnp.maximum(jnp.max(s, axis=-1, keepdims=True), s_new), sink3)
    p = jnp.exp2(s - m)
    p_new = jnp.exp2(s_new - m)
    l = jnp.sum(p, axis=-1, keepdims=True) + p_new + jnp.exp2(sink3 - m)

    def finish():
        o = lax.dot_general(p.astype(BF16), vt.astype(BF16), (((2,), (2,)), ((0,), (0,))), preferred_element_type=F32)
        o = (o + p_new * vrep[qrows, :].reshape(CB, QROWS, LANES)) / l
        oall[qrows, :] = o.reshape(CB * QROWS, LANES)

    return finish


def _sample_step(j, last, x_ref, pscale_ref, sp_ref, y_ref, spnew_ref, wout_ref, wpool_ref, z_ref, oall, o3, mixed):
    lo = _lo_mask()
    row0 = pl.multiple_of(j * BB, BB)
    rows = pl.ds(row0, BB)
    o3[...] = oall[pl.ds(pl.multiple_of(j * (BB * QROWS), BB * QROWS), BB * QROWS), :]

    for jj in range(4):
        grp_lo = jj < 2
        oh = [o3[pl.ds(2 * jj + half, BB, stride=QROWS), :] for half in range(2)]
        a = oh[0] if grp_lo else pltpu.roll(oh[0], HEAD_DIM, axis=1)
        b = pltpu.roll(oh[1], HEAD_DIM, axis=1) if grp_lo else oh[1]
        ga = z_ref[rows, O_GA + jj * LANES:O_GA + (jj + 1) * LANES]
        mixed[rows, jj * LANES:(jj + 1) * LANES] = (jnp.where(lo, a, b) * _silu_half(ga)).astype(BF16)

    u = z_ref[rows, O_U:O_U + D_POOL]
    spnew_ref[0:POOL_STATE - 1] = sp_ref[1:POOL_STATE]
    spnew_ref[POOL_STATE - 1] = u
    ds = []
    for g, w in enumerate(POOL_WINDOWS):
        cols = slice(g * LANES, (g + 1) * LANES)
        ug = u[:, cols]
        win_sum = ug
        for r in range(POOL_STATE - (w - 1), POOL_STATE):
            win_sum = win_sum + sp_ref[r, :, cols]
        ds.append((win_sum * (1.0 / w) - ug).astype(BF16))
    for pr in range(len(POOL_WINDOWS) // 2):
        cols = slice(pr * 2 * LANES, (pr + 1) * 2 * LANES)
        d2 = jnp.concatenate(ds[2 * pr:2 * pr + 2], axis=1)
        po = jnp.dot(d2, wpool_ref[pr], preferred_element_type=F32) * pscale_ref[:, cols]
        gp = z_ref[rows, O_GP + pr * 2 * LANES:O_GP + (pr + 1) * 2 * LANES]
        mixed[rows, D_ATTN + pr * 2 * LANES:D_ATTN + (pr + 1) * 2 * LANES] = (po * _silu_half(gp)).astype(BF16)

    @pl.when(last)
    def _():
        y_ref[:, 0, :] = x_ref[:, 0, :] + jnp.dot(mixed[...], wout_ref[...], preferred_element_type=F32)


def _fused_kernel(n_prompt, sinks_ref, xp_ref, xs_ref, nw_ref, win_hbm, qn_ref, kn_ref, wpool_f32, pscale_ref, wout_hbm,
                  sp_ref, ckc_ref, cvc_ref,
                  yp_ref, kp_ref, vp_ref, up_ref, ys_ref, kq_ref, vq_ref, pq_ref,
                  win_ref, wout_ref, wpool_ref, qw_ref, kw_ref, stage, sem,
                  z_ref, hbuf, qbuf, kbuf, vbuf, uext, mixed, sink2, eye4, maskt,
                  zs_ref, q3, krep, vrep, snew, kvt, oall, o3, mixed_s):
    i = pl.program_id(0)

    @pl.when(i == 0)
    def _():
        convert_wout = _load_weights(win_hbm, wout_hbm, wpool_f32, nw_ref, qn_ref, kn_ref, stage, sem,
                                     win_ref, wout_ref, wpool_ref, qw_ref, kw_ref)
        _prompt_init(sinks_ref, kbuf, vbuf, uext, sink2, eye4, maskt)
        _sample_init(xs_ref, win_ref, qw_ref, kw_ref, zs_ref, q3, krep, vrep, snew, kvt)
        convert_wout()

    @pl.when(i < n_prompt)
    def _():
        _shift_caches(i, xs_ref.shape[0], ckc_ref, cvc_ref, kq_ref, vq_ref, kvt)
        attend = _sample_attend(i, sinks_ref, ckc_ref, cvc_ref, q3, vrep, snew, oall)
        _prompt_step(i, i == n_prompt - 1, xp_ref, pscale_ref, yp_ref, kp_ref, vp_ref, up_ref,
                     win_ref, wout_ref, wpool_ref, qw_ref, kw_ref, z_ref, hbuf, qbuf, kbuf, vbuf, uext, mixed, sink2, eye4, maskt,
                     attend)

    @pl.when(i >= n_prompt)
    def _():
        _sample_step(i - n_prompt, i == pl.num_programs(0) - 1, xs_ref, pscale_ref, sp_ref,
                     ys_ref, pq_ref, wout_ref, wpool_ref, zs_ref, oall, o3, mixed_s)


def _fused_call(sinks, xp, xs, nw, win, qn, kn, wpool, pscale, wout, ck, cv, sp):
    seq, nb = xp.shape[0], xs.shape[0]
    n_prompt, n_sample = seq // TB, nb // BB
    assert nb == CB * n_prompt
    assert n_sample == 1
    const2 = lambda i: (0, 0)
    const3 = lambda i: (0, 0, 0)
    pblk = lambda i: (jnp.minimum(i, n_prompt - 1), 0)
    cblk = lambda i: (jnp.minimum(i, n_prompt - 1), 0, 0)
    sidx = lambda i: jnp.clip(i - n_prompt, 0, n_sample - 1)
    mid3 = lambda i: (0, sidx(i), 0)
    once = pl.Buffered(1)
    return pl.pallas_call(
        functools.partial(_fused_kernel, n_prompt),
        grid=(n_prompt + n_sample,),
        in_specs=[
            pl.BlockSpec(memory_space=pltpu.SMEM),
            pl.BlockSpec((TB, D_MODEL), pblk),
            pl.BlockSpec((nb, 1, D_MODEL), const3, pipeline_mode=once),
            pl.BlockSpec((1, D_MODEL), const2),
            pl.BlockSpec(memory_space=pl.ANY),
            pl.BlockSpec((1, HEAD_DIM), const2),
            pl.BlockSpec((1, HEAD_DIM), const2),
            pl.BlockSpec((4, LANES, LANES), const3, pipeline_mode=once),
            pl.BlockSpec((1, D_POOL), const2),
            pl.BlockSpec(memory_space=pl.ANY),
            pl.BlockSpec((POOL_STATE, BB, D_POOL), mid3, pipeline_mode=once),
            pl.BlockSpec((CB, LANES, WINDOW), cblk),
            pl.BlockSpec((CB, LANES, WINDOW), cblk),
        ],
        out_specs=[
            pl.BlockSpec((TB, D_MODEL), pblk),
            pl.BlockSpec((LANES, WINDOW), const2),
            pl.BlockSpec((LANES, WINDOW), const2),
            pl.BlockSpec((POOL_STATE, 1, D_POOL), const3),
            pl.BlockSpec((nb, 1, D_MODEL), const3, pipeline_mode=once),
            pl.BlockSpec((CB, LANES, WINDOW), cblk),
            pl.BlockSpec((CB, LANES, WINDOW), cblk),
            pl.BlockSpec((POOL_STATE, BB, D_POOL), mid3, pipeline_mode=once),
        ],
        out_shape=[
            jax.ShapeDtypeStruct((seq, D_MODEL), F32),
            jax.ShapeDtypeStruct((LANES, WINDOW), F32),
            jax.ShapeDtypeStruct((LANES, WINDOW), F32),
            jax.ShapeDtypeStruct((POOL_STATE, 1, D_POOL), F32),
            jax.ShapeDtypeStruct((nb, 1, D_MODEL), F32),
            jax.ShapeDtypeStruct((nb, LANES, WINDOW), F32),
            jax.ShapeDtypeStruct((nb, LANES, WINDOW), F32),
            jax.ShapeDtypeStruct((POOL_STATE, nb, D_POOL), F32),
        ],
        scratch_shapes=[
            pltpu.VMEM((D_MODEL, D_IN_PROJ), BF16),
            pltpu.VMEM((D_MODEL, D_MODEL), BF16),
            pltpu.VMEM((2, 2 * LANES, 2 * LANES), BF16),
            pltpu.VMEM((1, LANES), F32),
            pltpu.VMEM((1, LANES), F32),
            pltpu.VMEM((NSTAGE, D_MODEL, WCHUNK), F32),
            pltpu.SemaphoreType.DMA((NSTAGE,)),
            pltpu.VMEM((TB, D_IN_PROJ), F32),
            pltpu.VMEM((TB, D_MODEL), BF16),
            pltpu.VMEM((2, NSB, GROUP * BLOCK, LANES), BF16),
            pltpu.VMEM((2, TB + BLOCK, LANES), BF16),
            pltpu.VMEM((2, TB + BLOCK, LANES), BF16),
            pltpu.VMEM((TB + 16, D_POOL), F32),
            pltpu.VMEM((TB, D_MODEL), BF16),
            pltpu.VMEM((N_HEADS // 2, 1, LANES), F32),
            pltpu.VMEM((GROUP * BLOCK, LANES), BF16),
            pltpu.VMEM((2, 2 * BLOCK, LANES), BF16),
            pltpu.VMEM((nb, D_IN_PROJ), F32),
            pltpu.VMEM((nb * QROWS, LANES), F32),
            pltpu.VMEM((nb * QROWS, LANES), F32),
            pltpu.VMEM((nb * QROWS, LANES), F32),
            pltpu.VMEM((nb * QROWS, LANES), F32),
            pltpu.VMEM((2, LANES, nb), F32),
            pltpu.VMEM((nb * QROWS, LANES), F32),
            pltpu.VMEM((BB * QROWS, LANES), F32),
            pltpu.VMEM((nb, D_MODEL), BF16),
        ],
        compiler_params=pltpu.CompilerParams(dimension_semantics=("arbitrary",), vmem_limit_bytes=VMEM_LIMIT),
        name="hybrid_step",
    )(sinks, xp, xs, nw, win, qn, kn, wpool, pscale, wout, sp, ck, cv)


def _cache_in(c):
    nb = c.shape[0]
    return jnp.transpose(c, (0, 2, 3, 1)).reshape(nb, LANES, WINDOW)


def _cache_out(c):
    nb = c.shape[0]
    return jnp.transpose(c.reshape(nb, N_KV_HEADS, HEAD_DIM, WINDOW), (0, 3, 1, 2))


def kernel(x_prompt, x_sample, cache_k, cache_v, state_pool, norm_w, w_in, q_norm_w, k_norm_w, sinks, w_pool,
           pool_scale, w_out):
    depth = norm_w.shape[0]
    assert depth == 1 and x_prompt.shape[0] == 1 and x_sample.shape[1] == 1
    seq = x_prompt.shape[1]
    nb = x_sample.shape[0]
    assert seq % TB == 0 and nb % BB == 0 and nb == LANES

    yp, kp, vp, up, ys, kq, vq, pq = _fused_call(
        sinks[0], x_prompt[0], x_sample, norm_w, w_in[0], q_norm_w, k_norm_w, w_pool[0], pool_scale, w_out[0],
        _cache_in(cache_k[0]), _cache_in(cache_v[0]), jnp.transpose(state_pool[0], (1, 0, 2)))
    return (yp[None], ys, _cache_out(kp[None])[None], _cache_out(vp[None])[None],
            jnp.transpose(up, (1, 0, 2))[None],
            _cache_out(kq)[None], _cache_out(vq)[None], jnp.transpose(pq, (1, 0, 2))[None])
```

```python
import functools

import jax
import jax.numpy as jnp
from jax import lax
from jax.experimental import pallas as pl
from jax.experimental.pallas import tpu as pltpu

D_MODEL = 1024
HEAD_DIM = 64
N_HEADS = 8
N_KV_HEADS = 2
GROUP = 4
WINDOW = 128
BLOCK = 128
POOL_WINDOWS = (2, 4, 8, 16)
POOL_STATE = 15
D_ATTN = 512
D_POOL = 512
D_IN_PROJ = 2304
EPS = 1e-6
NEG_INF = -1e30
LOG2E = 1.4426950408889634

O_K = 512
O_V = 640
O_GA = 768
O_U = 1280
O_GP = 1792

LANES = 128
TB = 512
NSB = TB // BLOCK
BB = 128
CB = 4
QROWS = 16
WCHUNK = 256
NSTAGE = 6
VMEM_LIMIT = 58 * 1024 * 1024

F32 = jnp.float32
BF16 = jnp.bfloat16
_NT = (((1,), (1,)), ((), ()))


def _silu_half(h):
    return h + h * jnp.tanh(h)


def _lo_mask():
    return lax.broadcasted_iota(jnp.int32, (1, LANES), 1) < HEAD_DIM


def _pair_rms(zs, lo, w2):
    sq = zs * zs
    s_lo = jnp.sum(jnp.where(lo, sq, 0.0), axis=-1, keepdims=True)
    s_hi = jnp.sum(jnp.where(lo, 0.0, sq), axis=-1, keepdims=True)
    r = lax.rsqrt(jnp.where(lo, s_lo, s_hi) * (1.0 / HEAD_DIM) + EPS)
    return zs * r * w2


def _row_scale(x):
    return lax.rsqrt(jnp.mean(x * x, axis=-1, keepdims=True) + EPS)


def _project(x, win):
    return jnp.dot(x.astype(BF16), win, preferred_element_type=F32) * _row_scale(x)


def _weight_chunks():
    chunks = []
    for c0 in range(0, D_IN_PROJ, WCHUNK):
        gate = O_GA <= c0 < O_U or c0 >= O_GP
        chunks.append((0, c0, 0.5 if gate else 1.0))
    chunks += [(1, c0, 1.0) for c0 in range(0, D_MODEL, WCHUNK)]
    return chunks


def _load_weights(win_hbm, wout_hbm, wpool_f32, nw_ref, qn_ref, kn_ref, stage, sem, win_bf, wout_bf, wpool_bf,
                  qw_ref, kw_ref):
    chunks = _weight_chunks()
    srcs, dsts = (win_hbm, wout_hbm), (win_bf, wout_bf)

    def copy(k):
        which, c0, _ = chunks[k]
        return pltpu.make_async_copy(srcs[which].at[:, pl.ds(c0, WCHUNK)], stage.at[k % NSTAGE], sem.at[k % NSTAGE])

    for k in range(NSTAGE):
        copy(k).start()
    nw_tile = jnp.concatenate([jnp.broadcast_to(nw_ref[:, t * LANES:(t + 1) * LANES], (LANES, LANES)).T
                               for t in range(D_MODEL // LANES)], axis=0)
    nw_rows = jnp.concatenate([nw_tile] * (WCHUNK // LANES), axis=1)

    def convert(ks):
        for k in ks:
            which, c0, scale = chunks[k]
            copy(k).wait()
            w = stage[k % NSTAGE]
            if which == 0:
                w = w * nw_rows if scale == 1.0 else w * (scale * nw_rows)
            dsts[which][:, c0:c0 + WCHUNK] = w.astype(BF16)
            if k + NSTAGE < len(chunks):
                copy(k + NSTAGE).start()

    n_in = sum(1 for which, _, _ in chunks if which == 0)
    convert(range(n_in))
    wpool_bf[...] = jnp.zeros(wpool_bf.shape, BF16)
    for g in range(len(POOL_WINDOWS)):
        d0 = (g % 2) * LANES
        wpool_bf[g // 2, d0:d0 + LANES, d0:d0 + LANES] = wpool_f32[g].astype(BF16)
    qn = qn_ref[...] * (HEAD_DIM ** -0.5 * LOG2E)
    qw_ref[...] = jnp.concatenate([qn, qn], axis=1)
    kw_ref[...] = jnp.concatenate([kn_ref[...], kn_ref[...]], axis=1)
    return lambda: convert(range(n_in, len(chunks)))


def _prompt_init(sinks_ref, kbuf, vbuf, uext, sink2, eye4, maskt):
    kbuf[:, 0:BLOCK, :] = jnp.zeros((2, BLOCK, LANES), BF16)
    vbuf[:, 0:BLOCK, :] = jnp.zeros((2, BLOCK, LANES), BF16)
    uext[0:16, :] = jnp.zeros((16, D_POOL), F32)
    lo = _lo_mask()
    for jj in range(N_HEADS // 2):
        sink2[jj] = jnp.where(lo, sinks_ref[2 * jj], sinks_ref[2 * jj + 1]) * LOG2E
    rr = lax.broadcasted_iota(jnp.int32, (GROUP * BLOCK, LANES), 0)
    ll = lax.broadcasted_iota(jnp.int32, (GROUP * BLOCK, LANES), 1)
    eye4[...] = jnp.where((rr & (BLOCK - 1)) == ll, 1.0, 0.0).astype(BF16)
    cc = lax.broadcasted_iota(jnp.int32, (2 * BLOCK, LANES), 0)
    qr = lax.broadcasted_iota(jnp.int32, (2 * BLOCK, LANES), 1)
    band = (cc >= qr) & (cc <= qr + WINDOW)
    maskt[0] = jnp.where(band & (cc >= BLOCK), 0.0, NEG_INF).astype(BF16)
    maskt[1] = jnp.where(band, 0.0, NEG_INF).astype(BF16)


def _prompt_step(i, last, x_ref, pscale_ref, y_ref, knew_ref, vnew_ref, unew_ref,
                 win_ref, wout_ref, wpool_ref, qw_ref, kw_ref, z_ref, hbuf, qbuf, kbuf, vbuf, uext, mixed, sink2, eye4, maskt,
                 side_work):
    lo = _lo_mask()
    hbuf[...] = x_ref[...].astype(BF16)
    rs = _row_scale(x_ref[...])

    def proj(cols, rows=slice(None)):
        return jnp.dot(hbuf[rows, :], win_ref[:, cols], preferred_element_type=F32) * rs[rows]

    z_ref[:, O_U:] = proj(slice(O_U, D_IN_PROJ))
    uext[16:, :] = z_ref[:, O_U:O_U + D_POOL]
    pos16 = i * TB + lax.broadcasted_iota(jnp.int32, (16, LANES), 0)
    ds = []
    for g, w in enumerate(POOL_WINDOWS):
        acc = uext[:, g * LANES:(g + 1) * LANES]
        sh = 1
        while sh < w:
            acc = acc + pltpu.roll(acc, sh, axis=0)
            sh *= 2
        ug = z_ref[:, O_U + g * LANES:O_U + (g + 1) * LANES]
        cnt = jnp.minimum(w, pos16 + 1).astype(F32)
        d_first = acc[16:32, :] / cnt - ug[0:16, :]
        d_rest = acc[32:, :] * (1.0 / w) - ug[16:, :]
        ds.append(jnp.concatenate([d_first, d_rest], axis=0).astype(BF16))
    uext[0:16, :] = uext[TB:TB + 16, :]

    z_ref[:, :O_K] = proj(slice(0, O_K))

    for pr in range(len(POOL_WINDOWS) // 2):
        cols = slice(pr * 2 * LANES, (pr + 1) * 2 * LANES)
        d2 = jnp.concatenate(ds[2 * pr:2 * pr + 2], axis=1)
        po = jnp.dot(d2, wpool_ref[pr], preferred_element_type=F32) * pscale_ref[:, cols]
        gp = z_ref[:, O_GP + pr * 2 * LANES:O_GP + (pr + 1) * 2 * LANES]
        mixed[:, D_ATTN + pr * 2 * LANES:D_ATTN + (pr + 1) * 2 * LANES] = (po * _silu_half(gp)).astype(BF16)

    for j in range(4):
        qhat = _pair_rms(z_ref[:, j * LANES:(j + 1) * LANES], lo, qw_ref[...])
        q_lo = jnp.where(lo, qhat, 0.0).astype(BF16)
        q_hi = jnp.where(lo, 0.0, qhat).astype(BF16)
        c, g0 = j // 2, 2 * (j % 2)
        for sb in range(NSB):
            rows = slice(sb * BLOCK, (sb + 1) * BLOCK)
            qbuf[c, sb, g0 * BLOCK:(g0 + 1) * BLOCK, :] = q_lo[rows]
            qbuf[c, sb, (g0 + 1) * BLOCK:(g0 + 2) * BLOCK, :] = q_hi[rows]

    half = TB // 2
    for r0 in (0, half):
        z_ref[r0:r0 + half, O_K:O_GA] = proj(slice(O_K, O_GA), slice(r0, r0 + half))
    z_ref[:, O_GA:O_U] = proj(slice(O_GA, O_U))

    khat = _pair_rms(z_ref[:, O_K:O_K + LANES], lo, kw_ref[...])
    kr = pltpu.roll(khat, HEAD_DIM, axis=1)
    kbuf[0, BLOCK:, :] = jnp.where(lo, khat, kr).astype(BF16)
    kbuf[1, BLOCK:, :] = jnp.where(lo, kr, khat).astype(BF16)
    vz = z_ref[:, O_V:O_V + LANES]
    vr = pltpu.roll(vz, HEAD_DIM, axis=1)
    vbuf[0, BLOCK:, :] = jnp.where(lo, vz, vr).astype(BF16)
    vbuf[1, BLOCK:, :] = jnp.where(lo, vr, vz).astype(BF16)

    first = jnp.where(i > 0, 1, 0)
    for sb in range(NSB):
        rows = slice(sb * BLOCK, (sb + 1) * BLOCK)
        mcols = maskt[first] if sb == 0 else maskt[1]
        for c in range(2):
            keys = kbuf[c, sb * BLOCK:(sb + 2) * BLOCK, :]
            vals = vbuf[c, sb * BLOCK:(sb + 2) * BLOCK, :]
            s = lax.dot_general(jnp.concatenate([qbuf[c, sb], eye4[...]], axis=1),
                                jnp.concatenate([keys, mcols], axis=1), _NT,
                                preferred_element_type=F32)
            ps, ms, ls = [], [], []
            for g in range(GROUP):
                sg = s[g * BLOCK:(g + 1) * BLOCK, :]
                m = jnp.max(sg, axis=-1, keepdims=True)
                p = jnp.exp2(sg - m)
                ps.append(p.astype(BF16))
                ms.append(m)
                ls.append(jnp.sum(p, axis=-1, keepdims=True))
            o = jnp.dot(jnp.concatenate(ps, axis=0), vals, preferred_element_type=F32)
            for jj in range(2):
                ev, od = 2 * jj, 2 * jj + 1
                slab = jnp.where(lo, o[ev * BLOCK:(ev + 1) * BLOCK, :], o[od * BLOCK:(od + 1) * BLOCK, :])
                l = jnp.where(lo, ls[ev], ls[od]) + jnp.exp2(sink2[2 * c + jj] - jnp.where(lo, ms[ev], ms[od]))
                col = (2 * c + jj) * LANES
                ga = z_ref[rows, O_GA + col:O_GA + col + LANES]
                mixed[rows, col:col + LANES] = (slab * (1.0 / l) * _silu_half(ga)).astype(BF16)

    kbuf[:, 0:BLOCK, :] = kbuf[:, TB:TB + BLOCK, :]
    vbuf[:, 0:BLOCK, :] = vbuf[:, TB:TB + BLOCK, :]

    y_ref[...] = x_ref[...] + jnp.dot(mixed[...], wout_ref[...], preferred_element_type=F32)
    side_work()

    @pl.when(last)
    def _():
        unew_ref[:, 0, :] = z_ref[TB - POOL_STATE:TB, O_U:O_U + D_POOL]
        tail = slice(TB - WINDOW, TB)
        knew_ref[...] = _pair_rms(z_ref[tail, O_K:O_K + LANES], lo, kw_ref[...]).T
        vnew_ref[...] = z_ref[tail, O_V:O_V + LANES].T


def _sample_init(x_ref, win_ref, qw_ref, kw_ref, z_ref, q3, krep, vrep, snew, kvt):
    nb = x_ref.shape[0]
    lo = _lo_mask()
    z_ref[...] = _project(x_ref[:, 0, :], win_ref[...])
    khat = _pair_rms(z_ref[:, O_K:O_K + LANES], lo, kw_ref[...])
    vnew = z_ref[:, O_V:O_V + LANES]
    kvt[0] = khat.T
    kvt[1] = vnew.T
    q3[...] = jnp.zeros(q3.shape, F32)
    krep[...] = jnp.zeros(krep.shape, F32)
    vrep[...] = jnp.zeros(vrep.shape, F32)
    for j in range(4):
        qhat = _pair_rms(z_ref[:, j * LANES:(j + 1) * LANES], lo, qw_ref[...])
        qrot = pltpu.roll(qhat, HEAD_DIM, axis=1)
        grp_lo = j < 2
        for half in range(2):
            h = 2 * j + half
            src = qhat if (half == 0) == grp_lo else qrot
            q3[pl.ds(h, nb, stride=QROWS), :] = jnp.where(lo if grp_lo else ~lo, src, 0.0)
            krep[pl.ds(h, nb, stride=QROWS), :] = khat
            vrep[pl.ds(h, nb, stride=QROWS), :] = vnew
    s_new = jnp.sum(q3[...] * krep[...], axis=-1, keepdims=True)
    snew[...] = jnp.broadcast_to(s_new, snew.shape)


def _shift_caches(i, nb, ck_ref, cv_ref, knew_ref, vnew_ref, kvt):
    shift = lax.rem(nb - i * CB, nb)
    newest = lax.broadcasted_iota(jnp.int32, (LANES, WINDOW), 1) == WINDOW - 1
    for src, dst, which in ((ck_ref, knew_ref, 0), (cv_ref, vnew_ref, 1)):
        cols = pltpu.roll(kvt[which], shift, axis=1)
        shifted = pltpu.roll(src[...], WINDOW - 1, axis=2)
        for bb in range(CB):
            dst[bb] = jnp.where(newest, cols[:, bb:bb + 1], shifted[bb])


def _sample_attend(i, sinks_ref, ck_ref, cv_ref, q3, vrep, snew, oall):
    qrows = pl.ds(pl.multiple_of(i * (CB * QROWS), CB * QROWS), CB * QROWS)
    rid = lax.broadcasted_iota(jnp.int32, (1, QROWS, 1), 1)
    sink3 = jnp.zeros((1, QROWS, 1), F32)
    for h in range(N_HEADS):
        sink3 = jnp.where(rid == h, sinks_ref[h] * LOG2E, sink3)

    q = q3[qrows, :].reshape(CB, QROWS, LANES)
    s_new = snew[qrows, :].reshape(CB, QROWS, LANES)
    kt = ck_ref[...]
    vt = cv_ref[...]
    s = lax.dot_general(q.astype(BF16), kt.astype(BF16), (((2,), (1,)), ((0,), (0,))), preferred_element_type=F32)
    m = jnp.maximum(jnp.maximum(jnp.max(s, axis=-1, keepdims=True), s_new), sink3)
    p = jnp.exp2(s - m)
    p_new = jnp.exp2(s_new - m)
    l = jnp.sum(p, axis=-1, keepdims=True) + p_new + jnp.exp2(sink3 - m)

    def finish():
        o = lax.dot_general(p.astype(BF16), vt.astype(BF16), (((2,), (2,)), ((0,), (0,))), preferred_element_type=F32)
        o = (o + p_new * vrep[qrows, :].reshape(CB, QROWS, LANES)) / l
        oall[qrows, :] = o.reshape(CB * QROWS, LANES)

    return finish


def _sample_step(j, last, x_ref, pscale_ref, sp_ref, y_ref, spnew_ref, wout_ref, wpool_ref, z_ref, oall, mixed):
    lo = _lo_mask()
    row0 = pl.multiple_of(j * BB, BB)
    rows = pl.ds(row0, BB)
    assert BB * QROWS == oall.shape[0]

    for jj in range(4):
        grp_lo = jj < 2
        oh = [oall[pl.ds(2 * jj + half, BB, stride=QROWS), :] for half in range(2)]
        a = oh[0] if grp_lo else pltpu.roll(oh[0], HEAD_DIM, axis=1)
        b = pltpu.roll(oh[1], HEAD_DIM, axis=1) if grp_lo else oh[1]
        ga = z_ref[rows, O_GA + jj * LANES:O_GA + (jj + 1) * LANES]
        mixed[rows, jj * LANES:(jj + 1) * LANES] = (jnp.where(lo, a, b) * _silu_half(ga)).astype(BF16)

    u = z_ref[rows, O_U:O_U + D_POOL]
    spnew_ref[0:POOL_STATE - 1] = sp_ref[1:POOL_STATE]
    spnew_ref[POOL_STATE - 1] = u
    ds = []
    for g, w in enumerate(POOL_WINDOWS):
        cols = slice(g * LANES, (g + 1) * LANES)
        ug = u[:, cols]
        win_sum = ug
        for r in range(POOL_STATE - (w - 1), POOL_STATE):
            win_sum = win_sum + sp_ref[r, :, cols]
        ds.append((win_sum * (1.0 / w) - ug).astype(BF16))
    for pr in range(len(POOL_WINDOWS) // 2):
        cols = slice(pr * 2 * LANES, (pr + 1) * 2 * LANES)
        d2 = jnp.concatenate(ds[2 * pr:2 * pr + 2], axis=1)
        po = jnp.dot(d2, wpool_ref[pr], preferred_element_type=F32) * pscale_ref[:, cols]
        gp = z_ref[rows, O_GP + pr * 2 * LANES:O_GP + (pr + 1) * 2 * LANES]
        mixed[rows, D_ATTN + pr * 2 * LANES:D_ATTN + (pr + 1) * 2 * LANES] = (po * _silu_half(gp)).astype(BF16)

    @pl.when(last)
    def _():
        y_ref[:, 0, :] = x_ref[:, 0, :] + jnp.dot(mixed[...], wout_ref[...], preferred_element_type=F32)


def _fused_kernel(n_prompt, sinks_ref, xp_ref, xs_ref, nw_ref, win_hbm, qn_ref, kn_ref, wpool_f32, pscale_ref, wout_hbm,
                  sp_ref, ckc_ref, cvc_ref,
                  yp_ref, kp_ref, vp_ref, up_ref, ys_ref, kq_ref, vq_ref, pq_ref,
                  win_ref, wout_ref, wpool_ref, qw_ref, kw_ref, stage, sem,
                  z_ref, hbuf, qbuf, kbuf, vbuf, uext, mixed, sink2, eye4, maskt,
                  zs_ref, q3, krep, vrep, snew, kvt, oall, mixed_s):
    i = pl.program_id(0)

    @pl.when(i == 0)
    def _():
        convert_wout = _load_weights(win_hbm, wout_hbm, wpool_f32, nw_ref, qn_ref, kn_ref, stage, sem,
                                     win_ref, wout_ref, wpool_ref, qw_ref, kw_ref)
        _prompt_init(sinks_ref, kbuf, vbuf, uext, sink2, eye4, maskt)
        _sample_init(xs_ref, win_ref, qw_ref, kw_ref, zs_ref, q3, krep, vrep, snew, kvt)
        convert_wout()

    @pl.when(i < n_prompt)
    def _():
        _shift_caches(i, xs_ref.shape[0], ckc_ref, cvc_ref, kq_ref, vq_ref, kvt)
        attend = _sample_attend(i, sinks_ref, ckc_ref, cvc_ref, q3, vrep, snew, oall)
        _prompt_step(i, i == n_prompt - 1, xp_ref, pscale_ref, yp_ref, kp_ref, vp_ref, up_ref,
                     win_ref, wout_ref, wpool_ref, qw_ref, kw_ref, z_ref, hbuf, qbuf, kbuf, vbuf, uext, mixed, sink2, eye4, maskt,
                     attend)

    @pl.when(i >= n_prompt)
    def _():
        _sample_step(i - n_prompt, i == pl.num_programs(0) - 1, xs_ref, pscale_ref, sp_ref,
                     ys_ref, pq_ref, wout_ref, wpool_ref, zs_ref, oall, mixed_s)


def _fused_call(sinks, xp, xs, nw, win, qn, kn, wpool, pscale, wout, ck, cv, sp):
    seq, nb = xp.shape[0], xs.shape[0]
    n_prompt, n_sample = seq // TB, nb // BB
    assert nb == CB * n_prompt
    assert n_sample == 1
    const2 = lambda i: (0, 0)
    const3 = lambda i: (0, 0, 0)
    pblk = lambda i: (jnp.minimum(i, n_prompt - 1), 0)
    cblk = lambda i: (jnp.minimum(i, n_prompt - 1), 0, 0)
    sidx = lambda i: jnp.clip(i - n_prompt, 0, n_sample - 1)
    mid3 = lambda i: (0, sidx(i), 0)
    once = pl.Buffered(1)
    return pl.pallas_call(
        functools.partial(_fused_kernel, n_prompt),
        grid=(n_prompt + n_sample,),
        in_specs=[
            pl.BlockSpec(memory_space=pltpu.SMEM),
            pl.BlockSpec((TB, D_MODEL), pblk),
            pl.BlockSpec((nb, 1, D_MODEL), const3, pipeline_mode=once),
            pl.BlockSpec((1, D_MODEL), const2),
            pl.BlockSpec(memory_space=pl.ANY),
            pl.BlockSpec((1, HEAD_DIM), const2),
            pl.BlockSpec((1, HEAD_DIM), const2),
            pl.BlockSpec((4, LANES, LANES), const3, pipeline_mode=once),
            pl.BlockSpec((1, D_POOL), const2),
            pl.BlockSpec(memory_space=pl.ANY),
            pl.BlockSpec((POOL_STATE, BB, D_POOL), mid3, pipeline_mode=once),
            pl.BlockSpec((CB, LANES, WINDOW), cblk),
            pl.BlockSpec((CB, LANES, WINDOW), cblk),
        ],
        out_specs=[
            pl.BlockSpec((TB, D_MODEL), pblk),
            pl.BlockSpec((LANES, WINDOW), const2),
            pl.BlockSpec((LANES, WINDOW), const2),
            pl.BlockSpec((POOL_STATE, 1, D_POOL), const3),
            pl.BlockSpec((nb, 1, D_MODEL), const3, pipeline_mode=once),
            pl.BlockSpec((CB, LANES, WINDOW), cblk),
            pl.BlockSpec((CB, LANES, WINDOW), cblk),
            pl.BlockSpec((POOL_STATE, BB, D_POOL), mid3, pipeline_mode=once),
        ],
        out_shape=[
            jax.ShapeDtypeStruct((seq, D_MODEL), F32),
            jax.ShapeDtypeStruct((LANES, WINDOW), F32),
            jax.ShapeDtypeStruct((LANES, WINDOW), F32),
            jax.ShapeDtypeStruct((POOL_STATE, 1, D_POOL), F32),
            jax.ShapeDtypeStruct((nb, 1, D_MODEL), F32),
            jax.ShapeDtypeStruct((nb, LANES, WINDOW), F32),
            jax.ShapeDtypeStruct((nb, LANES, WINDOW), F32),
            jax.ShapeDtypeStruct((POOL_STATE, nb, D_POOL), F32),
        ],
        scratch_shapes=[
            pltpu.VMEM((D_MODEL, D_IN_PROJ), BF16),
            pltpu.VMEM((D_MODEL, D_MODEL), BF16),
            pltpu.VMEM((2, 2 * LANES, 2 * LANES), BF16),
            pltpu.VMEM((1, LANES), F32),
            pltpu.VMEM((1, LANES), F32),
            pltpu.VMEM((NSTAGE, D_MODEL, WCHUNK), F32),
            pltpu.SemaphoreType.DMA((NSTAGE,)),
            pltpu.VMEM((TB, D_IN_PROJ), F32),
            pltpu.VMEM((TB, D_MODEL), BF16),
            pltpu.VMEM((2, NSB, GROUP * BLOCK, LANES), BF16),
            pltpu.VMEM((2, TB + BLOCK, LANES), BF16),
            pltpu.VMEM((2, TB + BLOCK, LANES), BF16),
            pltpu.VMEM((TB + 16, D_POOL), F32),
            pltpu.VMEM((TB, D_MODEL), BF16),
            pltpu.VMEM((N_HEADS // 2, 1, LANES), F32),
            pltpu.VMEM((GROUP * BLOCK, LANES), BF16),
            pltpu.VMEM((2, 2 * BLOCK, LANES), BF16),
            pltpu.VMEM((nb, D_IN_PROJ), F32),
            pltpu.VMEM((nb * QROWS, LANES), F32),
            pltpu.VMEM((nb * QROWS, LANES), F32),
            pltpu.VMEM((nb * QROWS, LANES), F32),
            pltpu.VMEM((nb * QROWS, LANES), F32),
            pltpu.VMEM((2, LANES, nb), F32),
            pltpu.VMEM((nb * QROWS, LANES), F32),
            pltpu.VMEM((nb, D_MODEL), BF16),
        ],
        compiler_params=pltpu.CompilerParams(dimension_semantics=("arbitrary",), vmem_limit_bytes=VMEM_LIMIT),
        name="hybrid_step",
    )(sinks, xp, xs, nw, win, qn, kn, wpool, pscale, wout, sp, ck, cv)


def _cache_in(c):
    nb = c.shape[0]
    return jnp.transpose(c, (0, 2, 3, 1)).reshape(nb, LANES, WINDOW)


def _cache_out(c):
    nb = c.shape[0]
    return jnp.transpose(c.reshape(nb, N_KV_HEADS, HEAD_DIM, WINDOW), (0, 3, 1, 2))


def kernel(x_prompt, x_sample, cache_k, cache_v, state_pool, norm_w, w_in, q_norm_w, k_norm_w, sinks, w_pool,
           pool_scale, w_out):
    depth = norm_w.shape[0]
    assert depth == 1 and x_prompt.shape[0] == 1 and x_sample.shape[1] == 1
    seq = x_prompt.shape[1]
    nb = x_sample.shape[0]
    assert seq % TB == 0 and nb % BB == 0 and nb == LANES

    yp, kp, vp, up, ys, kq, vq, pq = _fused_call(
        sinks[0], x_prompt[0], x_sample, norm_w, w_in[0], q_norm_w, k_norm_w, w_pool[0], pool_scale, w_out[0],
        _cache_in(cache_k[0]), _cache_in(cache_v[0]), jnp.transpose(state_pool[0], (1, 0, 2)))
    return (yp[None], ys, _cache_out(kp[None])[None], _cache_out(vp[None])[None],
            jnp.transpose(up, (1, 0, 2))[None],
            _cache_out(kq)[None], _cache_out(vq)[None], jnp.transpose(pq, (1, 0, 2))[None])
```
